```python
import jax, jax.numpy as jnp
from jax import lax
import numpy as np

D_MODEL = 2048
BATCH = 4
SEQ = 2048
DEPTH = 1

HEAD_DIM = 128
NSA_HEADS = 8
NSA_KV_HEADS = 2
SB_HEADS = 8
ROT_DIM = HEAD_DIM // 4
ROPE_THETA = 500000.0
CMP_LEN = 32
CMP_STRIDE = 16
SEL_LEN = 64
SEL_TOPN = 16
WINDOW = 512
FORCE_BONUS = 1000.0
QBLOCK = 128
SEL_QBLOCK = 32
MEM_LEN = 256
X_HEADS = 4
N_GROUPS = 4
EXPERTS_PER_GROUP = 8
N_EXPERTS = N_GROUPS * EXPERTS_PER_GROUP
D_EXPERT = 512
TOPK_IN_GROUP = 2
EPS = 1e-6

NSA_Q_W = NSA_HEADS * HEAD_DIM
NSA_KV_W = NSA_KV_HEADS * HEAD_DIM
SB_W = SB_HEADS * HEAD_DIM
X_W = X_HEADS * HEAD_DIM
IN_SIZES = (NSA_Q_W,) + (NSA_KV_W,) * 6 + (3 * NSA_HEADS, SB_W, SB_W, SB_W, D_MODEL, D_MODEL)
IN_WIDTH = sum(IN_SIZES)

kernel_name = "hybrid_nsa_stickbreak_hmoe_block"


def rms_norm(x, g):
    xf = x.astype(jnp.float32)
    y = xf * lax.rsqrt(jnp.mean(xf * xf, axis=-1, keepdims=True) + EPS)
    return (y * g.astype(jnp.float32)).astype(x.dtype)


def to_heads(x, n):
    b, t, _ = x.shape
    return x.reshape(b, t, n, HEAD_DIM).transpose(0, 2, 1, 3)


def from_heads(x):
    b, h, t, d = x.shape
    return x.transpose(0, 2, 1, 3).reshape(b, t, h * d)


def partial_rope(x, positions):
    half = ROT_DIM // 2
    inv = ROPE_THETA ** (-jnp.arange(half, dtype=jnp.float32) * (2.0 / ROT_DIM))
    ang = positions.astype(jnp.float32)[:, None, :, None] * inv
    cos, sin = jnp.cos(ang), jnp.sin(ang)
    xf = x.astype(jnp.float32)
    x1, x2 = xf[..., :half], xf[..., half:ROT_DIM]
    out = jnp.concatenate([x1 * cos - x2 * sin, x2 * cos + x1 * sin, xf[..., ROT_DIM:]], axis=-1)
    return out.astype(x.dtype)


def masked_softmax(s, mask):
    s = jnp.where(mask, s, -jnp.inf)
    m = jnp.max(s, axis=-1, keepdims=True)
    m = jnp.where(jnp.isfinite(m), m, 0.0)
    e = jnp.where(mask, jnp.exp(s - m), 0.0)
    den = jnp.sum(e, axis=-1, keepdims=True)
    return e / jnp.where(den > 0, den, 1.0)


def compress_blocks(tok_blocks, pe, w1, w2):
    flat = (tok_blocks + pe).reshape(*tok_blocks.shape[:-2], CMP_LEN * HEAD_DIM)
    return jax.nn.silu(flat @ w1) @ w2


def nsa_attention(q, kc_tok, vc_tok, ks, vs, kw, vw, gates,
                  pe_k, w1_k, w2_k, pe_v, w1_v, w2_v):
    b, h, t_len, d = q.shape
    hkv = kc_tok.shape[1]
    grp = h // hkv
    scale = d ** -0.5
    qg = q.reshape(b, hkv, grp, t_len, d)
    t_all = jnp.arange(t_len)

    n_cmp = (t_len - CMP_LEN) // CMP_STRIDE + 1
    cmp_idx = np.arange(n_cmp)[:, None] * CMP_STRIDE + np.arange(CMP_LEN)[None, :]
    kc = compress_blocks(kc_tok[:, :, cmp_idx], pe_k, w1_k, w2_k)
    vc = compress_blocks(vc_tok[:, :, cmp_idx], pe_v, w1_v, w2_v)
    cmp_vis = jnp.asarray(cmp_idx[:, -1])[None, :] <= t_all[:, None]
    s_c = jnp.einsum("bkgtd,bkcd->bkgtc", qg, kc).astype(jnp.float32) * scale
    p_c = masked_softmax(s_c, cmp_vis)
    o_c = jnp.einsum("bkgtc,bkcd->bkgtd", p_c.astype(vc.dtype), vc)

    n_sel = t_len // SEL_LEN
    cmp_start = np.arange(n_cmp) * CMP_STRIDE
    sel_start = np.arange(n_sel) * SEL_LEN
    overlap = ((cmp_start[:, None] < sel_start[None, :] + SEL_LEN)
               & (cmp_start[:, None] + CMP_LEN > sel_start[None, :])).astype(np.float32)
    imp = jnp.einsum("bkgtc,cj->bktj", p_c, jnp.asarray(overlap))
    cur = t_all // SEL_LEN
    jb = jnp.arange(n_sel)
    forced = (jb[None, :] == 0) | (jb[None, :] == cur[:, None]) | (jb[None, :] == cur[:, None] - 1)
    valid = jnp.asarray(sel_start)[None, :] <= t_all[:, None]
    score = jnp.where(valid, imp + jnp.where(forced, FORCE_BONUS, 0.0), -jnp.inf)
    top_n = min(SEL_TOPN, n_sel)
    top_val, top_idx = lax.top_k(score, top_n)
    top_ok = jnp.isfinite(top_val)

    k_blk = ks.reshape(b, hkv, n_sel, SEL_LEN, d)
    v_blk = vs.reshape(b, hkv, n_sel, SEL_LEN, d)
    kw_pad = jnp.pad(kw, ((0, 0), (0, 0), (WINDOW, 0), (0, 0)))
    vw_pad = jnp.pad(vw, ((0, 0), (0, 0), (WINDOW, 0), (0, 0)))
    nq = t_len // SEL_QBLOCK
    q_blocks = jnp.moveaxis(qg.reshape(b, hkv, grp, nq, SEL_QBLOCK, d), 3, 0)
    i_blocks = jnp.moveaxis(top_idx.reshape(b, hkv, nq, SEL_QBLOCK, top_n), 2, 0)
    ok_blocks = jnp.moveaxis(top_ok.reshape(b, hkv, nq, SEL_QBLOCK, top_n), 2, 0)
    bi = jnp.arange(b)[:, None, None, None]
    hi = jnp.arange(hkv)[None, :, None, None]

    def one_block(args):
        qb, ib, okb, c = args
        tq = c * SEL_QBLOCK + jnp.arange(SEL_QBLOCK)
        kg = k_blk[bi, hi, ib]
        vg = v_blk[bi, hi, ib]
        s_pos = ib[..., None] * SEL_LEN + jnp.arange(SEL_LEN)
        allow = okb[..., None] & (s_pos <= tq[None, None, :, None, None])
        s = jnp.einsum("bkgtd,bktnld->bkgtnl", qb, kg).astype(jnp.float32) * scale
        p = masked_softmax(s.reshape(*s.shape[:4], -1),
                           allow.reshape(b, hkv, 1, SEL_QBLOCK, -1)).reshape(s.shape)
        o_s = jnp.einsum("bkgtnl,bktnld->bkgtd", p.astype(vg.dtype), vg)
        kwb = lax.dynamic_slice_in_dim(kw_pad, c * SEL_QBLOCK, WINDOW + SEL_QBLOCK, axis=2)
        vwb = lax.dynamic_slice_in_dim(vw_pad, c * SEL_QBLOCK, WINDOW + SEL_QBLOCK, axis=2)
        w_pos = c * SEL_QBLOCK - WINDOW + jnp.arange(WINDOW + SEL_QBLOCK)
        diff = tq[:, None] - w_pos[None, :]
        allow_w = (diff >= 0) & (diff < WINDOW) & (w_pos[None, :] >= 0)
        s_w = jnp.einsum("bkgtd,bksd->bkgts", qb, kwb).astype(jnp.float32) * scale
        p_w = masked_softmax(s_w, allow_w)
        o_w = jnp.einsum("bkgts,bksd->bkgtd", p_w.astype(vwb.dtype), vwb)
        return o_s, o_w

    o_s, o_w = lax.map(one_block, (q_blocks, i_blocks, ok_blocks, jnp.arange(nq)))
    o_s = jnp.moveaxis(o_s, 0, 3).reshape(b, hkv, grp, t_len, d)
    o_w = jnp.moveaxis(o_w, 0, 3).reshape(b, hkv, grp, t_len, d)
    g = gates.transpose(0, 2, 1, 3).reshape(b, hkv, grp, t_len, 3).astype(q.dtype)
    o = g[..., 0:1] * o_c + g[..., 1:2] * o_s + g[..., 2:3] * o_w
    return o.reshape(b, h, t_len, d)


def stick_breaking_attention(q, k, v):
    b, h, t_len, d = q.shape
    scale = d ** -0.5
    outs = []
    for c in range(t_len // QBLOCK):
        end = (c + 1) * QBLOCK
        qb = q[:, :, c * QBLOCK:end]
        z = jnp.einsum("bhtd,bhsd->bhts", qb, k[:, :, :end]).astype(jnp.float32) * scale
        tq = c * QBLOCK + jnp.arange(QBLOCK)
        strict = jnp.arange(end)[None, :] < tq[:, None]
        log_1m = jnp.where(strict, jax.nn.log_sigmoid(-z), 0.0)
        log_w = jax.nn.log_sigmoid(z) + lax.cumsum(log_1m, axis=3, reverse=True) - log_1m
        a = jnp.where(strict, jnp.exp(log_w), 0.0)
        outs.append(jnp.einsum("bhts,bhsd->bhtd", a.astype(v.dtype), v[:, :, :end]))
    return jnp.concatenate(outs, axis=2)


def cross_attention(h, m, w_q, w_k, w_v, w_out):
    q = to_heads(h @ w_q, X_HEADS)
    k = to_heads(m @ w_k, X_HEADS)
    v = to_heads(m @ w_v, X_HEADS)
    s = jnp.einsum("bhtd,bhmd->bhtm", q, k).astype(jnp.float32) * (HEAD_DIM ** -0.5)
    p = jax.nn.softmax(s, axis=-1).astype(v.dtype)
    return from_heads(jnp.einsum("bhtm,bhmd->bhtd", p, v)) @ w_out


def hier_moe(h, w_rg, b_rg, w_re, b_re, w_eg, w_eu, w_ed):
    b, t_len, d = h.shape
    hn = h.reshape(b * t_len, d)
    p_grp = jax.nn.softmax((hn @ w_rg + b_rg).astype(jnp.float32), axis=-1)
    g_top = jnp.argmax(p_grp, axis=-1)
    p_gtop = jnp.max(p_grp, axis=-1)
    e_logits = (hn @ w_re + b_re).astype(jnp.float32).reshape(-1, N_GROUPS, EXPERTS_PER_GROUP)
    e_in = jnp.take_along_axis(e_logits, g_top[:, None, None], axis=1)[:, 0]
    p_in = jax.nn.softmax(e_in, axis=-1)
    top_p, top_e = lax.top_k(p_in, TOPK_IN_GROUP)
    w_sel = top_p / jnp.sum(top_p, axis=-1, keepdims=True) * p_gtop[:, None]
    w_grp = jnp.sum(jax.nn.one_hot(top_e, EXPERTS_PER_GROUP, dtype=jnp.float32) * w_sel[..., None], axis=1)
    combine = (jax.nn.one_hot(g_top, N_GROUPS, dtype=jnp.float32)[:, :, None] * w_grp[:, None, :]).astype(h.dtype)
    y = jnp.zeros_like(hn)
    for gi in range(N_GROUPS):
        a = jnp.einsum("nd,edf->nef", hn, w_eg[gi])
        u = jnp.einsum("nd,edf->nef", hn, w_eu[gi])
        y = y + jnp.einsum("nef,efd->nd", jax.nn.silu(a) * u * combine[:, gi, :, None], w_ed[gi])
    return y.reshape(b, t_len, d)


def setup_inputs(seed: int = 0) -> dict:
    key = jax.random.key(seed)
    ks = iter(jax.random.split(key, 40))
    f32 = jnp.float32

    def dense(shape, fan_in):
        return jax.random.normal(next(ks), shape, f32) * (fan_in ** -0.5)

    def gain(shape):
        return 1.0 + 0.02 * jax.random.normal(next(ks), shape, f32)

    def small(shape, s):
        return s * jax.random.normal(next(ks), shape, f32)

    L = DEPTH
    return {
        "x": jax.random.normal(next(ks), (BATCH, SEQ, D_MODEL), f32),
        "mem": jax.random.normal(next(ks), (BATCH, MEM_LEN, D_MODEL), f32),
        "positions": jnp.broadcast_to(jnp.arange(SEQ, dtype=jnp.int32), (BATCH, SEQ)),
        "g_mix": gain((L, D_MODEL)),
        "w_in": dense((L, D_MODEL, IN_WIDTH), D_MODEL),
        "cmp_pe_k": small((L, CMP_LEN, HEAD_DIM), 0.1),
        "cmp_w1_k": dense((L, CMP_LEN * HEAD_DIM, HEAD_DIM), CMP_LEN * HEAD_DIM),
        "cmp_w2_k": dense((L, HEAD_DIM, HEAD_DIM), HEAD_DIM),
        "cmp_pe_v": small((L, CMP_LEN, HEAD_DIM), 0.1),
        "cmp_w1_v": dense((L, CMP_LEN * HEAD_DIM, HEAD_DIM), CMP_LEN * HEAD_DIM),
        "cmp_w2_v": dense((L, HEAD_DIM, HEAD_DIM), HEAD_DIM),
        "w_br_nsa": dense((L, NSA_Q_W, D_MODEL), NSA_Q_W),
        "w_br_sb": dense((L, SB_W, D_MODEL), SB_W),
        "w_o": dense((L, D_MODEL, D_MODEL), D_MODEL),
        "g_cross": gain((L, D_MODEL)),
        "g_mem": gain((L, D_MODEL)),
        "w_cq": dense((L, D_MODEL, X_W), D_MODEL),
        "w_ck": dense((L, D_MODEL, X_W), D_MODEL),
        "w_cv": dense((L, D_MODEL, X_W), D_MODEL),
        "w_co": dense((L, X_W, D_MODEL), X_W),
        "g_moe": gain((L, D_MODEL)),
        "w_rg": dense((L, D_MODEL, N_GROUPS), D_MODEL),
        "b_rg": small((L, N_GROUPS), 0.01),
        "w_re": dense((L, D_MODEL, N_EXPERTS), D_MODEL),
        "b_re": small((L, N_EXPERTS), 0.01),
        "w_eg": dense((L, N_GROUPS, EXPERTS_PER_GROUP, D_MODEL, D_EXPERT), D_MODEL),
        "w_eu": dense((L, N_GROUPS, EXPERTS_PER_GROUP, D_MODEL, D_EXPERT), D_MODEL),
        "w_ed": dense((L, N_GROUPS, EXPERTS_PER_GROUP, D_EXPERT, D_MODEL), D_EXPERT),
        "g_final": gain((D_MODEL,)),
    }


def reference(x, mem, positions, g_mix, w_in, cmp_pe_k, cmp_w1_k, cmp_w2_k,
              cmp_pe_v, cmp_w1_v, cmp_w2_v, w_br_nsa, w_br_sb, w_o,
              g_cross, g_mem, w_cq, w_ck, w_cv, w_co,
              g_moe, w_rg, b_rg, w_re, b_re, w_eg, w_eu, w_ed, g_final):
    b, t_len, _ = x.shape
    split_at = [int(v) for v in np.cumsum(IN_SIZES)[:-1]]
    for l in range(DEPTH):
        h = rms_norm(x, g_mix[l])
        (q_n, kc_t, vc_t, ks_t, vs_t, kw_t, vw_t, nsa_g,
         q_s, k_s, v_s, gate_a, gate_b) = jnp.split(h @ w_in[l], split_at, axis=-1)
        q_n = partial_rope(to_heads(q_n, NSA_HEADS), positions)
        kc_t = partial_rope(to_heads(kc_t, NSA_KV_HEADS), positions)
        ks_t = partial_rope(to_heads(ks_t, NSA_KV_HEADS), positions)
        kw_t = partial_rope(to_heads(kw_t, NSA_KV_HEADS), positions)
        nsa_gates = jax.nn.sigmoid(nsa_g.astype(jnp.float32)).reshape(b, t_len, NSA_HEADS, 3)
        o_nsa = nsa_attention(q_n, kc_t, to_heads(vc_t, NSA_KV_HEADS), ks_t,
                              to_heads(vs_t, NSA_KV_HEADS), kw_t, to_heads(vw_t, NSA_KV_HEADS),
                              nsa_gates, cmp_pe_k[l], cmp_w1_k[l], cmp_w2_k[l],
                              cmp_pe_v[l], cmp_w1_v[l], cmp_w2_v[l])
        o_sb = stick_breaking_attention(to_heads(q_s, SB_HEADS), to_heads(k_s, SB_HEADS),
                                        to_heads(v_s, SB_HEADS))
        y_a = from_heads(o_nsa) @ w_br_nsa[l]
        y_b = from_heads(o_sb) @ w_br_sb[l]
        ga = jax.nn.sigmoid(gate_a.astype(jnp.float32)).astype(x.dtype)
        gb = jax.nn.sigmoid(gate_b.astype(jnp.float32)).astype(x.dtype)
        x = x + (ga * y_a + gb * y_b) @ w_o[l]
        x = x + cross_attention(rms_norm(x, g_cross[l]), rms_norm(mem, g_mem[l]),
                                w_cq[l], w_ck[l], w_cv[l], w_co[l])
        x = x + hier_moe(rms_norm(x, g_moe[l]), w_rg[l], b_rg[l], w_re[l], b_re[l],
                         w_eg[l], w_eu[l], w_ed[l])
    return rms_norm(x, g_final)
```

```python
import functools

import jax
import jax.numpy as jnp
from jax import lax
from jax.experimental import pallas as pl
from jax.experimental.pallas import tpu as pltpu

F32 = jnp.float32
BF16 = jnp.bfloat16

HEAD_DIM = 128
NSA_HEADS = 8
NSA_KV_HEADS = 2
NSA_GROUP = NSA_HEADS // NSA_KV_HEADS
SB_HEADS = 8
ROT_DIM = HEAD_DIM // 4
ROPE_THETA = 500000.0
CMP_LEN = 32
CMP_STRIDE = 16
SEL_LEN = 64
SEL_SHIFT = 6
SEL_TOPN = 16
WINDOW = 512
FORCE_BONUS = 1000.0
X_HEADS = 4
N_GROUPS = 4
EXPERTS_PER_GROUP = 8
N_EXPERTS = N_GROUPS * EXPERTS_PER_GROUP
EPS = 1e-6
SCALE = HEAD_DIM ** -0.5
NEG = -1e30

NSA_Q_W = NSA_HEADS * HEAD_DIM
NSA_KV_W = NSA_KV_HEADS * HEAD_DIM
NSA_W = NSA_Q_W + 6 * NSA_KV_W
NSA_GATES = 3 * NSA_HEADS
SB_W = SB_HEADS * HEAD_DIM
LANES = 128
VMEM_LIMIT = 56 * 1024 * 1024

EXPERT_TILE = 256


def _cp(*sem, vmem=VMEM_LIMIT):
    return pltpu.CompilerParams(dimension_semantics=sem, vmem_limit_bytes=vmem)


def _nt_dot(a, b):
    return lax.dot_general(a, b, (((1,), (1,)), ((), ())), preferred_element_type=F32)


def _dot(a, b):
    return jnp.dot(a, b, preferred_element_type=F32)


def _split3(x):
    hi = x.astype(BF16)
    r1 = x - hi.astype(F32)
    mid = r1.astype(BF16)
    lo = (r1 - mid.astype(F32)).astype(BF16)
    return hi, mid, lo


def _rms(x, g):
    return x * lax.rsqrt(jnp.mean(x * x, axis=-1, keepdims=True) + EPS) * g


def _rmsnorm_kernel(x_ref, g_ref, o_ref):
    o_ref[...] = _rms(x_ref[...].astype(F32), g_ref[...]).astype(o_ref.dtype)


def _rmsnorm(x, g, out_dtype, tm=512):
    n, d = x.shape
    return pl.pallas_call(
        _rmsnorm_kernel,
        grid=(n // tm,),
        in_specs=[pl.BlockSpec((tm, d), lambda i: (i, 0)), pl.BlockSpec((1, d), lambda i: (0, 0))],
        out_specs=pl.BlockSpec((tm, d), lambda i: (i, 0)),
        out_shape=jax.ShapeDtypeStruct((n, d), out_dtype),
        compiler_params=_cp("parallel"),
        name="rmsnorm",
    )(x, g.reshape(1, d))


def _rope_half(x, c, s1, s2):
    return x * c + pltpu.roll(x, LANES - ROT_DIM // 2, 1) * s1 + pltpu.roll(x, ROT_DIM // 2, 1) * s2


def _mm_nsa_kernel(a_ref, w_ref, c_ref, s1_ref, s2_ref, o_ref, *, tn):
    j = pl.program_id(1)
    acc = _dot(a_ref[...], w_ref[...])
    is_rope = (j < 5) | (j == 6) | (j == 8)

    @pl.when(is_rope)
    def _():
        c, s1, s2 = c_ref[...], s1_ref[...], s2_ref[...]
        for hh in range(tn // LANES):
            sl = slice(hh * LANES, (hh + 1) * LANES)
            o_ref[:, sl] = _rope_half(acc[:, sl], c, s1, s2).astype(o_ref.dtype)

    @pl.when(jnp.logical_not(is_rope))
    def _():
        o_ref[...] = acc.astype(o_ref.dtype)


def _mm_plain_kernel(a_ref, w_ref, o_ref):
    o_ref[...] = _dot(a_ref[...], w_ref[...]).astype(o_ref.dtype)


def _mm_sigmoid_kernel(a_ref, w_ref, o_ref):
    o_ref[...] = jax.nn.sigmoid(_dot(a_ref[...], w_ref[...])).astype(o_ref.dtype)


def _mm_resid_kernel(a_ref, w_ref, x_ref, o_ref):
    o_ref[...] = x_ref[...] + _dot(a_ref[...], w_ref[...])


def _mm(kern, a, w, n_cols, col_block0, tm, tn, out_dtype, extra=(), extra_specs=(), name="mm"):
    m, k = a.shape
    return pl.pallas_call(
        kern,
        grid=(m // tm, n_cols // tn),
        in_specs=[pl.BlockSpec((tm, k), lambda i, j: (i, 0)),
                  pl.BlockSpec((k, tn), lambda i, j: (0, col_block0 + j)), *extra_specs],
        out_specs=pl.BlockSpec((tm, tn), lambda i, j: (i, j)),
        out_shape=jax.ShapeDtypeStruct((m, n_cols), out_dtype),
        compiler_params=_cp("parallel", "arbitrary"),
        name=name,
    )(a, w, *extra)


def _merge_kernel(a1_ref, a2_ref, w1_ref, w2_ref, ga_ref, gb_ref, o_ref):
    ya = _dot(a1_ref[...], w1_ref[...])
    yb = _dot(a2_ref[...], w2_ref[...])
    o_ref[...] = (ga_ref[...].astype(F32) * ya + gb_ref[...].astype(F32) * yb).astype(o_ref.dtype)


def _merge(o_nsa, o_sb, w1, w2, gates, tm=1024, tn=512):
    n, k = o_nsa.shape
    d = w1.shape[1]
    nb = d // tn
    return pl.pallas_call(
        _merge_kernel,
        grid=(n // tm, nb),
        in_specs=[pl.BlockSpec((tm, k), lambda i, j: (i, 0)), pl.BlockSpec((tm, k), lambda i, j: (i, 0)),
                  pl.BlockSpec((k, tn), lambda i, j: (0, j)), pl.BlockSpec((k, tn), lambda i, j: (0, j)),
                  pl.BlockSpec((tm, tn), lambda i, j: (i, j)), pl.BlockSpec((tm, tn), lambda i, j: (i, nb + j))],
        out_specs=pl.BlockSpec((tm, tn), lambda i, j: (i, j)),
        out_shape=jax.ShapeDtypeStruct((n, d), BF16),
        compiler_params=_cp("parallel", "arbitrary"),
        name="merge",
    )(o_nsa, o_sb, w1, w2, gates, gates)


def _norm_mm_kernel(x_ref, g_ref, w_ref, o_ref):
    h = _rms(x_ref[...].astype(F32), g_ref[...]).astype(BF16)
    o_ref[...] = _dot(h, w_ref[...]).astype(o_ref.dtype)


def _norm_mm(x, g, w, out_dtype, tm=256):
    n, d = x.shape
    nc = w.shape[1]
    return pl.pallas_call(
        _norm_mm_kernel,
        grid=(n // tm,),
        in_specs=[pl.BlockSpec((tm, d), lambda i: (i, 0)), pl.BlockSpec((1, d), lambda i: (0, 0)),
                  pl.BlockSpec((d, nc), lambda i: (0, 0))],
        out_specs=pl.BlockSpec((tm, nc), lambda i: (i, 0)),
        out_shape=jax.ShapeDtypeStruct((n, nc), out_dtype),
        compiler_params=_cp("parallel"),
        name="norm_mm",
    )(x, g.reshape(1, d), w)


def _compress_kernel(x_ref, pe_ref, w1_ref, w2_ref, o_ref):
    x = x_ref[...].astype(F32)
    half = x.shape[1]
    xa = (x + pe_ref[0:1, :]).astype(BF16)
    xb = (x + pe_ref[1:2, :]).astype(BF16)
    y1 = _dot(xa, w1_ref[0:half, :].astype(BF16))
    y2 = _dot(xb, w1_ref[half:2 * half, :].astype(BF16))
    pre = y1 + pltpu.roll(y2, x.shape[0] - 1, 0)
    act = pre * (1.0 / (1.0 + jnp.exp(-pre)))
    o_ref[...] = _dot(act.astype(BF16), w2_ref[...].astype(BF16)).astype(o_ref.dtype)


def _compress(tok, pe, w1, w2):
    b, hkv, nchunk, width = tok.shape
    return pl.pallas_call(
        _compress_kernel,
        grid=(b, hkv),
        in_specs=[pl.BlockSpec((None, None, nchunk, width), lambda i, j: (i, j, 0, 0)),
                  pl.BlockSpec((2, width), lambda i, j: (0, 0)),
                  pl.BlockSpec((2 * width, HEAD_DIM), lambda i, j: (0, 0)),
                  pl.BlockSpec((HEAD_DIM, HEAD_DIM), lambda i, j: (0, 0))],
        out_specs=pl.BlockSpec((None, None, nchunk, HEAD_DIM), lambda i, j: (i, j, 0, 0)),
        out_shape=jax.ShapeDtypeStruct((b, hkv, nchunk, HEAD_DIM), BF16),
        compiler_params=_cp("parallel", "parallel"),
        name="nsa_compress",
    )(tok, pe.reshape(2, width), w1, w2)


def _nsa_cmp_kernel(q_ref, kc_ref, vc_ref, oc_ref, sel_ref, *, tq, nc, ns):
    q0 = pl.program_id(2) * tq
    kc = kc_ref[...]
    vc = vc_ref[...]
    t_row = q0 + lax.broadcasted_iota(jnp.int32, (tq, nc), 0)
    c_idx = lax.broadcasted_iota(jnp.int32, (tq, nc), 1)
    vis = (c_idx * CMP_STRIDE + (CMP_LEN - 1) <= t_row) & (c_idx < nc - 1)
    imp = jnp.zeros((tq, nc), F32)
    for h in range(NSA_GROUP):
        sl = slice(h * HEAD_DIM, (h + 1) * HEAD_DIM)
        s = jnp.where(vis, _nt_dot(q_ref[:, sl], kc) * SCALE, NEG)
        m = jnp.max(s, axis=-1, keepdims=True)
        e = jnp.where(vis, jnp.exp(s - m), 0.0)
        den = jnp.sum(e, axis=-1, keepdims=True)
        p = e / jnp.where(den > 0, den, 1.0)
        oc_ref[:, sl] = _dot(p.astype(BF16), vc).astype(oc_ref.dtype)
        imp = imp + p
    jj = lax.broadcasted_iota(jnp.int32, (ns, nc), 0)
    cc = lax.broadcasted_iota(jnp.int32, (ns, nc), 1) * CMP_STRIDE
    ov = jnp.where((cc < (jj + 1) * SEL_LEN) & (cc + CMP_LEN > jj * SEL_LEN), 1.0, 0.0).astype(BF16)
    imp_t = sum(_nt_dot(ov, part) for part in _split3(imp))
    jb = lax.broadcasted_iota(jnp.int32, (ns, tq), 0)
    tt = q0 + lax.broadcasted_iota(jnp.int32, (ns, tq), 1)
    cur = lax.shift_right_logical(tt, jnp.full_like(tt, SEL_SHIFT))
    forced = (jb == 0) | (jb == cur) | (jb == cur - 1)
    valid = jb * SEL_LEN <= tt
    score = jnp.where(valid, imp_t + jnp.where(forced, FORCE_BONUS, 0.0), NEG)
    rank = jnp.zeros((ns, tq), F32)
    for i in range(ns):
        row = score[i:i + 1, :]
        rank = rank + jnp.where(row > score, 1.0, jnp.where((row == score) & (jb > i), 1.0, 0.0))
    sel_t = jnp.where(valid & (rank < float(min(SEL_TOPN, ns))), 1.0, 0.0).astype(BF16)
    eye = jnp.where(lax.broadcasted_iota(jnp.int32, (tq, tq), 0) == lax.broadcasted_iota(jnp.int32, (tq, tq), 1),
                    1.0, 0.0).astype(BF16)
    sel_ref[...] = _nt_dot(eye, sel_t)


def _nsa_cmp(qkv, kc, vc, b, t, tq=256):
    nc = kc.shape[2]
    ns = t // SEL_LEN
    nq = t // tq
    gw = NSA_GROUP * HEAD_DIM
    return pl.pallas_call(
        functools.partial(_nsa_cmp_kernel, tq=tq, nc=nc, ns=ns),
        grid=(b, NSA_KV_HEADS, nq),
        in_specs=[pl.BlockSpec((tq, gw), lambda i, k, j: (i * nq + j, k)),
                  pl.BlockSpec((None, None, nc, HEAD_DIM), lambda i, k, j: (i, k, 0, 0)),
                  pl.BlockSpec((None, None, nc, HEAD_DIM), lambda i, k, j: (i, k, 0, 0))],
        out_specs=[pl.BlockSpec((tq, gw), lambda i, k, j: (i * nq + j, k)),
                   pl.BlockSpec((None, None, tq, ns), lambda i, k, j: (i, k, j, 0))],
        out_shape=[jax.ShapeDtypeStruct((b * t, NSA_Q_W), BF16),
                   jax.ShapeDtypeStruct((b, NSA_KV_HEADS, t, ns), F32)],
        compiler_params=_cp("parallel", "parallel", "parallel"),
        name="nsa_cmp_select",
    )(qkv, kc, vc)


def _nsa_sel_win_kernel(q_ref, ks_ref, vs_ref, kw_ref, vw_ref, sel_ref, oc_ref, g_ref, o_ref, *, tq, tk, t_len):
    q0 = pl.program_id(2) * tq
    g = NSA_GROUP
    q = jnp.concatenate([q_ref[:, h * HEAD_DIM:(h + 1) * HEAD_DIM] for h in range(g)], axis=0)
    sel = sel_ref[...].astype(BF16)
    ns = sel.shape[1]
    per_tile = tk // SEL_LEN
    t_row = q0 + lax.broadcasted_iota(jnp.int32, (tq, tk), 0)
    col = lax.broadcasted_iota(jnp.int32, (tq, tk), 1)
    ej = lax.broadcasted_iota(jnp.int32, (ns, tk), 0)
    eb = lax.broadcasted_iota(jnp.int32, (ns, tk), 1)
    eb = lax.shift_right_logical(eb, jnp.full_like(eb, SEL_SHIFT))

    def sel_body(kt, carry):
        m, l, acc = carry
        k0 = pl.multiple_of(kt * tk, tk)
        k = ks_ref[pl.ds(k0, tk), :]
        v = vs_ref[pl.ds(k0, tk), :]
        s = (_nt_dot(q, k) * SCALE).reshape(g, tq, tk)
        expand = jnp.where(ej == eb + kt * per_tile, 1.0, 0.0).astype(BF16)
        allow = ((_dot(sel, expand) > 0.5) & (col + k0 <= t_row))[None]
        s = jnp.where(allow, s, NEG)
        m_new = jnp.maximum(m, jnp.max(s, axis=-1, keepdims=True))
        alpha = jnp.exp(m - m_new)
        p = jnp.where(allow, jnp.exp(s - m_new), 0.0)
        l = alpha * l + jnp.sum(p, axis=-1, keepdims=True)
        pv = _dot(p.reshape(g * tq, tk).astype(BF16), v).reshape(g, tq, HEAD_DIM)
        return m_new, l, alpha * acc + pv

    n_kt = (q0 + tq + tk - 1) // tk
    m0 = jnp.full((g, tq, 1), NEG, F32)
    l0 = jnp.zeros((g, tq, 1), F32)
    a0 = jnp.zeros((g, tq, HEAD_DIM), F32)
    _, l, acc = lax.fori_loop(0, n_kt, sel_body, (m0, l0, a0))
    o_s = acc / l

    wk = WINDOW + tq
    start = pl.multiple_of(jnp.clip(q0 - WINDOW, 0, t_len - wk), tq)
    kwin = kw_ref[pl.ds(start, wk), :]
    vwin = vw_ref[pl.ds(start, wk), :]
    tw = q0 + lax.broadcasted_iota(jnp.int32, (tq, wk), 0)
    diff = tw - (start + lax.broadcasted_iota(jnp.int32, (tq, wk), 1))
    allow_w = ((diff >= 0) & (diff < WINDOW))[None]
    s_w = jnp.where(allow_w, (_nt_dot(q, kwin) * SCALE).reshape(g, tq, wk), NEG)
    m_w = jnp.max(s_w, axis=-1, keepdims=True)
    e_w = jnp.where(allow_w, jnp.exp(s_w - m_w), 0.0)
    p_w = e_w / jnp.sum(e_w, axis=-1, keepdims=True)
    o_w = _dot(p_w.reshape(g * tq, wk).astype(BF16), vwin).reshape(g, tq, HEAD_DIM)

    gates = g_ref[...]
    for h in range(g):
        sl = slice(h * HEAD_DIM, (h + 1) * HEAD_DIM)
        out = (gates[:, 3 * h:3 * h + 1] * oc_ref[:, sl].astype(F32)
               + gates[:, 3 * h + 1:3 * h + 2] * o_s[h] + gates[:, 3 * h + 2:3 * h + 3] * o_w[h])
        o_ref[:, sl] = out.astype(o_ref.dtype)


def _nsa_sel_win(qkv, sel, oc, gates, b, t, tq=128, tk=256):
    nq = t // tq
    ns = t // SEL_LEN
    gw = NSA_GROUP * HEAD_DIM
    kv_block = NSA_Q_W // HEAD_DIM

    def kv_spec(which):
        return pl.BlockSpec((t, HEAD_DIM), lambda i, k, j: (i, kv_block + which * NSA_KV_HEADS + k))

    return pl.pallas_call(
        functools.partial(_nsa_sel_win_kernel, tq=tq, tk=tk, t_len=t),
        grid=(b, NSA_KV_HEADS, nq),
        in_specs=[pl.BlockSpec((tq, gw), lambda i, k, j: (i * nq + j, k)),
                  kv_spec(2), kv_spec(3), kv_spec(4), kv_spec(5),
                  pl.BlockSpec((None, None, tq, ns), lambda i, k, j: (i, k, j, 0)),
                  pl.BlockSpec((tq, gw), lambda i, k, j: (i * nq + j, k)),
                  pl.BlockSpec((tq, LANES), lambda i, k, j: (i * nq + j, k))],
        out_specs=pl.BlockSpec((tq, gw), lambda i, k, j: (i * nq + j, k)),
        out_shape=jax.ShapeDtypeStruct((b * t, NSA_Q_W), BF16),
        compiler_params=_cp("parallel", "parallel", "parallel"),
        name="nsa_select_window",
    )(qkv, qkv, qkv, qkv, qkv, sel, oc, gates)


def _stickbreak_kernel(q_ref, k_ref, v_ref, o_ref, *, tq, tk):
    q0 = pl.program_id(2) * tq
    q = q_ref[...]
    t_row = q0 + lax.broadcasted_iota(jnp.int32, (tq, tk), 0)
    col = lax.broadcasted_iota(jnp.int32, (tq, tk), 1)
    tri = jnp.where(lax.broadcasted_iota(jnp.int32, (tk, tk), 0) > lax.broadcasted_iota(jnp.int32, (tk, tk), 1),
                    1.0, 0.0).astype(BF16)
    n_kt = (q0 + tq) // tk

    def body(i, carry):
        later, acc = carry
        k0 = pl.multiple_of((n_kt - 1 - i) * tk, tk)
        k = k_ref[pl.ds(k0, tk), :]
        v = v_ref[pl.ds(k0, tk), :]
        z = _nt_dot(q, k) * SCALE
        strict = col + k0 < t_row
        log_b = jnp.minimum(z, 0.0) - jnp.log(1.0 + jnp.exp(-jnp.abs(z)))
        log_1m = jnp.where(strict, log_b - z, 0.0)
        hi = log_1m.astype(BF16)
        lo = (log_1m - hi.astype(F32)).astype(BF16)
        suffix = _dot(hi, tri) + _dot(lo, tri) + later
        a = jnp.where(strict, jnp.exp(log_b + suffix), 0.0)
        acc = acc + _dot(a.astype(BF16), v)
        return later + jnp.sum(log_1m, axis=-1, keepdims=True), acc

    _, acc = lax.fori_loop(0, n_kt, body, (jnp.zeros((tq, 1), F32), jnp.zeros((tq, HEAD_DIM), F32)))
    o_ref[...] = acc.astype(o_ref.dtype)


def _stickbreak(qkv, b, t, tq=256, tk=128):
    nq = t // tq
    return pl.pallas_call(
        functools.partial(_stickbreak_kernel, tq=tq, tk=tk),
        grid=(b, SB_HEADS, nq),
        in_specs=[pl.BlockSpec((tq, HEAD_DIM), lambda i, h, j: (i * nq + j, h)),
                  pl.BlockSpec((t, HEAD_DIM), lambda i, h, j: (i, SB_HEADS + h)),
                  pl.BlockSpec((t, HEAD_DIM), lambda i, h, j: (i, 2 * SB_HEADS + h))],
        out_specs=pl.BlockSpec((tq, HEAD_DIM), lambda i, h, j: (i * nq + j, h)),
        out_shape=jax.ShapeDtypeStruct((b * t, SB_W), BF16),
        compiler_params=_cp("parallel", "parallel", "parallel"),
        name="stickbreak",
    )(qkv, qkv, qkv)


def _cross_kernel(x_ref, g_ref, wq_ref, kv_ref, wo_ref, o_ref):
    x = x_ref[...]
    h = _rms(x, g_ref[...]).astype(BF16)
    q = _dot(h, wq_ref[...]).astype(BF16)
    xw = X_HEADS * HEAD_DIM
    outs = []
    for hd in range(X_HEADS):
        sl = slice(hd * HEAD_DIM, (hd + 1) * HEAD_DIM)
        s = _nt_dot(q[:, sl], kv_ref[:, sl]) * SCALE
        e = jnp.exp(s - jnp.max(s, axis=-1, keepdims=True))
        p = e / jnp.sum(e, axis=-1, keepdims=True)
        outs.append(_dot(p.astype(BF16), kv_ref[:, xw + hd * HEAD_DIM:xw + (hd + 1) * HEAD_DIM]).astype(BF16))
    o_ref[...] = x + _dot(jnp.concatenate(outs, axis=1), wo_ref[...])


def _cross(x, g, wq, kv, wo, b, t, tq=512):
    n, d = x.shape
    nq = t // tq
    mem_len = kv.shape[0] // b
    return pl.pallas_call(
        _cross_kernel,
        grid=(b, nq),
        in_specs=[pl.BlockSpec((tq, d), lambda i, j: (i * nq + j, 0)),
                  pl.BlockSpec((1, d), lambda i, j: (0, 0)),
                  pl.BlockSpec(wq.shape, lambda i, j: (0, 0)),
                  pl.BlockSpec((mem_len, kv.shape[1]), lambda i, j: (i, 0)),
                  pl.BlockSpec(wo.shape, lambda i, j: (0, 0))],
        out_specs=pl.BlockSpec((tq, d), lambda i, j: (i * nq + j, 0)),
        out_shape=jax.ShapeDtypeStruct((n, d), F32),
        compiler_params=_cp("parallel", "parallel"),
        name="cross_attention",
    )(x, g.reshape(1, d), wq, kv, wo)


def _router_kernel(x_ref, g_ref, w_ref, b_ref, h_ref, r_ref):
    h = _rms(x_ref[...], g_ref[...])
    h_ref[...] = h
    logits = jnp.dot(h, w_ref[...], preferred_element_type=F32, precision=lax.Precision.HIGHEST) + b_ref[...]
    lane = lax.broadcasted_iota(jnp.int32, logits.shape, 1).astype(F32)
    big = float(LANES)
    is_g = lane < N_GROUPS
    lg = jnp.where(is_g, logits, NEG)
    mg = jnp.max(lg, axis=-1, keepdims=True)
    p_gtop = 1.0 / jnp.sum(jnp.where(is_g, jnp.exp(lg - mg), 0.0), axis=-1, keepdims=True)
    g_top = jnp.min(jnp.where(lg == mg, lane, big), axis=-1, keepdims=True)
    lo = N_GROUPS + EXPERTS_PER_GROUP * g_top
    in_grp = (lane >= lo) & (lane < lo + EXPERTS_PER_GROUP)
    le = jnp.where(in_grp, logits, NEG)
    ee = jnp.where(in_grp, jnp.exp(le - jnp.max(le, axis=-1, keepdims=True)), 0.0)
    p_in = ee / jnp.sum(ee, axis=-1, keepdims=True)
    p1 = jnp.where(in_grp, p_in, -1.0)
    m1 = jnp.max(p1, axis=-1, keepdims=True)
    i1 = jnp.min(jnp.where(p1 == m1, lane, big), axis=-1, keepdims=True)
    p2 = jnp.where(lane == i1, -1.0, p1)
    m2 = jnp.max(p2, axis=-1, keepdims=True)
    i2 = jnp.min(jnp.where(p2 == m2, lane, big), axis=-1, keepdims=True)
    tot = m1 + m2
    r_ref[...] = jnp.where(lane == 0, i1 - N_GROUPS,
                           jnp.where(lane == 1, i2 - N_GROUPS,
                                     jnp.where(lane == 2, m1 / tot * p_gtop,
                                               jnp.where(lane == 3, m2 / tot * p_gtop, 0.0))))


def _router(x, g, w_r, b_r, tm=512):
    n, d = x.shape
    return pl.pallas_call(
        _router_kernel,
        grid=(n // tm,),
        in_specs=[pl.BlockSpec((tm, d), lambda i: (i, 0)), pl.BlockSpec((1, d), lambda i: (0, 0)),
                  pl.BlockSpec((d, LANES), lambda i: (0, 0)), pl.BlockSpec((1, LANES), lambda i: (0, 0))],
        out_specs=[pl.BlockSpec((tm, d), lambda i: (i, 0)), pl.BlockSpec((tm, LANES), lambda i: (i, 0))],
        out_shape=[jax.ShapeDtypeStruct((n, d), F32), jax.ShapeDtypeStruct((n, LANES), F32)],
        compiler_params=_cp("parallel"),
        name="moe_router",
    )(x, g.reshape(1, d), w_r, b_r)


def _row_copy(src_hbm, dst, sem, src_row, dst_row):
    return pltpu.make_async_copy(src_hbm.at[pl.ds(src_row, 1), :], dst.at[pl.ds(dst_row, 1), :], sem)


def _gather_kernel(idx_ref, src_hbm, o_ref, sem, *, tm):
    base = pl.program_id(0) * tm

    def issue(r, c):
        _row_copy(src_hbm, o_ref, sem, idx_ref[base + r], r).start()
        return c

    def drain(r, c):
        _row_copy(src_hbm, o_ref, sem, 0, r).wait()
        return c

    lax.fori_loop(0, tm, issue, 0)
    lax.fori_loop(0, tm, drain, 0)


def _gather_rows(src, idx, tm=EXPERT_TILE):
    rows = idx.shape[0]
    d = src.shape[1]
    return pl.pallas_call(
        functools.partial(_gather_kernel, tm=tm),
        grid_spec=pltpu.PrefetchScalarGridSpec(
            num_scalar_prefetch=1,
            grid=(rows // tm,),
            in_specs=[pl.BlockSpec(memory_space=pl.ANY)],
            out_specs=pl.BlockSpec((tm, d), lambda i, idx: (i, 0)),
            scratch_shapes=[pltpu.SemaphoreType.DMA(())]),
        out_shape=jax.ShapeDtypeStruct((rows, d), src.dtype),
        compiler_params=_cp("arbitrary"),
        name="moe_gather",
    )(idx, src)


def _expert_kernel(te_ref, nt_ref, x_ref, rw_ref, wg_ref, wu_ref, wd_ref, o_ref, wg_s, wu_s, wd_s):
    i = pl.program_id(0)
    live = i < nt_ref[0]
    fresh = (i == 0) | (te_ref[i] != te_ref[jnp.maximum(i - 1, 0)])

    @pl.when(live & fresh)
    def _():
        wg_s[...] = wg_ref[...].astype(BF16)
        wu_s[...] = wu_ref[...].astype(BF16)
        wd_s[...] = wd_ref[...].astype(BF16)

    @pl.when(live)
    def _():
        x = x_ref[...].astype(BF16)
        a = _dot(x, wg_s[...])
        u = _dot(x, wu_s[...])
        act = a * (1.0 / (1.0 + jnp.exp(-a))) * u * rw_ref[...]
        o_ref[...] = _dot(act.astype(BF16), wd_s[...])

    @pl.when(jnp.logical_not(live))
    def _():
        o_ref[...] = jnp.zeros_like(o_ref)


def _experts(xs, row_w, w_g, w_u, w_d, tile_expert, n_tiles, tm=EXPERT_TILE):
    rows, d = xs.shape
    f = w_g.shape[2]
    return pl.pallas_call(
        _expert_kernel,
        grid_spec=pltpu.PrefetchScalarGridSpec(
            num_scalar_prefetch=2,
            grid=(rows // tm,),
            in_specs=[pl.BlockSpec((tm, d), lambda i, te, nt: (i, 0)),
                      pl.BlockSpec((tm, 1), lambda i, te, nt: (i, 0)),
                      pl.BlockSpec((None, d, f), lambda i, te, nt: (te[i], 0, 0)),
                      pl.BlockSpec((None, d, f), lambda i, te, nt: (te[i], 0, 0)),
                      pl.BlockSpec((None, f, d), lambda i, te, nt: (te[i], 0, 0))],
            out_specs=pl.BlockSpec((tm, d), lambda i, te, nt: (i, 0)),
            scratch_shapes=[pltpu.VMEM((d, f), BF16), pltpu.VMEM((d, f), BF16), pltpu.VMEM((f, d), BF16)]),
        out_shape=jax.ShapeDtypeStruct((rows, d), F32),
        compiler_params=_cp("arbitrary"),
        name="moe_experts",
    )(tile_expert, n_tiles, xs, row_w, w_g, w_u, w_d)


def _combine_kernel(p0_ref, p1_ref, ys_hbm, x_ref, g_ref, o_ref, buf0, buf1, sem, *, tm):
    base = pl.program_id(0) * tm

    def issue(r, c):
        _row_copy(ys_hbm, buf0, sem, p0_ref[base + r], r).start()
        _row_copy(ys_hbm, buf1, sem, p1_ref[base + r], r).start()
        return c

    def drain(r, c):
        _row_copy(ys_hbm, buf0, sem, 0, r).wait()
        _row_copy(ys_hbm, buf1, sem, 0, r).wait()
        return c

    lax.fori_loop(0, tm, issue, 0)
    lax.fori_loop(0, tm, drain, 0)
    o_ref[...] = _rms(x_ref[...] + buf0[...] + buf1[...], g_ref[...])


def _combine(ys, pos0, pos1, x, g, tm=256):
    n, d = x.shape
    return pl.pallas_call(
        functools.partial(_combine_kernel, tm=tm),
        grid_spec=pltpu.PrefetchScalarGridSpec(
            num_scalar_prefetch=2,
            grid=(n // tm,),
            in_specs=[pl.BlockSpec(memory_space=pl.ANY),
                      pl.BlockSpec((tm, d), lambda i, p0, p1: (i, 0)),
                      pl.BlockSpec((1, d), lambda i, p0, p1: (0, 0))],
            out_specs=pl.BlockSpec((tm, d), lambda i, p0, p1: (i, 0)),
            scratch_shapes=[pltpu.VMEM((tm, d), F32), pltpu.VMEM((tm, d), F32), pltpu.SemaphoreType.DMA(())]),
        out_shape=jax.ShapeDtypeStruct((n, d), F32),
        compiler_params=_cp("arbitrary"),
        name="moe_combine_norm",
    )(pos0, pos1, ys, x, g.reshape(1, d))


def _dispatch_plan(e1, e2, w1, w2, tm=EXPERT_TILE):
    n = e1.shape[0]
    n_tiles_max = (2 * n) // tm + N_EXPERTS
    rows = n_tiles_max * tm
    e_flat = jnp.concatenate([e1, e2])
    w_flat = jnp.concatenate([w1, w2])
    tok = jnp.concatenate([jnp.arange(n, dtype=jnp.int32)] * 2)
    onehot = (e_flat[:, None] == jnp.arange(N_EXPERTS, dtype=jnp.int32)[None, :]).astype(jnp.int32)
    csum = jnp.cumsum(onehot, axis=0)
    count = csum[-1]
    rank = jnp.sum((csum - 1) * onehot, axis=1)
    tiles_per = (count + tm - 1) // tm
    tile_end = jnp.cumsum(tiles_per)
    row_start = (tile_end - tiles_per) * tm
    dest = (row_start[e_flat] + rank).astype(jnp.int32)
    tok_of_row = jnp.zeros((rows,), jnp.int32).at[dest].set(tok)
    w_of_row = jnp.zeros((rows,), F32).at[dest].set(w_flat)
    n_tiles = tile_end[-1].astype(jnp.int32)
    tile_id = jnp.minimum(jnp.arange(n_tiles_max, dtype=jnp.int32), n_tiles - 1)
    tile_expert = jnp.sum((tile_id[:, None] >= tile_end[None, :]).astype(jnp.int32), axis=1).astype(jnp.int32)
    return tok_of_row, w_of_row.reshape(rows, 1), tile_expert, n_tiles.reshape(1), dest[:n], dest[n:]


def _rope_tables(positions):
    half = ROT_DIM // 2
    inv = ROPE_THETA ** (-jnp.arange(half, dtype=F32) * (2.0 / ROT_DIM))
    ang = positions.astype(F32).reshape(-1, 1) * inv[None, :]
    cos, sin = jnp.cos(ang), jnp.sin(ang)
    n = ang.shape[0]
    c = jnp.concatenate([cos, cos, jnp.ones((n, LANES - ROT_DIM), F32)], axis=1)
    s1 = jnp.concatenate([-sin, jnp.zeros((n, LANES - half), F32)], axis=1)
    s2 = jnp.concatenate([jnp.zeros((n, half), F32), sin, jnp.zeros((n, LANES - ROT_DIM), F32)], axis=1)
    return c, s1, s2


def _layer(x, mem, positions, g_mix, w_in, cmp_pe_k, cmp_w1_k, cmp_w2_k, cmp_pe_v, cmp_w1_v, cmp_w2_v,
           w_br_nsa, w_br_sb, w_o, g_cross, g_mem, w_cq, w_ck, w_cv, w_co,
           g_moe, w_rg, b_rg, w_re, b_re, w_eg, w_eu, w_ed):
    b, t, d = x.shape
    n = b * t
    xf = x.reshape(n, d)

    g0 = NSA_W
    w_main = jnp.concatenate([w_in[:, :g0], w_in[:, g0 + NSA_GATES:]], axis=1).astype(BF16)
    w_g = w_in[:, g0:g0 + NSA_GATES].reshape(d, NSA_KV_HEADS, 3 * NSA_GROUP)
    w_g = jnp.pad(w_g, ((0, 0), (0, 0), (0, LANES - 3 * NSA_GROUP))).reshape(d, NSA_KV_HEADS * LANES).astype(BF16)
    h = _rmsnorm(xf, g_mix, BF16)
    c, s1, s2 = _rope_tables(positions)
    row_spec = pl.BlockSpec((1024, LANES), lambda i, j: (i, 0))
    qkv_nsa = _mm(functools.partial(_mm_nsa_kernel, tn=256), h, w_main, NSA_W, 0, 1024, 256, BF16,
                  extra=(c, s1, s2), extra_specs=(row_spec, row_spec, row_spec), name="in_proj_nsa")
    qkv_sb = _mm(_mm_plain_kernel, h, w_main, 3 * SB_W, NSA_W // 512, 1024, 512, BF16, name="in_proj_sb")
    gates = _mm(_mm_sigmoid_kernel, h, w_main, 2 * d, (NSA_W + 3 * SB_W) // 512, 1024, 512, BF16,
                name="in_proj_gates")
    nsa_gates = _mm(_mm_sigmoid_kernel, h, w_g, NSA_KV_HEADS * LANES, 0, 1024, NSA_KV_HEADS * LANES, F32,
                    name="in_proj_nsa_gates")

    def chunked(col0):
        tok = qkv_nsa[:, col0:col0 + NSA_KV_W].reshape(b, t, NSA_KV_HEADS, HEAD_DIM).transpose(0, 2, 1, 3)
        return tok.reshape(b, NSA_KV_HEADS, t // CMP_STRIDE, CMP_STRIDE * HEAD_DIM)

    kc = _compress(chunked(NSA_Q_W), cmp_pe_k, cmp_w1_k, cmp_w2_k)
    vc = _compress(chunked(NSA_Q_W + NSA_KV_W), cmp_pe_v, cmp_w1_v, cmp_w2_v)
    o_c, sel = _nsa_cmp(qkv_nsa, kc, vc, b, t)
    o_nsa = _nsa_sel_win(qkv_nsa, sel, o_c, nsa_gates, b, t)
    o_sb = _stickbreak(qkv_sb, b, t)
    merged = _merge(o_nsa, o_sb, w_br_nsa.astype(BF16), w_br_sb.astype(BF16), gates)
    x1 = _mm(_mm_resid_kernel, merged, w_o.astype(BF16), d, 0, 1024, 512, F32,
             extra=(xf,), extra_specs=(pl.BlockSpec((1024, 512), lambda i, j: (i, j)),), name="out_proj")

    w_kv = jnp.concatenate([w_ck, w_cv], axis=1).astype(BF16)
    kv = _norm_mm(mem.reshape(-1, d), g_mem, w_kv, BF16)
    x2 = _cross(x1, g_cross, w_cq.astype(BF16), kv, w_co.astype(BF16), b, t)

    w_r = jnp.pad(jnp.concatenate([w_rg, w_re], axis=1), ((0, 0), (0, LANES - N_GROUPS - N_EXPERTS)))
    b_r = jnp.pad(jnp.concatenate([b_rg, b_re]), (0, LANES - N_GROUPS - N_EXPERTS)).reshape(1, LANES)
    hm, route = _router(x2, g_moe, w_r, b_r)
    e1 = route[:, 0].astype(jnp.int32)
    e2 = route[:, 1].astype(jnp.int32)
    tok_of_row, row_w, tile_expert, n_tiles, pos0, pos1 = _dispatch_plan(e1, e2, route[:, 2], route[:, 3])
    xs = _gather_rows(hm, tok_of_row)
    f = w_eg.shape[-1]
    ys = _experts(xs, row_w, w_eg.reshape(N_EXPERTS, d, f), w_eu.reshape(N_EXPERTS, d, f),
                  w_ed.reshape(N_EXPERTS, f, d), tile_expert, n_tiles)
    return ys, pos0, pos1, x2


def kernel(x, mem, positions, g_mix, w_in, cmp_pe_k, cmp_w1_k, cmp_w2_k, cmp_pe_v, cmp_w1_v, cmp_w2_v, w_br_nsa, w_br_sb, w_o, g_cross, g_mem, w_cq, w_ck, w_cv, w_co, g_moe, w_rg, b_rg, w_re, b_re, w_eg, w_eu, w_ed, g_final):
    b, t, d = x.shape
    assert w_in.shape[0] == 1, "one layer"
    ys, pos0, pos1, x2 = _layer(
        x, mem, positions, g_mix[0], w_in[0], cmp_pe_k[0], cmp_w1_k[0], cmp_w2_k[0], cmp_pe_v[0], cmp_w1_v[0],
        cmp_w2_v[0], w_br_nsa[0], w_br_sb[0], w_o[0], g_cross[0], g_mem[0], w_cq[0], w_ck[0], w_cv[0], w_co[0],
        g_moe[0], w_rg[0], b_rg[0], w_re[0], b_re[0], w_eg[0], w_eu[0], w_ed[0])
    return _combine(ys, pos0, pos1, x2, g_final).reshape(b, t, d)
```

```python
import functools

import jax
import jax.numpy as jnp
from jax import lax
from jax.experimental import pallas as pl
from jax.experimental.pallas import tpu as pltpu

F32 = jnp.float32
BF16 = jnp.bfloat16

HEAD_DIM = 128
NSA_HEADS = 8
NSA_KV_HEADS = 2
NSA_GROUP = NSA_HEADS // NSA_KV_HEADS
SB_HEADS = 8
ROT_DIM = HEAD_DIM // 4
ROPE_THETA = 500000.0
CMP_LEN = 32
CMP_STRIDE = 16
SEL_LEN = 64
SEL_SHIFT = 6
SEL_TOPN = 16
WINDOW = 512
FORCE_BONUS = 1000.0
X_HEADS = 4
N_GROUPS = 4
EXPERTS_PER_GROUP = 8
N_EXPERTS = N_GROUPS * EXPERTS_PER_GROUP
EPS = 1e-6
SCALE = HEAD_DIM ** -0.5
NEG = -1e30

NSA_Q_W = NSA_HEADS * HEAD_DIM
NSA_KV_W = NSA_KV_HEADS * HEAD_DIM
NSA_W = NSA_Q_W + 6 * NSA_KV_W
NSA_GATES = 3 * NSA_HEADS
SB_W = SB_HEADS * HEAD_DIM
LANES = 128
VMEM_LIMIT = 56 * 1024 * 1024

EXPERT_TILE = 256


def _cp(*sem, vmem=VMEM_LIMIT):
    return pltpu.CompilerParams(dimension_semantics=sem, vmem_limit_bytes=vmem)


def _nt_dot(a, b):
    return lax.dot_general(a, b, (((1,), (1,)), ((), ())), preferred_element_type=F32)


def _dot(a, b):
    return jnp.dot(a, b, preferred_element_type=F32)


def _split3(x):
    hi = x.astype(BF16)
    r1 = x - hi.astype(F32)
    mid = r1.astype(BF16)
    lo = (r1 - mid.astype(F32)).astype(BF16)
    return hi, mid, lo


def _rms(x, g):
    return x * lax.rsqrt(jnp.mean(x * x, axis=-1, keepdims=True) + EPS) * g


def _rmsnorm_kernel(x_ref, g_ref, o_ref):
    o_ref[...] = _rms(x_ref[...].astype(F32), g_ref[...]).astype(o_ref.dtype)


def _rmsnorm(x, g, out_dtype, tm=512):
    n, d = x.shape
    return pl.pallas_call(
        _rmsnorm_kernel,
        grid=(n // tm,),
        in_specs=[pl.BlockSpec((tm, d), lambda i: (i, 0)), pl.BlockSpec((1, d), lambda i: (0, 0))],
        out_specs=pl.BlockSpec((tm, d), lambda i: (i, 0)),
        out_shape=jax.ShapeDtypeStruct((n, d), out_dtype),
        compiler_params=_cp("parallel"),
        name="rmsnorm",
    )(x, g.reshape(1, d))


def _rope_half(x, c, s1, s2):
    return x * c + pltpu.roll(x, LANES - ROT_DIM // 2, 1) * s1 + pltpu.roll(x, ROT_DIM // 2, 1) * s2


def _mm_nsa_kernel(a_ref, w_ref, c_ref, s1_ref, s2_ref, o_ref, *, tn):
    j = pl.program_id(1)
    acc = _dot(a_ref[...], w_ref[...])
    is_rope = (j < 5) | (j == 6) | (j == 8)

    @pl.when(is_rope)
    def _():
        qs = jnp.where(j < NSA_Q_W // tn, SCALE, 1.0)
        c, s1, s2 = c_ref[...] * qs, s1_ref[...] * qs, s2_ref[...] * qs
        for hh in range(tn // LANES):
            sl = slice(hh * LANES, (hh + 1) * LANES)
            o_ref[:, sl] = _rope_half(acc[:, sl], c, s1, s2).astype(o_ref.dtype)

    @pl.when(jnp.logical_not(is_rope))
    def _():
        o_ref[...] = acc.astype(o_ref.dtype)


def _mm_sb_kernel(a_ref, w_ref, o_ref, *, q_tiles):
    acc = _dot(a_ref[...], w_ref[...])
    is_q = pl.program_id(1) < q_tiles
    o_ref[...] = (acc * jnp.where(is_q, SCALE, 1.0)).astype(o_ref.dtype)


def _mm_sigmoid_kernel(a_ref, w_ref, o_ref):
    o_ref[...] = jax.nn.sigmoid(_dot(a_ref[...], w_ref[...])).astype(o_ref.dtype)


def _mm_resid_kernel(a_ref, w_ref, x_ref, o_ref):
    o_ref[...] = x_ref[...] + _dot(a_ref[...], w_ref[...])


def _mm(kern, a, w, n_cols, col_block0, tm, tn, out_dtype, extra=(), extra_specs=(), name="mm"):
    m, k = a.shape
    return pl.pallas_call(
        kern,
        grid=(m // tm, n_cols // tn),
        in_specs=[pl.BlockSpec((tm, k), lambda i, j: (i, 0)),
                  pl.BlockSpec((k, tn), lambda i, j: (0, col_block0 + j)), *extra_specs],
        out_specs=pl.BlockSpec((tm, tn), lambda i, j: (i, j)),
        out_shape=jax.ShapeDtypeStruct((m, n_cols), out_dtype),
        compiler_params=_cp("parallel", "arbitrary"),
        name=name,
    )(a, w, *extra)


def _merge_kernel(a1_ref, a2_ref, w1_ref, w2_ref, ga_ref, gb_ref, o_ref):
    ya = _dot(a1_ref[...], w1_ref[...])
    yb = _dot(a2_ref[...], w2_ref[...])
    o_ref[...] = (ga_ref[...].astype(F32) * ya + gb_ref[...].astype(F32) * yb).astype(o_ref.dtype)


def _merge(o_nsa, o_sb, w1, w2, gates, tm=1024, tn=512):
    n, k = o_nsa.shape
    d = w1.shape[1]
    nb = d // tn
    return pl.pallas_call(
        _merge_kernel,
        grid=(n // tm, nb),
        in_specs=[pl.BlockSpec((tm, k), lambda i, j: (i, 0)), pl.BlockSpec((tm, k), lambda i, j: (i, 0)),
                  pl.BlockSpec((k, tn), lambda i, j: (0, j)), pl.BlockSpec((k, tn), lambda i, j: (0, j)),
                  pl.BlockSpec((tm, tn), lambda i, j: (i, j)), pl.BlockSpec((tm, tn), lambda i, j: (i, nb + j))],
        out_specs=pl.BlockSpec((tm, tn), lambda i, j: (i, j)),
        out_shape=jax.ShapeDtypeStruct((n, d), BF16),
        compiler_params=_cp("parallel", "arbitrary"),
        name="merge",
    )(o_nsa, o_sb, w1, w2, gates, gates)


def _norm_mm_kernel(x_ref, g_ref, w_ref, o_ref):
    h = _rms(x_ref[...].astype(F32), g_ref[...]).astype(BF16)
    o_ref[...] = _dot(h, w_ref[...]).astype(o_ref.dtype)


def _norm_mm(x, g, w, out_dtype, tm=256):
    n, d = x.shape
    nc = w.shape[1]
    return pl.pallas_call(
        _norm_mm_kernel,
        grid=(n // tm,),
        in_specs=[pl.BlockSpec((tm, d), lambda i: (i, 0)), pl.BlockSpec((1, d), lambda i: (0, 0)),
                  pl.BlockSpec((d, nc), lambda i: (0, 0))],
        out_specs=pl.BlockSpec((tm, nc), lambda i: (i, 0)),
        out_shape=jax.ShapeDtypeStruct((n, nc), out_dtype),
        compiler_params=_cp("parallel"),
        name="norm_mm",
    )(x, g.reshape(1, d), w)


def _compress_kernel(x_ref, pe_ref, w1_ref, w2_ref, o_ref):
    x = x_ref[...].astype(F32)
    half = x.shape[1]
    xa = (x + pe_ref[0:1, :]).astype(BF16)
    xb = (x + pe_ref[1:2, :]).astype(BF16)
    y1 = _dot(xa, w1_ref[0:half, :].astype(BF16))
    y2 = _dot(xb, w1_ref[half:2 * half, :].astype(BF16))
    pre = y1 + pltpu.roll(y2, x.shape[0] - 1, 0)
    act = pre * (1.0 / (1.0 + jnp.exp(-pre)))
    o_ref[...] = _dot(act.astype(BF16), w2_ref[...].astype(BF16)).astype(o_ref.dtype)


def _compress(tok, pe, w1, w2):
    b, hkv, nchunk, width = tok.shape
    return pl.pallas_call(
        _compress_kernel,
        grid=(b, hkv),
        in_specs=[pl.BlockSpec((None, None, nchunk, width), lambda i, j: (i, j, 0, 0)),
                  pl.BlockSpec((2, width), lambda i, j: (0, 0)),
                  pl.BlockSpec((2 * width, HEAD_DIM), lambda i, j: (0, 0)),
                  pl.BlockSpec((HEAD_DIM, HEAD_DIM), lambda i, j: (0, 0))],
        out_specs=pl.BlockSpec((None, None, nchunk, HEAD_DIM), lambda i, j: (i, j, 0, 0)),
        out_shape=jax.ShapeDtypeStruct((b, hkv, nchunk, HEAD_DIM), BF16),
        compiler_params=_cp("parallel", "parallel"),
        name="nsa_compress",
    )(tok, pe.reshape(2, width), w1, w2)


def _nsa_cmp_kernel(q_ref, kc_ref, vc_ref, oc_ref, sel_ref, *, tq, nc, ns):
    q0 = pl.program_id(2) * tq
    kc = kc_ref[...]
    vc = vc_ref[...]
    t_row = q0 + lax.broadcasted_iota(jnp.int32, (tq, nc), 0)
    c_idx = lax.broadcasted_iota(jnp.int32, (tq, nc), 1)
    vis = (c_idx * CMP_STRIDE + (CMP_LEN - 1) <= t_row) & (c_idx < nc - 1)
    imp = jnp.zeros((tq, nc), F32)
    for h in range(NSA_GROUP):
        sl = slice(h * HEAD_DIM, (h + 1) * HEAD_DIM)
        s = jnp.where(vis, _nt_dot(q_ref[:, sl], kc), NEG)
        m = jnp.max(s, axis=-1, keepdims=True)
        e = jnp.where(vis, jnp.exp(s - m), 0.0)
        den = jnp.sum(e, axis=-1, keepdims=True)
        p = e / jnp.where(den > 0, den, 1.0)
        oc_ref[:, sl] = _dot(p.astype(BF16), vc).astype(oc_ref.dtype)
        imp = imp + p
    jj = lax.broadcasted_iota(jnp.int32, (ns, nc), 0)
    cc = lax.broadcasted_iota(jnp.int32, (ns, nc), 1) * CMP_STRIDE
    ov = jnp.where((cc < (jj + 1) * SEL_LEN) & (cc + CMP_LEN > jj * SEL_LEN), 1.0, 0.0).astype(BF16)
    imp_t = sum(_nt_dot(ov, part) for part in _split3(imp))
    jb = lax.broadcasted_iota(jnp.int32, (ns, tq), 0)
    tt = q0 + lax.broadcasted_iota(jnp.int32, (ns, tq), 1)
    cur = lax.shift_right_logical(tt, jnp.full_like(tt, SEL_SHIFT))
    forced = (jb == 0) | (jb == cur) | (jb == cur - 1)
    valid = jb * SEL_LEN <= tt
    score = jnp.where(valid, imp_t + jnp.where(forced, FORCE_BONUS, 0.0), NEG)
    rank = jnp.zeros((ns, tq), F32)
    for i in range(ns):
        row = score[i:i + 1, :]
        rank = rank + jnp.where(row > score, 1.0, jnp.where((row == score) & (jb > i), 1.0, 0.0))
    sel_t = jnp.where(valid & (rank < float(min(SEL_TOPN, ns))), 1.0, 0.0).astype(BF16)
    eye = jnp.where(lax.broadcasted_iota(jnp.int32, (tq, tq), 0) == lax.broadcasted_iota(jnp.int32, (tq, tq), 1),
                    1.0, 0.0).astype(BF16)
    sel_ref[...] = _nt_dot(eye, sel_t)


def _nsa_cmp(qkv, kc, vc, b, t, tq=256):
    nc = kc.shape[2]
    ns = t // SEL_LEN
    nq = t // tq
    gw = NSA_GROUP * HEAD_DIM
    return pl.pallas_call(
        functools.partial(_nsa_cmp_kernel, tq=tq, nc=nc, ns=ns),
        grid=(b, NSA_KV_HEADS, nq),
        in_specs=[pl.BlockSpec((tq, gw), lambda i, k, j: (i * nq + j, k)),
                  pl.BlockSpec((None, None, nc, HEAD_DIM), lambda i, k, j: (i, k, 0, 0)),
                  pl.BlockSpec((None, None, nc, HEAD_DIM), lambda i, k, j: (i, k, 0, 0))],
        out_specs=[pl.BlockSpec((tq, gw), lambda i, k, j: (i * nq + j, k)),
                   pl.BlockSpec((None, None, tq, ns), lambda i, k, j: (i, k, j, 0))],
        out_shape=[jax.ShapeDtypeStruct((b * t, NSA_Q_W), BF16),
                   jax.ShapeDtypeStruct((b, NSA_KV_HEADS, t, ns), F32)],
        compiler_params=_cp("parallel", "parallel", "parallel"),
        name="nsa_cmp_select",
    )(qkv, kc, vc)


def _flash_update(s, v, bias, state, g, tq):
    m, l, acc = state
    tk = v.shape[0]
    if bias is not None:
        s = (s.reshape(g, tq, tk) + bias[None]).reshape(g * tq, tk)
    m_new = jnp.maximum(m, jnp.max(s, axis=-1, keepdims=True))
    alpha = jnp.exp(m - m_new)
    p = jnp.exp(s - m_new)
    l = alpha * l + jnp.sum(p, axis=-1, keepdims=True)
    return m_new, l, alpha * acc + _dot(p.astype(BF16), v)


def _nsa_sel_win_kernel(q_ref, ks_ref, vs_ref, kw_ref, vw_ref, sel_ref, oc_ref, g_ref, o_ref, *, tq, tk, stack):
    q0 = pl.program_id(1) * tq
    g = NSA_GROUP
    nkv = NSA_KV_HEADS
    ns = sel_ref.shape[-1]
    per_tile = tk // SEL_LEN
    units = [(kv, h0) for kv in range(nkv) for h0 in range(0, g, stack)]
    kv_lanes = [slice(kv * HEAD_DIM, (kv + 1) * HEAD_DIM) for kv in range(nkv)]

    def q_lanes(kv, h):
        return slice((kv * g + h) * HEAD_DIM, (kv * g + h + 1) * HEAD_DIM)

    qs = [jnp.concatenate([q_ref[:, q_lanes(kv, h0 + j)] for j in range(stack)], axis=0) for kv, h0 in units]
    sel_bias = [((sel_ref[kv] - 1.0) * (-NEG)).astype(BF16) for kv in range(nkv)]
    ej = lax.broadcasted_iota(jnp.int32, (ns, tk), 0)
    eb = lax.broadcasted_iota(jnp.int32, (ns, tk), 1)
    eb = lax.shift_right_logical(eb, jnp.full_like(eb, SEL_SHIFT))

    def init():
        return tuple((jnp.full((stack * tq, 1), NEG, F32), jnp.zeros((stack * tq, 1), F32),
                      jnp.zeros((stack * tq, HEAD_DIM), F32)) for _ in units)

    def flash_all(k_ref, v_ref, k0, width, biases, states):
        scores = [_nt_dot(qs[u], k_ref[pl.ds(k0, width), kv_lanes[kv]]) for u, (kv, _) in enumerate(units)]
        return tuple(_flash_update(scores[u], v_ref[pl.ds(k0, width), kv_lanes[kv]], biases[kv], states[u], stack, tq)
                     for u, (kv, _) in enumerate(units))

    def sel_step(kt, states, causal):
        k0 = pl.multiple_of(kt * tk, tk)
        expand = jnp.where(ej == eb + kt * per_tile, 1.0, 0.0).astype(BF16)
        biases = [_dot(sel_bias[kv], expand) for kv in range(nkv)]
        if causal is not None:
            biases = [bias + causal for bias in biases]
        return flash_all(ks_ref, vs_ref, k0, tk, biases, states)

    kt_d = q0 // tk
    t_row = q0 + lax.broadcasted_iota(jnp.int32, (tq, tk), 0)
    col = kt_d * tk + lax.broadcasted_iota(jnp.int32, (tq, tk), 1)
    states = sel_step(kt_d, init(), jnp.where(col <= t_row, 0.0, NEG))
    states = lax.fori_loop(0, kt_d, lambda kt, st: sel_step(kt, st, None), states)
    o_s = [acc * (1.0 / l) for (_, l, acc) in states]

    def win_step(k0, states, bias):
        return flash_all(kw_ref, vw_ref, k0, tq, [bias] * nkv, states)

    ri = lax.broadcasted_iota(jnp.int32, (tq, tq), 0)
    ci = lax.broadcasted_iota(jnp.int32, (tq, tq), 1)
    states = win_step(pl.multiple_of(q0, tq), init(), jnp.where(ci <= ri, 0.0, NEG))
    n_full = jnp.minimum(WINDOW // tq - 1, q0 // tq)
    states = lax.fori_loop(0, n_full, lambda i, st: win_step(pl.multiple_of(q0 - (i + 1) * tq, tq), st, None),
                           states)
    edge_bias = jnp.where(ci > ri, 0.0, NEG) + jnp.where(q0 >= WINDOW, 0.0, NEG)
    states = win_step(pl.multiple_of(jnp.maximum(q0 - WINDOW, 0), tq), states, edge_bias)
    o_w = [acc * (1.0 / l) for (_, l, acc) in states]

    for u, (kv, h0) in enumerate(units):
        gates = g_ref[:, kv * LANES:(kv + 1) * LANES]
        for j in range(stack):
            h = h0 + j
            sl = q_lanes(kv, h)
            rows = slice(j * tq, (j + 1) * tq)
            out = (gates[:, 3 * h:3 * h + 1] * oc_ref[:, sl].astype(F32)
                   + gates[:, 3 * h + 1:3 * h + 2] * o_s[u][rows] + gates[:, 3 * h + 2:3 * h + 3] * o_w[u][rows])
            o_ref[:, sl] = out.astype(o_ref.dtype)


def _nsa_sel_win(qkv, sel, oc, gates, b, t, tq=128, tk=256, stack=4):
    nq = t // tq
    ns = t // SEL_LEN
    kv_block = NSA_Q_W // NSA_KV_W

    def kv_spec(which):
        return pl.BlockSpec((t, NSA_KV_W), lambda i, j: (i, kv_block + which))

    return pl.pallas_call(
        functools.partial(_nsa_sel_win_kernel, tq=tq, tk=tk, stack=stack),
        grid=(b, nq),
        in_specs=[pl.BlockSpec((tq, NSA_Q_W), lambda i, j: (i * nq + j, 0)),
                  kv_spec(2), kv_spec(3), kv_spec(4), kv_spec(5),
                  pl.BlockSpec((None, NSA_KV_HEADS, tq, ns), lambda i, j: (i, 0, j, 0)),
                  pl.BlockSpec((tq, NSA_Q_W), lambda i, j: (i * nq + j, 0)),
                  pl.BlockSpec((tq, NSA_KV_HEADS * LANES), lambda i, j: (i * nq + j, 0))],
        out_specs=pl.BlockSpec((tq, NSA_Q_W), lambda i, j: (i * nq + j, 0)),
        out_shape=jax.ShapeDtypeStruct((b * t, NSA_Q_W), BF16),
        compiler_params=_cp("parallel", "parallel"),
        name="nsa_select_window",
    )(qkv, qkv, qkv, qkv, qkv, sel, oc, gates)


def _stickbreak_kernel(q_ref, k_ref, v_ref, o_ref, *, tile, nh):
    q0 = pl.program_id(2) * tile
    ri = lax.broadcasted_iota(jnp.int32, (tile, tile), 0)
    ci = lax.broadcasted_iota(jnp.int32, (tile, tile), 1)
    strict = ci < ri
    tri = jnp.where(ri > ci, 1.0, 0.0).astype(BF16)
    lanes = [slice(h * HEAD_DIM, (h + 1) * HEAD_DIM) for h in range(nh)]

    def step(k0, diag, carry):
        zs = [_nt_dot(q_ref[:, lanes[h]], k_ref[pl.ds(k0, tile), lanes[h]]) for h in range(nh)]
        log_bs, sufs, sums = [], [], []
        for z in zs:
            log_b = jnp.minimum(z, 0.0) - jnp.log(1.0 + jnp.exp(-jnp.abs(z)))
            log_1m = log_b - z
            if diag:
                log_1m = jnp.where(strict, log_1m, 0.0)
            hi = log_1m.astype(BF16)
            lo = (log_1m - hi.astype(F32)).astype(BF16)
            log_bs.append(log_b)
            sufs.append(_dot(hi, tri) + _dot(lo, tri))
            sums.append(jnp.sum(log_1m, axis=-1, keepdims=True))
        out = []
        for h in range(nh):
            later, acc = carry[2 * h], carry[2 * h + 1]
            a = jnp.exp(log_bs[h] + (sufs[h] + later))
            if diag:
                a = jnp.where(strict, a, 0.0)
            out.extend((later + sums[h], acc + _dot(a.astype(BF16), v_ref[pl.ds(k0, tile), lanes[h]])))
        return tuple(out)

    zero = (jnp.zeros((tile, 1), F32), jnp.zeros((tile, HEAD_DIM), F32)) * nh
    carry = step(pl.multiple_of(q0, tile), True, zero)
    carry = lax.fori_loop(0, q0 // tile,
                          lambda i, c: step(pl.multiple_of(q0 - (i + 1) * tile, tile), False, c), carry)
    for h in range(nh):
        o_ref[:, lanes[h]] = carry[2 * h + 1].astype(o_ref.dtype)


def _stickbreak(qkv, b, t, tile=256, nh=4):
    nq = t // tile
    w = nh * HEAD_DIM
    nhb = SB_HEADS // nh
    return pl.pallas_call(
        functools.partial(_stickbreak_kernel, tile=tile, nh=nh),
        grid=(b, nhb, nq),
        in_specs=[pl.BlockSpec((tile, w), lambda i, h, j: (i * nq + j, h)),
                  pl.BlockSpec((t, w), lambda i, h, j: (i, nhb + h)),
                  pl.BlockSpec((t, w), lambda i, h, j: (i, 2 * nhb + h))],
        out_specs=pl.BlockSpec((tile, w), lambda i, h, j: (i * nq + j, h)),
        out_shape=jax.ShapeDtypeStruct((b * t, SB_W), BF16),
        compiler_params=_cp("parallel", "parallel", "parallel"),
        name="stickbreak",
    )(qkv, qkv, qkv)


def _cross_kernel(x_ref, g_ref, wq_ref, kv_ref, wo_ref, o_ref):
    x = x_ref[...]
    h = _rms(x, g_ref[...]).astype(BF16)
    q = _dot(h, wq_ref[...]).astype(BF16)
    xw = X_HEADS * HEAD_DIM
    outs = []
    for hd in range(X_HEADS):
        sl = slice(hd * HEAD_DIM, (hd + 1) * HEAD_DIM)
        s = _nt_dot(q[:, sl], kv_ref[:, sl]) * SCALE
        e = jnp.exp(s - jnp.max(s, axis=-1, keepdims=True))
        p = e / jnp.sum(e, axis=-1, keepdims=True)
        outs.append(_dot(p.astype(BF16), kv_ref[:, xw + hd * HEAD_DIM:xw + (hd + 1) * HEAD_DIM]).astype(BF16))
    o_ref[...] = x + _dot(jnp.concatenate(outs, axis=1), wo_ref[...])


def _cross(x, g, wq, kv, wo, b, t, tq=512):
    n, d = x.shape
    nq = t // tq
    mem_len = kv.shape[0] // b
    return pl.pallas_call(
        _cross_kernel,
        grid=(b, nq),
        in_specs=[pl.BlockSpec((tq, d), lambda i, j: (i * nq + j, 0)),
                  pl.BlockSpec((1, d), lambda i, j: (0, 0)),
                  pl.BlockSpec(wq.shape, lambda i, j: (0, 0)),
                  pl.BlockSpec((mem_len, kv.shape[1]), lambda i, j: (i, 0)),
                  pl.BlockSpec(wo.shape, lambda i, j: (0, 0))],
        out_specs=pl.BlockSpec((tq, d), lambda i, j: (i * nq + j, 0)),
        out_shape=jax.ShapeDtypeStruct((n, d), F32),
        compiler_params=_cp("parallel", "parallel"),
        name="cross_attention",
    )(x, g.reshape(1, d), wq, kv, wo)


def _to_row_tiles(ref, val):
    rows, width = val.shape
    c_per = width // LANES
    for c in range(c_per):
        ref[pl.ds(c, rows, stride=c_per), :] = val[:, c * LANES:(c + 1) * LANES]


def _from_row_tiles(ref, c_per):
    rows = ref.shape[0] // c_per
    return jnp.concatenate([ref[pl.ds(c, rows, stride=c_per), :] for c in range(c_per)], axis=1)


def _router_kernel(x_ref, g_ref, w_ref, b_ref, h_ref, r_ref):
    h = _rms(x_ref[...], g_ref[...])
    _to_row_tiles(h_ref, h)
    h_hi = h.astype(BF16)
    h_mid = (h - h_hi.astype(F32)).astype(BF16)
    two = _dot(h_hi, w_ref[...])
    logits = two[:, :LANES] + two[:, LANES:] + _dot(h_mid, w_ref[:, :LANES]) + b_ref[...]
    lane = lax.broadcasted_iota(jnp.int32, logits.shape, 1).astype(F32)
    big = float(LANES)
    is_g = lane < N_GROUPS
    lg = jnp.where(is_g, logits, NEG)
    mg = jnp.max(lg, axis=-1, keepdims=True)
    p_gtop = 1.0 / jnp.sum(jnp.where(is_g, jnp.exp(lg - mg), 0.0), axis=-1, keepdims=True)
    g_top = jnp.min(jnp.where(lg == mg, lane, big), axis=-1, keepdims=True)
    lo = N_GROUPS + EXPERTS_PER_GROUP * g_top
    in_grp = (lane >= lo) & (lane < lo + EXPERTS_PER_GROUP)
    le = jnp.where(in_grp, logits, NEG)
    ee = jnp.where(in_grp, jnp.exp(le - jnp.max(le, axis=-1, keepdims=True)), 0.0)
    p_in = ee / jnp.sum(ee, axis=-1, keepdims=True)
    p1 = jnp.where(in_grp, p_in, -1.0)
    m1 = jnp.max(p1, axis=-1, keepdims=True)
    i1 = jnp.min(jnp.where(p1 == m1, lane, big), axis=-1, keepdims=True)
    p2 = jnp.where(lane == i1, -1.0, p1)
    m2 = jnp.max(p2, axis=-1, keepdims=True)
    i2 = jnp.min(jnp.where(p2 == m2, lane, big), axis=-1, keepdims=True)
    tot = m1 + m2
    r_ref[...] = jnp.where(lane == 0, i1 - N_GROUPS,
                           jnp.where(lane == 1, i2 - N_GROUPS,
                                     jnp.where(lane == 2, m1 / tot * p_gtop,
                                               jnp.where(lane == 3, m2 / tot * p_gtop, 0.0))))


def _router(x, g, w_r, b_r, tm=512):
    n, d = x.shape
    return pl.pallas_call(
        _router_kernel,
        grid=(n // tm,),
        in_specs=[pl.BlockSpec((tm, d), lambda i: (i, 0)), pl.BlockSpec((1, d), lambda i: (0, 0)),
                  pl.BlockSpec((d, 2 * LANES), lambda i: (0, 0)), pl.BlockSpec((1, LANES), lambda i: (0, 0))],
        out_specs=[pl.BlockSpec((tm * (d // LANES), LANES), lambda i: (i, 0)),
                   pl.BlockSpec((tm, LANES), lambda i: (i, 0))],
        out_shape=[jax.ShapeDtypeStruct((n * (d // LANES), LANES), F32), jax.ShapeDtypeStruct((n, LANES), F32)],
        compiler_params=_cp("parallel"),
        name="moe_router",
    )(x, g.reshape(1, d), w_r, b_r)


ROW_CHUNKS = 16


def _row_copy(src_hbm, dst, sem, src_row, dst_row):
    c = ROW_CHUNKS
    return pltpu.make_async_copy(src_hbm.at[pl.ds(pl.multiple_of(src_row * c, c), c), :],
                                 dst.at[pl.ds(pl.multiple_of(dst_row * c, c), c), :], sem)


def _expert_kernel(idx_ref, te_ref, nt_ref, src_hbm, rw_ref, wg_ref, wu_ref, wd_ref, o_ref,
                   xbuf, sems, wg_s, wu_s, wd_s, *, tm):
    i = pl.program_id(0)
    nt = nt_ref[0]
    live = i < nt
    slot = lax.rem(i, 2)

    def gather(tile, s):
        def issue(r, c):
            _row_copy(src_hbm, xbuf.at[s], sems.at[s], idx_ref[tile * tm + r], r).start()
            return c
        lax.fori_loop(0, tm, issue, 0)

    @pl.when(i == 0)
    def _():
        gather(0, 0)

    @pl.when(i + 1 < nt)
    def _():
        gather(i + 1, 1 - slot)

    fresh = (i == 0) | (te_ref[i] != te_ref[jnp.maximum(i - 1, 0)])

    @pl.when(live & fresh)
    def _():
        wg_s[...] = wg_ref[...].astype(BF16)
        wu_s[...] = wu_ref[...].astype(BF16)
        wd_s[...] = wd_ref[...].astype(BF16)

    @pl.when(live)
    def _():
        def drain(r, c):
            _row_copy(src_hbm, xbuf.at[slot], sems.at[slot], 0, r).wait()
            return c
        lax.fori_loop(0, tm, drain, 0)
        x = _from_row_tiles(xbuf.at[slot], ROW_CHUNKS).astype(BF16)
        a = _dot(x, wg_s[...])
        u = _dot(x, wu_s[...])
        act = a * (1.0 / (1.0 + jnp.exp(-a))) * u * rw_ref[...]
        _to_row_tiles(o_ref, _dot(act.astype(BF16), wd_s[...]))

    @pl.when(jnp.logical_not(live))
    def _():
        o_ref[...] = jnp.zeros_like(o_ref)


def _experts(src, tok_of_row, row_w, w_g, w_u, w_d, tile_expert, n_tiles, tm=EXPERT_TILE):
    c = ROW_CHUNKS
    rows = tok_of_row.shape[0]
    d = c * LANES
    f = w_g.shape[2]
    return pl.pallas_call(
        functools.partial(_expert_kernel, tm=tm),
        grid_spec=pltpu.PrefetchScalarGridSpec(
            num_scalar_prefetch=3,
            grid=(rows // tm,),
            in_specs=[pl.BlockSpec(memory_space=pl.ANY),
                      pl.BlockSpec((tm, 1), lambda i, idx, te, nt: (i, 0)),
                      pl.BlockSpec((None, d, f), lambda i, idx, te, nt: (te[i], 0, 0)),
                      pl.BlockSpec((None, d, f), lambda i, idx, te, nt: (te[i], 0, 0)),
                      pl.BlockSpec((None, f, d), lambda i, idx, te, nt: (te[i], 0, 0))],
            out_specs=pl.BlockSpec((tm * c, LANES), lambda i, idx, te, nt: (i, 0)),
            scratch_shapes=[pltpu.VMEM((2, tm * c, LANES), F32), pltpu.SemaphoreType.DMA((2,)),
                            pltpu.VMEM((d, f), BF16), pltpu.VMEM((d, f), BF16), pltpu.VMEM((f, d), BF16)]),
        out_shape=jax.ShapeDtypeStruct((rows * c, LANES), F32),
        compiler_params=_cp("arbitrary"),
        name="moe_experts",
    )(tok_of_row, tile_expert, n_tiles, src, row_w, w_g, w_u, w_d)


def _combine_kernel(p0_ref, p1_ref, ys_hbm, x_ref, g_ref, o_ref, buf0, buf1, sem, *, tm):
    base = pl.program_id(0) * tm

    def issue(r, c):
        _row_copy(ys_hbm, buf0, sem, p0_ref[base + r], r).start()
        _row_copy(ys_hbm, buf1, sem, p1_ref[base + r], r).start()
        return c

    def drain(r, c):
        _row_copy(ys_hbm, buf0, sem, 0, r).wait()
        _row_copy(ys_hbm, buf1, sem, 0, r).wait()
        return c

    lax.fori_loop(0, tm, issue, 0)
    lax.fori_loop(0, tm, drain, 0)
    y = _from_row_tiles(buf0, ROW_CHUNKS) + _from_row_tiles(buf1, ROW_CHUNKS)
    o_ref[...] = _rms(x_ref[...] + y, g_ref[...])


def _combine(ys, pos0, pos1, x, g, tm=256):
    n, d = x.shape
    blk = (tm * ROW_CHUNKS, LANES)
    return pl.pallas_call(
        functools.partial(_combine_kernel, tm=tm),
        grid_spec=pltpu.PrefetchScalarGridSpec(
            num_scalar_prefetch=2,
            grid=(n // tm,),
            in_specs=[pl.BlockSpec(memory_space=pl.ANY),
                      pl.BlockSpec((tm, d), lambda i, p0, p1: (i, 0)),
                      pl.BlockSpec((1, d), lambda i, p0, p1: (0, 0))],
            out_specs=pl.BlockSpec((tm, d), lambda i, p0, p1: (i, 0)),
            scratch_shapes=[pltpu.VMEM(blk, F32), pltpu.VMEM(blk, F32), pltpu.SemaphoreType.DMA(())]),
        out_shape=jax.ShapeDtypeStruct((n, d), F32),
        compiler_params=_cp("arbitrary"),
        name="moe_combine_norm",
    )(pos0, pos1, ys, x, g.reshape(1, d))


def _dispatch_plan(e1, e2, w1, w2, tm=EXPERT_TILE):
    n = e1.shape[0]
    n_tiles_max = (2 * n) // tm + N_EXPERTS
    rows = n_tiles_max * tm
    e_flat = jnp.concatenate([e1, e2])
    w_flat = jnp.concatenate([w1, w2])
    tok = jnp.concatenate([jnp.arange(n, dtype=jnp.int32)] * 2)
    onehot = (e_flat[:, None] == jnp.arange(N_EXPERTS, dtype=jnp.int32)[None, :]).astype(jnp.int32)
    csum = jnp.cumsum(onehot, axis=0)
    count = csum[-1]
    rank = jnp.sum((csum - 1) * onehot, axis=1)
    tiles_per = (count + tm - 1) // tm
    tile_end = jnp.cumsum(tiles_per)
    row_start = (tile_end - tiles_per) * tm
    dest = (row_start[e_flat] + rank).astype(jnp.int32)
    tok_of_row = jnp.zeros((rows,), jnp.int32).at[dest].set(tok)
    w_of_row = jnp.zeros((rows,), F32).at[dest].set(w_flat)
    n_tiles = tile_end[-1].astype(jnp.int32)
    tile_id = jnp.minimum(jnp.arange(n_tiles_max, dtype=jnp.int32), n_tiles - 1)
    tile_expert = jnp.sum((tile_id[:, None] >= tile_end[None, :]).astype(jnp.int32), axis=1).astype(jnp.int32)
    return tok_of_row, w_of_row.reshape(rows, 1), tile_expert, n_tiles.reshape(1), dest[:n], dest[n:]


def _rope_tables(positions):
    half = ROT_DIM // 2
    inv = ROPE_THETA ** (-jnp.arange(half, dtype=F32) * (2.0 / ROT_DIM))
    ang = positions.astype(F32).reshape(-1, 1) * inv[None, :]
    cos, sin = jnp.cos(ang), jnp.sin(ang)
    n = ang.shape[0]
    c = jnp.concatenate([cos, cos, jnp.ones((n, LANES - ROT_DIM), F32)], axis=1)
    s1 = jnp.concatenate([-sin, jnp.zeros((n, LANES - half), F32)], axis=1)
    s2 = jnp.concatenate([jnp.zeros((n, half), F32), sin, jnp.zeros((n, LANES - ROT_DIM), F32)], axis=1)
    return c, s1, s2


def _layer(x, mem, positions, g_mix, w_in, cmp_pe_k, cmp_w1_k, cmp_w2_k, cmp_pe_v, cmp_w1_v, cmp_w2_v,
           w_br_nsa, w_br_sb, w_o, g_cross, g_mem, w_cq, w_ck, w_cv, w_co,
           g_moe, w_rg, b_rg, w_re, b_re, w_eg, w_eu, w_ed):
    b, t, d = x.shape
    n = b * t
    xf = x.reshape(n, d)

    g0 = NSA_W
    w_main = jnp.concatenate([w_in[:, :g0], w_in[:, g0 + NSA_GATES:]], axis=1).astype(BF16)
    w_g = w_in[:, g0:g0 + NSA_GATES].reshape(d, NSA_KV_HEADS, 3 * NSA_GROUP)
    w_g = jnp.pad(w_g, ((0, 0), (0, 0), (0, LANES - 3 * NSA_GROUP))).reshape(d, NSA_KV_HEADS * LANES).astype(BF16)
    h = _rmsnorm(xf, g_mix, BF16)
    c, s1, s2 = _rope_tables(positions)
    row_spec = pl.BlockSpec((1024, LANES), lambda i, j: (i, 0))
    qkv_nsa = _mm(functools.partial(_mm_nsa_kernel, tn=256), h, w_main, NSA_W, 0, 1024, 256, BF16,
                  extra=(c, s1, s2), extra_specs=(row_spec, row_spec, row_spec), name="in_proj_nsa")
    qkv_sb = _mm(functools.partial(_mm_sb_kernel, q_tiles=SB_W // 512), h, w_main, 3 * SB_W, NSA_W // 512,
                 1024, 512, BF16, name="in_proj_sb")
    gates = _mm(_mm_sigmoid_kernel, h, w_main, 2 * d, (NSA_W + 3 * SB_W) // 512, 1024, 512, BF16,
                name="in_proj_gates")
    nsa_gates = _mm(_mm_sigmoid_kernel, h, w_g, NSA_KV_HEADS * LANES, 0, 1024, NSA_KV_HEADS * LANES, F32,
                    name="in_proj_nsa_gates")

    def chunked(col0):
        tok = qkv_nsa[:, col0:col0 + NSA_KV_W].reshape(b, t, NSA_KV_HEADS, HEAD_DIM).transpose(0, 2, 1, 3)
        return tok.reshape(b, NSA_KV_HEADS, t // CMP_STRIDE, CMP_STRIDE * HEAD_DIM)

    kc = _compress(chunked(NSA_Q_W), cmp_pe_k, cmp_w1_k, cmp_w2_k)
    vc = _compress(chunked(NSA_Q_W + NSA_KV_W), cmp_pe_v, cmp_w1_v, cmp_w2_v)
    o_c, sel = _nsa_cmp(qkv_nsa, kc, vc, b, t)
    o_nsa = _nsa_sel_win(qkv_nsa, sel, o_c, nsa_gates, b, t)
    o_sb = _stickbreak(qkv_sb, b, t)
    merged = _merge(o_nsa, o_sb, w_br_nsa.astype(BF16), w_br_sb.astype(BF16), gates)
    x1 = _mm(_mm_resid_kernel, merged, w_o.astype(BF16), d, 0, 1024, 512, F32,
             extra=(xf,), extra_specs=(pl.BlockSpec((1024, 512), lambda i, j: (i, j)),), name="out_proj")

    w_kv = jnp.concatenate([w_ck, w_cv], axis=1).astype(BF16)
    kv = _norm_mm(mem.reshape(-1, d), g_mem, w_kv, BF16)
    x2 = _cross(x1, g_cross, w_cq.astype(BF16), kv, w_co.astype(BF16), b, t)

    w_r = jnp.pad(jnp.concatenate([w_rg, w_re], axis=1), ((0, 0), (0, LANES - N_GROUPS - N_EXPERTS)))
    b_r = jnp.pad(jnp.concatenate([b_rg, b_re]), (0, LANES - N_GROUPS - N_EXPERTS)).reshape(1, LANES)
    w_r_hi = w_r.astype(BF16)
    w_r2 = jnp.concatenate([w_r_hi, (w_r - w_r_hi.astype(F32)).astype(BF16)], axis=1)
    hm, route = _router(x2, g_moe, w_r2, b_r)
    e1 = route[:, 0].astype(jnp.int32)
    e2 = route[:, 1].astype(jnp.int32)
    tok_of_row, row_w, tile_expert, n_tiles, pos0, pos1 = _dispatch_plan(e1, e2, route[:, 2], route[:, 3])
    f = w_eg.shape[-1]
    ys = _experts(hm, tok_of_row, row_w, w_eg.reshape(N_EXPERTS, d, f), w_eu.reshape(N_EXPERTS, d, f),
                  w_ed.reshape(N_EXPERTS, f, d), tile_expert, n_tiles)
    return ys, pos0, pos1, x2


def kernel(x, mem, positions, g_mix, w_in, cmp_pe_k, cmp_w1_k, cmp_w2_k, cmp_pe_v, cmp_w1_v, cmp_w2_v, w_br_nsa, w_br_sb, w_o, g_cross, g_mem, w_cq, w_ck, w_cv, w_co, g_moe, w_rg, b_rg, w_re, b_re, w_eg, w_eu, w_ed, g_final):
    b, t, d = x.shape
    assert w_in.shape[0] == 1, "one layer"
    ys, pos0, pos1, x2 = _layer(
        x, mem, positions, g_mix[0], w_in[0], cmp_pe_k[0], cmp_w1_k[0], cmp_w2_k[0], cmp_pe_v[0], cmp_w1_v[0],
        cmp_w2_v[0], w_br_nsa[0], w_br_sb[0], w_o[0], g_cross[0], g_mem[0], w_cq[0], w_ck[0], w_cv[0], w_co[0],
        g_moe[0], w_rg[0], b_rg[0], w_re[0], b_re[0], w_eg[0], w_eu[0], w_ed[0])
    return _combine(ys, pos0, pos1, x2, g_final).reshape(b, t, d)
```

```python
import functools

import jax
import jax.numpy as jnp
from jax import lax
from jax.experimental import pallas as pl
from jax.experimental.pallas import tpu as pltpu

F32 = jnp.float32
BF16 = jnp.bfloat16

HEAD_DIM = 128
NSA_HEADS = 8
NSA_KV_HEADS = 2
NSA_GROUP = NSA_HEADS // NSA_KV_HEADS
SB_HEADS = 8
ROT_DIM = HEAD_DIM // 4
ROPE_THETA = 500000.0
CMP_LEN = 32
CMP_STRIDE = 16
SEL_LEN = 64
SEL_SHIFT = 6
SEL_TOPN = 16
WINDOW = 512
FORCE_BONUS = 1000.0
X_HEADS = 4
N_GROUPS = 4
EXPERTS_PER_GROUP = 8
N_EXPERTS = N_GROUPS * EXPERTS_PER_GROUP
EPS = 1e-6
SCALE = HEAD_DIM ** -0.5
NEG = -1e30

NSA_Q_W = NSA_HEADS * HEAD_DIM
NSA_KV_W = NSA_KV_HEADS * HEAD_DIM
NSA_W = NSA_Q_W + 6 * NSA_KV_W
NSA_GATES = 3 * NSA_HEADS
SB_W = SB_HEADS * HEAD_DIM
LANES = 128
VMEM_LIMIT = 56 * 1024 * 1024

EXPERT_TILE = 256


def _cp(*sem, vmem=VMEM_LIMIT):
    return pltpu.CompilerParams(dimension_semantics=sem, vmem_limit_bytes=vmem)


def _nt_dot(a, b):
    return lax.dot_general(a, b, (((1,), (1,)), ((), ())), preferred_element_type=F32)


def _dot(a, b):
    return jnp.dot(a, b, preferred_element_type=F32)


def _split3(x):
    hi = x.astype(BF16)
    r1 = x - hi.astype(F32)
    mid = r1.astype(BF16)
    lo = (r1 - mid.astype(F32)).astype(BF16)
    return hi, mid, lo


def _rms(x, g):
    return x * lax.rsqrt(jnp.mean(x * x, axis=-1, keepdims=True) + EPS) * g


def _rmsnorm_kernel(x_ref, g_ref, o_ref):
    o_ref[...] = _rms(x_ref[...].astype(F32), g_ref[...]).astype(o_ref.dtype)


def _rmsnorm(x, g, out_dtype, tm=512):
    n, d = x.shape
    return pl.pallas_call(
        _rmsnorm_kernel,
        grid=(n // tm,),
        in_specs=[pl.BlockSpec((tm, d), lambda i: (i, 0)), pl.BlockSpec((1, d), lambda i: (0, 0))],
        out_specs=pl.BlockSpec((tm, d), lambda i: (i, 0)),
        out_shape=jax.ShapeDtypeStruct((n, d), out_dtype),
        compiler_params=_cp("parallel"),
        name="rmsnorm",
    )(x, g.reshape(1, d))


def _cast_rows_kernel(w_ref, o_ref):
    o_ref[...] = w_ref[...].astype(o_ref.dtype)


def _cast_rows(w_t, row0, n_rows, tn=512):
    k = w_t.shape[1]
    assert n_rows % tn == 0 and row0 % 8 == 0
    return pl.pallas_call(
        _cast_rows_kernel,
        grid=(n_rows // tn,),
        in_specs=[pl.BlockSpec((pl.Element(tn), pl.Element(k)), lambda j: ((row0 // 8 + j * (tn // 8)) * 8, 0))],
        out_specs=pl.BlockSpec((tn, k), lambda j: (j, 0)),
        out_shape=jax.ShapeDtypeStruct((n_rows, k), BF16),
        compiler_params=_cp("parallel"),
        name="cast_rows",
    )(w_t)


def _rope_half(x, c, s1, s2):
    return x * c + pltpu.roll(x, LANES - ROT_DIM // 2, 1) * s1 + pltpu.roll(x, ROT_DIM // 2, 1) * s2


def _mm_nsa_kernel(a_ref, w_ref, c_ref, s1_ref, s2_ref, o_ref, *, tn):
    j = pl.program_id(1)
    acc = _nt_dot(a_ref[...], w_ref[...])
    is_rope = (j < 5) | (j == 6) | (j == 8)

    @pl.when(is_rope)
    def _():
        qs = jnp.where(j < NSA_Q_W // tn, SCALE, 1.0)
        c, s1, s2 = c_ref[...] * qs, s1_ref[...] * qs, s2_ref[...] * qs
        for hh in range(tn // LANES):
            sl = slice(hh * LANES, (hh + 1) * LANES)
            o_ref[:, sl] = _rope_half(acc[:, sl], c, s1, s2).astype(o_ref.dtype)

    @pl.when(jnp.logical_not(is_rope))
    def _():
        o_ref[...] = acc.astype(o_ref.dtype)


def _mm_sb_kernel(a_ref, w_ref, o_ref, *, q_tiles):
    acc = _nt_dot(a_ref[...], w_ref[...])
    is_q = pl.program_id(1) < q_tiles
    o_ref[...] = (acc * jnp.where(is_q, SCALE, 1.0)).astype(o_ref.dtype)


def _mm_sigmoid_kernel(a_ref, w_ref, o_ref):
    o_ref[...] = jax.nn.sigmoid(_nt_dot(a_ref[...], w_ref[...])).astype(o_ref.dtype)


def _mm_resid_kernel(a_ref, w_ref, x_ref, o_ref):
    o_ref[...] = x_ref[...] + _dot(a_ref[...], w_ref[...])


def _mm(kern, a, w, n_cols, col_block0, tm, tn, out_dtype, extra=(), extra_specs=(), name="mm", w_t=False):
    m, k = a.shape
    w_spec = (pl.BlockSpec((tn, k), lambda i, j: (col_block0 + j, 0)) if w_t
              else pl.BlockSpec((k, tn), lambda i, j: (0, col_block0 + j)))
    return pl.pallas_call(
        kern,
        grid=(m // tm, n_cols // tn),
        in_specs=[pl.BlockSpec((tm, k), lambda i, j: (i, 0)), w_spec, *extra_specs],
        out_specs=pl.BlockSpec((tm, tn), lambda i, j: (i, j)),
        out_shape=jax.ShapeDtypeStruct((m, n_cols), out_dtype),
        compiler_params=_cp("parallel", "arbitrary"),
        name=name,
    )(a, w, *extra)


def _merge_kernel(a1_ref, a2_ref, w1_ref, w2_ref, ga_ref, gb_ref, o_ref):
    ya = _dot(a1_ref[...], w1_ref[...])
    yb = _dot(a2_ref[...], w2_ref[...])
    o_ref[...] = (ga_ref[...].astype(F32) * ya + gb_ref[...].astype(F32) * yb).astype(o_ref.dtype)


def _merge(o_nsa, o_sb, w1, w2, gates, tm=1024, tn=512):
    n, k = o_nsa.shape
    d = w1.shape[1]
    nb = d // tn
    return pl.pallas_call(
        _merge_kernel,
        grid=(n // tm, nb),
        in_specs=[pl.BlockSpec((tm, k), lambda i, j: (i, 0)), pl.BlockSpec((tm, k), lambda i, j: (i, 0)),
                  pl.BlockSpec((k, tn), lambda i, j: (0, j)), pl.BlockSpec((k, tn), lambda i, j: (0, j)),
                  pl.BlockSpec((tm, tn), lambda i, j: (i, j)), pl.BlockSpec((tm, tn), lambda i, j: (i, nb + j))],
        out_specs=pl.BlockSpec((tm, tn), lambda i, j: (i, j)),
        out_shape=jax.ShapeDtypeStruct((n, d), BF16),
        compiler_params=_cp("parallel", "arbitrary"),
        name="merge",
    )(o_nsa, o_sb, w1, w2, gates, gates)


def _norm_mm_kernel(x_ref, g_ref, w_ref, o_ref):
    h = _rms(x_ref[...].astype(F32), g_ref[...]).astype(BF16)
    o_ref[...] = _dot(h, w_ref[...]).astype(o_ref.dtype)


def _norm_mm(x, g, w, out_dtype, tm=256):
    n, d = x.shape
    nc = w.shape[1]
    return pl.pallas_call(
        _norm_mm_kernel,
        grid=(n // tm,),
        in_specs=[pl.BlockSpec((tm, d), lambda i: (i, 0)), pl.BlockSpec((1, d), lambda i: (0, 0)),
                  pl.BlockSpec((d, nc), lambda i: (0, 0))],
        out_specs=pl.BlockSpec((tm, nc), lambda i: (i, 0)),
        out_shape=jax.ShapeDtypeStruct((n, nc), out_dtype),
        compiler_params=_cp("parallel"),
        name="norm_mm",
    )(x, g.reshape(1, d), w)


def _compress_kernel(x_ref, pe_ref, w1_ref, w2_ref, o_ref):
    x = x_ref[...].astype(F32)
    half = x.shape[1]
    xa = (x + pe_ref[0:1, :]).astype(BF16)
    xb = (x + pe_ref[1:2, :]).astype(BF16)
    y1 = _dot(xa, w1_ref[0:half, :].astype(BF16))
    y2 = _dot(xb, w1_ref[half:2 * half, :].astype(BF16))
    pre = y1 + pltpu.roll(y2, x.shape[0] - 1, 0)
    act = pre * (1.0 / (1.0 + jnp.exp(-pre)))
    o_ref[...] = _dot(act.astype(BF16), w2_ref[...].astype(BF16)).astype(o_ref.dtype)


def _compress(tok, pe, w1, w2):
    b, hkv, nchunk, width = tok.shape
    return pl.pallas_call(
        _compress_kernel,
        grid=(b, hkv),
        in_specs=[pl.BlockSpec((None, None, nchunk, width), lambda i, j: (i, j, 0, 0)),
                  pl.BlockSpec((2, width), lambda i, j: (0, 0)),
                  pl.BlockSpec((2 * width, HEAD_DIM), lambda i, j: (0, 0)),
                  pl.BlockSpec((HEAD_DIM, HEAD_DIM), lambda i, j: (0, 0))],
        out_specs=pl.BlockSpec((None, None, nchunk, HEAD_DIM), lambda i, j: (i, j, 0, 0)),
        out_shape=jax.ShapeDtypeStruct((b, hkv, nchunk, HEAD_DIM), BF16),
        compiler_params=_cp("parallel", "parallel"),
        name="nsa_compress",
    )(tok, pe.reshape(2, width), w1, w2)


def _nsa_cmp_kernel(q_ref, kc_ref, vc_ref, oc_ref, sel_ref, *, tq, nc, ns):
    q0 = pl.program_id(2) * tq
    kc = kc_ref[...]
    vc = vc_ref[...]
    t_row = q0 + lax.broadcasted_iota(jnp.int32, (tq, nc), 0)
    c_idx = lax.broadcasted_iota(jnp.int32, (tq, nc), 1)
    vis = (c_idx * CMP_STRIDE + (CMP_LEN - 1) <= t_row) & (c_idx < nc - 1)
    imp = jnp.zeros((tq, nc), F32)
    for h in range(NSA_GROUP):
        sl = slice(h * HEAD_DIM, (h + 1) * HEAD_DIM)
        s = jnp.where(vis, _nt_dot(q_ref[:, sl], kc), NEG)
        m = jnp.max(s, axis=-1, keepdims=True)
        e = jnp.where(vis, jnp.exp(s - m), 0.0)
        den = jnp.sum(e, axis=-1, keepdims=True)
        p = e / jnp.where(den > 0, den, 1.0)
        oc_ref[:, sl] = _dot(p.astype(BF16), vc).astype(oc_ref.dtype)
        imp = imp + p
    jj = lax.broadcasted_iota(jnp.int32, (ns, nc), 0)
    cc = lax.broadcasted_iota(jnp.int32, (ns, nc), 1) * CMP_STRIDE
    ov = jnp.where((cc < (jj + 1) * SEL_LEN) & (cc + CMP_LEN > jj * SEL_LEN), 1.0, 0.0).astype(BF16)
    imp_t = sum(_nt_dot(ov, part) for part in _split3(imp))
    jb = lax.broadcasted_iota(jnp.int32, (ns, tq), 0)
    tt = q0 + lax.broadcasted_iota(jnp.int32, (ns, tq), 1)
    cur = lax.shift_right_logical(tt, jnp.full_like(tt, SEL_SHIFT))
    forced = (jb == 0) | (jb == cur) | (jb == cur - 1)
    valid = jb * SEL_LEN <= tt
    score = jnp.where(valid, imp_t + jnp.where(forced, FORCE_BONUS, 0.0), NEG)
    rank = jnp.zeros((ns, tq), F32)
    for i in range(ns):
        row = score[i:i + 1, :]
        rank = rank + jnp.where(row > score, 1.0, jnp.where((row == score) & (jb > i), 1.0, 0.0))
    sel_t = jnp.where(valid & (rank < float(min(SEL_TOPN, ns))), 1.0, 0.0).astype(BF16)
    eye = jnp.where(lax.broadcasted_iota(jnp.int32, (tq, tq), 0) == lax.broadcasted_iota(jnp.int32, (tq, tq), 1),
                    1.0, 0.0).astype(BF16)
    sel_ref[...] = _nt_dot(eye, sel_t)


def _nsa_cmp(qkv, kc, vc, b, t, tq=256):
    nc = kc.shape[2]
    ns = t // SEL_LEN
    nq = t // tq
    gw = NSA_GROUP * HEAD_DIM
    return pl.pallas_call(
        functools.partial(_nsa_cmp_kernel, tq=tq, nc=nc, ns=ns),
        grid=(b, NSA_KV_HEADS, nq),
        in_specs=[pl.BlockSpec((tq, gw), lambda i, k, j: (i * nq + j, k)),
                  pl.BlockSpec((None, None, nc, HEAD_DIM), lambda i, k, j: (i, k, 0, 0)),
                  pl.BlockSpec((None, None, nc, HEAD_DIM), lambda i, k, j: (i, k, 0, 0))],
        out_specs=[pl.BlockSpec((tq, gw), lambda i, k, j: (i * nq + j, k)),
                   pl.BlockSpec((None, None, tq, ns), lambda i, k, j: (i, k, j, 0))],
        out_shape=[jax.ShapeDtypeStruct((b * t, NSA_Q_W), BF16),
                   jax.ShapeDtypeStruct((b, NSA_KV_HEADS, t, ns), F32)],
        compiler_params=_cp("parallel", "parallel", "parallel"),
        name="nsa_cmp_select",
    )(qkv, kc, vc)


def _flash_update(s, v, bias, state, g, tq):
    m, l, acc = state
    tk = v.shape[0]
    if bias is not None:
        s = (s.reshape(g, tq, tk) + bias[None]).reshape(g * tq, tk)
    m_new = jnp.maximum(m, jnp.max(s, axis=-1, keepdims=True))
    alpha = jnp.exp(m - m_new)
    p = jnp.exp(s - m_new)
    l = alpha * l + jnp.sum(p, axis=-1, keepdims=True)
    return m_new, l, alpha * acc + _dot(p.astype(BF16), v)


def _nsa_sel_win_kernel(q_ref, ks_ref, vs_ref, kw_ref, vw_ref, sel_ref, oc_ref, g_ref, o_ref, *, tq, tk, stack, ns):
    q0 = pl.program_id(1) * tq
    g = NSA_GROUP
    nkv = NSA_KV_HEADS
    per_tile = tk // SEL_LEN
    units = [(kv, h0) for kv in range(nkv) for h0 in range(0, g, stack)]
    kv_lanes = [slice(kv * HEAD_DIM, (kv + 1) * HEAD_DIM) for kv in range(nkv)]

    def q_lanes(kv, h):
        return slice((kv * g + h) * HEAD_DIM, (kv * g + h + 1) * HEAD_DIM)

    qs = [jnp.concatenate([q_ref[:, q_lanes(kv, h0 + j)] for j in range(stack)], axis=0) for kv, h0 in units]
    sel_bias = [((sel_ref[kv] - 1.0) * (-NEG)).astype(BF16) for kv in range(nkv)]
    ej = lax.broadcasted_iota(jnp.int32, (ns, tk), 0)
    eb = lax.broadcasted_iota(jnp.int32, (ns, tk), 1)
    eb = lax.shift_right_logical(eb, jnp.full_like(eb, SEL_SHIFT))

    def init():
        return tuple((jnp.full((stack * tq, 1), NEG, F32), jnp.zeros((stack * tq, 1), F32),
                      jnp.zeros((stack * tq, HEAD_DIM), F32)) for _ in units)

    def flash_all(k_ref, v_ref, k0, width, biases, states):
        scores = [_nt_dot(qs[u], k_ref[pl.ds(k0, width), kv_lanes[kv]]) for u, (kv, _) in enumerate(units)]
        return tuple(_flash_update(scores[u], v_ref[pl.ds(k0, width), kv_lanes[kv]], biases[kv], states[u], stack, tq)
                     for u, (kv, _) in enumerate(units))

    def sel_step(kt, states, causal):
        k0 = pl.multiple_of(kt * tk, tk)
        expand = jnp.where(ej == eb + kt * per_tile, 1.0, 0.0).astype(BF16)
        biases = [_dot(sel_bias[kv], expand) for kv in range(nkv)]
        if causal is not None:
            biases = [bias + causal for bias in biases]
        return flash_all(ks_ref, vs_ref, k0, tk, biases, states)

    kt_d = q0 // tk
    t_row = q0 + lax.broadcasted_iota(jnp.int32, (tq, tk), 0)
    col = kt_d * tk + lax.broadcasted_iota(jnp.int32, (tq, tk), 1)
    states = sel_step(kt_d, init(), jnp.where(col <= t_row, 0.0, NEG))
    states = lax.fori_loop(0, kt_d, lambda kt, st: sel_step(kt, st, None), states)
    o_s = [acc * (1.0 / l) for (_, l, acc) in states]

    def win_step(k0, states, bias):
        return flash_all(kw_ref, vw_ref, k0, tq, [bias] * nkv, states)

    ri = lax.broadcasted_iota(jnp.int32, (tq, tq), 0)
    ci = lax.broadcasted_iota(jnp.int32, (tq, tq), 1)
    states = win_step(pl.multiple_of(q0, tq), init(), jnp.where(ci <= ri, 0.0, NEG))
    n_full = jnp.minimum(WINDOW // tq - 1, q0 // tq)
    states = lax.fori_loop(0, n_full, lambda i, st: win_step(pl.multiple_of(q0 - (i + 1) * tq, tq), st, None),
                           states)
    edge_bias = jnp.where(ci > ri, 0.0, NEG) + jnp.where(q0 >= WINDOW, 0.0, NEG)
    states = win_step(pl.multiple_of(jnp.maximum(q0 - WINDOW, 0), tq), states, edge_bias)
    o_w = [acc * (1.0 / l) for (_, l, acc) in states]

    for u, (kv, h0) in enumerate(units):
        gates = g_ref[:, kv * LANES:(kv + 1) * LANES]
        for j in range(stack):
            h = h0 + j
            sl = q_lanes(kv, h)
            rows = slice(j * tq, (j + 1) * tq)
            out = (gates[:, 3 * h:3 * h + 1] * oc_ref[:, sl].astype(F32)
                   + gates[:, 3 * h + 1:3 * h + 2] * o_s[u][rows] + gates[:, 3 * h + 2:3 * h + 3] * o_w[u][rows])
            o_ref[:, sl] = out.astype(o_ref.dtype)


def _nsa_sel_win(qkv, sel, oc, gates, b, t, tq=128, tk=256, stack=4):
    nq = t // tq
    ns = t // SEL_LEN
    kv_block = NSA_Q_W // NSA_KV_W

    def kv_spec(which):
        return pl.BlockSpec((t, NSA_KV_W), lambda i, j: (i, kv_block + which))

    return pl.pallas_call(
        functools.partial(_nsa_sel_win_kernel, tq=tq, tk=tk, stack=stack, ns=ns),
        grid=(b, nq),
        in_specs=[pl.BlockSpec((tq, NSA_Q_W), lambda i, j: (i * nq + j, 0)),
                  kv_spec(2), kv_spec(3), kv_spec(4), kv_spec(5),
                  pl.BlockSpec((None, NSA_KV_HEADS, tq, ns), lambda i, j: (i, 0, j, 0)),
                  pl.BlockSpec((tq, NSA_Q_W), lambda i, j: (i * nq + j, 0)),
                  pl.BlockSpec((tq, NSA_KV_HEADS * LANES), lambda i, j: (i * nq + j, 0))],
        out_specs=pl.BlockSpec((tq, NSA_Q_W), lambda i, j: (i * nq + j, 0)),
        out_shape=jax.ShapeDtypeStruct((b * t, NSA_Q_W), BF16),
        compiler_params=_cp("parallel", "parallel"),
        name="nsa_select_window",
    )(qkv, qkv, qkv, qkv, qkv, sel, oc, gates)


def _stickbreak_kernel(q_ref, k_ref, v_ref, o_ref, *, tile, nh):
    q0 = pl.program_id(2) * tile
    ri = lax.broadcasted_iota(jnp.int32, (tile, tile), 0)
    ci = lax.broadcasted_iota(jnp.int32, (tile, tile), 1)
    strict = ci < ri
    tri = jnp.where(ri > ci, 1.0, 0.0).astype(BF16)
    lanes = [slice(h * HEAD_DIM, (h + 1) * HEAD_DIM) for h in range(nh)]

    def step(k0, diag, carry):
        zs = [_nt_dot(q_ref[:, lanes[h]], k_ref[pl.ds(k0, tile), lanes[h]]) for h in range(nh)]
        log_bs, sufs, sums = [], [], []
        for z in zs:
            log_b = jnp.minimum(z, 0.0) - jnp.log(1.0 + jnp.exp(-jnp.abs(z)))
            log_1m = log_b - z
            if diag:
                log_1m = jnp.where(strict, log_1m, 0.0)
            hi = log_1m.astype(BF16)
            lo = (log_1m - hi.astype(F32)).astype(BF16)
            log_bs.append(log_b)
            sufs.append(_dot(hi, tri) + _dot(lo, tri))
            sums.append(jnp.sum(log_1m, axis=-1, keepdims=True))
        out = []
        for h in range(nh):
            later, acc = carry[2 * h], carry[2 * h + 1]
            a = jnp.exp(log_bs[h] + (sufs[h] + later))
            if diag:
                a = jnp.where(strict, a, 0.0)
            out.extend((later + sums[h], acc + _dot(a.astype(BF16), v_ref[pl.ds(k0, tile), lanes[h]])))
        return tuple(out)

    zero = (jnp.zeros((tile, 1), F32), jnp.zeros((tile, HEAD_DIM), F32)) * nh
    carry = step(pl.multiple_of(q0, tile), True, zero)
    carry = lax.fori_loop(0, q0 // tile,
                          lambda i, c: step(pl.multiple_of(q0 - (i + 1) * tile, tile), False, c), carry)
    for h in range(nh):
        o_ref[:, lanes[h]] = carry[2 * h + 1].astype(o_ref.dtype)


def _stickbreak(qkv, b, t, tile=256, nh=4):
    nq = t // tile
    w = nh * HEAD_DIM
    nhb = SB_HEADS // nh
    return pl.pallas_call(
        functools.partial(_stickbreak_kernel, tile=tile, nh=nh),
        grid=(b, nhb, nq),
        in_specs=[pl.BlockSpec((tile, w), lambda i, h, j: (i * nq + j, h)),
                  pl.BlockSpec((t, w), lambda i, h, j: (i, nhb + h)),
                  pl.BlockSpec((t, w), lambda i, h, j: (i, 2 * nhb + h))],
        out_specs=pl.BlockSpec((tile, w), lambda i, h, j: (i * nq + j, h)),
        out_shape=jax.ShapeDtypeStruct((b * t, SB_W), BF16),
        compiler_params=_cp("parallel", "parallel", "parallel"),
        name="stickbreak",
    )(qkv, qkv, qkv)


def _cross_kernel(x_ref, g_ref, wq_ref, kv_ref, wo_ref, o_ref):
    x = x_ref[...]
    h = _rms(x, g_ref[...]).astype(BF16)
    q = _dot(h, wq_ref[...]).astype(BF16)
    xw = X_HEADS * HEAD_DIM
    outs = []
    for hd in range(X_HEADS):
        sl = slice(hd * HEAD_DIM, (hd + 1) * HEAD_DIM)
        s = _nt_dot(q[:, sl], kv_ref[:, sl]) * SCALE
        e = jnp.exp(s - jnp.max(s, axis=-1, keepdims=True))
        p = e / jnp.sum(e, axis=-1, keepdims=True)
        outs.append(_dot(p.astype(BF16), kv_ref[:, xw + hd * HEAD_DIM:xw + (hd + 1) * HEAD_DIM]).astype(BF16))
    o_ref[...] = x + _dot(jnp.concatenate(outs, axis=1), wo_ref[...])


def _cross(x, g, wq, kv, wo, b, t, tq=512):
    n, d = x.shape
    nq = t // tq
    mem_len = kv.shape[0] // b
    return pl.pallas_call(
        _cross_kernel,
        grid=(b, nq),
        in_specs=[pl.BlockSpec((tq, d), lambda i, j: (i * nq + j, 0)),
                  pl.BlockSpec((1, d), lambda i, j: (0, 0)),
                  pl.BlockSpec(wq.shape, lambda i, j: (0, 0)),
                  pl.BlockSpec((mem_len, kv.shape[1]), lambda i, j: (i, 0)),
                  pl.BlockSpec(wo.shape, lambda i, j: (0, 0))],
        out_specs=pl.BlockSpec((tq, d), lambda i, j: (i * nq + j, 0)),
        out_shape=jax.ShapeDtypeStruct((n, d), F32),
        compiler_params=_cp("parallel", "parallel"),
        name="cross_attention",
    )(x, g.reshape(1, d), wq, kv, wo)


def _to_row_tiles(ref, val):
    rows, width = val.shape
    c_per = width // LANES
    for c in range(c_per):
        ref[pl.ds(c, rows, stride=c_per), :] = val[:, c * LANES:(c + 1) * LANES]


def _from_row_tiles(ref, c_per):
    rows = ref.shape[0] // c_per
    return jnp.concatenate([ref[pl.ds(c, rows, stride=c_per), :] for c in range(c_per)], axis=1)


def _router_kernel(x_ref, g_ref, w_ref, b_ref, h_ref, r_ref):
    h = _rms(x_ref[...], g_ref[...])
    _to_row_tiles(h_ref, h)
    h_hi = h.astype(BF16)
    h_mid = (h - h_hi.astype(F32)).astype(BF16)
    two = _dot(h_hi, w_ref[...])
    logits = two[:, :LANES] + two[:, LANES:] + _dot(h_mid, w_ref[:, :LANES]) + b_ref[...]
    lane = lax.broadcasted_iota(jnp.int32, logits.shape, 1).astype(F32)
    big = float(LANES)
    is_g = lane < N_GROUPS
    lg = jnp.where(is_g, logits, NEG)
    mg = jnp.max(lg, axis=-1, keepdims=True)
    p_gtop = 1.0 / jnp.sum(jnp.where(is_g, jnp.exp(lg - mg), 0.0), axis=-1, keepdims=True)
    g_top = jnp.min(jnp.where(lg == mg, lane, big), axis=-1, keepdims=True)
    lo = N_GROUPS + EXPERTS_PER_GROUP * g_top
    in_grp = (lane >= lo) & (lane < lo + EXPERTS_PER_GROUP)
    le = jnp.where(in_grp, logits, NEG)
    ee = jnp.where(in_grp, jnp.exp(le - jnp.max(le, axis=-1, keepdims=True)), 0.0)
    p_in = ee / jnp.sum(ee, axis=-1, keepdims=True)
    p1 = jnp.where(in_grp, p_in, -1.0)
    m1 = jnp.max(p1, axis=-1, keepdims=True)
    i1 = jnp.min(jnp.where(p1 == m1, lane, big), axis=-1, keepdims=True)
    p2 = jnp.where(lane == i1, -1.0, p1)
    m2 = jnp.max(p2, axis=-1, keepdims=True)
    i2 = jnp.min(jnp.where(p2 == m2, lane, big), axis=-1, keepdims=True)
    tot = m1 + m2
    r_ref[...] = jnp.where(lane == 0, i1 - N_GROUPS,
                           jnp.where(lane == 1, i2 - N_GROUPS,
                                     jnp.where(lane == 2, m1 / tot * p_gtop,
                                               jnp.where(lane == 3, m2 / tot * p_gtop, 0.0))))


def _router(x, g, w_r, b_r, tm=512):
    n, d = x.shape
    return pl.pallas_call(
        _router_kernel,
        grid=(n // tm,),
        in_specs=[pl.BlockSpec((tm, d), lambda i: (i, 0)), pl.BlockSpec((1, d), lambda i: (0, 0)),
                  pl.BlockSpec((d, 2 * LANES), lambda i: (0, 0)), pl.BlockSpec((1, LANES), lambda i: (0, 0))],
        out_specs=[pl.BlockSpec((tm * (d // LANES), LANES), lambda i: (i, 0)),
                   pl.BlockSpec((tm, LANES), lambda i: (i, 0))],
        out_shape=[jax.ShapeDtypeStruct((n * (d // LANES), LANES), F32), jax.ShapeDtypeStruct((n, LANES), F32)],
        compiler_params=_cp("parallel"),
        name="moe_router",
    )(x, g.reshape(1, d), w_r, b_r)


ROW_CHUNKS = 16


def _wait_rows(src_hbm, dst, sem):
    pltpu.make_async_copy(src_hbm.at[pl.ds(0, dst.shape[0]), :], dst, sem).wait()


def _row_copy(src_hbm, dst, sem, src_row, dst_row):
    c = ROW_CHUNKS
    return pltpu.make_async_copy(src_hbm.at[pl.ds(pl.multiple_of(src_row * c, c), c), :],
                                 dst.at[pl.ds(pl.multiple_of(dst_row * c, c), c), :], sem)


def _expert_kernel(idx_ref, ord_ref, oe_ref, cnt_ref, src_hbm, wg_hbm, wu_hbm, wd_hbm, o_ref,
                   xbuf, sems, wg_f, wu_f, wd_f, wsems, wg_s, wu_s, wd_s, *, tm):
    i = pl.program_id(0)
    nt = cnt_ref[0]
    n_ord = cnt_ref[1]
    live = i < nt
    slot = lax.rem(i, 2)

    def gather(tile, s):
        def issue(r, c):
            _row_copy(src_hbm, xbuf.at[s], sems.at[s], idx_ref[tile * tm + r], r).start()
            return c
        lax.fori_loop(0, tm, issue, 0, unroll=8)

    def weight_copies(k, s):
        e = oe_ref[k]
        return (pltpu.make_async_copy(wg_hbm.at[e], wg_f.at[s], wsems.at[s, 0]),
                pltpu.make_async_copy(wu_hbm.at[e], wu_f.at[s], wsems.at[s, 1]),
                pltpu.make_async_copy(wd_hbm.at[e], wd_f.at[s], wsems.at[s, 2]))

    @pl.when(i == 0)
    def _():
        gather(0, 0)
        for cp in weight_copies(0, 0):
            cp.start()

    @pl.when(i + 1 < nt)
    def _():
        gather(i + 1, 1 - slot)

    k = ord_ref[i]
    fresh = live & ((i == 0) | (k != ord_ref[jnp.maximum(i - 1, 0)]))
    ws = lax.rem(k, 2)

    @pl.when(fresh)
    def _():
        for cp in weight_copies(k, ws):
            cp.wait()

        @pl.when(k + 1 < n_ord)
        def _():
            for cp in weight_copies(k + 1, 1 - ws):
                cp.start()

        wg_s[...] = wg_f[ws].astype(BF16)
        wu_s[...] = wu_f[ws].astype(BF16)
        wd_s[...] = wd_f[ws].astype(BF16)

    @pl.when(live)
    def _():
        _wait_rows(src_hbm, xbuf.at[slot], sems.at[slot])
        x = _from_row_tiles(xbuf.at[slot], ROW_CHUNKS).astype(BF16)
        a = _dot(x, wg_s[...])
        u = _dot(x, wu_s[...])
        act = a * (1.0 / (1.0 + jnp.exp(-a))) * u
        _to_row_tiles(o_ref, _dot(act.astype(BF16), wd_s[...]))

    @pl.when(jnp.logical_not(live))
    def _():
        o_ref[...] = jnp.zeros_like(o_ref)


def _experts(src, tok_of_row, w_g, w_u, w_d, tile_ord, ord_expert, counts, tm=EXPERT_TILE):
    c = ROW_CHUNKS
    n_tiles_max = tok_of_row.shape[0] // tm
    d = c * LANES
    f = w_g.shape[2]
    any_spec = pl.BlockSpec(memory_space=pl.ANY)
    return pl.pallas_call(
        functools.partial(_expert_kernel, tm=tm),
        grid_spec=pltpu.PrefetchScalarGridSpec(
            num_scalar_prefetch=4,
            grid=(n_tiles_max,),
            in_specs=[any_spec, any_spec, any_spec, any_spec],
            out_specs=pl.BlockSpec((tm * c, LANES), lambda i, *_: (i, 0)),
            scratch_shapes=[pltpu.VMEM((2, tm * c, LANES), F32), pltpu.SemaphoreType.DMA((2,)),
                            pltpu.VMEM((2, d, f), F32), pltpu.VMEM((2, d, f), F32), pltpu.VMEM((2, f, d), F32),
                            pltpu.SemaphoreType.DMA((2, 3)),
                            pltpu.VMEM((d, f), BF16), pltpu.VMEM((d, f), BF16), pltpu.VMEM((f, d), BF16)]),
        out_shape=jax.ShapeDtypeStruct((n_tiles_max * tm * c, LANES), F32),
        compiler_params=_cp("arbitrary"),
        name="moe_experts",
    )(tok_of_row, tile_ord, ord_expert, counts, src, w_g, w_u, w_d)


def _combine_kernel(p0_ref, p1_ref, ys_hbm, x_ref, r_ref, g_ref, o_ref, buf, sems, *, tm):
    i = pl.program_id(0)
    slot = lax.rem(i, 2)

    def fetch(tile, s):
        def issue(r, c):
            _row_copy(ys_hbm, buf.at[s, 0], sems.at[s], p0_ref[tile * tm + r], r).start()
            _row_copy(ys_hbm, buf.at[s, 1], sems.at[s], p1_ref[tile * tm + r], r).start()
            return c
        lax.fori_loop(0, tm, issue, 0, unroll=8)

    @pl.when(i == 0)
    def _():
        fetch(0, 0)

    @pl.when(i + 1 < pl.num_programs(0))
    def _():
        fetch(i + 1, 1 - slot)

    _wait_rows(ys_hbm, buf.at[slot, 0], sems.at[slot])
    _wait_rows(ys_hbm, buf.at[slot, 1], sems.at[slot])
    route = r_ref[...]
    y = (route[:, 2:3] * _from_row_tiles(buf.at[slot, 0], ROW_CHUNKS)
         + route[:, 3:4] * _from_row_tiles(buf.at[slot, 1], ROW_CHUNKS))
    o_ref[...] = _rms(x_ref[...] + y, g_ref[...])


def _combine(ys, pos0, pos1, x, route, g, tm=256):
    n, d = x.shape
    return pl.pallas_call(
        functools.partial(_combine_kernel, tm=tm),
        grid_spec=pltpu.PrefetchScalarGridSpec(
            num_scalar_prefetch=2,
            grid=(n // tm,),
            in_specs=[pl.BlockSpec(memory_space=pl.ANY),
                      pl.BlockSpec((tm, d), lambda i, p0, p1: (i, 0)),
                      pl.BlockSpec((tm, LANES), lambda i, p0, p1: (i, 0)),
                      pl.BlockSpec((1, d), lambda i, p0, p1: (0, 0))],
            out_specs=pl.BlockSpec((tm, d), lambda i, p0, p1: (i, 0)),
            scratch_shapes=[pltpu.VMEM((2, 2, tm * ROW_CHUNKS, LANES), F32), pltpu.SemaphoreType.DMA((2,))]),
        out_shape=jax.ShapeDtypeStruct((n, d), F32),
        compiler_params=_cp("arbitrary"),
        name="moe_combine_norm",
    )(pos0, pos1, ys, x, route, g.reshape(1, d))


def _dispatch_plan(e1, e2, tm=EXPERT_TILE):
    n = e1.shape[0]
    n_tiles_max = (2 * n) // tm + N_EXPERTS
    rows = n_tiles_max * tm
    e_flat = jnp.concatenate([e1, e2])
    tok = jnp.concatenate([jnp.arange(n, dtype=jnp.int32)] * 2)
    onehot = (e_flat[:, None] == jnp.arange(N_EXPERTS, dtype=jnp.int32)[None, :]).astype(jnp.int32)
    csum = jnp.cumsum(onehot, axis=0)
    count = csum[-1]
    rank = jnp.sum((csum - 1) * onehot, axis=1)
    tiles_per = (count + tm - 1) // tm
    tile_end = jnp.cumsum(tiles_per)
    row_start = (tile_end - tiles_per) * tm
    dest = (row_start[e_flat] + rank).astype(jnp.int32)
    tok_of_row = jnp.zeros((rows,), jnp.int32).at[dest].set(tok)
    n_tiles = tile_end[-1].astype(jnp.int32)
    tile_id = jnp.minimum(jnp.arange(n_tiles_max, dtype=jnp.int32), n_tiles - 1)
    tile_expert = jnp.sum((tile_id[:, None] >= tile_end[None, :]).astype(jnp.int32), axis=1)
    used = jnp.cumsum((tiles_per > 0).astype(jnp.int32))
    tile_ord = (used[tile_expert] - 1).astype(jnp.int32)
    ordinals = jnp.arange(N_EXPERTS, dtype=jnp.int32)
    ord_expert = jnp.minimum(jnp.sum((used[None, :] <= ordinals[:, None]).astype(jnp.int32), axis=1),
                             N_EXPERTS - 1).astype(jnp.int32)
    counts = jnp.stack([n_tiles, used[-1]]).astype(jnp.int32)
    return tok_of_row, tile_ord, ord_expert, counts, dest[:n], dest[n:]


def _rope_tables(positions):
    half = ROT_DIM // 2
    inv = ROPE_THETA ** (-jnp.arange(half, dtype=F32) * (2.0 / ROT_DIM))
    ang = positions.astype(F32).reshape(-1, 1) * inv[None, :]
    cos, sin = jnp.cos(ang), jnp.sin(ang)
    n = ang.shape[0]
    c = jnp.concatenate([cos, cos, jnp.ones((n, LANES - ROT_DIM), F32)], axis=1)
    s1 = jnp.concatenate([-sin, jnp.zeros((n, LANES - half), F32)], axis=1)
    s2 = jnp.concatenate([jnp.zeros((n, half), F32), sin, jnp.zeros((n, LANES - ROT_DIM), F32)], axis=1)
    return c, s1, s2


def _layer(x, mem, positions, g_mix, w_in, cmp_pe_k, cmp_w1_k, cmp_w2_k, cmp_pe_v, cmp_w1_v, cmp_w2_v,
           w_br_nsa, w_br_sb, w_o, g_cross, g_mem, w_cq, w_ck, w_cv, w_co,
           g_moe, w_rg, b_rg, w_re, b_re, w_eg, w_eu, w_ed):
    b, t, d = x.shape
    n = b * t
    xf = x.reshape(n, d)

    g0 = NSA_W
    w_t = w_in.T
    w_nsa = _cast_rows(w_t, 0, g0)
    w_rest = _cast_rows(w_t, g0 + NSA_GATES, w_t.shape[0] - g0 - NSA_GATES)
    w_g = w_t[g0:g0 + NSA_GATES].reshape(NSA_KV_HEADS, 3 * NSA_GROUP, d)
    w_g = jnp.pad(w_g, ((0, 0), (0, LANES - 3 * NSA_GROUP), (0, 0))).reshape(NSA_KV_HEADS * LANES, d).astype(BF16)
    h = _rmsnorm(xf, g_mix, BF16)
    c, s1, s2 = _rope_tables(positions)
    row_spec = pl.BlockSpec((1024, LANES), lambda i, j: (i, 0))
    qkv_nsa = _mm(functools.partial(_mm_nsa_kernel, tn=256), h, w_nsa, NSA_W, 0, 1024, 256, BF16, w_t=True,
                  extra=(c, s1, s2), extra_specs=(row_spec, row_spec, row_spec), name="in_proj_nsa")
    qkv_sb = _mm(functools.partial(_mm_sb_kernel, q_tiles=SB_W // 512), h, w_rest, 3 * SB_W, 0,
                 1024, 512, BF16, w_t=True, name="in_proj_sb")
    gates = _mm(_mm_sigmoid_kernel, h, w_rest, 2 * d, 3 * SB_W // 512, 1024, 512, BF16, w_t=True,
                name="in_proj_gates")
    nsa_gates = _mm(_mm_sigmoid_kernel, h, w_g, NSA_KV_HEADS * LANES, 0, 1024, NSA_KV_HEADS * LANES, F32,
                    w_t=True, name="in_proj_nsa_gates")

    def chunked(col0):
        tok = qkv_nsa[:, col0:col0 + NSA_KV_W].reshape(b, t, NSA_KV_HEADS, HEAD_DIM).transpose(0, 2, 1, 3)
        return tok.reshape(b, NSA_KV_HEADS, t // CMP_STRIDE, CMP_STRIDE * HEAD_DIM)

    kc = _compress(chunked(NSA_Q_W), cmp_pe_k, cmp_w1_k, cmp_w2_k)
    vc = _compress(chunked(NSA_Q_W + NSA_KV_W), cmp_pe_v, cmp_w1_v, cmp_w2_v)
    o_c, sel = _nsa_cmp(qkv_nsa, kc, vc, b, t)
    o_nsa = _nsa_sel_win(qkv_nsa, sel, o_c, nsa_gates, b, t)
    o_sb = _stickbreak(qkv_sb, b, t)
    merged = _merge(o_nsa, o_sb, w_br_nsa.astype(BF16), w_br_sb.astype(BF16), gates)
    x1 = _mm(_mm_resid_kernel, merged, w_o.astype(BF16), d, 0, 1024, 512, F32,
             extra=(xf,), extra_specs=(pl.BlockSpec((1024, 512), lambda i, j: (i, j)),), name="out_proj")

    w_kv = jnp.concatenate([w_ck, w_cv], axis=1).astype(BF16)
    kv = _norm_mm(mem.reshape(-1, d), g_mem, w_kv, BF16)
    x2 = _cross(x1, g_cross, w_cq.astype(BF16), kv, w_co.astype(BF16), b, t)

    w_r = jnp.pad(jnp.concatenate([w_rg, w_re], axis=1), ((0, 0), (0, LANES - N_GROUPS - N_EXPERTS)))
    b_r = jnp.pad(jnp.concatenate([b_rg, b_re]), (0, LANES - N_GROUPS - N_EXPERTS)).reshape(1, LANES)
    w_r_hi = w_r.astype(BF16)
    w_r2 = jnp.concatenate([w_r_hi, (w_r - w_r_hi.astype(F32)).astype(BF16)], axis=1)
    hm, route = _router(x2, g_moe, w_r2, b_r)
    e1 = route[:, 0].astype(jnp.int32)
    e2 = route[:, 1].astype(jnp.int32)
    tok_of_row, tile_ord, ord_expert, counts, pos0, pos1 = _dispatch_plan(e1, e2)
    f = w_eg.shape[-1]
    ys = _experts(hm, tok_of_row, w_eg.reshape(N_EXPERTS, d, f), w_eu.reshape(N_EXPERTS, d, f),
                  w_ed.reshape(N_EXPERTS, f, d), tile_ord, ord_expert, counts)
    return ys, pos0, pos1, x2, route


def kernel(x, mem, positions, g_mix, w_in, cmp_pe_k, cmp_w1_k, cmp_w2_k, cmp_pe_v, cmp_w1_v, cmp_w2_v, w_br_nsa, w_br_sb, w_o, g_cross, g_mem, w_cq, w_ck, w_cv, w_co, g_moe, w_rg, b_rg, w_re, b_re, w_eg, w_eu, w_ed, g_final):
    b, t, d = x.shape
    assert w_in.shape[0] == 1, "one layer"
    ys, pos0, pos1, x2, route = _layer(
        x, mem, positions, g_mix[0], w_in[0], cmp_pe_k[0], cmp_w1_k[0], cmp_w2_k[0], cmp_pe_v[0], cmp_w1_v[0],
        cmp_w2_v[0], w_br_nsa[0], w_br_sb[0], w_o[0], g_cross[0], g_mem[0], w_cq[0], w_ck[0], w_cv[0], w_co[0],
        g_moe[0], w_rg[0], b_rg[0], w_re[0], b_re[0], w_eg[0], w_eu[0], w_ed[0])
    return _combine(ys, pos0, pos1, x2, route, g_final).reshape(b, t, d)
```

```python
import functools

import jax
import jax.numpy as jnp
from jax import lax
from jax.experimental import pallas as pl
from jax.experimental.pallas import tpu as pltpu

F32 = jnp.float32
BF16 = jnp.bfloat16

HEAD_DIM = 128
NSA_HEADS = 8
NSA_KV_HEADS = 2
NSA_GROUP = NSA_HEADS // NSA_KV_HEADS
SB_HEADS = 8
ROT_DIM = HEAD_DIM // 4
ROPE_THETA = 500000.0
CMP_LEN = 32
CMP_STRIDE = 16
SEL_LEN = 64
SEL_SHIFT = 6
SEL_TOPN = 16
WINDOW = 512
FORCE_BONUS = 1000.0
X_HEADS = 4
N_GROUPS = 4
EXPERTS_PER_GROUP = 8
N_EXPERTS = N_GROUPS * EXPERTS_PER_GROUP
EPS = 1e-6
SCALE = HEAD_DIM ** -0.5
NEG = -1e30
F32_EXP_UNDERFLOW = -104.0

NSA_Q_W = NSA_HEADS * HEAD_DIM
NSA_KV_W = NSA_KV_HEADS * HEAD_DIM
NSA_W = NSA_Q_W + 6 * NSA_KV_W
NSA_GATES = 3 * NSA_HEADS
SB_W = SB_HEADS * HEAD_DIM
LANES = 128
VMEM_LIMIT = 56 * 1024 * 1024

EXPERT_TILE = 256


def _cp(*sem, vmem=VMEM_LIMIT):
    return pltpu.CompilerParams(dimension_semantics=sem, vmem_limit_bytes=vmem)


def _nt_dot(a, b):
    return lax.dot_general(a, b, (((1,), (1,)), ((), ())), preferred_element_type=F32)


def _dot(a, b):
    return jnp.dot(a, b, preferred_element_type=F32)


def _split3(x):
    hi = x.astype(BF16)
    r1 = x - hi.astype(F32)
    mid = r1.astype(BF16)
    lo = (r1 - mid.astype(F32)).astype(BF16)
    return hi, mid, lo


def _rms(x, g):
    return x * lax.rsqrt(jnp.mean(x * x, axis=-1, keepdims=True) + EPS) * g


def _rmsnorm_kernel(x_ref, g_ref, o_ref):
    o_ref[...] = _rms(x_ref[...].astype(F32), g_ref[...]).astype(o_ref.dtype)


def _rmsnorm(x, g, out_dtype, tm=512):
    n, d = x.shape
    return pl.pallas_call(
        _rmsnorm_kernel,
        grid=(n // tm,),
        in_specs=[pl.BlockSpec((tm, d), lambda i: (i, 0)), pl.BlockSpec((1, d), lambda i: (0, 0))],
        out_specs=pl.BlockSpec((tm, d), lambda i: (i, 0)),
        out_shape=jax.ShapeDtypeStruct((n, d), out_dtype),
        compiler_params=_cp("parallel"),
        name="rmsnorm",
    )(x, g.reshape(1, d))


def _cast_rows_kernel(w_ref, o_ref):
    o_ref[...] = w_ref[...].astype(o_ref.dtype)


def _cast_rows(w_t, row0, n_rows, tn=512):
    k = w_t.shape[1]
    assert n_rows % tn == 0 and row0 % 8 == 0
    return pl.pallas_call(
        _cast_rows_kernel,
        grid=(n_rows // tn,),
        in_specs=[pl.BlockSpec((pl.Element(tn), pl.Element(k)), lambda j: ((row0 // 8 + j * (tn // 8)) * 8, 0))],
        out_specs=pl.BlockSpec((tn, k), lambda j: (j, 0)),
        out_shape=jax.ShapeDtypeStruct((n_rows, k), BF16),
        compiler_params=_cp("parallel"),
        name="cast_rows",
    )(w_t)


def _rope_half(x, c, s1, s2):
    return x * c + pltpu.roll(x, LANES - ROT_DIM // 2, 1) * s1 + pltpu.roll(x, ROT_DIM // 2, 1) * s2


def _mm_nsa_kernel(a_ref, w_ref, c_ref, s1_ref, s2_ref, o_ref, *, tn):
    j = pl.program_id(1)
    acc = _nt_dot(a_ref[...], w_ref[...])
    is_rope = (j < 5) | (j == 6) | (j == 8)

    @pl.when(is_rope)
    def _():
        qs = jnp.where(j < NSA_Q_W // tn, SCALE, 1.0)
        c, s1, s2 = c_ref[...] * qs, s1_ref[...] * qs, s2_ref[...] * qs
        for hh in range(tn // LANES):
            sl = slice(hh * LANES, (hh + 1) * LANES)
            o_ref[:, sl] = _rope_half(acc[:, sl], c, s1, s2).astype(o_ref.dtype)

    @pl.when(jnp.logical_not(is_rope))
    def _():
        o_ref[...] = acc.astype(o_ref.dtype)


def _mm_sb_kernel(a_ref, w_ref, o_ref, *, q_tiles):
    acc = _nt_dot(a_ref[...], w_ref[...])
    is_q = pl.program_id(1) < q_tiles
    o_ref[...] = (acc * jnp.where(is_q, SCALE, 1.0)).astype(o_ref.dtype)


def _mm_sigmoid_kernel(a_ref, w_ref, o_ref):
    o_ref[...] = jax.nn.sigmoid(_nt_dot(a_ref[...], w_ref[...])).astype(o_ref.dtype)


def _mm_resid_kernel(a_ref, w_ref, x_ref, o_ref):
    o_ref[...] = x_ref[...] + _dot(a_ref[...], w_ref[...])


def _mm(kern, a, w, n_cols, col_block0, tm, tn, out_dtype, extra=(), extra_specs=(), name="mm", w_t=False):
    m, k = a.shape
    w_spec = (pl.BlockSpec((tn, k), lambda i, j: (col_block0 + j, 0)) if w_t
              else pl.BlockSpec((k, tn), lambda i, j: (0, col_block0 + j)))
    return pl.pallas_call(
        kern,
        grid=(m // tm, n_cols // tn),
        in_specs=[pl.BlockSpec((tm, k), lambda i, j: (i, 0)), w_spec, *extra_specs],
        out_specs=pl.BlockSpec((tm, tn), lambda i, j: (i, j)),
        out_shape=jax.ShapeDtypeStruct((m, n_cols), out_dtype),
        compiler_params=_cp("parallel", "arbitrary"),
        name=name,
    )(a, w, *extra)


def _merge_kernel(a1_ref, a2_ref, w1_ref, w2_ref, ga_ref, gb_ref, o_ref):
    ya = _dot(a1_ref[...], w1_ref[...])
    yb = _dot(a2_ref[...], w2_ref[...])
    o_ref[...] = (ga_ref[...].astype(F32) * ya + gb_ref[...].astype(F32) * yb).astype(o_ref.dtype)


def _merge(o_nsa, o_sb, w1, w2, gates, tm=1024, tn=512):
    n, k = o_nsa.shape
    d = w1.shape[1]
    nb = d // tn
    return pl.pallas_call(
        _merge_kernel,
        grid=(n // tm, nb),
        in_specs=[pl.BlockSpec((tm, k), lambda i, j: (i, 0)), pl.BlockSpec((tm, k), lambda i, j: (i, 0)),
                  pl.BlockSpec((k, tn), lambda i, j: (0, j)), pl.BlockSpec((k, tn), lambda i, j: (0, j)),
                  pl.BlockSpec((tm, tn), lambda i, j: (i, j)), pl.BlockSpec((tm, tn), lambda i, j: (i, nb + j))],
        out_specs=pl.BlockSpec((tm, tn), lambda i, j: (i, j)),
        out_shape=jax.ShapeDtypeStruct((n, d), BF16),
        compiler_params=_cp("parallel", "arbitrary"),
        name="merge",
    )(o_nsa, o_sb, w1, w2, gates, gates)


def _norm_mm_kernel(x_ref, g_ref, w_ref, o_ref):
    h = _rms(x_ref[...].astype(F32), g_ref[...]).astype(BF16)
    o_ref[...] = _dot(h, w_ref[...]).astype(o_ref.dtype)


def _norm_mm(x, g, w, out_dtype, tm=256):
    n, d = x.shape
    nc = w.shape[1]
    return pl.pallas_call(
        _norm_mm_kernel,
        grid=(n // tm,),
        in_specs=[pl.BlockSpec((tm, d), lambda i: (i, 0)), pl.BlockSpec((1, d), lambda i: (0, 0)),
                  pl.BlockSpec((d, nc), lambda i: (0, 0))],
        out_specs=pl.BlockSpec((tm, nc), lambda i: (i, 0)),
        out_shape=jax.ShapeDtypeStruct((n, nc), out_dtype),
        compiler_params=_cp("parallel"),
        name="norm_mm",
    )(x, g.reshape(1, d), w)


def _compress_kernel(x_ref, pe_ref, w1_ref, w2_ref, o_ref):
    x = x_ref[...].astype(F32)
    half = x.shape[1]
    xa = (x + pe_ref[0:1, :]).astype(BF16)
    xb = (x + pe_ref[1:2, :]).astype(BF16)
    y1 = _dot(xa, w1_ref[0:half, :].astype(BF16))
    y2 = _dot(xb, w1_ref[half:2 * half, :].astype(BF16))
    pre = y1 + pltpu.roll(y2, x.shape[0] - 1, 0)
    act = pre * (1.0 / (1.0 + jnp.exp(-pre)))
    o_ref[...] = _dot(act.astype(BF16), w2_ref[...].astype(BF16)).astype(o_ref.dtype)


def _compress(tok, pe, w1, w2):
    b, hkv, nchunk, width = tok.shape
    return pl.pallas_call(
        _compress_kernel,
        grid=(b, hkv),
        in_specs=[pl.BlockSpec((None, None, nchunk, width), lambda i, j: (i, j, 0, 0)),
                  pl.BlockSpec((2, width), lambda i, j: (0, 0)),
                  pl.BlockSpec((2 * width, HEAD_DIM), lambda i, j: (0, 0)),
                  pl.BlockSpec((HEAD_DIM, HEAD_DIM), lambda i, j: (0, 0))],
        out_specs=pl.BlockSpec((None, None, nchunk, HEAD_DIM), lambda i, j: (i, j, 0, 0)),
        out_shape=jax.ShapeDtypeStruct((b, hkv, nchunk, HEAD_DIM), BF16),
        compiler_params=_cp("parallel", "parallel"),
        name="nsa_compress",
    )(tok, pe.reshape(2, width), w1, w2)


def _nsa_cmp_kernel(q_ref, kc_ref, vc_ref, oc_ref, sel_ref, *, tq, nc, ns):
    q0 = pl.program_id(2) * tq
    kc = kc_ref[...]
    vc = vc_ref[...]
    t_row = q0 + lax.broadcasted_iota(jnp.int32, (tq, nc), 0)
    c_idx = lax.broadcasted_iota(jnp.int32, (tq, nc), 1)
    vis = (c_idx * CMP_STRIDE + (CMP_LEN - 1) <= t_row) & (c_idx < nc - 1)
    imp = jnp.zeros((tq, nc), F32)
    for h in range(NSA_GROUP):
        sl = slice(h * HEAD_DIM, (h + 1) * HEAD_DIM)
        s = jnp.where(vis, _nt_dot(q_ref[:, sl], kc), NEG)
        m = jnp.max(s, axis=-1, keepdims=True)
        e = jnp.where(vis, jnp.exp(s - m), 0.0)
        den = jnp.sum(e, axis=-1, keepdims=True)
        p = e / jnp.where(den > 0, den, 1.0)
        oc_ref[:, sl] = _dot(p.astype(BF16), vc).astype(oc_ref.dtype)
        imp = imp + p
    jj = lax.broadcasted_iota(jnp.int32, (ns, nc), 0)
    cc = lax.broadcasted_iota(jnp.int32, (ns, nc), 1) * CMP_STRIDE
    ov = jnp.where((cc < (jj + 1) * SEL_LEN) & (cc + CMP_LEN > jj * SEL_LEN), 1.0, 0.0).astype(BF16)
    imp_t = sum(_nt_dot(ov, part) for part in _split3(imp))
    jb = lax.broadcasted_iota(jnp.int32, (ns, tq), 0)
    tt = q0 + lax.broadcasted_iota(jnp.int32, (ns, tq), 1)
    cur = lax.shift_right_logical(tt, jnp.full_like(tt, SEL_SHIFT))
    forced = (jb == 0) | (jb == cur) | (jb == cur - 1)
    valid = jb * SEL_LEN <= tt
    score = jnp.where(valid, imp_t + jnp.where(forced, FORCE_BONUS, 0.0), NEG)
    rank = jnp.zeros((ns, tq), F32)
    for i in range(ns):
        row = score[i:i + 1, :]
        rank = rank + jnp.where(row > score, 1.0, jnp.where((row == score) & (jb > i), 1.0, 0.0))
    sel_t = jnp.where(valid & (rank < float(min(SEL_TOPN, ns))), 1.0, 0.0).astype(BF16)
    eye = jnp.where(lax.broadcasted_iota(jnp.int32, (tq, tq), 0) == lax.broadcasted_iota(jnp.int32, (tq, tq), 1),
                    1.0, 0.0).astype(BF16)
    sel_ref[...] = _nt_dot(eye, sel_t)


def _nsa_cmp(qkv, kc, vc, b, t, tq=256):
    nc = kc.shape[2]
    ns = t // SEL_LEN
    nq = t // tq
    gw = NSA_GROUP * HEAD_DIM
    return pl.pallas_call(
        functools.partial(_nsa_cmp_kernel, tq=tq, nc=nc, ns=ns),
        grid=(b, NSA_KV_HEADS, nq),
        in_specs=[pl.BlockSpec((tq, gw), lambda i, k, j: (i * nq + j, k)),
                  pl.BlockSpec((None, None, nc, HEAD_DIM), lambda i, k, j: (i, k, 0, 0)),
                  pl.BlockSpec((None, None, nc, HEAD_DIM), lambda i, k, j: (i, k, 0, 0))],
        out_specs=[pl.BlockSpec((tq, gw), lambda i, k, j: (i * nq + j, k)),
                   pl.BlockSpec((None, None, tq, ns), lambda i, k, j: (i, k, j, 0))],
        out_shape=[jax.ShapeDtypeStruct((b * t, NSA_Q_W), BF16),
                   jax.ShapeDtypeStruct((b, NSA_KV_HEADS, t, ns), F32)],
        compiler_params=_cp("parallel", "parallel", "parallel"),
        name="nsa_cmp_select",
    )(qkv, kc, vc)


def _flash_update(s, v, bias, state, g, tq):
    m, l, acc = state
    tk = v.shape[0]
    if bias is not None:
        s = (s.reshape(g, tq, tk) + bias[None]).reshape(g * tq, tk)
    m_new = jnp.maximum(m, jnp.max(s, axis=-1, keepdims=True))
    alpha = jnp.exp(m - m_new)
    p = jnp.exp(s - m_new)
    l = alpha * l + jnp.sum(p, axis=-1, keepdims=True)
    return m_new, l, alpha * acc + _dot(p.astype(BF16), v)


def _nsa_sel_win_kernel(q_ref, ks_ref, vs_ref, kw_ref, vw_ref, sel_ref, oc_ref, g_ref, o_ref, *, tq, tk, stack, ns):
    q0 = pl.program_id(1) * tq
    g = NSA_GROUP
    nkv = NSA_KV_HEADS
    per_tile = tk // SEL_LEN
    units = [(kv, h0) for kv in range(nkv) for h0 in range(0, g, stack)]
    kv_lanes = [slice(kv * HEAD_DIM, (kv + 1) * HEAD_DIM) for kv in range(nkv)]

    def q_lanes(kv, h):
        return slice((kv * g + h) * HEAD_DIM, (kv * g + h + 1) * HEAD_DIM)

    qs = [jnp.concatenate([q_ref[:, q_lanes(kv, h0 + j)] for j in range(stack)], axis=0) for kv, h0 in units]
    sel_bias = [((sel_ref[kv] - 1.0) * (-NEG)).astype(BF16) for kv in range(nkv)]
    ej = lax.broadcasted_iota(jnp.int32, (ns, tk), 0)
    eb = lax.broadcasted_iota(jnp.int32, (ns, tk), 1)
    eb = lax.shift_right_logical(eb, jnp.full_like(eb, SEL_SHIFT))

    def init():
        return tuple((jnp.full((stack * tq, 1), NEG, F32), jnp.zeros((stack * tq, 1), F32),
                      jnp.zeros((stack * tq, HEAD_DIM), F32)) for _ in units)

    def flash_all(k_ref, v_ref, k0, width, biases, states):
        scores = [_nt_dot(qs[u], k_ref[pl.ds(k0, width), kv_lanes[kv]]) for u, (kv, _) in enumerate(units)]
        return tuple(_flash_update(scores[u], v_ref[pl.ds(k0, width), kv_lanes[kv]], biases[kv], states[u], stack, tq)
                     for u, (kv, _) in enumerate(units))

    def sel_step(kt, states, causal):
        k0 = pl.multiple_of(kt * tk, tk)
        expand = jnp.where(ej == eb + kt * per_tile, 1.0, 0.0).astype(BF16)
        biases = [_dot(sel_bias[kv], expand) for kv in range(nkv)]
        if causal is not None:
            biases = [bias + causal for bias in biases]
        return flash_all(ks_ref, vs_ref, k0, tk, biases, states)

    kt_d = q0 // tk
    t_row = q0 + lax.broadcasted_iota(jnp.int32, (tq, tk), 0)
    col = kt_d * tk + lax.broadcasted_iota(jnp.int32, (tq, tk), 1)
    states = sel_step(kt_d, init(), jnp.where(col <= t_row, 0.0, NEG))
    states = lax.fori_loop(0, kt_d, lambda kt, st: sel_step(kt, st, None), states)
    o_s = [acc * (1.0 / l) for (_, l, acc) in states]

    def win_step(k0, states, bias):
        return flash_all(kw_ref, vw_ref, k0, tq, [bias] * nkv, states)

    ri = lax.broadcasted_iota(jnp.int32, (tq, tq), 0)
    ci = lax.broadcasted_iota(jnp.int32, (tq, tq), 1)
    states = win_step(pl.multiple_of(q0, tq), init(), jnp.where(ci <= ri, 0.0, NEG))
    n_full = jnp.minimum(WINDOW // tq - 1, q0 // tq)
    states = lax.fori_loop(0, n_full, lambda i, st: win_step(pl.multiple_of(q0 - (i + 1) * tq, tq), st, None),
                           states)
    edge_bias = jnp.where(ci > ri, 0.0, NEG) + jnp.where(q0 >= WINDOW, 0.0, NEG)
    states = win_step(pl.multiple_of(jnp.maximum(q0 - WINDOW, 0), tq), states, edge_bias)
    o_w = [acc * (1.0 / l) for (_, l, acc) in states]

    for u, (kv, h0) in enumerate(units):
        gates = g_ref[:, kv * LANES:(kv + 1) * LANES]
        for j in range(stack):
            h = h0 + j
            sl = q_lanes(kv, h)
            rows = slice(j * tq, (j + 1) * tq)
            out = (gates[:, 3 * h:3 * h + 1] * oc_ref[:, sl].astype(F32)
                   + gates[:, 3 * h + 1:3 * h + 2] * o_s[u][rows] + gates[:, 3 * h + 2:3 * h + 3] * o_w[u][rows])
            o_ref[:, sl] = out.astype(o_ref.dtype)


def _nsa_sel_win(qkv, sel, oc, gates, b, t, tq=128, tk=256, stack=4):
    nq = t // tq
    ns = t // SEL_LEN
    kv_block = NSA_Q_W // NSA_KV_W

    def kv_spec(which):
        return pl.BlockSpec((t, NSA_KV_W), lambda i, j: (i, kv_block + which))

    return pl.pallas_call(
        functools.partial(_nsa_sel_win_kernel, tq=tq, tk=tk, stack=stack, ns=ns),
        grid=(b, nq),
        in_specs=[pl.BlockSpec((tq, NSA_Q_W), lambda i, j: (i * nq + j, 0)),
                  kv_spec(2), kv_spec(3), kv_spec(4), kv_spec(5),
                  pl.BlockSpec((None, NSA_KV_HEADS, tq, ns), lambda i, j: (i, 0, j, 0)),
                  pl.BlockSpec((tq, NSA_Q_W), lambda i, j: (i * nq + j, 0)),
                  pl.BlockSpec((tq, NSA_KV_HEADS * LANES), lambda i, j: (i * nq + j, 0))],
        out_specs=pl.BlockSpec((tq, NSA_Q_W), lambda i, j: (i * nq + j, 0)),
        out_shape=jax.ShapeDtypeStruct((b * t, NSA_Q_W), BF16),
        compiler_params=_cp("parallel", "parallel"),
        name="nsa_select_window",
    )(qkv, qkv, qkv, qkv, qkv, sel, oc, gates)


def _stickbreak_kernel(q_ref, k_ref, v_ref, o_ref, *, tile, nh):
    q0 = pl.program_id(2) * tile
    ri = lax.broadcasted_iota(jnp.int32, (tile, tile), 0)
    ci = lax.broadcasted_iota(jnp.int32, (tile, tile), 1)
    strict = ci < ri
    tri = jnp.where(ri > ci, 1.0, 0.0).astype(BF16)
    lanes = [slice(h * HEAD_DIM, (h + 1) * HEAD_DIM) for h in range(nh)]

    def step(k0, diag, carry):
        zs = [_nt_dot(q_ref[:, lanes[h]], k_ref[pl.ds(k0, tile), lanes[h]]) for h in range(nh)]
        log_bs, sufs, sums = [], [], []
        for z in zs:
            log_b = jnp.minimum(z, 0.0) - jnp.log(1.0 + jnp.exp(-jnp.abs(z)))
            log_1m = log_b - z
            if diag:
                log_1m = jnp.where(strict, log_1m, 0.0)
            hi = log_1m.astype(BF16)
            lo = (log_1m - hi.astype(F32)).astype(BF16)
            log_bs.append(log_b)
            sufs.append(_dot(hi, tri) + _dot(lo, tri))
            sums.append(jnp.sum(log_1m, axis=-1, keepdims=True))
        out = []
        for h in range(nh):
            later, acc = carry[2 * h], carry[2 * h + 1]
            a = jnp.exp(log_bs[h] + (sufs[h] + later))
            if diag:
                a = jnp.where(strict, a, 0.0)
            out.extend((later + sums[h], acc + _dot(a.astype(BF16), v_ref[pl.ds(k0, tile), lanes[h]])))
        return tuple(out)

    zero = (jnp.zeros((tile, 1), F32), jnp.zeros((tile, HEAD_DIM), F32)) * nh
    carry = step(pl.multiple_of(q0, tile), True, zero)

    def live(state):
        i, c = state
        top = c[0]
        for h in range(1, nh):
            top = jnp.maximum(top, c[2 * h])
        return (i < q0 // tile) & (jnp.max(top) > F32_EXP_UNDERFLOW)

    def advance(state):
        i, c = state
        return i + 1, step(pl.multiple_of(q0 - (i + 1) * tile, tile), False, c)

    _, carry = lax.while_loop(live, advance, (jnp.int32(0), carry))
    for h in range(nh):
        o_ref[:, lanes[h]] = carry[2 * h + 1].astype(o_ref.dtype)


def _stickbreak(qkv, b, t, tile=256, nh=4):
    nq = t // tile
    w = nh * HEAD_DIM
    nhb = SB_HEADS // nh
    return pl.pallas_call(
        functools.partial(_stickbreak_kernel, tile=tile, nh=nh),
        grid=(b, nhb, nq),
        in_specs=[pl.BlockSpec((tile, w), lambda i, h, j: (i * nq + j, h)),
                  pl.BlockSpec((t, w), lambda i, h, j: (i, nhb + h)),
                  pl.BlockSpec((t, w), lambda i, h, j: (i, 2 * nhb + h))],
        out_specs=pl.BlockSpec((tile, w), lambda i, h, j: (i * nq + j, h)),
        out_shape=jax.ShapeDtypeStruct((b * t, SB_W), BF16),
        compiler_params=_cp("parallel", "parallel", "parallel"),
        name="stickbreak",
    )(qkv, qkv, qkv)


def _cross_kernel(x_ref, g_ref, wq_ref, kv_ref, wo_ref, o_ref):
    x = x_ref[...]
    h = _rms(x, g_ref[...]).astype(BF16)
    q = _dot(h, wq_ref[...]).astype(BF16)
    xw = X_HEADS * HEAD_DIM
    outs = []
    for hd in range(X_HEADS):
        sl = slice(hd * HEAD_DIM, (hd + 1) * HEAD_DIM)
        s = _nt_dot(q[:, sl], kv_ref[:, sl]) * SCALE
        e = jnp.exp(s - jnp.max(s, axis=-1, keepdims=True))
        p = e / jnp.sum(e, axis=-1, keepdims=True)
        outs.append(_dot(p.astype(BF16), kv_ref[:, xw + hd * HEAD_DIM:xw + (hd + 1) * HEAD_DIM]).astype(BF16))
    o_ref[...] = x + _dot(jnp.concatenate(outs, axis=1), wo_ref[...])


def _cross(x, g, wq, kv, wo, b, t, tq=512):
    n, d = x.shape
    nq = t // tq
    mem_len = kv.shape[0] // b
    return pl.pallas_call(
        _cross_kernel,
        grid=(b, nq),
        in_specs=[pl.BlockSpec((tq, d), lambda i, j: (i * nq + j, 0)),
                  pl.BlockSpec((1, d), lambda i, j: (0, 0)),
                  pl.BlockSpec(wq.shape, lambda i, j: (0, 0)),
                  pl.BlockSpec((mem_len, kv.shape[1]), lambda i, j: (i, 0)),
                  pl.BlockSpec(wo.shape, lambda i, j: (0, 0))],
        out_specs=pl.BlockSpec((tq, d), lambda i, j: (i * nq + j, 0)),
        out_shape=jax.ShapeDtypeStruct((n, d), F32),
        compiler_params=_cp("parallel", "parallel"),
        name="cross_attention",
    )(x, g.reshape(1, d), wq, kv, wo)


def _to_row_tiles(ref, val):
    rows, width = val.shape
    c_per = width // LANES
    for c in range(c_per):
        ref[pl.ds(c, rows, stride=c_per), :] = val[:, c * LANES:(c + 1) * LANES]


def _from_row_tiles(ref, c_per):
    rows = ref.shape[0] // c_per
    return jnp.concatenate([ref[pl.ds(c, rows, stride=c_per), :] for c in range(c_per)], axis=1)


def _router_kernel(x_ref, g_ref, w_ref, b_ref, h_ref, r_ref):
    h = _rms(x_ref[...], g_ref[...])
    _to_row_tiles(h_ref, h)
    h_hi = h.astype(BF16)
    h_mid = (h - h_hi.astype(F32)).astype(BF16)
    two = _dot(h_hi, w_ref[...])
    logits = two[:, :LANES] + two[:, LANES:] + _dot(h_mid, w_ref[:, :LANES]) + b_ref[...]
    lane = lax.broadcasted_iota(jnp.int32, logits.shape, 1).astype(F32)
    big = float(LANES)
    is_g = lane < N_GROUPS
    lg = jnp.where(is_g, logits, NEG)
    mg = jnp.max(lg, axis=-1, keepdims=True)
    p_gtop = 1.0 / jnp.sum(jnp.where(is_g, jnp.exp(lg - mg), 0.0), axis=-1, keepdims=True)
    g_top = jnp.min(jnp.where(lg == mg, lane, big), axis=-1, keepdims=True)
    lo = N_GROUPS + EXPERTS_PER_GROUP * g_top
    in_grp = (lane >= lo) & (lane < lo + EXPERTS_PER_GROUP)
    le = jnp.where(in_grp, logits, NEG)
    ee = jnp.where(in_grp, jnp.exp(le - jnp.max(le, axis=-1, keepdims=True)), 0.0)
    p_in = ee / jnp.sum(ee, axis=-1, keepdims=True)
    p1 = jnp.where(in_grp, p_in, -1.0)
    m1 = jnp.max(p1, axis=-1, keepdims=True)
    i1 = jnp.min(jnp.where(p1 == m1, lane, big), axis=-1, keepdims=True)
    p2 = jnp.where(lane == i1, -1.0, p1)
    m2 = jnp.max(p2, axis=-1, keepdims=True)
    i2 = jnp.min(jnp.where(p2 == m2, lane, big), axis=-1, keepdims=True)
    tot = m1 + m2
    r_ref[...] = jnp.where(lane == 0, i1 - N_GROUPS,
                           jnp.where(lane == 1, i2 - N_GROUPS,
                                     jnp.where(lane == 2, m1 / tot * p_gtop,
                                               jnp.where(lane == 3, m2 / tot * p_gtop, 0.0))))


def _router(x, g, w_r, b_r, tm=512):
    n, d = x.shape
    return pl.pallas_call(
        _router_kernel,
        grid=(n // tm,),
        in_specs=[pl.BlockSpec((tm, d), lambda i: (i, 0)), pl.BlockSpec((1, d), lambda i: (0, 0)),
                  pl.BlockSpec((d, 2 * LANES), lambda i: (0, 0)), pl.BlockSpec((1, LANES), lambda i: (0, 0))],
        out_specs=[pl.BlockSpec((tm * (d // LANES), LANES), lambda i: (i, 0)),
                   pl.BlockSpec((tm, LANES), lambda i: (i, 0))],
        out_shape=[jax.ShapeDtypeStruct((n * (d // LANES), LANES), F32), jax.ShapeDtypeStruct((n, LANES), F32)],
        compiler_params=_cp("parallel"),
        name="moe_router",
    )(x, g.reshape(1, d), w_r, b_r)


ROW_CHUNKS = 16
WEIGHT_DMA_PRIORITY = 1


def _wait_rows(src_hbm, dst, sem):
    pltpu.make_async_copy(src_hbm.at[pl.ds(0, dst.shape[0]), :], dst, sem).wait()


def _row_copy(src_hbm, dst, sem, src_row, dst_row):
    c = ROW_CHUNKS
    return pltpu.make_async_copy(src_hbm.at[pl.ds(pl.multiple_of(src_row * c, c), c), :],
                                 dst.at[pl.ds(pl.multiple_of(dst_row * c, c), c), :], sem)


def _expert_kernel(idx_ref, ord_ref, oe_ref, cnt_ref, src_hbm, wg_hbm, wu_hbm, wd_hbm, o_ref,
                   xbuf, sems, wg_f, wu_f, wd_f, wsems, wg_s, wu_s, wd_s, *, tm):
    i = pl.program_id(0)
    nt = cnt_ref[0]
    n_ord = cnt_ref[1]
    live = i < nt
    slot = lax.rem(i, 2)

    def gather(tile, s):
        def issue(r, c):
            _row_copy(src_hbm, xbuf.at[s], sems.at[s], idx_ref[tile * tm + r], r).start()
            return c
        lax.fori_loop(0, tm, issue, 0, unroll=8)

    def weight_copies(k, s):
        e = oe_ref[k]
        return (pltpu.make_async_copy(wg_hbm.at[e], wg_f.at[s], wsems.at[s, 0]),
                pltpu.make_async_copy(wu_hbm.at[e], wu_f.at[s], wsems.at[s, 1]),
                pltpu.make_async_copy(wd_hbm.at[e], wd_f.at[s], wsems.at[s, 2]))

    @pl.when(i == 0)
    def _():
        gather(0, 0)
        for cp in weight_copies(0, 0):
            cp.start(priority=WEIGHT_DMA_PRIORITY)

    @pl.when(i + 1 < nt)
    def _():
        gather(i + 1, 1 - slot)

    k = ord_ref[i]
    fresh = live & ((i == 0) | (k != ord_ref[jnp.maximum(i - 1, 0)]))
    ws = lax.rem(k, 2)

    @pl.when(fresh)
    def _():
        for cp in weight_copies(k, ws):
            cp.wait()

        @pl.when(k + 1 < n_ord)
        def _():
            for cp in weight_copies(k + 1, 1 - ws):
                cp.start(priority=WEIGHT_DMA_PRIORITY)

        wg_s[...] = wg_f[ws].astype(BF16)
        wu_s[...] = wu_f[ws].astype(BF16)
        wd_s[...] = wd_f[ws].astype(BF16)

    @pl.when(live)
    def _():
        _wait_rows(src_hbm, xbuf.at[slot], sems.at[slot])
        x = _from_row_tiles(xbuf.at[slot], ROW_CHUNKS).astype(BF16)
        a = _dot(x, wg_s[...])
        u = _dot(x, wu_s[...])
        act = a * (1.0 / (1.0 + jnp.exp(-a))) * u
        _to_row_tiles(o_ref, _dot(act.astype(BF16), wd_s[...]))

    @pl.when(jnp.logical_not(live))
    def _():
        o_ref[...] = jnp.zeros_like(o_ref)


def _experts(src, tok_of_row, w_g, w_u, w_d, tile_ord, ord_expert, counts, tm=EXPERT_TILE):
    c = ROW_CHUNKS
    n_tiles_max = tok_of_row.shape[0] // tm
    d = c * LANES
    f = w_g.shape[2]
    any_spec = pl.BlockSpec(memory_space=pl.ANY)
    return pl.pallas_call(
        functools.partial(_expert_kernel, tm=tm),
        grid_spec=pltpu.PrefetchScalarGridSpec(
            num_scalar_prefetch=4,
            grid=(n_tiles_max,),
            in_specs=[any_spec, any_spec, any_spec, any_spec],
            out_specs=pl.BlockSpec((tm * c, LANES), lambda i, *_: (i, 0)),
            scratch_shapes=[pltpu.VMEM((2, tm * c, LANES), F32), pltpu.SemaphoreType.DMA((2,)),
                            pltpu.VMEM((2, d, f), F32), pltpu.VMEM((2, d, f), F32), pltpu.VMEM((2, f, d), F32),
                            pltpu.SemaphoreType.DMA((2, 3)),
                            pltpu.VMEM((d, f), BF16), pltpu.VMEM((d, f), BF16), pltpu.VMEM((f, d), BF16)]),
        out_shape=jax.ShapeDtypeStruct((n_tiles_max * tm * c, LANES), F32),
        compiler_params=_cp("arbitrary"),
        name="moe_experts",
    )(tok_of_row, tile_ord, ord_expert, counts, src, w_g, w_u, w_d)


def _combine_kernel(p0_ref, p1_ref, ys_hbm, x_ref, r_ref, g_ref, o_ref, buf, sems, *, tm):
    i = pl.program_id(0)
    slot = lax.rem(i, 2)

    def fetch(tile, s):
        def issue(r, c):
            _row_copy(ys_hbm, buf.at[s, 0], sems.at[s], p0_ref[tile * tm + r], r).start()
            _row_copy(ys_hbm, buf.at[s, 1], sems.at[s], p1_ref[tile * tm + r], r).start(priority=1)
            return c
        lax.fori_loop(0, tm, issue, 0, unroll=8)

    @pl.when(i == 0)
    def _():
        fetch(0, 0)

    @pl.when(i + 1 < pl.num_programs(0))
    def _():
        fetch(i + 1, 1 - slot)

    _wait_rows(ys_hbm, buf.at[slot, 0], sems.at[slot])
    _wait_rows(ys_hbm, buf.at[slot, 1], sems.at[slot])
    route = r_ref[...]
    y = (route[:, 2:3] * _from_row_tiles(buf.at[slot, 0], ROW_CHUNKS)
         + route[:, 3:4] * _from_row_tiles(buf.at[slot, 1], ROW_CHUNKS))
    o_ref[...] = _rms(x_ref[...] + y, g_ref[...])


def _combine(ys, pos0, pos1, x, route, g, tm=256):
    n, d = x.shape
    return pl.pallas_call(
        functools.partial(_combine_kernel, tm=tm),
        grid_spec=pltpu.PrefetchScalarGridSpec(
            num_scalar_prefetch=2,
            grid=(n // tm,),
            in_specs=[pl.BlockSpec(memory_space=pl.ANY),
                      pl.BlockSpec((tm, d), lambda i, p0, p1: (i, 0)),
                      pl.BlockSpec((tm, LANES), lambda i, p0, p1: (i, 0)),
                      pl.BlockSpec((1, d), lambda i, p0, p1: (0, 0))],
            out_specs=pl.BlockSpec((tm, d), lambda i, p0, p1: (i, 0)),
            scratch_shapes=[pltpu.VMEM((2, 2, tm * ROW_CHUNKS, LANES), F32), pltpu.SemaphoreType.DMA((2,))]),
        out_shape=jax.ShapeDtypeStruct((n, d), F32),
        compiler_params=_cp("arbitrary"),
        name="moe_combine_norm",
    )(pos0, pos1, ys, x, route, g.reshape(1, d))


def _dispatch_plan(e1, e2, tm=EXPERT_TILE):
    n = e1.shape[0]
    n_tiles_max = (2 * n) // tm + N_EXPERTS
    rows = n_tiles_max * tm
    e_flat = jnp.concatenate([e1, e2])
    tok = jnp.concatenate([jnp.arange(n, dtype=jnp.int32)] * 2)
    onehot = (e_flat[:, None] == jnp.arange(N_EXPERTS, dtype=jnp.int32)[None, :]).astype(jnp.int32)
    csum = jnp.cumsum(onehot, axis=0)
    count = csum[-1]
    rank = jnp.sum((csum - 1) * onehot, axis=1)
    tiles_per = (count + tm - 1) // tm
    tile_end = jnp.cumsum(tiles_per)
    row_start = (tile_end - tiles_per) * tm
    dest = (row_start[e_flat] + rank).astype(jnp.int32)
    tok_of_row = jnp.zeros((rows,), jnp.int32).at[dest].set(tok)
    n_tiles = tile_end[-1].astype(jnp.int32)
    tile_id = jnp.minimum(jnp.arange(n_tiles_max, dtype=jnp.int32), n_tiles - 1)
    tile_expert = jnp.sum((tile_id[:, None] >= tile_end[None, :]).astype(jnp.int32), axis=1)
    used = jnp.cumsum((tiles_per > 0).astype(jnp.int32))
    tile_ord = (used[tile_expert] - 1).astype(jnp.int32)
    ordinals = jnp.arange(N_EXPERTS, dtype=jnp.int32)
    ord_expert = jnp.minimum(jnp.sum((used[None, :] <= ordinals[:, None]).astype(jnp.int32), axis=1),
                             N_EXPERTS - 1).astype(jnp.int32)
    counts = jnp.stack([n_tiles, used[-1]]).astype(jnp.int32)
    return tok_of_row, tile_ord, ord_expert, counts, dest[:n], dest[n:]


def _rope_tables(positions):
    half = ROT_DIM // 2
    inv = ROPE_THETA ** (-jnp.arange(half, dtype=F32) * (2.0 / ROT_DIM))
    ang = positions.astype(F32).reshape(-1, 1) * inv[None, :]
    cos, sin = jnp.cos(ang), jnp.sin(ang)
    n = ang.shape[0]
    c = jnp.concatenate([cos, cos, jnp.ones((n, LANES - ROT_DIM), F32)], axis=1)
    s1 = jnp.concatenate([-sin, jnp.zeros((n, LANES - half), F32)], axis=1)
    s2 = jnp.concatenate([jnp.zeros((n, half), F32), sin, jnp.zeros((n, LANES - ROT_DIM), F32)], axis=1)
    return c, s1, s2


def _layer(x, mem, positions, g_mix, w_in, cmp_pe_k, cmp_w1_k, cmp_w2_k, cmp_pe_v, cmp_w1_v, cmp_w2_v,
           w_br_nsa, w_br_sb, w_o, g_cross, g_mem, w_cq, w_ck, w_cv, w_co,
           g_moe, w_rg, b_rg, w_re, b_re, w_eg, w_eu, w_ed):
    b, t, d = x.shape
    n = b * t
    xf = x.reshape(n, d)

    g0 = NSA_W
    w_t = w_in.T
    w_nsa = _cast_rows(w_t, 0, g0)
    w_rest = _cast_rows(w_t, g0 + NSA_GATES, w_t.shape[0] - g0 - NSA_GATES)
    w_g = w_t[g0:g0 + NSA_GATES].reshape(NSA_KV_HEADS, 3 * NSA_GROUP, d)
    w_g = jnp.pad(w_g, ((0, 0), (0, LANES - 3 * NSA_GROUP), (0, 0))).reshape(NSA_KV_HEADS * LANES, d).astype(BF16)
    h = _rmsnorm(xf, g_mix, BF16)
    c, s1, s2 = _rope_tables(positions)
    row_spec = pl.BlockSpec((1024, LANES), lambda i, j: (i, 0))
    qkv_nsa = _mm(functools.partial(_mm_nsa_kernel, tn=256), h, w_nsa, NSA_W, 0, 1024, 256, BF16, w_t=True,
                  extra=(c, s1, s2), extra_specs=(row_spec, row_spec, row_spec), name="in_proj_nsa")
    qkv_sb = _mm(functools.partial(_mm_sb_kernel, q_tiles=SB_W // 512), h, w_rest, 3 * SB_W, 0,
                 1024, 512, BF16, w_t=True, name="in_proj_sb")
    gates = _mm(_mm_sigmoid_kernel, h, w_rest, 2 * d, 3 * SB_W // 512, 1024, 512, BF16, w_t=True,
                name="in_proj_gates")
    nsa_gates = _mm(_mm_sigmoid_kernel, h, w_g, NSA_KV_HEADS * LANES, 0, 1024, NSA_KV_HEADS * LANES, F32,
                    w_t=True, name="in_proj_nsa_gates")

    def chunked(col0):
        tok = qkv_nsa[:, col0:col0 + NSA_KV_W].reshape(b, t, NSA_KV_HEADS, HEAD_DIM).transpose(0, 2, 1, 3)
        return tok.reshape(b, NSA_KV_HEADS, t // CMP_STRIDE, CMP_STRIDE * HEAD_DIM)

    kc = _compress(chunked(NSA_Q_W), cmp_pe_k, cmp_w1_k, cmp_w2_k)
    vc = _compress(chunked(NSA_Q_W + NSA_KV_W), cmp_pe_v, cmp_w1_v, cmp_w2_v)
    o_c, sel = _nsa_cmp(qkv_nsa, kc, vc, b, t)
    o_nsa = _nsa_sel_win(qkv_nsa, sel, o_c, nsa_gates, b, t)
    o_sb = _stickbreak(qkv_sb, b, t)
    merged = _merge(o_nsa, o_sb, w_br_nsa.astype(BF16), w_br_sb.astype(BF16), gates)
    x1 = _mm(_mm_resid_kernel, merged, w_o.astype(BF16), d, 0, 1024, 512, F32,
             extra=(xf,), extra_specs=(pl.BlockSpec((1024, 512), lambda i, j: (i, j)),), name="out_proj")

    w_kv = jnp.concatenate([w_ck, w_cv], axis=1).astype(BF16)
    kv = _norm_mm(mem.reshape(-1, d), g_mem, w_kv, BF16)
    x2 = _cross(x1, g_cross, w_cq.astype(BF16), kv, w_co.astype(BF16), b, t)

    w_r = jnp.pad(jnp.concatenate([w_rg, w_re], axis=1), ((0, 0), (0, LANES - N_GROUPS - N_EXPERTS)))
    b_r = jnp.pad(jnp.concatenate([b_rg, b_re]), (0, LANES - N_GROUPS - N_EXPERTS)).reshape(1, LANES)
    w_r_hi = w_r.astype(BF16)
    w_r2 = jnp.concatenate([w_r_hi, (w_r - w_r_hi.astype(F32)).astype(BF16)], axis=1)
    hm, route = _router(x2, g_moe, w_r2, b_r)
    e1 = route[:, 0].astype(jnp.int32)
    e2 = route[:, 1].astype(jnp.int32)
    tok_of_row, tile_ord, ord_expert, counts, pos0, pos1 = _dispatch_plan(e1, e2)
    f = w_eg.shape[-1]
    ys = _experts(hm, tok_of_row, w_eg.reshape(N_EXPERTS, d, f), w_eu.reshape(N_EXPERTS, d, f),
                  w_ed.reshape(N_EXPERTS, f, d), tile_ord, ord_expert, counts)
    return ys, pos0, pos1, x2, route


def kernel(x, mem, positions, g_mix, w_in, cmp_pe_k, cmp_w1_k, cmp_w2_k, cmp_pe_v, cmp_w1_v, cmp_w2_v, w_br_nsa, w_br_sb, w_o, g_cross, g_mem, w_cq, w_ck, w_cv, w_co, g_moe, w_rg, b_rg, w_re, b_re, w_eg, w_eu, w_ed, g_final):
    b, t, d = x.shape
    assert w_in.shape[0] == 1, "one layer"
    ys, pos0, pos1, x2, route = _layer(
        x, mem, positions, g_mix[0], w_in[0], cmp_pe_k[0], cmp_w1_k[0], cmp_w2_k[0], cmp_pe_v[0], cmp_w1_v[0],
        cmp_w2_v[0], w_br_nsa[0], w_br_sb[0], w_o[0], g_cross[0], g_mem[0], w_cq[0], w_ck[0], w_cv[0], w_co[0],
        g_moe[0], w_rg[0], b_rg[0], w_re[0], b_re[0], w_eg[0], w_eu[0], w_ed[0])
    return _combine(ys, pos0, pos1, x2, route, g_final).reshape(b, t, d)
```

```python
import functools

import jax
import jax.numpy as jnp
from jax import lax
from jax.experimental import pallas as pl
from jax.experimental.pallas import tpu as pltpu

F32 = jnp.float32
BF16 = jnp.bfloat16

HEAD_DIM = 128
NSA_HEADS = 8
NSA_KV_HEADS = 2
NSA_GROUP = NSA_HEADS // NSA_KV_HEADS
SB_HEADS = 8
ROT_DIM = HEAD_DIM // 4
ROPE_THETA = 500000.0
CMP_LEN = 32
CMP_STRIDE = 16
SEL_LEN = 64
SEL_SHIFT = 6
SEL_TOPN = 16
WINDOW = 512
FORCE_BONUS = 1000.0
X_HEADS = 4
N_GROUPS = 4
EXPERTS_PER_GROUP = 8
N_EXPERTS = N_GROUPS * EXPERTS_PER_GROUP
EPS = 1e-6
SCALE = HEAD_DIM ** -0.5
NEG = -1e30
F32_EXP_UNDERFLOW = -104.0

NSA_Q_W = NSA_HEADS * HEAD_DIM
NSA_KV_W = NSA_KV_HEADS * HEAD_DIM
NSA_W = NSA_Q_W + 6 * NSA_KV_W
NSA_GATES = 3 * NSA_HEADS
SB_W = SB_HEADS * HEAD_DIM
LANES = 128
VMEM_LIMIT = 56 * 1024 * 1024

EXPERT_TILE = 256


def _cp(*sem, vmem=VMEM_LIMIT):
    return pltpu.CompilerParams(dimension_semantics=sem, vmem_limit_bytes=vmem)


def _nt_dot(a, b):
    return lax.dot_general(a, b, (((1,), (1,)), ((), ())), preferred_element_type=F32)


def _dot(a, b):
    return jnp.dot(a, b, preferred_element_type=F32)


def _split3(x):
    hi = x.astype(BF16)
    r1 = x - hi.astype(F32)
    mid = r1.astype(BF16)
    lo = (r1 - mid.astype(F32)).astype(BF16)
    return hi, mid, lo


def _rms(x, g):
    return x * lax.rsqrt(jnp.mean(x * x, axis=-1, keepdims=True) + EPS) * g


def _rmsnorm_kernel(x_ref, g_ref, o_ref):
    o_ref[...] = _rms(x_ref[...].astype(F32), g_ref[...]).astype(o_ref.dtype)


def _rmsnorm(x, g, out_dtype, tm=512):
    n, d = x.shape
    return pl.pallas_call(
        _rmsnorm_kernel,
        grid=(n // tm,),
        in_specs=[pl.BlockSpec((tm, d), lambda i: (i, 0)), pl.BlockSpec((1, d), lambda i: (0, 0))],
        out_specs=pl.BlockSpec((tm, d), lambda i: (i, 0)),
        out_shape=jax.ShapeDtypeStruct((n, d), out_dtype),
        compiler_params=_cp("parallel"),
        name="rmsnorm",
    )(x, g.reshape(1, d))


def _cast_rows_kernel(w_ref, o_ref):
    o_ref[...] = w_ref[...].astype(o_ref.dtype)


def _cast_rows(w_t, row0, n_rows, tn=512):
    k = w_t.shape[1]
    assert n_rows % tn == 0 and row0 % 8 == 0
    return pl.pallas_call(
        _cast_rows_kernel,
        grid=(n_rows // tn,),
        in_specs=[pl.BlockSpec((pl.Element(tn), pl.Element(k)), lambda j: ((row0 // 8 + j * (tn // 8)) * 8, 0))],
        out_specs=pl.BlockSpec((tn, k), lambda j: (j, 0)),
        out_shape=jax.ShapeDtypeStruct((n_rows, k), BF16),
        compiler_params=_cp("parallel"),
        name="cast_rows",
    )(w_t)


def _rope_half(x, c, s1, s2):
    return x * c + pltpu.roll(x, LANES - ROT_DIM // 2, 1) * s1 + pltpu.roll(x, ROT_DIM // 2, 1) * s2


def _mm_nsa_kernel(a_ref, w_ref, c_ref, s1_ref, s2_ref, o_ref, *, tn):
    j = pl.program_id(1)
    acc = _nt_dot(a_ref[...], w_ref[...])
    is_rope = (j < 5) | (j == 6) | (j == 8)

    @pl.when(is_rope)
    def _():
        qs = jnp.where(j < NSA_Q_W // tn, SCALE, 1.0)
        c, s1, s2 = c_ref[...] * qs, s1_ref[...] * qs, s2_ref[...] * qs
        for hh in range(tn // LANES):
            sl = slice(hh * LANES, (hh + 1) * LANES)
            o_ref[:, sl] = _rope_half(acc[:, sl], c, s1, s2).astype(o_ref.dtype)

    @pl.when(jnp.logical_not(is_rope))
    def _():
        o_ref[...] = acc.astype(o_ref.dtype)


def _mm_sb_kernel(a_ref, w_ref, o_ref, *, q_tiles):
    acc = _nt_dot(a_ref[...], w_ref[...])
    is_q = pl.program_id(1) < q_tiles
    o_ref[...] = (acc * jnp.where(is_q, SCALE, 1.0)).astype(o_ref.dtype)


def _mm_sigmoid_kernel(a_ref, w_ref, o_ref):
    o_ref[...] = jax.nn.sigmoid(_nt_dot(a_ref[...], w_ref[...])).astype(o_ref.dtype)


def _mm_resid_kernel(a_ref, w_ref, x_ref, o_ref):
    o_ref[...] = x_ref[...] + _dot(a_ref[...], w_ref[...])


def _mm(kern, a, w, n_cols, col_block0, tm, tn, out_dtype, extra=(), extra_specs=(), name="mm", w_t=False):
    m, k = a.shape
    w_spec = (pl.BlockSpec((tn, k), lambda i, j: (col_block0 + j, 0)) if w_t
              else pl.BlockSpec((k, tn), lambda i, j: (0, col_block0 + j)))
    return pl.pallas_call(
        kern,
        grid=(m // tm, n_cols // tn),
        in_specs=[pl.BlockSpec((tm, k), lambda i, j: (i, 0)), w_spec, *extra_specs],
        out_specs=pl.BlockSpec((tm, tn), lambda i, j: (i, j)),
        out_shape=jax.ShapeDtypeStruct((m, n_cols), out_dtype),
        compiler_params=_cp("parallel", "arbitrary"),
        name=name,
    )(a, w, *extra)


def _merge_kernel(a1_ref, a2_ref, w1_ref, w2_ref, ga_ref, gb_ref, o_ref):
    ya = _dot(a1_ref[...], w1_ref[...])
    yb = _dot(a2_ref[...], w2_ref[...])
    o_ref[...] = (ga_ref[...].astype(F32) * ya + gb_ref[...].astype(F32) * yb).astype(o_ref.dtype)


def _merge(o_nsa, o_sb, w1, w2, gates, tm=1024, tn=512):
    n, k = o_nsa.shape
    d = w1.shape[1]
    nb = d // tn
    return pl.pallas_call(
        _merge_kernel,
        grid=(n // tm, nb),
        in_specs=[pl.BlockSpec((tm, k), lambda i, j: (i, 0)), pl.BlockSpec((tm, k), lambda i, j: (i, 0)),
                  pl.BlockSpec((k, tn), lambda i, j: (0, j)), pl.BlockSpec((k, tn), lambda i, j: (0, j)),
                  pl.BlockSpec((tm, tn), lambda i, j: (i, j)), pl.BlockSpec((tm, tn), lambda i, j: (i, nb + j))],
        out_specs=pl.BlockSpec((tm, tn), lambda i, j: (i, j)),
        out_shape=jax.ShapeDtypeStruct((n, d), BF16),
        compiler_params=_cp("parallel", "arbitrary"),
        name="merge",
    )(o_nsa, o_sb, w1, w2, gates, gates)


def _norm_mm_kernel(x_ref, g_ref, w_ref, o_ref):
    h = _rms(x_ref[...].astype(F32), g_ref[...]).astype(BF16)
    o_ref[...] = _dot(h, w_ref[...]).astype(o_ref.dtype)


def _norm_mm(x, g, w, out_dtype, tm=256):
    n, d = x.shape
    nc = w.shape[1]
    return pl.pallas_call(
        _norm_mm_kernel,
        grid=(n // tm,),
        in_specs=[pl.BlockSpec((tm, d), lambda i: (i, 0)), pl.BlockSpec((1, d), lambda i: (0, 0)),
                  pl.BlockSpec((d, nc), lambda i: (0, 0))],
        out_specs=pl.BlockSpec((tm, nc), lambda i: (i, 0)),
        out_shape=jax.ShapeDtypeStruct((n, nc), out_dtype),
        compiler_params=_cp("parallel"),
        name="norm_mm",
    )(x, g.reshape(1, d), w)


def _compress_kernel(x_ref, pe_ref, w1_ref, w2_ref, o_ref):
    x = x_ref[...].astype(F32)
    half = x.shape[1]
    xa = (x + pe_ref[0:1, :]).astype(BF16)
    xb = (x + pe_ref[1:2, :]).astype(BF16)
    y1 = _dot(xa, w1_ref[0:half, :].astype(BF16))
    y2 = _dot(xb, w1_ref[half:2 * half, :].astype(BF16))
    pre = y1 + pltpu.roll(y2, x.shape[0] - 1, 0)
    act = pre * (1.0 / (1.0 + jnp.exp(-pre)))
    o_ref[...] = _dot(act.astype(BF16), w2_ref[...].astype(BF16)).astype(o_ref.dtype)


def _compress(tok, pe, w1, w2):
    b, hkv, nchunk, width = tok.shape
    return pl.pallas_call(
        _compress_kernel,
        grid=(b, hkv),
        in_specs=[pl.BlockSpec((None, None, nchunk, width), lambda i, j: (i, j, 0, 0)),
                  pl.BlockSpec((2, width), lambda i, j: (0, 0)),
                  pl.BlockSpec((2 * width, HEAD_DIM), lambda i, j: (0, 0)),
                  pl.BlockSpec((HEAD_DIM, HEAD_DIM), lambda i, j: (0, 0))],
        out_specs=pl.BlockSpec((None, None, nchunk, HEAD_DIM), lambda i, j: (i, j, 0, 0)),
        out_shape=jax.ShapeDtypeStruct((b, hkv, nchunk, HEAD_DIM), BF16),
        compiler_params=_cp("parallel", "parallel"),
        name="nsa_compress",
    )(tok, pe.reshape(2, width), w1, w2)


def _nsa_cmp_kernel(q_ref, kc_ref, vc_ref, oc_ref, sel_ref, *, tq, nc, ns):
    q0 = pl.program_id(2) * tq
    kc = kc_ref[...]
    vc = vc_ref[...]
    t_row = q0 + lax.broadcasted_iota(jnp.int32, (tq, nc), 0)
    c_idx = lax.broadcasted_iota(jnp.int32, (tq, nc), 1)
    vis = (c_idx * CMP_STRIDE + (CMP_LEN - 1) <= t_row) & (c_idx < nc - 1)
    heads = [slice(h * HEAD_DIM, (h + 1) * HEAD_DIM) for h in range(NSA_GROUP)]
    scores = [_nt_dot(q_ref[:, sl], kc) for sl in heads]
    probs = []
    for s in scores:
        s = jnp.where(vis, s, NEG)
        e = jnp.where(vis, jnp.exp(s - jnp.max(s, axis=-1, keepdims=True)), 0.0)
        den = jnp.sum(e, axis=-1, keepdims=True)
        probs.append(e / jnp.where(den > 0, den, 1.0))
    for sl, p in zip(heads, probs):
        oc_ref[:, sl] = _dot(p.astype(BF16), vc).astype(oc_ref.dtype)
    imp = sum(probs[1:], probs[0])
    jj = lax.broadcasted_iota(jnp.int32, (ns, nc), 0)
    cc = lax.broadcasted_iota(jnp.int32, (ns, nc), 1) * CMP_STRIDE
    ov = jnp.where((cc < (jj + 1) * SEL_LEN) & (cc + CMP_LEN > jj * SEL_LEN), 1.0, 0.0).astype(BF16)
    imp_t = sum(_nt_dot(ov, part) for part in _split3(imp))
    jb = lax.broadcasted_iota(jnp.int32, (ns, tq), 0)
    tt = q0 + lax.broadcasted_iota(jnp.int32, (ns, tq), 1)
    cur = lax.shift_right_logical(tt, jnp.full_like(tt, SEL_SHIFT))
    forced = (jb == 0) | (jb == cur) | (jb == cur - 1)
    valid = jb * SEL_LEN <= tt
    score = jnp.where(valid, imp_t + jnp.where(forced, FORCE_BONUS, 0.0), NEG)
    rank = jnp.zeros((ns, tq), F32)
    for i in range(ns):
        row = score[i:i + 1, :]
        rank = rank + jnp.where(row > score, 1.0, jnp.where((row == score) & (jb > i), 1.0, 0.0))
    sel_t = jnp.where(valid & (rank < float(min(SEL_TOPN, ns))), 1.0, 0.0).astype(BF16)
    eye = jnp.where(lax.broadcasted_iota(jnp.int32, (tq, tq), 0) == lax.broadcasted_iota(jnp.int32, (tq, tq), 1),
                    1.0, 0.0).astype(BF16)
    sel_ref[...] = _nt_dot(eye, sel_t)


def _nsa_cmp(qkv, kc, vc, b, t, tq=256):
    nc = kc.shape[2]
    ns = t // SEL_LEN
    nq = t // tq
    gw = NSA_GROUP * HEAD_DIM
    return pl.pallas_call(
        functools.partial(_nsa_cmp_kernel, tq=tq, nc=nc, ns=ns),
        grid=(b, NSA_KV_HEADS, nq),
        in_specs=[pl.BlockSpec((tq, gw), lambda i, k, j: (i * nq + j, k)),
                  pl.BlockSpec((None, None, nc, HEAD_DIM), lambda i, k, j: (i, k, 0, 0)),
                  pl.BlockSpec((None, None, nc, HEAD_DIM), lambda i, k, j: (i, k, 0, 0))],
        out_specs=[pl.BlockSpec((tq, gw), lambda i, k, j: (i * nq + j, k)),
                   pl.BlockSpec((None, None, tq, ns), lambda i, k, j: (i, k, j, 0))],
        out_shape=[jax.ShapeDtypeStruct((b * t, NSA_Q_W), BF16),
                   jax.ShapeDtypeStruct((b, NSA_KV_HEADS, t, ns), F32)],
        compiler_params=_cp("parallel", "parallel", "parallel"),
        name="nsa_cmp_select",
    )(qkv, kc, vc)


def _flash_update(s, v, bias, state, g, tq):
    m, l, acc = state
    tk = v.shape[0]
    if bias is not None:
        s = (s.reshape(g, tq, tk) + bias[None]).reshape(g * tq, tk)
    m_new = jnp.maximum(m, jnp.max(s, axis=-1, keepdims=True))
    alpha = jnp.exp(m - m_new)
    p = jnp.exp((s - m_new).astype(BF16))
    pv = _dot(p, jnp.concatenate([v, jnp.ones_like(v)], axis=1))
    return m_new, alpha * l + pv[:, HEAD_DIM:], alpha * acc + pv[:, :HEAD_DIM]


def _nsa_sel_win_kernel(q_ref, ks_ref, vs_ref, kw_ref, vw_ref, sel_ref, oc_ref, g_ref, o_ref, *, tq, tk, stack, ns):
    q0 = pl.program_id(1) * tq
    g = NSA_GROUP
    nkv = NSA_KV_HEADS
    per_tile = tk // SEL_LEN
    units = [(kv, h0) for kv in range(nkv) for h0 in range(0, g, stack)]
    kv_lanes = [slice(kv * HEAD_DIM, (kv + 1) * HEAD_DIM) for kv in range(nkv)]

    def q_lanes(kv, h):
        return slice((kv * g + h) * HEAD_DIM, (kv * g + h + 1) * HEAD_DIM)

    qs = [jnp.concatenate([q_ref[:, q_lanes(kv, h0 + j)] for j in range(stack)], axis=0) for kv, h0 in units]
    sel_bias = [((sel_ref[kv] - 1.0) * (-NEG)).astype(BF16) for kv in range(nkv)]
    ej = lax.broadcasted_iota(jnp.int32, (ns, tk), 0)
    eb = lax.broadcasted_iota(jnp.int32, (ns, tk), 1)
    eb = lax.shift_right_logical(eb, jnp.full_like(eb, SEL_SHIFT))

    def init():
        return tuple((jnp.full((stack * tq, 1), NEG, F32), jnp.zeros((stack * tq, HEAD_DIM), F32),
                      jnp.zeros((stack * tq, HEAD_DIM), F32)) for _ in units)

    def flash_all(k_ref, v_ref, k0, width, biases, states):
        scores = [_nt_dot(qs[u], k_ref[pl.ds(k0, width), kv_lanes[kv]]) for u, (kv, _) in enumerate(units)]
        return tuple(_flash_update(scores[u], v_ref[pl.ds(k0, width), kv_lanes[kv]], biases[kv], states[u], stack, tq)
                     for u, (kv, _) in enumerate(units))

    def sel_step(kt, states, causal):
        k0 = pl.multiple_of(kt * tk, tk)
        expand = jnp.where(ej == eb + kt * per_tile, 1.0, 0.0).astype(BF16)
        biases = [_dot(sel_bias[kv], expand) for kv in range(nkv)]
        if causal is not None:
            biases = [bias + causal for bias in biases]
        return flash_all(ks_ref, vs_ref, k0, tk, biases, states)

    kt_d = q0 // tk
    t_row = q0 + lax.broadcasted_iota(jnp.int32, (tq, tk), 0)
    col = kt_d * tk + lax.broadcasted_iota(jnp.int32, (tq, tk), 1)
    states = sel_step(kt_d, init(), jnp.where(col <= t_row, 0.0, NEG))
    states = lax.fori_loop(0, kt_d, lambda kt, st: sel_step(kt, st, None), states)
    o_s = [acc * (1.0 / l) for (_, l, acc) in states]

    def win_step(k0, states, bias):
        return flash_all(kw_ref, vw_ref, k0, tq, [bias] * nkv, states)

    ri = lax.broadcasted_iota(jnp.int32, (tq, tq), 0)
    ci = lax.broadcasted_iota(jnp.int32, (tq, tq), 1)
    states = win_step(pl.multiple_of(q0, tq), init(), jnp.where(ci <= ri, 0.0, NEG))
    n_full = jnp.minimum(WINDOW // tq - 1, q0 // tq)
    states = lax.fori_loop(0, n_full, lambda i, st: win_step(pl.multiple_of(q0 - (i + 1) * tq, tq), st, None),
                           states)
    edge_bias = jnp.where(ci > ri, 0.0, NEG) + jnp.where(q0 >= WINDOW, 0.0, NEG)
    states = win_step(pl.multiple_of(jnp.maximum(q0 - WINDOW, 0), tq), states, edge_bias)
    o_w = [acc * (1.0 / l) for (_, l, acc) in states]

    for u, (kv, h0) in enumerate(units):
        gates = g_ref[:, kv * LANES:(kv + 1) * LANES]
        for j in range(stack):
            h = h0 + j
            sl = q_lanes(kv, h)
            rows = slice(j * tq, (j + 1) * tq)
            out = (gates[:, 3 * h:3 * h + 1] * oc_ref[:, sl].astype(F32)
                   + gates[:, 3 * h + 1:3 * h + 2] * o_s[u][rows] + gates[:, 3 * h + 2:3 * h + 3] * o_w[u][rows])
            o_ref[:, sl] = out.astype(o_ref.dtype)


def _nsa_sel_win(qkv, sel, oc, gates, b, t, tq=128, tk=256, stack=4):
    nq = t // tq
    ns = t // SEL_LEN
    kv_block = NSA_Q_W // NSA_KV_W

    def kv_spec(which):
        return pl.BlockSpec((t, NSA_KV_W), lambda i, j: (i, kv_block + which))

    return pl.pallas_call(
        functools.partial(_nsa_sel_win_kernel, tq=tq, tk=tk, stack=stack, ns=ns),
        grid=(b, nq),
        in_specs=[pl.BlockSpec((tq, NSA_Q_W), lambda i, j: (i * nq + j, 0)),
                  kv_spec(2), kv_spec(3), kv_spec(4), kv_spec(5),
                  pl.BlockSpec((None, NSA_KV_HEADS, tq, ns), lambda i, j: (i, 0, j, 0)),
                  pl.BlockSpec((tq, NSA_Q_W), lambda i, j: (i * nq + j, 0)),
                  pl.BlockSpec((tq, NSA_KV_HEADS * LANES), lambda i, j: (i * nq + j, 0))],
        out_specs=pl.BlockSpec((tq, NSA_Q_W), lambda i, j: (i * nq + j, 0)),
        out_shape=jax.ShapeDtypeStruct((b * t, NSA_Q_W), BF16),
        compiler_params=_cp("parallel", "parallel"),
        name="nsa_select_window",
    )(qkv, qkv, qkv, qkv, qkv, sel, oc, gates)


def _stickbreak_kernel(q_ref, k_ref, v_ref, o_ref, *, tile, nh):
    q0 = pl.program_id(2) * tile
    ri = lax.broadcasted_iota(jnp.int32, (tile, tile), 0)
    ci = lax.broadcasted_iota(jnp.int32, (tile, tile), 1)
    strict = ci < ri
    tri = jnp.where(ri > ci, 1.0, 0.0).astype(BF16)
    lanes = [slice(h * HEAD_DIM, (h + 1) * HEAD_DIM) for h in range(nh)]

    def step(k0, diag, carry):
        zs = [_nt_dot(q_ref[:, lanes[h]], k_ref[pl.ds(k0, tile), lanes[h]]) for h in range(nh)]
        log_bs, sufs, sums = [], [], []
        for z in zs:
            log_b = jnp.minimum(z, 0.0) - jnp.log(1.0 + jnp.exp(-jnp.abs(z)))
            log_1m = log_b - z
            if diag:
                log_1m = jnp.where(strict, log_1m, 0.0)
            hi = log_1m.astype(BF16)
            lo = (log_1m - hi.astype(F32)).astype(BF16)
            log_bs.append(log_b)
            sufs.append(_dot(hi, tri) + _dot(lo, tri))
            sums.append(jnp.sum(log_1m, axis=-1, keepdims=True))
        out = []
        for h in range(nh):
            later, acc = carry[2 * h], carry[2 * h + 1]
            a = jnp.exp(log_bs[h] + (sufs[h] + later))
            if diag:
                a = jnp.where(strict, a, 0.0)
            out.extend((later + sums[h], acc + _dot(a.astype(BF16), v_ref[pl.ds(k0, tile), lanes[h]])))
        return tuple(out)

    zero = (jnp.zeros((tile, 1), F32), jnp.zeros((tile, HEAD_DIM), F32)) * nh
    carry = step(pl.multiple_of(q0, tile), True, zero)

    def live(state):
        i, c = state
        top = c[0]
        for h in range(1, nh):
            top = jnp.maximum(top, c[2 * h])
        return (i < q0 // tile) & (jnp.max(top) > F32_EXP_UNDERFLOW)

    def advance(state):
        i, c = state
        return i + 1, step(pl.multiple_of(q0 - (i + 1) * tile, tile), False, c)

    _, carry = lax.while_loop(live, advance, (jnp.int32(0), carry))
    for h in range(nh):
        o_ref[:, lanes[h]] = carry[2 * h + 1].astype(o_ref.dtype)


def _stickbreak(qkv, b, t, tile=256, nh=4):
    nq = t // tile
    w = nh * HEAD_DIM
    nhb = SB_HEADS // nh
    return pl.pallas_call(
        functools.partial(_stickbreak_kernel, tile=tile, nh=nh),
        grid=(b, nhb, nq),
        in_specs=[pl.BlockSpec((tile, w), lambda i, h, j: (i * nq + j, h)),
                  pl.BlockSpec((t, w), lambda i, h, j: (i, nhb + h)),
                  pl.BlockSpec((t, w), lambda i, h, j: (i, 2 * nhb + h))],
        out_specs=pl.BlockSpec((tile, w), lambda i, h, j: (i * nq + j, h)),
        out_shape=jax.ShapeDtypeStruct((b * t, SB_W), BF16),
        compiler_params=_cp("parallel", "parallel", "parallel"),
        name="stickbreak",
    )(qkv, qkv, qkv)


def _cross_kernel(x_ref, g_ref, wq_ref, kv_ref, wo_ref, o_ref):
    x = x_ref[...]
    h = _rms(x, g_ref[...]).astype(BF16)
    q = _dot(h, wq_ref[...]).astype(BF16)
    xw = X_HEADS * HEAD_DIM
    outs = []
    for hd in range(X_HEADS):
        sl = slice(hd * HEAD_DIM, (hd + 1) * HEAD_DIM)
        s = _nt_dot(q[:, sl], kv_ref[:, sl]) * SCALE
        e = jnp.exp(s - jnp.max(s, axis=-1, keepdims=True))
        p = e / jnp.sum(e, axis=-1, keepdims=True)
        outs.append(_dot(p.astype(BF16), kv_ref[:, xw + hd * HEAD_DIM:xw + (hd + 1) * HEAD_DIM]).astype(BF16))
    o_ref[...] = x + _dot(jnp.concatenate(outs, axis=1), wo_ref[...])


def _cross(x, g, wq, kv, wo, b, t, tq=512):
    n, d = x.shape
    nq = t // tq
    mem_len = kv.shape[0] // b
    return pl.pallas_call(
        _cross_kernel,
        grid=(b, nq),
        in_specs=[pl.BlockSpec((tq, d), lambda i, j: (i * nq + j, 0)),
                  pl.BlockSpec((1, d), lambda i, j: (0, 0)),
                  pl.BlockSpec(wq.shape, lambda i, j: (0, 0)),
                  pl.BlockSpec((mem_len, kv.shape[1]), lambda i, j: (i, 0)),
                  pl.BlockSpec(wo.shape, lambda i, j: (0, 0))],
        out_specs=pl.BlockSpec((tq, d), lambda i, j: (i * nq + j, 0)),
        out_shape=jax.ShapeDtypeStruct((n, d), F32),
        compiler_params=_cp("parallel", "parallel"),
        name="cross_attention",
    )(x, g.reshape(1, d), wq, kv, wo)


def _to_row_tiles(ref, val):
    rows, width = val.shape
    c_per = width // LANES
    for c in range(c_per):
        ref[pl.ds(c, rows, stride=c_per), :] = val[:, c * LANES:(c + 1) * LANES]


def _from_row_tiles(ref, c_per):
    rows = ref.shape[0] // c_per
    return jnp.concatenate([ref[pl.ds(c, rows, stride=c_per), :] for c in range(c_per)], axis=1)


def _router_kernel(x_ref, g_ref, w_ref, b_ref, h_ref, r_ref):
    h = _rms(x_ref[...], g_ref[...])
    _to_row_tiles(h_ref, h)
    h_hi = h.astype(BF16)
    h_mid = (h - h_hi.astype(F32)).astype(BF16)
    two = _dot(h_hi, w_ref[...])
    logits = two[:, :LANES] + two[:, LANES:] + _dot(h_mid, w_ref[:, :LANES]) + b_ref[...]
    lane = lax.broadcasted_iota(jnp.int32, logits.shape, 1).astype(F32)
    big = float(LANES)
    is_g = lane < N_GROUPS
    lg = jnp.where(is_g, logits, NEG)
    mg = jnp.max(lg, axis=-1, keepdims=True)
    p_gtop = 1.0 / jnp.sum(jnp.where(is_g, jnp.exp(lg - mg), 0.0), axis=-1, keepdims=True)
    g_top = jnp.min(jnp.where(lg == mg, lane, big), axis=-1, keepdims=True)
    lo = N_GROUPS + EXPERTS_PER_GROUP * g_top
    in_grp = (lane >= lo) & (lane < lo + EXPERTS_PER_GROUP)
    le = jnp.where(in_grp, logits, NEG)
    ee = jnp.where(in_grp, jnp.exp(le - jnp.max(le, axis=-1, keepdims=True)), 0.0)
    p_in = ee / jnp.sum(ee, axis=-1, keepdims=True)
    p1 = jnp.where(in_grp, p_in, -1.0)
    m1 = jnp.max(p1, axis=-1, keepdims=True)
    i1 = jnp.min(jnp.where(p1 == m1, lane, big), axis=-1, keepdims=True)
    p2 = jnp.where(lane == i1, -1.0, p1)
    m2 = jnp.max(p2, axis=-1, keepdims=True)
    i2 = jnp.min(jnp.where(p2 == m2, lane, big), axis=-1, keepdims=True)
    tot = m1 + m2
    r_ref[...] = jnp.where(lane == 0, i1 - N_GROUPS,
                           jnp.where(lane == 1, i2 - N_GROUPS,
                                     jnp.where(lane == 2, m1 / tot * p_gtop,
                                               jnp.where(lane == 3, m2 / tot * p_gtop, 0.0))))


def _router(x, g, w_r, b_r, tm=512):
    n, d = x.shape
    return pl.pallas_call(
        _router_kernel,
        grid=(n // tm,),
        in_specs=[pl.BlockSpec((tm, d), lambda i: (i, 0)), pl.BlockSpec((1, d), lambda i: (0, 0)),
                  pl.BlockSpec((d, 2 * LANES), lambda i: (0, 0)), pl.BlockSpec((1, LANES), lambda i: (0, 0))],
        out_specs=[pl.BlockSpec((tm * (d // LANES), LANES), lambda i: (i, 0)),
                   pl.BlockSpec((tm, LANES), lambda i: (i, 0))],
        out_shape=[jax.ShapeDtypeStruct((n * (d // LANES), LANES), F32), jax.ShapeDtypeStruct((n, LANES), F32)],
        compiler_params=_cp("parallel"),
        name="moe_router",
    )(x, g.reshape(1, d), w_r, b_r)


ROW_CHUNKS = 16
WEIGHT_DMA_PRIORITY = 0


def _wait_rows(src_hbm, dst, sem):
    pltpu.make_async_copy(src_hbm.at[pl.ds(0, dst.shape[0]), :], dst, sem).wait()


def _row_copy(src_hbm, dst, sem, src_row, dst_row):
    c = ROW_CHUNKS
    return pltpu.make_async_copy(src_hbm.at[pl.ds(pl.multiple_of(src_row * c, c), c), :],
                                 dst.at[pl.ds(pl.multiple_of(dst_row * c, c), c), :], sem)


def _expert_kernel(idx_ref, ord_ref, oe_ref, cnt_ref, src_hbm, wg_hbm, wu_hbm, wd_hbm, o_ref,
                   xbuf, sems, wg_f, wu_f, wd_f, wsems, wg_s, wu_s, wd_s, *, tm):
    i = pl.program_id(0)
    nt = cnt_ref[0]
    n_ord = cnt_ref[1]
    live = i < nt
    slot = lax.rem(i, 2)

    def gather(tile, s):
        def issue(r, c):
            _row_copy(src_hbm, xbuf.at[s], sems.at[s], idx_ref[tile * tm + r], r).start()
            return c
        lax.fori_loop(0, tm, issue, 0, unroll=8)

    def weight_copies(k, s):
        e = oe_ref[k]
        return (pltpu.make_async_copy(wg_hbm.at[e], wg_f.at[s], wsems.at[s, 0]),
                pltpu.make_async_copy(wu_hbm.at[e], wu_f.at[s], wsems.at[s, 1]),
                pltpu.make_async_copy(wd_hbm.at[e], wd_f.at[s], wsems.at[s, 2]))

    @pl.when(i == 0)
    def _():
        gather(0, 0)
        for cp in weight_copies(0, 0):
            cp.start(priority=WEIGHT_DMA_PRIORITY)

    @pl.when(i + 1 < nt)
    def _():
        gather(i + 1, 1 - slot)

    k = ord_ref[i]
    fresh = live & ((i == 0) | (k != ord_ref[jnp.maximum(i - 1, 0)]))
    ws = lax.rem(k, 2)

    @pl.when(fresh)
    def _():
        for cp in weight_copies(k, ws):
            cp.wait()

        @pl.when(k + 1 < n_ord)
        def _():
            for cp in weight_copies(k + 1, 1 - ws):
                cp.start(priority=WEIGHT_DMA_PRIORITY)

        wg_s[...] = wg_f[ws].astype(BF16)
        wu_s[...] = wu_f[ws].astype(BF16)
        wd_s[...] = wd_f[ws].astype(BF16)

    @pl.when(live)
    def _():
        _wait_rows(src_hbm, xbuf.at[slot], sems.at[slot])
        x = _from_row_tiles(xbuf.at[slot], ROW_CHUNKS).astype(BF16)
        a = _dot(x, wg_s[...])
        u = _dot(x, wu_s[...])
        act = a * (1.0 / (1.0 + jnp.exp(-a))) * u
        _to_row_tiles(o_ref, _dot(act.astype(BF16), wd_s[...]))

    @pl.when(jnp.logical_not(live))
    def _():
        o_ref[...] = jnp.zeros_like(o_ref)


def _experts(src, tok_of_row, w_g, w_u, w_d, tile_ord, ord_expert, counts, tm=EXPERT_TILE):
    c = ROW_CHUNKS
    n_tiles_max = tok_of_row.shape[0] // tm
    d = c * LANES
    f = w_g.shape[2]
    any_spec = pl.BlockSpec(memory_space=pl.ANY)
    return pl.pallas_call(
        functools.partial(_expert_kernel, tm=tm),
        grid_spec=pltpu.PrefetchScalarGridSpec(
            num_scalar_prefetch=4,
            grid=(n_tiles_max,),
            in_specs=[any_spec, any_spec, any_spec, any_spec],
            out_specs=pl.BlockSpec((tm * c, LANES), lambda i, *_: (i, 0)),
            scratch_shapes=[pltpu.VMEM((2, tm * c, LANES), F32), pltpu.SemaphoreType.DMA((2,)),
                            pltpu.VMEM((2, d, f), F32), pltpu.VMEM((2, d, f), F32), pltpu.VMEM((2, f, d), F32),
                            pltpu.SemaphoreType.DMA((2, 3)),
                            pltpu.VMEM((d, f), BF16), pltpu.VMEM((d, f), BF16), pltpu.VMEM((f, d), BF16)]),
        out_shape=jax.ShapeDtypeStruct((n_tiles_max * tm * c, LANES), F32),
        compiler_params=_cp("arbitrary"),
        name="moe_experts",
    )(tok_of_row, tile_ord, ord_expert, counts, src, w_g, w_u, w_d)


def _combine_kernel(p0_ref, p1_ref, ys_hbm, x_ref, r_ref, g_ref, o_ref, buf, sems, *, tm):
    i = pl.program_id(0)
    slot = lax.rem(i, 2)

    def fetch(tile, s):
        def issue(r, c):
            _row_copy(ys_hbm, buf.at[s, 0], sems.at[s], p0_ref[tile * tm + r], r).start()
            _row_copy(ys_hbm, buf.at[s, 1], sems.at[s], p1_ref[tile * tm + r], r).start(priority=1)
            return c
        lax.fori_loop(0, tm, issue, 0, unroll=8)

    @pl.when(i == 0)
    def _():
        fetch(0, 0)

    @pl.when(i + 1 < pl.num_programs(0))
    def _():
        fetch(i + 1, 1 - slot)

    _wait_rows(ys_hbm, buf.at[slot, 0], sems.at[slot])
    _wait_rows(ys_hbm, buf.at[slot, 1], sems.at[slot])
    route = r_ref[...]
    y = (route[:, 2:3] * _from_row_tiles(buf.at[slot, 0], ROW_CHUNKS)
         + route[:, 3:4] * _from_row_tiles(buf.at[slot, 1], ROW_CHUNKS))
    o_ref[...] = _rms(x_ref[...] + y, g_ref[...])


def _combine(ys, pos0, pos1, x, route, g, tm=256):
    n, d = x.shape
    return pl.pallas_call(
        functools.partial(_combine_kernel, tm=tm),
        grid_spec=pltpu.PrefetchScalarGridSpec(
            num_scalar_prefetch=2,
            grid=(n // tm,),
            in_specs=[pl.BlockSpec(memory_space=pl.ANY),
                      pl.BlockSpec((tm, d), lambda i, p0, p1: (i, 0)),
                      pl.BlockSpec((tm, LANES), lambda i, p0, p1: (i, 0)),
                      pl.BlockSpec((1, d), lambda i, p0, p1: (0, 0))],
            out_specs=pl.BlockSpec((tm, d), lambda i, p0, p1: (i, 0)),
            scratch_shapes=[pltpu.VMEM((2, 2, tm * ROW_CHUNKS, LANES), F32), pltpu.SemaphoreType.DMA((2,))]),
        out_shape=jax.ShapeDtypeStruct((n, d), F32),
        compiler_params=_cp("arbitrary"),
        name="moe_combine_norm",
    )(pos0, pos1, ys, x, route, g.reshape(1, d))


def _dispatch_plan(e1, e2, tm=EXPERT_TILE):
    n = e1.shape[0]
    n_tiles_max = (2 * n) // tm + N_EXPERTS
    rows = n_tiles_max * tm
    e_flat = jnp.concatenate([e1, e2])
    tok = jnp.concatenate([jnp.arange(n, dtype=jnp.int32)] * 2)
    onehot = (e_flat[:, None] == jnp.arange(N_EXPERTS, dtype=jnp.int32)[None, :]).astype(jnp.int32)
    csum = jnp.cumsum(onehot, axis=0)
    count = csum[-1]
    rank = jnp.sum((csum - 1) * onehot, axis=1)
    tiles_per = (count + tm - 1) // tm
    tile_end = jnp.cumsum(tiles_per)
    row_start = (tile_end - tiles_per) * tm
    dest = (row_start[e_flat] + rank).astype(jnp.int32)
    tok_of_row = jnp.zeros((rows,), jnp.int32).at[dest].set(tok)
    n_tiles = tile_end[-1].astype(jnp.int32)
    tile_id = jnp.minimum(jnp.arange(n_tiles_max, dtype=jnp.int32), n_tiles - 1)
    tile_expert = jnp.sum((tile_id[:, None] >= tile_end[None, :]).astype(jnp.int32), axis=1)
    used = jnp.cumsum((tiles_per > 0).astype(jnp.int32))
    tile_ord = (used[tile_expert] - 1).astype(jnp.int32)
    ordinals = jnp.arange(N_EXPERTS, dtype=jnp.int32)
    ord_expert = jnp.minimum(jnp.sum((used[None, :] <= ordinals[:, None]).astype(jnp.int32), axis=1),
                             N_EXPERTS - 1).astype(jnp.int32)
    counts = jnp.stack([n_tiles, used[-1]]).astype(jnp.int32)
    return tok_of_row, tile_ord, ord_expert, counts, dest[:n], dest[n:]


def _rope_tables(positions):
    half = ROT_DIM // 2
    inv = ROPE_THETA ** (-jnp.arange(half, dtype=F32) * (2.0 / ROT_DIM))
    ang = positions.astype(F32).reshape(-1, 1) * inv[None, :]
    cos, sin = jnp.cos(ang), jnp.sin(ang)
    n = ang.shape[0]
    c = jnp.concatenate([cos, cos, jnp.ones((n, LANES - ROT_DIM), F32)], axis=1)
    s1 = jnp.concatenate([-sin, jnp.zeros((n, LANES - half), F32)], axis=1)
    s2 = jnp.concatenate([jnp.zeros((n, half), F32), sin, jnp.zeros((n, LANES - ROT_DIM), F32)], axis=1)
    return c, s1, s2


def _layer(x, mem, positions, g_mix, w_in, cmp_pe_k, cmp_w1_k, cmp_w2_k, cmp_pe_v, cmp_w1_v, cmp_w2_v,
           w_br_nsa, w_br_sb, w_o, g_cross, g_mem, w_cq, w_ck, w_cv, w_co,
           g_moe, w_rg, b_rg, w_re, b_re, w_eg, w_eu, w_ed):
    b, t, d = x.shape
    n = b * t
    xf = x.reshape(n, d)

    g0 = NSA_W
    w_t = w_in.T
    w_nsa = _cast_rows(w_t, 0, g0)
    w_rest = _cast_rows(w_t, g0 + NSA_GATES, w_t.shape[0] - g0 - NSA_GATES)
    w_g = w_t[g0:g0 + NSA_GATES].reshape(NSA_KV_HEADS, 3 * NSA_GROUP, d)
    w_g = jnp.pad(w_g, ((0, 0), (0, LANES - 3 * NSA_GROUP), (0, 0))).reshape(NSA_KV_HEADS * LANES, d).astype(BF16)
    h = _rmsnorm(xf, g_mix, BF16)
    c, s1, s2 = _rope_tables(positions)
    row_spec = pl.BlockSpec((1024, LANES), lambda i, j: (i, 0))
    qkv_nsa = _mm(functools.partial(_mm_nsa_kernel, tn=256), h, w_nsa, NSA_W, 0, 1024, 256, BF16, w_t=True,
                  extra=(c, s1, s2), extra_specs=(row_spec, row_spec, row_spec), name="in_proj_nsa")
    qkv_sb = _mm(functools.partial(_mm_sb_kernel, q_tiles=SB_W // 512), h, w_rest, 3 * SB_W, 0,
                 1024, 512, BF16, w_t=True, name="in_proj_sb")
    gates = _mm(_mm_sigmoid_kernel, h, w_rest, 2 * d, 3 * SB_W // 512, 1024, 512, BF16, w_t=True,
                name="in_proj_gates")
    nsa_gates = _mm(_mm_sigmoid_kernel, h, w_g, NSA_KV_HEADS * LANES, 0, 1024, NSA_KV_HEADS * LANES, F32,
                    w_t=True, name="in_proj_nsa_gates")

    def chunked(col0):
        tok = qkv_nsa[:, col0:col0 + NSA_KV_W].reshape(b, t, NSA_KV_HEADS, HEAD_DIM).transpose(0, 2, 1, 3)
        return tok.reshape(b, NSA_KV_HEADS, t // CMP_STRIDE, CMP_STRIDE * HEAD_DIM)

    kc = _compress(chunked(NSA_Q_W), cmp_pe_k, cmp_w1_k, cmp_w2_k)
    vc = _compress(chunked(NSA_Q_W + NSA_KV_W), cmp_pe_v, cmp_w1_v, cmp_w2_v)
    o_c, sel = _nsa_cmp(qkv_nsa, kc, vc, b, t)
    o_nsa = _nsa_sel_win(qkv_nsa, sel, o_c, nsa_gates, b, t)
    o_sb = _stickbreak(qkv_sb, b, t)
    merged = _merge(o_nsa, o_sb, w_br_nsa.astype(BF16), w_br_sb.astype(BF16), gates)
    x1 = _mm(_mm_resid_kernel, merged, w_o.astype(BF16), d, 0, 1024, 512, F32,
             extra=(xf,), extra_specs=(pl.BlockSpec((1024, 512), lambda i, j: (i, j)),), name="out_proj")

    w_kv = jnp.concatenate([w_ck, w_cv], axis=1).astype(BF16)
    kv = _norm_mm(mem.reshape(-1, d), g_mem, w_kv, BF16)
    x2 = _cross(x1, g_cross, w_cq.astype(BF16), kv, w_co.astype(BF16), b, t)

    w_r = jnp.pad(jnp.concatenate([w_rg, w_re], axis=1), ((0, 0), (0, LANES - N_GROUPS - N_EXPERTS)))
    b_r = jnp.pad(jnp.concatenate([b_rg, b_re]), (0, LANES - N_GROUPS - N_EXPERTS)).reshape(1, LANES)
    w_r_hi = w_r.astype(BF16)
    w_r2 = jnp.concatenate([w_r_hi, (w_r - w_r_hi.astype(F32)).astype(BF16)], axis=1)
    hm, route = _router(x2, g_moe, w_r2, b_r)
    e1 = route[:, 0].astype(jnp.int32)
    e2 = route[:, 1].astype(jnp.int32)
    tok_of_row, tile_ord, ord_expert, counts, pos0, pos1 = _dispatch_plan(e1, e2)
    f = w_eg.shape[-1]
    ys = _experts(hm, tok_of_row, w_eg.reshape(N_EXPERTS, d, f), w_eu.reshape(N_EXPERTS, d, f),
                  w_ed.reshape(N_EXPERTS, f, d), tile_ord, ord_expert, counts)
    return ys, pos0, pos1, x2, route


def kernel(x, mem, positions, g_mix, w_in, cmp_pe_k, cmp_w1_k, cmp_w2_k, cmp_pe_v, cmp_w1_v, cmp_w2_v, w_br_nsa, w_br_sb, w_o, g_cross, g_mem, w_cq, w_ck, w_cv, w_co, g_moe, w_rg, b_rg, w_re, b_re, w_eg, w_eu, w_ed, g_final):
    b, t, d = x.shape
    assert w_in.shape[0] == 1, "one layer"
    ys, pos0, pos1, x2, route = _layer(
        x, mem, positions, g_mix[0], w_in[0], cmp_pe_k[0], cmp_w1_k[0], cmp_w2_k[0], cmp_pe_v[0], cmp_w1_v[0],
        cmp_w2_v[0], w_br_nsa[0], w_br_sb[0], w_o[0], g_cross[0], g_mem[0], w_cq[0], w_ck[0], w_cv[0], w_co[0],
        g_moe[0], w_rg[0], b_rg[0], w_re[0], b_re[0], w_eg[0], w_eu[0], w_ed[0])
    return _combine(ys, pos0, pos1, x2, route, g_final).reshape(b, t, d)
```

```python
import functools

import jax
import jax.numpy as jnp
from jax import lax
from jax.experimental import pallas as pl
from jax.experimental.pallas import tpu as pltpu

F32 = jnp.float32
BF16 = jnp.bfloat16

HEAD_DIM = 128
NSA_HEADS = 8
NSA_KV_HEADS = 2
NSA_GROUP = NSA_HEADS // NSA_KV_HEADS
SB_HEADS = 8
ROT_DIM = HEAD_DIM // 4
ROPE_THETA = 500000.0
CMP_LEN = 32
CMP_STRIDE = 16
SEL_LEN = 64
SEL_SHIFT = 6
SEL_TOPN = 16
WINDOW = 512
FORCE_BONUS = 1000.0
X_HEADS = 4
N_GROUPS = 4
EXPERTS_PER_GROUP = 8
N_EXPERTS = N_GROUPS * EXPERTS_PER_GROUP
EPS = 1e-6
SCALE = HEAD_DIM ** -0.5
NEG = -1e30
F32_EXP_UNDERFLOW = -104.0

NSA_Q_W = NSA_HEADS * HEAD_DIM
NSA_KV_W = NSA_KV_HEADS * HEAD_DIM
NSA_W = NSA_Q_W + 6 * NSA_KV_W
NSA_GATES = 3 * NSA_HEADS
SB_W = SB_HEADS * HEAD_DIM
LANES = 128
VMEM_LIMIT = 56 * 1024 * 1024

EXPERT_TILE = 256


def _cp(*sem, vmem=VMEM_LIMIT):
    return pltpu.CompilerParams(dimension_semantics=sem, vmem_limit_bytes=vmem)


def _nt_dot(a, b):
    return lax.dot_general(a, b, (((1,), (1,)), ((), ())), preferred_element_type=F32)


def _dot(a, b):
    return jnp.dot(a, b, preferred_element_type=F32)


def _split3(x):
    hi = x.astype(BF16)
    r1 = x - hi.astype(F32)
    mid = r1.astype(BF16)
    lo = (r1 - mid.astype(F32)).astype(BF16)
    return hi, mid, lo


def _rms(x, g):
    return x * lax.rsqrt(jnp.mean(x * x, axis=-1, keepdims=True) + EPS) * g


def _rmsnorm_kernel(x_ref, g_ref, o_ref):
    o_ref[...] = _rms(x_ref[...].astype(F32), g_ref[...]).astype(o_ref.dtype)


def _rmsnorm(x, g, out_dtype, tm=512):
    n, d = x.shape
    return pl.pallas_call(
        _rmsnorm_kernel,
        grid=(n // tm,),
        in_specs=[pl.BlockSpec((tm, d), lambda i: (i, 0)), pl.BlockSpec((1, d), lambda i: (0, 0))],
        out_specs=pl.BlockSpec((tm, d), lambda i: (i, 0)),
        out_shape=jax.ShapeDtypeStruct((n, d), out_dtype),
        compiler_params=_cp("parallel"),
        name="rmsnorm",
    )(x, g.reshape(1, d))


def _cast_rows_kernel(w_ref, o_ref):
    o_ref[...] = w_ref[...].astype(o_ref.dtype)


def _cast_rows(w_t, row0, n_rows, tn=512):
    k = w_t.shape[1]
    assert n_rows % tn == 0 and row0 % 8 == 0
    return pl.pallas_call(
        _cast_rows_kernel,
        grid=(n_rows // tn,),
        in_specs=[pl.BlockSpec((pl.Element(tn), pl.Element(k)), lambda j: ((row0 // 8 + j * (tn // 8)) * 8, 0))],
        out_specs=pl.BlockSpec((tn, k), lambda j: (j, 0)),
        out_shape=jax.ShapeDtypeStruct((n_rows, k), BF16),
        compiler_params=_cp("parallel"),
        name="cast_rows",
    )(w_t)


def _rope_half(x, c, s1, s2):
    return x * c + pltpu.roll(x, LANES - ROT_DIM // 2, 1) * s1 + pltpu.roll(x, ROT_DIM // 2, 1) * s2


def _mm_nsa_kernel(a_ref, w_ref, c_ref, s1_ref, s2_ref, o_ref, *, tn):
    j = pl.program_id(1)
    acc = _nt_dot(a_ref[...], w_ref[...])
    is_rope = (j < 5) | (j == 6) | (j == 8)

    @pl.when(is_rope)
    def _():
        qs = jnp.where(j < NSA_Q_W // tn, SCALE, 1.0)
        c, s1, s2 = c_ref[...] * qs, s1_ref[...] * qs, s2_ref[...] * qs
        for hh in range(tn // LANES):
            sl = slice(hh * LANES, (hh + 1) * LANES)
            o_ref[:, sl] = _rope_half(acc[:, sl], c, s1, s2).astype(o_ref.dtype)

    @pl.when(jnp.logical_not(is_rope))
    def _():
        o_ref[...] = acc.astype(o_ref.dtype)


def _mm_sb_kernel(a_ref, w_ref, o_ref, *, q_tiles):
    acc = _nt_dot(a_ref[...], w_ref[...])
    is_q = pl.program_id(1) < q_tiles
    o_ref[...] = (acc * jnp.where(is_q, SCALE, 1.0)).astype(o_ref.dtype)


def _mm_sigmoid_kernel(a_ref, w_ref, o_ref):
    o_ref[...] = jax.nn.sigmoid(_nt_dot(a_ref[...], w_ref[...])).astype(o_ref.dtype)


def _mm_resid_kernel(a_ref, w_ref, x_ref, o_ref):
    o_ref[...] = x_ref[...] + _dot(a_ref[...], w_ref[...])


def _mm(kern, a, w, n_cols, col_block0, tm, tn, out_dtype, extra=(), extra_specs=(), name="mm", w_t=False):
    m, k = a.shape
    w_spec = (pl.BlockSpec((tn, k), lambda i, j: (col_block0 + j, 0)) if w_t
              else pl.BlockSpec((k, tn), lambda i, j: (0, col_block0 + j)))
    return pl.pallas_call(
        kern,
        grid=(m // tm, n_cols // tn),
        in_specs=[pl.BlockSpec((tm, k), lambda i, j: (i, 0)), w_spec, *extra_specs],
        out_specs=pl.BlockSpec((tm, tn), lambda i, j: (i, j)),
        out_shape=jax.ShapeDtypeStruct((m, n_cols), out_dtype),
        compiler_params=_cp("parallel", "arbitrary"),
        name=name,
    )(a, w, *extra)


def _merge_kernel(a1_ref, a2_ref, w1_ref, w2_ref, ga_ref, gb_ref, o_ref):
    ya = _dot(a1_ref[...], w1_ref[...])
    yb = _dot(a2_ref[...], w2_ref[...])
    o_ref[...] = (ga_ref[...].astype(F32) * ya + gb_ref[...].astype(F32) * yb).astype(o_ref.dtype)


def _merge(o_nsa, o_sb, w1, w2, gates, tm=1024, tn=512):
    n, k = o_nsa.shape
    d = w1.shape[1]
    nb = d // tn
    return pl.pallas_call(
        _merge_kernel,
        grid=(n // tm, nb),
        in_specs=[pl.BlockSpec((tm, k), lambda i, j: (i, 0)), pl.BlockSpec((tm, k), lambda i, j: (i, 0)),
                  pl.BlockSpec((k, tn), lambda i, j: (0, j)), pl.BlockSpec((k, tn), lambda i, j: (0, j)),
                  pl.BlockSpec((tm, tn), lambda i, j: (i, j)), pl.BlockSpec((tm, tn), lambda i, j: (i, nb + j))],
        out_specs=pl.BlockSpec((tm, tn), lambda i, j: (i, j)),
        out_shape=jax.ShapeDtypeStruct((n, d), BF16),
        compiler_params=_cp("parallel", "arbitrary"),
        name="merge",
    )(o_nsa, o_sb, w1, w2, gates, gates)


def _norm_mm_kernel(x_ref, g_ref, w_ref, o_ref):
    h = _rms(x_ref[...].astype(F32), g_ref[...]).astype(BF16)
    o_ref[...] = _dot(h, w_ref[...]).astype(o_ref.dtype)


def _norm_mm(x, g, w, out_dtype, tm=256):
    n, d = x.shape
    nc = w.shape[1]
    return pl.pallas_call(
        _norm_mm_kernel,
        grid=(n // tm,),
        in_specs=[pl.BlockSpec((tm, d), lambda i: (i, 0)), pl.BlockSpec((1, d), lambda i: (0, 0)),
                  pl.BlockSpec((d, nc), lambda i: (0, 0))],
        out_specs=pl.BlockSpec((tm, nc), lambda i: (i, 0)),
        out_shape=jax.ShapeDtypeStruct((n, nc), out_dtype),
        compiler_params=_cp("parallel"),
        name="norm_mm",
    )(x, g.reshape(1, d), w)


def _compress_kernel(x_ref, pe_ref, w1_ref, w2_ref, o_ref):
    x = x_ref[...].astype(F32)
    half = x.shape[1]
    xa = (x + pe_ref[0:1, :]).astype(BF16)
    xb = (x + pe_ref[1:2, :]).astype(BF16)
    y1 = _dot(xa, w1_ref[0:half, :].astype(BF16))
    y2 = _dot(xb, w1_ref[half:2 * half, :].astype(BF16))
    pre = y1 + pltpu.roll(y2, x.shape[0] - 1, 0)
    act = pre * (1.0 / (1.0 + jnp.exp(-pre)))
    o_ref[...] = _dot(act.astype(BF16), w2_ref[...].astype(BF16)).astype(o_ref.dtype)


def _compress(tok, pe, w1, w2):
    b, hkv, nchunk, width = tok.shape
    return pl.pallas_call(
        _compress_kernel,
        grid=(b, hkv),
        in_specs=[pl.BlockSpec((None, None, nchunk, width), lambda i, j: (i, j, 0, 0)),
                  pl.BlockSpec((2, width), lambda i, j: (0, 0)),
                  pl.BlockSpec((2 * width, HEAD_DIM), lambda i, j: (0, 0)),
                  pl.BlockSpec((HEAD_DIM, HEAD_DIM), lambda i, j: (0, 0))],
        out_specs=pl.BlockSpec((None, None, nchunk, HEAD_DIM), lambda i, j: (i, j, 0, 0)),
        out_shape=jax.ShapeDtypeStruct((b, hkv, nchunk, HEAD_DIM), BF16),
        compiler_params=_cp("parallel", "parallel"),
        name="nsa_compress",
    )(tok, pe.reshape(2, width), w1, w2)


def _nsa_cmp_kernel(q_ref, kc_ref, vc_ref, oc_ref, sel_ref, *, tq, nc, ns):
    q0 = pl.program_id(2) * tq
    kc = kc_ref[...]
    vc = vc_ref[...]
    t_row = q0 + lax.broadcasted_iota(jnp.int32, (tq, nc), 0)
    c_idx = lax.broadcasted_iota(jnp.int32, (tq, nc), 1)
    vis = (c_idx * CMP_STRIDE + (CMP_LEN - 1) <= t_row) & (c_idx < nc - 1)
    heads = [slice(h * HEAD_DIM, (h + 1) * HEAD_DIM) for h in range(NSA_GROUP)]
    scores = [_nt_dot(q_ref[:, sl], kc) for sl in heads]
    probs = []
    for s in scores:
        s = jnp.where(vis, s, NEG)
        e = jnp.where(vis, jnp.exp(s - jnp.max(s, axis=-1, keepdims=True)), 0.0)
        den = jnp.sum(e, axis=-1, keepdims=True)
        probs.append(e / jnp.where(den > 0, den, 1.0))
    for sl, p in zip(heads, probs):
        oc_ref[:, sl] = _dot(p.astype(BF16), vc).astype(oc_ref.dtype)
    imp = sum(probs[1:], probs[0])
    jj = lax.broadcasted_iota(jnp.int32, (ns, nc), 0)
    cc = lax.broadcasted_iota(jnp.int32, (ns, nc), 1) * CMP_STRIDE
    ov = jnp.where((cc < (jj + 1) * SEL_LEN) & (cc + CMP_LEN > jj * SEL_LEN), 1.0, 0.0).astype(BF16)
    imp_t = sum(_nt_dot(ov, part) for part in _split3(imp))
    jb = lax.broadcasted_iota(jnp.int32, (ns, tq), 0)
    tt = q0 + lax.broadcasted_iota(jnp.int32, (ns, tq), 1)
    cur = lax.shift_right_logical(tt, jnp.full_like(tt, SEL_SHIFT))
    forced = (jb == 0) | (jb == cur) | (jb == cur - 1)
    valid = jb * SEL_LEN <= tt
    score = jnp.where(valid, imp_t + jnp.where(forced, FORCE_BONUS, 0.0), NEG)
    rank = jnp.zeros((ns, tq), F32)
    for i in range(ns):
        row = score[i:i + 1, :]
        rank = rank + jnp.where(row > score, 1.0, jnp.where((row == score) & (jb > i), 1.0, 0.0))
    sel_t = jnp.where(valid & (rank < float(min(SEL_TOPN, ns))), 1.0, 0.0).astype(BF16)
    eye = jnp.where(lax.broadcasted_iota(jnp.int32, (tq, tq), 0) == lax.broadcasted_iota(jnp.int32, (tq, tq), 1),
                    1.0, 0.0).astype(BF16)
    sel_ref[...] = _nt_dot(eye, sel_t)


def _nsa_cmp(qkv, kc, vc, b, t, tq=256):
    nc = kc.shape[2]
    ns = t // SEL_LEN
    nq = t // tq
    gw = NSA_GROUP * HEAD_DIM
    return pl.pallas_call(
        functools.partial(_nsa_cmp_kernel, tq=tq, nc=nc, ns=ns),
        grid=(b, NSA_KV_HEADS, nq),
        in_specs=[pl.BlockSpec((tq, gw), lambda i, k, j: (i * nq + j, k)),
                  pl.BlockSpec((None, None, nc, HEAD_DIM), lambda i, k, j: (i, k, 0, 0)),
                  pl.BlockSpec((None, None, nc, HEAD_DIM), lambda i, k, j: (i, k, 0, 0))],
        out_specs=[pl.BlockSpec((tq, gw), lambda i, k, j: (i * nq + j, k)),
                   pl.BlockSpec((None, None, tq, ns), lambda i, k, j: (i, k, j, 0))],
        out_shape=[jax.ShapeDtypeStruct((b * t, NSA_Q_W), BF16),
                   jax.ShapeDtypeStruct((b, NSA_KV_HEADS, t, ns), F32)],
        compiler_params=_cp("parallel", "parallel", "parallel"),
        name="nsa_cmp_select",
    )(qkv, kc, vc)


def _flash_update(s, v, bias, state, g, tq):
    m, l, acc = state
    tk = v.shape[0]
    if bias is not None:
        s = (s.reshape(g, tq, tk) + bias[None]).reshape(g * tq, tk)
    m_new = jnp.maximum(m, jnp.max(s, axis=-1, keepdims=True))
    alpha = jnp.exp(m - m_new)
    p = jnp.exp((s - m_new).astype(BF16))
    pv = _dot(p, jnp.concatenate([v, jnp.ones_like(v)], axis=1))
    return m_new, alpha * l + pv[:, HEAD_DIM:], alpha * acc + pv[:, :HEAD_DIM]


def _nsa_sel_win_kernel(q_ref, ks_ref, vs_ref, kw_ref, vw_ref, sel_ref, oc_ref, g_ref, o_ref, *, tq, tk, stack, ns):
    q0 = pl.program_id(1) * tq
    g = NSA_GROUP
    nkv = NSA_KV_HEADS
    per_tile = tk // SEL_LEN
    units = [(kv, h0) for kv in range(nkv) for h0 in range(0, g, stack)]
    kv_lanes = [slice(kv * HEAD_DIM, (kv + 1) * HEAD_DIM) for kv in range(nkv)]

    def q_lanes(kv, h):
        return slice((kv * g + h) * HEAD_DIM, (kv * g + h + 1) * HEAD_DIM)

    qs = [jnp.concatenate([q_ref[:, q_lanes(kv, h0 + j)] for j in range(stack)], axis=0) for kv, h0 in units]
    sel_bias = [((sel_ref[kv] - 1.0) * (-NEG)).astype(BF16) for kv in range(nkv)]
    ej = lax.broadcasted_iota(jnp.int32, (ns, tk), 0)
    eb = lax.broadcasted_iota(jnp.int32, (ns, tk), 1)
    eb = lax.shift_right_logical(eb, jnp.full_like(eb, SEL_SHIFT))

    def init():
        return tuple((jnp.full((stack * tq, 1), NEG, F32), jnp.zeros((stack * tq, HEAD_DIM), F32),
                      jnp.zeros((stack * tq, HEAD_DIM), F32)) for _ in units)

    def flash_all(k_ref, v_ref, k0, width, biases, states):
        scores = [_nt_dot(qs[u], k_ref[pl.ds(k0, width), kv_lanes[kv]]) for u, (kv, _) in enumerate(units)]
        return tuple(_flash_update(scores[u], v_ref[pl.ds(k0, width), kv_lanes[kv]], biases[kv], states[u], stack, tq)
                     for u, (kv, _) in enumerate(units))

    def sel_step(kt, states, causal):
        k0 = pl.multiple_of(kt * tk, tk)
        expand = jnp.where(ej == eb + kt * per_tile, 1.0, 0.0).astype(BF16)
        biases = [_dot(sel_bias[kv], expand) for kv in range(nkv)]
        if causal is not None:
            biases = [bias + causal for bias in biases]
        return flash_all(ks_ref, vs_ref, k0, tk, biases, states)

    kt_d = q0 // tk
    t_row = q0 + lax.broadcasted_iota(jnp.int32, (tq, tk), 0)
    col = kt_d * tk + lax.broadcasted_iota(jnp.int32, (tq, tk), 1)
    states = sel_step(kt_d, init(), jnp.where(col <= t_row, 0.0, NEG))
    states = lax.fori_loop(0, kt_d, lambda kt, st: sel_step(kt, st, None), states)
    o_s = [acc * (1.0 / l) for (_, l, acc) in states]

    def win_step(k0, states, bias):
        return flash_all(kw_ref, vw_ref, k0, tq, [bias] * nkv, states)

    ri = lax.broadcasted_iota(jnp.int32, (tq, tq), 0)
    ci = lax.broadcasted_iota(jnp.int32, (tq, tq), 1)
    states = win_step(pl.multiple_of(q0, tq), init(), jnp.where(ci <= ri, 0.0, NEG))
    n_full = jnp.minimum(WINDOW // tq - 1, q0 // tq)
    states = lax.fori_loop(0, n_full, lambda i, st: win_step(pl.multiple_of(q0 - (i + 1) * tq, tq), st, None),
                           states)
    edge_bias = jnp.where(ci > ri, 0.0, NEG) + jnp.where(q0 >= WINDOW, 0.0, NEG)
    states = win_step(pl.multiple_of(jnp.maximum(q0 - WINDOW, 0), tq), states, edge_bias)
    o_w = [acc * (1.0 / l) for (_, l, acc) in states]

    for u, (kv, h0) in enumerate(units):
        gates = g_ref[:, kv * LANES:(kv + 1) * LANES]
        for j in range(stack):
            h = h0 + j
            sl = q_lanes(kv, h)
            rows = slice(j * tq, (j + 1) * tq)
            out = (gates[:, 3 * h:3 * h + 1] * oc_ref[:, sl].astype(F32)
                   + gates[:, 3 * h + 1:3 * h + 2] * o_s[u][rows] + gates[:, 3 * h + 2:3 * h + 3] * o_w[u][rows])
            o_ref[:, sl] = out.astype(o_ref.dtype)


def _nsa_sel_win(qkv, sel, oc, gates, b, t, tq=256, tk=256, stack=4):
    nq = t // tq
    ns = t // SEL_LEN
    kv_block = NSA_Q_W // NSA_KV_W

    def kv_spec(which):
        return pl.BlockSpec((t, NSA_KV_W), lambda i, j: (i, kv_block + which))

    return pl.pallas_call(
        functools.partial(_nsa_sel_win_kernel, tq=tq, tk=tk, stack=stack, ns=ns),
        grid=(b, nq),
        in_specs=[pl.BlockSpec((tq, NSA_Q_W), lambda i, j: (i * nq + j, 0)),
                  kv_spec(2), kv_spec(3), kv_spec(4), kv_spec(5),
                  pl.BlockSpec((None, NSA_KV_HEADS, tq, ns), lambda i, j: (i, 0, j, 0)),
                  pl.BlockSpec((tq, NSA_Q_W), lambda i, j: (i * nq + j, 0)),
                  pl.BlockSpec((tq, NSA_KV_HEADS * LANES), lambda i, j: (i * nq + j, 0))],
        out_specs=pl.BlockSpec((tq, NSA_Q_W), lambda i, j: (i * nq + j, 0)),
        out_shape=jax.ShapeDtypeStruct((b * t, NSA_Q_W), BF16),
        compiler_params=_cp("parallel", "parallel"),
        name="nsa_select_window",
    )(qkv, qkv, qkv, qkv, qkv, sel, oc, gates)


def _stickbreak_kernel(q_ref, k_ref, v_ref, o_ref, *, tile, nh):
    q0 = pl.program_id(2) * tile
    ri = lax.broadcasted_iota(jnp.int32, (tile, tile), 0)
    ci = lax.broadcasted_iota(jnp.int32, (tile, tile), 1)
    strict = ci < ri
    tri = jnp.where(ri > ci, 1.0, 0.0).astype(BF16)
    lanes = [slice(h * HEAD_DIM, (h + 1) * HEAD_DIM) for h in range(nh)]

    def step(k0, diag, carry):
        zs = [_nt_dot(q_ref[:, lanes[h]], k_ref[pl.ds(k0, tile), lanes[h]]) for h in range(nh)]
        log_bs, sufs, sums = [], [], []
        for z in zs:
            log_b = jnp.minimum(z, 0.0) - jnp.log(1.0 + jnp.exp(-jnp.abs(z)))
            log_1m = log_b - z
            if diag:
                log_1m = jnp.where(strict, log_1m, 0.0)
            hi = log_1m.astype(BF16)
            lo = (log_1m - hi.astype(F32)).astype(BF16)
            log_bs.append(log_b)
            sufs.append(_dot(hi, tri) + _dot(lo, tri))
            sums.append(jnp.sum(log_1m, axis=-1, keepdims=True))
        out = []
        for h in range(nh):
            later, acc = carry[2 * h], carry[2 * h + 1]
            a = jnp.exp(log_bs[h] + (sufs[h] + later))
            if diag:
                a = jnp.where(strict, a, 0.0)
            out.extend((later + sums[h], acc + _dot(a.astype(BF16), v_ref[pl.ds(k0, tile), lanes[h]])))
        return tuple(out)

    zero = (jnp.zeros((tile, 1), F32), jnp.zeros((tile, HEAD_DIM), F32)) * nh
    carry = step(pl.multiple_of(q0, tile), True, zero)

    def live(state):
        i, c = state
        top = c[0]
        for h in range(1, nh):
            top = jnp.maximum(top, c[2 * h])
        return (i < q0 // tile) & (jnp.max(top) > F32_EXP_UNDERFLOW)

    def advance(state):
        i, c = state
        return i + 1, step(pl.multiple_of(q0 - (i + 1) * tile, tile), False, c)

    _, carry = lax.while_loop(live, advance, (jnp.int32(0), carry))
    for h in range(nh):
        o_ref[:, lanes[h]] = carry[2 * h + 1].astype(o_ref.dtype)


def _stickbreak(qkv, b, t, tile=256, nh=4):
    nq = t // tile
    w = nh * HEAD_DIM
    nhb = SB_HEADS // nh
    return pl.pallas_call(
        functools.partial(_stickbreak_kernel, tile=tile, nh=nh),
        grid=(b, nhb, nq),
        in_specs=[pl.BlockSpec((tile, w), lambda i, h, j: (i * nq + j, h)),
                  pl.BlockSpec((t, w), lambda i, h, j: (i, nhb + h)),
                  pl.BlockSpec((t, w), lambda i, h, j: (i, 2 * nhb + h))],
        out_specs=pl.BlockSpec((tile, w), lambda i, h, j: (i * nq + j, h)),
        out_shape=jax.ShapeDtypeStruct((b * t, SB_W), BF16),
        compiler_params=_cp("parallel", "parallel", "parallel"),
        name="stickbreak",
    )(qkv, qkv, qkv)


def _cross_kernel(x_ref, g_ref, wq_ref, kv_ref, wo_ref, o_ref):
    x = x_ref[...]
    h = _rms(x, g_ref[...]).astype(BF16)
    q = _dot(h, wq_ref[...]).astype(BF16)
    xw = X_HEADS * HEAD_DIM
    outs = []
    for hd in range(X_HEADS):
        sl = slice(hd * HEAD_DIM, (hd + 1) * HEAD_DIM)
        s = _nt_dot(q[:, sl], kv_ref[:, sl]) * SCALE
        e = jnp.exp(s - jnp.max(s, axis=-1, keepdims=True))
        p = e / jnp.sum(e, axis=-1, keepdims=True)
        outs.append(_dot(p.astype(BF16), kv_ref[:, xw + hd * HEAD_DIM:xw + (hd + 1) * HEAD_DIM]).astype(BF16))
    o_ref[...] = x + _dot(jnp.concatenate(outs, axis=1), wo_ref[...])


def _cross(x, g, wq, kv, wo, b, t, tq=512):
    n, d = x.shape
    nq = t // tq
    mem_len = kv.shape[0] // b
    return pl.pallas_call(
        _cross_kernel,
        grid=(b, nq),
        in_specs=[pl.BlockSpec((tq, d), lambda i, j: (i * nq + j, 0)),
                  pl.BlockSpec((1, d), lambda i, j: (0, 0)),
                  pl.BlockSpec(wq.shape, lambda i, j: (0, 0)),
                  pl.BlockSpec((mem_len, kv.shape[1]), lambda i, j: (i, 0)),
                  pl.BlockSpec(wo.shape, lambda i, j: (0, 0))],
        out_specs=pl.BlockSpec((tq, d), lambda i, j: (i * nq + j, 0)),
        out_shape=jax.ShapeDtypeStruct((n, d), F32),
        compiler_params=_cp("parallel", "parallel"),
        name="cross_attention",
    )(x, g.reshape(1, d), wq, kv, wo)


def _to_row_tiles(ref, val):
    rows, width = val.shape
    c_per = width // LANES
    for c in range(c_per):
        ref[pl.ds(c, rows, stride=c_per), :] = val[:, c * LANES:(c + 1) * LANES]


def _from_row_tiles(ref, c_per):
    rows = ref.shape[0] // c_per
    return jnp.concatenate([ref[pl.ds(c, rows, stride=c_per), :] for c in range(c_per)], axis=1)


def _router_kernel(x_ref, g_ref, w_ref, b_ref, h_ref, r_ref):
    h = _rms(x_ref[...], g_ref[...])
    _to_row_tiles(h_ref, h)
    h_hi = h.astype(BF16)
    h_mid = (h - h_hi.astype(F32)).astype(BF16)
    two = _dot(h_hi, w_ref[...])
    logits = two[:, :LANES] + two[:, LANES:] + _dot(h_mid, w_ref[:, :LANES]) + b_ref[...]
    lane = lax.broadcasted_iota(jnp.int32, logits.shape, 1).astype(F32)
    big = float(LANES)
    is_g = lane < N_GROUPS
    lg = jnp.where(is_g, logits, NEG)
    mg = jnp.max(lg, axis=-1, keepdims=True)
    p_gtop = 1.0 / jnp.sum(jnp.where(is_g, jnp.exp(lg - mg), 0.0), axis=-1, keepdims=True)
    g_top = jnp.min(jnp.where(lg == mg, lane, big), axis=-1, keepdims=True)
    lo = N_GROUPS + EXPERTS_PER_GROUP * g_top
    in_grp = (lane >= lo) & (lane < lo + EXPERTS_PER_GROUP)
    le = jnp.where(in_grp, logits, NEG)
    ee = jnp.where(in_grp, jnp.exp(le - jnp.max(le, axis=-1, keepdims=True)), 0.0)
    p_in = ee / jnp.sum(ee, axis=-1, keepdims=True)
    p1 = jnp.where(in_grp, p_in, -1.0)
    m1 = jnp.max(p1, axis=-1, keepdims=True)
    i1 = jnp.min(jnp.where(p1 == m1, lane, big), axis=-1, keepdims=True)
    p2 = jnp.where(lane == i1, -1.0, p1)
    m2 = jnp.max(p2, axis=-1, keepdims=True)
    i2 = jnp.min(jnp.where(p2 == m2, lane, big), axis=-1, keepdims=True)
    tot = m1 + m2
    r_ref[...] = jnp.where(lane == 0, i1 - N_GROUPS,
                           jnp.where(lane == 1, i2 - N_GROUPS,
                                     jnp.where(lane == 2, m1 / tot * p_gtop,
                                               jnp.where(lane == 3, m2 / tot * p_gtop, 0.0))))


def _router(x, g, w_r, b_r, tm=512):
    n, d = x.shape
    return pl.pallas_call(
        _router_kernel,
        grid=(n // tm,),
        in_specs=[pl.BlockSpec((tm, d), lambda i: (i, 0)), pl.BlockSpec((1, d), lambda i: (0, 0)),
                  pl.BlockSpec((d, 2 * LANES), lambda i: (0, 0)), pl.BlockSpec((1, LANES), lambda i: (0, 0))],
        out_specs=[pl.BlockSpec((tm * (d // LANES), LANES), lambda i: (i, 0)),
                   pl.BlockSpec((tm, LANES), lambda i: (i, 0))],
        out_shape=[jax.ShapeDtypeStruct((n * (d // LANES), LANES), F32), jax.ShapeDtypeStruct((n, LANES), F32)],
        compiler_params=_cp("parallel"),
        name="moe_router",
    )(x, g.reshape(1, d), w_r, b_r)


ROW_CHUNKS = 16
WEIGHT_DMA_PRIORITY = 0


def _wait_rows(src_hbm, dst, sem):
    pltpu.make_async_copy(src_hbm.at[pl.ds(0, dst.shape[0]), :], dst, sem).wait()


def _row_copy(src_hbm, dst, sem, src_row, dst_row):
    c = ROW_CHUNKS
    return pltpu.make_async_copy(src_hbm.at[pl.ds(pl.multiple_of(src_row * c, c), c), :],
                                 dst.at[pl.ds(pl.multiple_of(dst_row * c, c), c), :], sem)


def _expert_kernel(idx_ref, ord_ref, oe_ref, cnt_ref, src_hbm, wg_hbm, wu_hbm, wd_hbm, o_ref,
                   xbuf, sems, wg_f, wu_f, wd_f, wsems, wg_s, wu_s, wd_s, *, tm):
    i = pl.program_id(0)
    nt = cnt_ref[0]
    n_ord = cnt_ref[1]
    live = i < nt
    slot = lax.rem(i, 2)

    def gather(tile, s):
        def issue(r2, c):
            for queue in range(2):
                r = 2 * r2 + queue
                _row_copy(src_hbm, xbuf.at[s], sems.at[s], idx_ref[tile * tm + r], r).start(priority=queue)
            return c
        lax.fori_loop(0, tm // 2, issue, 0, unroll=4)

    def weight_copies(k, s):
        e = oe_ref[k]
        return (pltpu.make_async_copy(wg_hbm.at[e], wg_f.at[s], wsems.at[s, 0]),
                pltpu.make_async_copy(wu_hbm.at[e], wu_f.at[s], wsems.at[s, 1]),
                pltpu.make_async_copy(wd_hbm.at[e], wd_f.at[s], wsems.at[s, 2]))

    @pl.when(i == 0)
    def _():
        gather(0, 0)
        for cp in weight_copies(0, 0):
            cp.start(priority=WEIGHT_DMA_PRIORITY)

    @pl.when(i + 1 < nt)
    def _():
        gather(i + 1, 1 - slot)

    k = ord_ref[i]
    fresh = live & ((i == 0) | (k != ord_ref[jnp.maximum(i - 1, 0)]))
    ws = lax.rem(k, 2)

    @pl.when(fresh)
    def _():
        for cp in weight_copies(k, ws):
            cp.wait()

        @pl.when(k + 1 < n_ord)
        def _():
            for cp in weight_copies(k + 1, 1 - ws):
                cp.start(priority=WEIGHT_DMA_PRIORITY)

        wg_s[...] = wg_f[ws].astype(BF16)
        wu_s[...] = wu_f[ws].astype(BF16)
        wd_s[...] = wd_f[ws].astype(BF16)

    @pl.when(live)
    def _():
        _wait_rows(src_hbm, xbuf.at[slot], sems.at[slot])
        x = _from_row_tiles(xbuf.at[slot], ROW_CHUNKS).astype(BF16)
        a = _dot(x, wg_s[...])
        u = _dot(x, wu_s[...])
        act = a * (1.0 / (1.0 + jnp.exp(-a))) * u
        _to_row_tiles(o_ref, _dot(act.astype(BF16), wd_s[...]))

    @pl.when(jnp.logical_not(live))
    def _():
        o_ref[...] = jnp.zeros_like(o_ref)


def _experts(src, tok_of_row, w_g, w_u, w_d, tile_ord, ord_expert, counts, tm=EXPERT_TILE):
    c = ROW_CHUNKS
    n_tiles_max = tok_of_row.shape[0] // tm
    d = c * LANES
    f = w_g.shape[2]
    any_spec = pl.BlockSpec(memory_space=pl.ANY)
    return pl.pallas_call(
        functools.partial(_expert_kernel, tm=tm),
        grid_spec=pltpu.PrefetchScalarGridSpec(
            num_scalar_prefetch=4,
            grid=(n_tiles_max,),
            in_specs=[any_spec, any_spec, any_spec, any_spec],
            out_specs=pl.BlockSpec((tm * c, LANES), lambda i, *_: (i, 0)),
            scratch_shapes=[pltpu.VMEM((2, tm * c, LANES), F32), pltpu.SemaphoreType.DMA((2,)),
                            pltpu.VMEM((2, d, f), F32), pltpu.VMEM((2, d, f), F32), pltpu.VMEM((2, f, d), F32),
                            pltpu.SemaphoreType.DMA((2, 3)),
                            pltpu.VMEM((d, f), BF16), pltpu.VMEM((d, f), BF16), pltpu.VMEM((f, d), BF16)]),
        out_shape=jax.ShapeDtypeStruct((n_tiles_max * tm * c, LANES), F32),
        compiler_params=_cp("arbitrary"),
        name="moe_experts",
    )(tok_of_row, tile_ord, ord_expert, counts, src, w_g, w_u, w_d)


def _combine_kernel(p0_ref, p1_ref, ys_hbm, x_ref, r_ref, g_ref, o_ref, buf, sems, *, tm):
    i = pl.program_id(0)
    slot = lax.rem(i, 2)

    def fetch(tile, s):
        def issue(r, c):
            _row_copy(ys_hbm, buf.at[s, 0], sems.at[s], p0_ref[tile * tm + r], r).start()
            _row_copy(ys_hbm, buf.at[s, 1], sems.at[s], p1_ref[tile * tm + r], r).start(priority=1)
            return c
        lax.fori_loop(0, tm, issue, 0, unroll=8)

    @pl.when(i == 0)
    def _():
        fetch(0, 0)

    @pl.when(i + 1 < pl.num_programs(0))
    def _():
        fetch(i + 1, 1 - slot)

    _wait_rows(ys_hbm, buf.at[slot, 0], sems.at[slot])
    _wait_rows(ys_hbm, buf.at[slot, 1], sems.at[slot])
    route = r_ref[...]
    y = (route[:, 2:3] * _from_row_tiles(buf.at[slot, 0], ROW_CHUNKS)
         + route[:, 3:4] * _from_row_tiles(buf.at[slot, 1], ROW_CHUNKS))
    o_ref[...] = _rms(x_ref[...] + y, g_ref[...])


def _combine(ys, pos0, pos1, x, route, g, tm=256):
    n, d = x.shape
    return pl.pallas_call(
        functools.partial(_combine_kernel, tm=tm),
        grid_spec=pltpu.PrefetchScalarGridSpec(
            num_scalar_prefetch=2,
            grid=(n // tm,),
            in_specs=[pl.BlockSpec(memory_space=pl.ANY),
                      pl.BlockSpec((tm, d), lambda i, p0, p1: (i, 0)),
                      pl.BlockSpec((tm, LANES), lambda i, p0, p1: (i, 0)),
                      pl.BlockSpec((1, d), lambda i, p0, p1: (0, 0))],
            out_specs=pl.BlockSpec((tm, d), lambda i, p0, p1: (i, 0)),
            scratch_shapes=[pltpu.VMEM((2, 2, tm * ROW_CHUNKS, LANES), F32), pltpu.SemaphoreType.DMA((2,))]),
        out_shape=jax.ShapeDtypeStruct((n, d), F32),
        compiler_params=_cp("arbitrary"),
        name="moe_combine_norm",
    )(pos0, pos1, ys, x, route, g.reshape(1, d))


def _dispatch_plan(e1, e2, tm=EXPERT_TILE):
    n = e1.shape[0]
    n_tiles_max = (2 * n) // tm + N_EXPERTS
    rows = n_tiles_max * tm
    e_flat = jnp.concatenate([e1, e2])
    tok = jnp.concatenate([jnp.arange(n, dtype=jnp.int32)] * 2)
    onehot = (e_flat[:, None] == jnp.arange(N_EXPERTS, dtype=jnp.int32)[None, :]).astype(jnp.int32)
    csum = jnp.cumsum(onehot, axis=0)
    count = csum[-1]
    rank = jnp.sum((csum - 1) * onehot, axis=1)
    tiles_per = (count + tm - 1) // tm
    tile_end = jnp.cumsum(tiles_per)
    row_start = (tile_end - tiles_per) * tm
    dest = (row_start[e_flat] + rank).astype(jnp.int32)
    tok_of_row = jnp.zeros((rows,), jnp.int32).at[dest].set(tok)
    n_tiles = tile_end[-1].astype(jnp.int32)
    tile_id = jnp.minimum(jnp.arange(n_tiles_max, dtype=jnp.int32), n_tiles - 1)
    tile_expert = jnp.sum((tile_id[:, None] >= tile_end[None, :]).astype(jnp.int32), axis=1)
    used = jnp.cumsum((tiles_per > 0).astype(jnp.int32))
    tile_ord = (used[tile_expert] - 1).astype(jnp.int32)
    ordinals = jnp.arange(N_EXPERTS, dtype=jnp.int32)
    ord_expert = jnp.minimum(jnp.sum((used[None, :] <= ordinals[:, None]).astype(jnp.int32), axis=1),
                             N_EXPERTS - 1).astype(jnp.int32)
    counts = jnp.stack([n_tiles, used[-1]]).astype(jnp.int32)
    return tok_of_row, tile_ord, ord_expert, counts, dest[:n], dest[n:]


def _rope_tables(positions):
    half = ROT_DIM // 2
    inv = ROPE_THETA ** (-jnp.arange(half, dtype=F32) * (2.0 / ROT_DIM))
    ang = positions.astype(F32).reshape(-1, 1) * inv[None, :]
    cos, sin = jnp.cos(ang), jnp.sin(ang)
    n = ang.shape[0]
    c = jnp.concatenate([cos, cos, jnp.ones((n, LANES - ROT_DIM), F32)], axis=1)
    s1 = jnp.concatenate([-sin, jnp.zeros((n, LANES - half), F32)], axis=1)
    s2 = jnp.concatenate([jnp.zeros((n, half), F32), sin, jnp.zeros((n, LANES - ROT_DIM), F32)], axis=1)
    return c, s1, s2


def _layer(x, mem, positions, g_mix, w_in, cmp_pe_k, cmp_w1_k, cmp_w2_k, cmp_pe_v, cmp_w1_v, cmp_w2_v,
           w_br_nsa, w_br_sb, w_o, g_cross, g_mem, w_cq, w_ck, w_cv, w_co,
           g_moe, w_rg, b_rg, w_re, b_re, w_eg, w_eu, w_ed):
    b, t, d = x.shape
    n = b * t
    xf = x.reshape(n, d)

    g0 = NSA_W
    w_t = w_in.T
    w_nsa = _cast_rows(w_t, 0, g0)
    w_rest = _cast_rows(w_t, g0 + NSA_GATES, w_t.shape[0] - g0 - NSA_GATES)
    w_g = w_t[g0:g0 + NSA_GATES].reshape(NSA_KV_HEADS, 3 * NSA_GROUP, d)
    w_g = jnp.pad(w_g, ((0, 0), (0, LANES - 3 * NSA_GROUP), (0, 0))).reshape(NSA_KV_HEADS * LANES, d).astype(BF16)
    h = _rmsnorm(xf, g_mix, BF16)
    c, s1, s2 = _rope_tables(positions)
    row_spec = pl.BlockSpec((1024, LANES), lambda i, j: (i, 0))
    qkv_nsa = _mm(functools.partial(_mm_nsa_kernel, tn=256), h, w_nsa, NSA_W, 0, 1024, 256, BF16, w_t=True,
                  extra=(c, s1, s2), extra_specs=(row_spec, row_spec, row_spec), name="in_proj_nsa")
    qkv_sb = _mm(functools.partial(_mm_sb_kernel, q_tiles=SB_W // 512), h, w_rest, 3 * SB_W, 0,
                 1024, 512, BF16, w_t=True, name="in_proj_sb")
    gates = _mm(_mm_sigmoid_kernel, h, w_rest, 2 * d, 3 * SB_W // 512, 1024, 512, BF16, w_t=True,
                name="in_proj_gates")
    nsa_gates = _mm(_mm_sigmoid_kernel, h, w_g, NSA_KV_HEADS * LANES, 0, 1024, NSA_KV_HEADS * LANES, F32,
                    w_t=True, name="in_proj_nsa_gates")

    def chunked(col0):
        tok = qkv_nsa[:, col0:col0 + NSA_KV_W].reshape(b, t, NSA_KV_HEADS, HEAD_DIM).transpose(0, 2, 1, 3)
        return tok.reshape(b, NSA_KV_HEADS, t // CMP_STRIDE, CMP_STRIDE * HEAD_DIM)

    kc = _compress(chunked(NSA_Q_W), cmp_pe_k, cmp_w1_k, cmp_w2_k)
    vc = _compress(chunked(NSA_Q_W + NSA_KV_W), cmp_pe_v, cmp_w1_v, cmp_w2_v)
    o_c, sel = _nsa_cmp(qkv_nsa, kc, vc, b, t)
    o_nsa = _nsa_sel_win(qkv_nsa, sel, o_c, nsa_gates, b, t)
    o_sb = _stickbreak(qkv_sb, b, t)
    merged = _merge(o_nsa, o_sb, w_br_nsa.astype(BF16), w_br_sb.astype(BF16), gates)
    x1 = _mm(_mm_resid_kernel, merged, w_o.astype(BF16), d, 0, 1024, 512, F32,
             extra=(xf,), extra_specs=(pl.BlockSpec((1024, 512), lambda i, j: (i, j)),), name="out_proj")

    w_kv = jnp.concatenate([w_ck, w_cv], axis=1).astype(BF16)
    kv = _norm_mm(mem.reshape(-1, d), g_mem, w_kv, BF16)
    x2 = _cross(x1, g_cross, w_cq.astype(BF16), kv, w_co.astype(BF16), b, t)

    w_r = jnp.pad(jnp.concatenate([w_rg, w_re], axis=1), ((0, 0), (0, LANES - N_GROUPS - N_EXPERTS)))
    b_r = jnp.pad(jnp.concatenate([b_rg, b_re]), (0, LANES - N_GROUPS - N_EXPERTS)).reshape(1, LANES)
    w_r_hi = w_r.astype(BF16)
    w_r2 = jnp.concatenate([w_r_hi, (w_r - w_r_hi.astype(F32)).astype(BF16)], axis=1)
    hm, route = _router(x2, g_moe, w_r2, b_r)
    e1 = route[:, 0].astype(jnp.int32)
    e2 = route[:, 1].astype(jnp.int32)
    tok_of_row, tile_ord, ord_expert, counts, pos0, pos1 = _dispatch_plan(e1, e2)
    f = w_eg.shape[-1]
    ys = _experts(hm, tok_of_row, w_eg.reshape(N_EXPERTS, d, f), w_eu.reshape(N_EXPERTS, d, f),
                  w_ed.reshape(N_EXPERTS, f, d), tile_ord, ord_expert, counts)
    return ys, pos0, pos1, x2, route


def kernel(x, mem, positions, g_mix, w_in, cmp_pe_k, cmp_w1_k, cmp_w2_k, cmp_pe_v, cmp_w1_v, cmp_w2_v, w_br_nsa, w_br_sb, w_o, g_cross, g_mem, w_cq, w_ck, w_cv, w_co, g_moe, w_rg, b_rg, w_re, b_re, w_eg, w_eu, w_ed, g_final):
    b, t, d = x.shape
    assert w_in.shape[0] == 1, "one layer"
    ys, pos0, pos1, x2, route = _layer(
        x, mem, positions, g_mix[0], w_in[0], cmp_pe_k[0], cmp_w1_k[0], cmp_w2_k[0], cmp_pe_v[0], cmp_w1_v[0],
        cmp_w2_v[0], w_br_nsa[0], w_br_sb[0], w_o[0], g_cross[0], g_mem[0], w_cq[0], w_ck[0], w_cv[0], w_co[0],
        g_moe[0], w_rg[0], b_rg[0], w_re[0], b_re[0], w_eg[0], w_eu[0], w_ed[0])
    return _combine(ys, pos0, pos1, x2, route, g_final).reshape(b, t, d)
```

```python
import functools

import jax
import jax.numpy as jnp
from jax import lax
from jax.experimental import pallas as pl
from jax.experimental.pallas import tpu as pltpu

F32 = jnp.float32
BF16 = jnp.bfloat16

HEAD_DIM = 128
NSA_HEADS = 8
NSA_KV_HEADS = 2
NSA_GROUP = NSA_HEADS // NSA_KV_HEADS
SB_HEADS = 8
ROT_DIM = HEAD_DIM // 4
ROPE_THETA = 500000.0
CMP_LEN = 32
CMP_STRIDE = 16
SEL_LEN = 64
SEL_SHIFT = 6
SEL_TOPN = 16
WINDOW = 512
FORCE_BONUS = 1000.0
X_HEADS = 4
N_GROUPS = 4
EXPERTS_PER_GROUP = 8
N_EXPERTS = N_GROUPS * EXPERTS_PER_GROUP
EPS = 1e-6
SCALE = HEAD_DIM ** -0.5
NEG = -1e30
F32_EXP_UNDERFLOW = -104.0

NSA_Q_W = NSA_HEADS * HEAD_DIM
NSA_KV_W = NSA_KV_HEADS * HEAD_DIM
NSA_W = NSA_Q_W + 6 * NSA_KV_W
NSA_GATES = 3 * NSA_HEADS
SB_W = SB_HEADS * HEAD_DIM
LANES = 128
VMEM_LIMIT = 56 * 1024 * 1024

EXPERT_TILE = 256


def _cp(*sem, vmem=VMEM_LIMIT):
    return pltpu.CompilerParams(dimension_semantics=sem, vmem_limit_bytes=vmem)


def _nt_dot(a, b):
    return lax.dot_general(a, b, (((1,), (1,)), ((), ())), preferred_element_type=F32)


def _dot(a, b):
    return jnp.dot(a, b, preferred_element_type=F32)


def _split3(x):
    hi = x.astype(BF16)
    r1 = x - hi.astype(F32)
    mid = r1.astype(BF16)
    lo = (r1 - mid.astype(F32)).astype(BF16)
    return hi, mid, lo


def _rms(x, g):
    return x * lax.rsqrt(jnp.mean(x * x, axis=-1, keepdims=True) + EPS) * g


def _rmsnorm_kernel(x_ref, g_ref, o_ref):
    o_ref[...] = _rms(x_ref[...].astype(F32), g_ref[...]).astype(o_ref.dtype)


def _rmsnorm(x, g, out_dtype, tm=512):
    n, d = x.shape
    return pl.pallas_call(
        _rmsnorm_kernel,
        grid=(n // tm,),
        in_specs=[pl.BlockSpec((tm, d), lambda i: (i, 0)), pl.BlockSpec((1, d), lambda i: (0, 0))],
        out_specs=pl.BlockSpec((tm, d), lambda i: (i, 0)),
        out_shape=jax.ShapeDtypeStruct((n, d), out_dtype),
        compiler_params=_cp("parallel"),
        name="rmsnorm",
    )(x, g.reshape(1, d))


def _cast_rows_kernel(w_ref, o_ref):
    o_ref[...] = w_ref[...].astype(o_ref.dtype)


def _cast_rows(w_t, row0, n_rows, tn=512):
    k = w_t.shape[1]
    assert n_rows % tn == 0 and row0 % 8 == 0
    return pl.pallas_call(
        _cast_rows_kernel,
        grid=(n_rows // tn,),
        in_specs=[pl.BlockSpec((pl.Element(tn), pl.Element(k)), lambda j: ((row0 // 8 + j * (tn // 8)) * 8, 0))],
        out_specs=pl.BlockSpec((tn, k), lambda j: (j, 0)),
        out_shape=jax.ShapeDtypeStruct((n_rows, k), BF16),
        compiler_params=_cp("parallel"),
        name="cast_rows",
    )(w_t)


def _rope_half(x, c, s1, s2):
    return x * c + pltpu.roll(x, LANES - ROT_DIM // 2, 1) * s1 + pltpu.roll(x, ROT_DIM // 2, 1) * s2


def _mm_nsa_kernel(a_ref, w_ref, c_ref, s1_ref, s2_ref, o_ref, *, tn):
    j = pl.program_id(1)
    acc = _nt_dot(a_ref[...], w_ref[...])
    is_rope = (j < 5) | (j == 6) | (j == 8)

    @pl.when(is_rope)
    def _():
        qs = jnp.where(j < NSA_Q_W // tn, SCALE, 1.0)
        c, s1, s2 = c_ref[...] * qs, s1_ref[...] * qs, s2_ref[...] * qs
        for hh in range(tn // LANES):
            sl = slice(hh * LANES, (hh + 1) * LANES)
            o_ref[:, sl] = _rope_half(acc[:, sl], c, s1, s2).astype(o_ref.dtype)

    @pl.when(jnp.logical_not(is_rope))
    def _():
        o_ref[...] = acc.astype(o_ref.dtype)


def _mm_sb_kernel(a_ref, w_ref, o_ref, *, q_tiles):
    acc = _nt_dot(a_ref[...], w_ref[...])
    is_q = pl.program_id(1) < q_tiles
    o_ref[...] = (acc * jnp.where(is_q, SCALE, 1.0)).astype(o_ref.dtype)


def _mm_sigmoid_kernel(a_ref, w_ref, o_ref):
    o_ref[...] = jax.nn.sigmoid(_nt_dot(a_ref[...], w_ref[...])).astype(o_ref.dtype)


def _mm_resid_kernel(a_ref, w_ref, x_ref, o_ref):
    o_ref[...] = x_ref[...] + _dot(a_ref[...], w_ref[...])


def _mm(kern, a, w, n_cols, col_block0, tm, tn, out_dtype, extra=(), extra_specs=(), name="mm", w_t=False):
    m, k = a.shape
    w_spec = (pl.BlockSpec((tn, k), lambda i, j: (col_block0 + j, 0)) if w_t
              else pl.BlockSpec((k, tn), lambda i, j: (0, col_block0 + j)))
    return pl.pallas_call(
        kern,
        grid=(m // tm, n_cols // tn),
        in_specs=[pl.BlockSpec((tm, k), lambda i, j: (i, 0)), w_spec, *extra_specs],
        out_specs=pl.BlockSpec((tm, tn), lambda i, j: (i, j)),
        out_shape=jax.ShapeDtypeStruct((m, n_cols), out_dtype),
        compiler_params=_cp("parallel", "arbitrary"),
        name=name,
    )(a, w, *extra)


def _merge_kernel(a1_ref, a2_ref, w1_ref, w2_ref, ga_ref, gb_ref, o_ref):
    ya = _dot(a1_ref[...], w1_ref[...])
    yb = _dot(a2_ref[...], w2_ref[...])
    o_ref[...] = (ga_ref[...].astype(F32) * ya + gb_ref[...].astype(F32) * yb).astype(o_ref.dtype)


def _merge(o_nsa, o_sb, w1, w2, gates, tm=1024, tn=512):
    n, k = o_nsa.shape
    d = w1.shape[1]
    nb = d // tn
    return pl.pallas_call(
        _merge_kernel,
        grid=(n // tm, nb),
        in_specs=[pl.BlockSpec((tm, k), lambda i, j: (i, 0)), pl.BlockSpec((tm, k), lambda i, j: (i, 0)),
                  pl.BlockSpec((k, tn), lambda i, j: (0, j)), pl.BlockSpec((k, tn), lambda i, j: (0, j)),
                  pl.BlockSpec((tm, tn), lambda i, j: (i, j)), pl.BlockSpec((tm, tn), lambda i, j: (i, nb + j))],
        out_specs=pl.BlockSpec((tm, tn), lambda i, j: (i, j)),
        out_shape=jax.ShapeDtypeStruct((n, d), BF16),
        compiler_params=_cp("parallel", "arbitrary"),
        name="merge",
    )(o_nsa, o_sb, w1, w2, gates, gates)


def _norm_mm_kernel(x_ref, g_ref, w_ref, o_ref):
    h = _rms(x_ref[...].astype(F32), g_ref[...]).astype(BF16)
    o_ref[...] = _dot(h, w_ref[...]).astype(o_ref.dtype)


def _norm_mm(x, g, w, out_dtype, tm=256):
    n, d = x.shape
    nc = w.shape[1]
    return pl.pallas_call(
        _norm_mm_kernel,
        grid=(n // tm,),
        in_specs=[pl.BlockSpec((tm, d), lambda i: (i, 0)), pl.BlockSpec((1, d), lambda i: (0, 0)),
                  pl.BlockSpec((d, nc), lambda i: (0, 0))],
        out_specs=pl.BlockSpec((tm, nc), lambda i: (i, 0)),
        out_shape=jax.ShapeDtypeStruct((n, nc), out_dtype),
        compiler_params=_cp("parallel"),
        name="norm_mm",
    )(x, g.reshape(1, d), w)


def _compress_kernel(x_ref, pe_ref, w1_ref, w2_ref, o_ref):
    x = x_ref[...].astype(F32)
    half = x.shape[1]
    xa = (x + pe_ref[0:1, :]).astype(BF16)
    xb = (x + pe_ref[1:2, :]).astype(BF16)
    y1 = _dot(xa, w1_ref[0:half, :].astype(BF16))
    y2 = _dot(xb, w1_ref[half:2 * half, :].astype(BF16))
    pre = y1 + pltpu.roll(y2, x.shape[0] - 1, 0)
    act = pre * (1.0 / (1.0 + jnp.exp(-pre)))
    o_ref[...] = _dot(act.astype(BF16), w2_ref[...].astype(BF16)).astype(o_ref.dtype)


def _compress(tok, pe, w1, w2):
    b, hkv, nchunk, width = tok.shape
    return pl.pallas_call(
        _compress_kernel,
        grid=(b, hkv),
        in_specs=[pl.BlockSpec((None, None, nchunk, width), lambda i, j: (i, j, 0, 0)),
                  pl.BlockSpec((2, width), lambda i, j: (0, 0)),
                  pl.BlockSpec((2 * width, HEAD_DIM), lambda i, j: (0, 0)),
                  pl.BlockSpec((HEAD_DIM, HEAD_DIM), lambda i, j: (0, 0))],
        out_specs=pl.BlockSpec((None, None, nchunk, HEAD_DIM), lambda i, j: (i, j, 0, 0)),
        out_shape=jax.ShapeDtypeStruct((b, hkv, nchunk, HEAD_DIM), BF16),
        compiler_params=_cp("parallel", "parallel"),
        name="nsa_compress",
    )(tok, pe.reshape(2, width), w1, w2)


def _nsa_cmp_kernel(q_ref, kc_ref, vc_ref, oc_ref, sel_ref, *, tq, nc, ns):
    q0 = pl.program_id(2) * tq
    kc = kc_ref[...]
    vc = vc_ref[...]
    t_row = q0 + lax.broadcasted_iota(jnp.int32, (tq, nc), 0)
    c_idx = lax.broadcasted_iota(jnp.int32, (tq, nc), 1)
    vis = (c_idx * CMP_STRIDE + (CMP_LEN - 1) <= t_row) & (c_idx < nc - 1)
    heads = [slice(h * HEAD_DIM, (h + 1) * HEAD_DIM) for h in range(NSA_GROUP)]
    scores = [_nt_dot(q_ref[:, sl], kc) for sl in heads]
    probs = []
    for s in scores:
        s = jnp.where(vis, s, NEG)
        e = jnp.where(vis, jnp.exp(s - jnp.max(s, axis=-1, keepdims=True)), 0.0)
        den = jnp.sum(e, axis=-1, keepdims=True)
        probs.append(e / jnp.where(den > 0, den, 1.0))
    for sl, p in zip(heads, probs):
        oc_ref[:, sl] = _dot(p.astype(BF16), vc).astype(oc_ref.dtype)
    imp = sum(probs[1:], probs[0])
    jj = lax.broadcasted_iota(jnp.int32, (ns, nc), 0)
    cc = lax.broadcasted_iota(jnp.int32, (ns, nc), 1) * CMP_STRIDE
    ov = jnp.where((cc < (jj + 1) * SEL_LEN) & (cc + CMP_LEN > jj * SEL_LEN), 1.0, 0.0).astype(BF16)
    imp_t = sum(_nt_dot(ov, part) for part in _split3(imp))
    jb = lax.broadcasted_iota(jnp.int32, (ns, tq), 0)
    tt = q0 + lax.broadcasted_iota(jnp.int32, (ns, tq), 1)
    cur = lax.shift_right_logical(tt, jnp.full_like(tt, SEL_SHIFT))
    forced = (jb == 0) | (jb == cur) | (jb == cur - 1)
    valid = jb * SEL_LEN <= tt
    score = jnp.where(valid, imp_t + jnp.where(forced, FORCE_BONUS, 0.0), NEG)
    rank = jnp.zeros((ns, tq), F32)
    for i in range(ns):
        row = score[i:i + 1, :]
        rank = rank + jnp.where(row > score, 1.0, jnp.where((row == score) & (jb > i), 1.0, 0.0))
    sel_t = jnp.where(valid & (rank < float(min(SEL_TOPN, ns))), 1.0, 0.0).astype(BF16)
    eye = jnp.where(lax.broadcasted_iota(jnp.int32, (tq, tq), 0) == lax.broadcasted_iota(jnp.int32, (tq, tq), 1),
                    1.0, 0.0).astype(BF16)
    sel_ref[...] = _nt_dot(eye, sel_t)


def _nsa_cmp(qkv, kc, vc, b, t, tq=512):
    nc = kc.shape[2]
    ns = t // SEL_LEN
    nq = t // tq
    gw = NSA_GROUP * HEAD_DIM
    return pl.pallas_call(
        functools.partial(_nsa_cmp_kernel, tq=tq, nc=nc, ns=ns),
        grid=(b, NSA_KV_HEADS, nq),
        in_specs=[pl.BlockSpec((tq, gw), lambda i, k, j: (i * nq + j, k)),
                  pl.BlockSpec((None, None, nc, HEAD_DIM), lambda i, k, j: (i, k, 0, 0)),
                  pl.BlockSpec((None, None, nc, HEAD_DIM), lambda i, k, j: (i, k, 0, 0))],
        out_specs=[pl.BlockSpec((tq, gw), lambda i, k, j: (i * nq + j, k)),
                   pl.BlockSpec((None, None, tq, ns), lambda i, k, j: (i, k, j, 0))],
        out_shape=[jax.ShapeDtypeStruct((b * t, NSA_Q_W), BF16),
                   jax.ShapeDtypeStruct((b, NSA_KV_HEADS, t, ns), F32)],
        compiler_params=_cp("parallel", "parallel", "parallel"),
        name="nsa_cmp_select",
    )(qkv, kc, vc)


def _flash_update(s, v, bias, state, g, tq):
    m, l, acc = state
    tk = v.shape[0]
    if bias is not None:
        s = (s.reshape(g, tq, tk) + bias[None]).reshape(g * tq, tk)
    m_new = jnp.maximum(m, jnp.max(s, axis=-1, keepdims=True))
    alpha = jnp.exp(m - m_new)
    p = jnp.exp((s - m_new).astype(BF16))
    pv = _dot(p, jnp.concatenate([v, jnp.ones_like(v)], axis=1))
    return m_new, alpha * l + pv[:, HEAD_DIM:], alpha * acc + pv[:, :HEAD_DIM]


def _nsa_sel_win_kernel(q_ref, ks_ref, vs_ref, kw_ref, vw_ref, sel_ref, oc_ref, g_ref, o_ref, *, tq, tk, stack, ns):
    q0 = pl.program_id(1) * tq
    g = NSA_GROUP
    nkv = NSA_KV_HEADS
    per_tile = tk // SEL_LEN
    units = [(kv, h0) for kv in range(nkv) for h0 in range(0, g, stack)]
    kv_lanes = [slice(kv * HEAD_DIM, (kv + 1) * HEAD_DIM) for kv in range(nkv)]

    def q_lanes(kv, h):
        return slice((kv * g + h) * HEAD_DIM, (kv * g + h + 1) * HEAD_DIM)

    qs = [jnp.concatenate([q_ref[:, q_lanes(kv, h0 + j)] for j in range(stack)], axis=0) for kv, h0 in units]
    sel_bias = [((sel_ref[kv] - 1.0) * (-NEG)).astype(BF16) for kv in range(nkv)]
    ej = lax.broadcasted_iota(jnp.int32, (ns, tk), 0)
    eb = lax.broadcasted_iota(jnp.int32, (ns, tk), 1)
    eb = lax.shift_right_logical(eb, jnp.full_like(eb, SEL_SHIFT))

    def init():
        return tuple((jnp.full((stack * tq, 1), NEG, F32), jnp.zeros((stack * tq, HEAD_DIM), F32),
                      jnp.zeros((stack * tq, HEAD_DIM), F32)) for _ in units)

    def flash_all(k_ref, v_ref, k0, width, biases, states):
        scores = [_nt_dot(qs[u], k_ref[pl.ds(k0, width), kv_lanes[kv]]) for u, (kv, _) in enumerate(units)]
        return tuple(_flash_update(scores[u], v_ref[pl.ds(k0, width), kv_lanes[kv]], biases[kv], states[u], stack, tq)
                     for u, (kv, _) in enumerate(units))

    def sel_step(kt, states, causal):
        k0 = pl.multiple_of(kt * tk, tk)
        expand = jnp.where(ej == eb + kt * per_tile, 1.0, 0.0).astype(BF16)
        biases = [_dot(sel_bias[kv], expand) for kv in range(nkv)]
        if causal is not None:
            biases = [bias + causal for bias in biases]
        return flash_all(ks_ref, vs_ref, k0, tk, biases, states)

    kt_d = q0 // tk
    t_row = q0 + lax.broadcasted_iota(jnp.int32, (tq, tk), 0)
    col = kt_d * tk + lax.broadcasted_iota(jnp.int32, (tq, tk), 1)
    states = sel_step(kt_d, init(), jnp.where(col <= t_row, 0.0, NEG))
    states = lax.fori_loop(0, kt_d, lambda kt, st: sel_step(kt, st, None), states)
    o_s = [acc * (1.0 / l) for (_, l, acc) in states]

    def win_step(k0, states, bias):
        return flash_all(kw_ref, vw_ref, k0, tq, [bias] * nkv, states)

    ri = lax.broadcasted_iota(jnp.int32, (tq, tq), 0)
    ci = lax.broadcasted_iota(jnp.int32, (tq, tq), 1)
    states = win_step(pl.multiple_of(q0, tq), init(), jnp.where(ci <= ri, 0.0, NEG))
    n_full = jnp.minimum(WINDOW // tq - 1, q0 // tq)
    states = lax.fori_loop(0, n_full, lambda i, st: win_step(pl.multiple_of(q0 - (i + 1) * tq, tq), st, None),
                           states)
    edge_bias = jnp.where(ci > ri, 0.0, NEG) + jnp.where(q0 >= WINDOW, 0.0, NEG)
    states = win_step(pl.multiple_of(jnp.maximum(q0 - WINDOW, 0), tq), states, edge_bias)
    o_w = [acc * (1.0 / l) for (_, l, acc) in states]

    for u, (kv, h0) in enumerate(units):
        gates = g_ref[:, kv * LANES:(kv + 1) * LANES]
        for j in range(stack):
            h = h0 + j
            sl = q_lanes(kv, h)
            rows = slice(j * tq, (j + 1) * tq)
            out = (gates[:, 3 * h:3 * h + 1] * oc_ref[:, sl].astype(F32)
                   + gates[:, 3 * h + 1:3 * h + 2] * o_s[u][rows] + gates[:, 3 * h + 2:3 * h + 3] * o_w[u][rows])
            o_ref[:, sl] = out.astype(o_ref.dtype)


def _nsa_sel_win(qkv, sel, oc, gates, b, t, tq=512, tk=512, stack=4):
    nq = t // tq
    ns = t // SEL_LEN
    kv_block = NSA_Q_W // NSA_KV_W

    def kv_spec(which):
        return pl.BlockSpec((t, NSA_KV_W), lambda i, j: (i, kv_block + which))

    return pl.pallas_call(
        functools.partial(_nsa_sel_win_kernel, tq=tq, tk=tk, stack=stack, ns=ns),
        grid=(b, nq),
        in_specs=[pl.BlockSpec((tq, NSA_Q_W), lambda i, j: (i * nq + j, 0)),
                  kv_spec(2), kv_spec(3), kv_spec(4), kv_spec(5),
                  pl.BlockSpec((None, NSA_KV_HEADS, tq, ns), lambda i, j: (i, 0, j, 0)),
                  pl.BlockSpec((tq, NSA_Q_W), lambda i, j: (i * nq + j, 0)),
                  pl.BlockSpec((tq, NSA_KV_HEADS * LANES), lambda i, j: (i * nq + j, 0))],
        out_specs=pl.BlockSpec((tq, NSA_Q_W), lambda i, j: (i * nq + j, 0)),
        out_shape=jax.ShapeDtypeStruct((b * t, NSA_Q_W), BF16),
        compiler_params=_cp("parallel", "parallel"),
        name="nsa_select_window",
    )(qkv, qkv, qkv, qkv, qkv, sel, oc, gates)


def _stickbreak_kernel(q_ref, k_ref, v_ref, o_ref, *, tile, nh):
    q0 = pl.program_id(2) * tile
    ri = lax.broadcasted_iota(jnp.int32, (tile, tile), 0)
    ci = lax.broadcasted_iota(jnp.int32, (tile, tile), 1)
    strict = ci < ri
    tri = jnp.where(ri > ci, 1.0, 0.0).astype(BF16)
    lanes = [slice(h * HEAD_DIM, (h + 1) * HEAD_DIM) for h in range(nh)]

    def step(k0, diag, carry):
        zs = [_nt_dot(q_ref[:, lanes[h]], k_ref[pl.ds(k0, tile), lanes[h]]) for h in range(nh)]
        log_bs, sufs, sums = [], [], []
        for z in zs:
            log_b = jnp.minimum(z, 0.0) - jnp.log(1.0 + jnp.exp(-jnp.abs(z)))
            log_1m = log_b - z
            if diag:
                log_1m = jnp.where(strict, log_1m, 0.0)
            hi = log_1m.astype(BF16)
            lo = (log_1m - hi.astype(F32)).astype(BF16)
            log_bs.append(log_b)
            sufs.append(_dot(hi, tri) + _dot(lo, tri))
            sums.append(jnp.sum(log_1m, axis=-1, keepdims=True))
        out = []
        for h in range(nh):
            later, acc = carry[2 * h], carry[2 * h + 1]
            a = jnp.exp(log_bs[h] + (sufs[h] + later))
            if diag:
                a = jnp.where(strict, a, 0.0)
            out.extend((later + sums[h], acc + _dot(a.astype(BF16), v_ref[pl.ds(k0, tile), lanes[h]])))
        return tuple(out)

    zero = (jnp.zeros((tile, 1), F32), jnp.zeros((tile, HEAD_DIM), F32)) * nh
    carry = step(pl.multiple_of(q0, tile), True, zero)

    def live(state):
        i, c = state
        top = c[0]
        for h in range(1, nh):
            top = jnp.maximum(top, c[2 * h])
        return (i < q0 // tile) & (jnp.max(top) > F32_EXP_UNDERFLOW)

    def advance(state):
        i, c = state
        return i + 1, step(pl.multiple_of(q0 - (i + 1) * tile, tile), False, c)

    _, carry = lax.while_loop(live, advance, (jnp.int32(0), carry))
    for h in range(nh):
        o_ref[:, lanes[h]] = carry[2 * h + 1].astype(o_ref.dtype)


def _stickbreak(qkv, b, t, tile=256, nh=4):
    nq = t // tile
    w = nh * HEAD_DIM
    nhb = SB_HEADS // nh
    return pl.pallas_call(
        functools.partial(_stickbreak_kernel, tile=tile, nh=nh),
        grid=(b, nhb, nq),
        in_specs=[pl.BlockSpec((tile, w), lambda i, h, j: (i * nq + j, h)),
                  pl.BlockSpec((t, w), lambda i, h, j: (i, nhb + h)),
                  pl.BlockSpec((t, w), lambda i, h, j: (i, 2 * nhb + h))],
        out_specs=pl.BlockSpec((tile, w), lambda i, h, j: (i * nq + j, h)),
        out_shape=jax.ShapeDtypeStruct((b * t, SB_W), BF16),
        compiler_params=_cp("parallel", "parallel", "parallel"),
        name="stickbreak",
    )(qkv, qkv, qkv)


def _cross_kernel(x_ref, g_ref, wq_ref, kv_ref, wo_ref, o_ref):
    x = x_ref[...]
    h = _rms(x, g_ref[...]).astype(BF16)
    q = _dot(h, wq_ref[...]).astype(BF16)
    xw = X_HEADS * HEAD_DIM
    outs = []
    for hd in range(X_HEADS):
        sl = slice(hd * HEAD_DIM, (hd + 1) * HEAD_DIM)
        s = _nt_dot(q[:, sl], kv_ref[:, sl]) * SCALE
        e = jnp.exp(s - jnp.max(s, axis=-1, keepdims=True))
        p = e / jnp.sum(e, axis=-1, keepdims=True)
        outs.append(_dot(p.astype(BF16), kv_ref[:, xw + hd * HEAD_DIM:xw + (hd + 1) * HEAD_DIM]).astype(BF16))
    o_ref[...] = x + _dot(jnp.concatenate(outs, axis=1), wo_ref[...])


def _cross(x, g, wq, kv, wo, b, t, tq=512):
    n, d = x.shape
    nq = t // tq
    mem_len = kv.shape[0] // b
    return pl.pallas_call(
        _cross_kernel,
        grid=(b, nq),
        in_specs=[pl.BlockSpec((tq, d), lambda i, j: (i * nq + j, 0)),
                  pl.BlockSpec((1, d), lambda i, j: (0, 0)),
                  pl.BlockSpec(wq.shape, lambda i, j: (0, 0)),
                  pl.BlockSpec((mem_len, kv.shape[1]), lambda i, j: (i, 0)),
                  pl.BlockSpec(wo.shape, lambda i, j: (0, 0))],
        out_specs=pl.BlockSpec((tq, d), lambda i, j: (i * nq + j, 0)),
        out_shape=jax.ShapeDtypeStruct((n, d), F32),
        compiler_params=_cp("parallel", "parallel"),
        name="cross_attention",
    )(x, g.reshape(1, d), wq, kv, wo)


def _to_row_tiles(ref, val):
    rows, width = val.shape
    c_per = width // LANES
    for c in range(c_per):
        ref[pl.ds(c, rows, stride=c_per), :] = val[:, c * LANES:(c + 1) * LANES]


def _from_row_tiles(ref, c_per):
    rows = ref.shape[0] // c_per
    return jnp.concatenate([ref[pl.ds(c, rows, stride=c_per), :] for c in range(c_per)], axis=1)


def _router_kernel(x_ref, g_ref, w_ref, b_ref, h_ref, r_ref):
    h = _rms(x_ref[...], g_ref[...])
    _to_row_tiles(h_ref, h)
    h_hi = h.astype(BF16)
    h_mid = (h - h_hi.astype(F32)).astype(BF16)
    two = _dot(h_hi, w_ref[...])
    logits = two[:, :LANES] + two[:, LANES:] + _dot(h_mid, w_ref[:, :LANES]) + b_ref[...]
    lane = lax.broadcasted_iota(jnp.int32, logits.shape, 1).astype(F32)
    big = float(LANES)
    is_g = lane < N_GROUPS
    lg = jnp.where(is_g, logits, NEG)
    mg = jnp.max(lg, axis=-1, keepdims=True)
    p_gtop = 1.0 / jnp.sum(jnp.where(is_g, jnp.exp(lg - mg), 0.0), axis=-1, keepdims=True)
    g_top = jnp.min(jnp.where(lg == mg, lane, big), axis=-1, keepdims=True)
    lo = N_GROUPS + EXPERTS_PER_GROUP * g_top
    in_grp = (lane >= lo) & (lane < lo + EXPERTS_PER_GROUP)
    le = jnp.where(in_grp, logits, NEG)
    ee = jnp.where(in_grp, jnp.exp(le - jnp.max(le, axis=-1, keepdims=True)), 0.0)
    p_in = ee / jnp.sum(ee, axis=-1, keepdims=True)
    p1 = jnp.where(in_grp, p_in, -1.0)
    m1 = jnp.max(p1, axis=-1, keepdims=True)
    i1 = jnp.min(jnp.where(p1 == m1, lane, big), axis=-1, keepdims=True)
    p2 = jnp.where(lane == i1, -1.0, p1)
    m2 = jnp.max(p2, axis=-1, keepdims=True)
    i2 = jnp.min(jnp.where(p2 == m2, lane, big), axis=-1, keepdims=True)
    tot = m1 + m2
    r_ref[...] = jnp.where(lane == 0, i1 - N_GROUPS,
                           jnp.where(lane == 1, i2 - N_GROUPS,
                                     jnp.where(lane == 2, m1 / tot * p_gtop,
                                               jnp.where(lane == 3, m2 / tot * p_gtop, 0.0))))


def _router(x, g, w_r, b_r, tm=512):
    n, d = x.shape
    return pl.pallas_call(
        _router_kernel,
        grid=(n // tm,),
        in_specs=[pl.BlockSpec((tm, d), lambda i: (i, 0)), pl.BlockSpec((1, d), lambda i: (0, 0)),
                  pl.BlockSpec((d, 2 * LANES), lambda i: (0, 0)), pl.BlockSpec((1, LANES), lambda i: (0, 0))],
        out_specs=[pl.BlockSpec((tm * (d // LANES), LANES), lambda i: (i, 0)),
                   pl.BlockSpec((tm, LANES), lambda i: (i, 0))],
        out_shape=[jax.ShapeDtypeStruct((n * (d // LANES), LANES), F32), jax.ShapeDtypeStruct((n, LANES), F32)],
        compiler_params=_cp("parallel"),
        name="moe_router",
    )(x, g.reshape(1, d), w_r, b_r)


ROW_CHUNKS = 16
WEIGHT_DMA_PRIORITY = 0


def _wait_rows(src_hbm, dst, sem):
    pltpu.make_async_copy(src_hbm.at[pl.ds(0, dst.shape[0]), :], dst, sem).wait()


def _row_copy(src_hbm, dst, sem, src_row, dst_row):
    c = ROW_CHUNKS
    return pltpu.make_async_copy(src_hbm.at[pl.ds(pl.multiple_of(src_row * c, c), c), :],
                                 dst.at[pl.ds(pl.multiple_of(dst_row * c, c), c), :], sem)


def _expert_kernel(idx_ref, ord_ref, oe_ref, cnt_ref, src_hbm, wg_hbm, wu_hbm, wd_hbm, o_ref,
                   xbuf, sems, wg_f, wu_f, wd_f, wsems, wg_s, wu_s, wd_s, *, tm):
    i = pl.program_id(0)
    nt = cnt_ref[0]
    n_ord = cnt_ref[1]
    live = i < nt
    slot = lax.rem(i, 2)

    def gather(tile, s):
        def issue(r2, c):
            for queue in range(2):
                r = 2 * r2 + queue
                _row_copy(src_hbm, xbuf.at[s], sems.at[s], idx_ref[tile * tm + r], r).start(priority=queue)
            return c
        lax.fori_loop(0, tm // 2, issue, 0, unroll=4)

    def weight_copies(k, s):
        e = oe_ref[k]
        return (pltpu.make_async_copy(wg_hbm.at[e], wg_f.at[s], wsems.at[s, 0]),
                pltpu.make_async_copy(wu_hbm.at[e], wu_f.at[s], wsems.at[s, 1]),
                pltpu.make_async_copy(wd_hbm.at[e], wd_f.at[s], wsems.at[s, 2]))

    @pl.when(i == 0)
    def _():
        gather(0, 0)
        for cp in weight_copies(0, 0):
            cp.start(priority=WEIGHT_DMA_PRIORITY)

    @pl.when(i + 1 < nt)
    def _():
        gather(i + 1, 1 - slot)

    k = ord_ref[i]
    fresh = live & ((i == 0) | (k != ord_ref[jnp.maximum(i - 1, 0)]))
    ws = lax.rem(k, 2)

    @pl.when(fresh)
    def _():
        for cp in weight_copies(k, ws):
            cp.wait()

        @pl.when(k + 1 < n_ord)
        def _():
            for cp in weight_copies(k + 1, 1 - ws):
                cp.start(priority=WEIGHT_DMA_PRIORITY)

        wg_s[...] = wg_f[ws].astype(BF16)
        wu_s[...] = wu_f[ws].astype(BF16)
        wd_s[...] = wd_f[ws].astype(BF16)

    @pl.when(live)
    def _():
        _wait_rows(src_hbm, xbuf.at[slot], sems.at[slot])
        x = _from_row_tiles(xbuf.at[slot], ROW_CHUNKS).astype(BF16)
        a = _dot(x, wg_s[...])
        u = _dot(x, wu_s[...])
        act = a * (1.0 / (1.0 + jnp.exp(-a))) * u
        _to_row_tiles(o_ref, _dot(act.astype(BF16), wd_s[...]))

    @pl.when(jnp.logical_not(live))
    def _():
        o_ref[...] = jnp.zeros_like(o_ref)


def _experts(src, tok_of_row, w_g, w_u, w_d, tile_ord, ord_expert, counts, tm=EXPERT_TILE):
    c = ROW_CHUNKS
    n_tiles_max = tok_of_row.shape[0] // tm
    d = c * LANES
    f = w_g.shape[2]
    any_spec = pl.BlockSpec(memory_space=pl.ANY)
    return pl.pallas_call(
        functools.partial(_expert_kernel, tm=tm),
        grid_spec=pltpu.PrefetchScalarGridSpec(
            num_scalar_prefetch=4,
            grid=(n_tiles_max,),
            in_specs=[any_spec, any_spec, any_spec, any_spec],
            out_specs=pl.BlockSpec((tm * c, LANES), lambda i, *_: (i, 0)),
            scratch_shapes=[pltpu.VMEM((2, tm * c, LANES), F32), pltpu.SemaphoreType.DMA((2,)),
                            pltpu.VMEM((2, d, f), F32), pltpu.VMEM((2, d, f), F32), pltpu.VMEM((2, f, d), F32),
                            pltpu.SemaphoreType.DMA((2, 3)),
                            pltpu.VMEM((d, f), BF16), pltpu.VMEM((d, f), BF16), pltpu.VMEM((f, d), BF16)]),
        out_shape=jax.ShapeDtypeStruct((n_tiles_max * tm * c, LANES), F32),
        compiler_params=_cp("arbitrary"),
        name="moe_experts",
    )(tok_of_row, tile_ord, ord_expert, counts, src, w_g, w_u, w_d)


def _combine_kernel(p0_ref, p1_ref, ys_hbm, x_ref, r_ref, g_ref, o_ref, buf, sems, *, tm):
    i = pl.program_id(0)
    slot = lax.rem(i, 2)

    def fetch(tile, s):
        def issue(r, c):
            _row_copy(ys_hbm, buf.at[s, 0], sems.at[s], p0_ref[tile * tm + r], r).start()
            _row_copy(ys_hbm, buf.at[s, 1], sems.at[s], p1_ref[tile * tm + r], r).start(priority=1)
            return c
        lax.fori_loop(0, tm, issue, 0, unroll=8)

    @pl.when(i == 0)
    def _():
        fetch(0, 0)

    @pl.when(i + 1 < pl.num_programs(0))
    def _():
        fetch(i + 1, 1 - slot)

    _wait_rows(ys_hbm, buf.at[slot, 0], sems.at[slot])
    _wait_rows(ys_hbm, buf.at[slot, 1], sems.at[slot])
    route = r_ref[...]
    y = (route[:, 2:3] * _from_row_tiles(buf.at[slot, 0], ROW_CHUNKS)
         + route[:, 3:4] * _from_row_tiles(buf.at[slot, 1], ROW_CHUNKS))
    o_ref[...] = _rms(x_ref[...] + y, g_ref[...])


def _combine(ys, pos0, pos1, x, route, g, tm=256):
    n, d = x.shape
    return pl.pallas_call(
        functools.partial(_combine_kernel, tm=tm),
        grid_spec=pltpu.PrefetchScalarGridSpec(
            num_scalar_prefetch=2,
            grid=(n // tm,),
            in_specs=[pl.BlockSpec(memory_space=pl.ANY),
                      pl.BlockSpec((tm, d), lambda i, p0, p1: (i, 0)),
                      pl.BlockSpec((tm, LANES), lambda i, p0, p1: (i, 0)),
                      pl.BlockSpec((1, d), lambda i, p0, p1: (0, 0))],
            out_specs=pl.BlockSpec((tm, d), lambda i, p0, p1: (i, 0)),
            scratch_shapes=[pltpu.VMEM((2, 2, tm * ROW_CHUNKS, LANES), F32), pltpu.SemaphoreType.DMA((2,))]),
        out_shape=jax.ShapeDtypeStruct((n, d), F32),
        compiler_params=_cp("arbitrary"),
        name="moe_combine_norm",
    )(pos0, pos1, ys, x, route, g.reshape(1, d))


def _dispatch_plan(e1, e2, tm=EXPERT_TILE):
    n = e1.shape[0]
    n_tiles_max = (2 * n) // tm + N_EXPERTS
    rows = n_tiles_max * tm
    e_flat = jnp.concatenate([e1, e2])
    tok = jnp.concatenate([jnp.arange(n, dtype=jnp.int32)] * 2)
    onehot = (e_flat[:, None] == jnp.arange(N_EXPERTS, dtype=jnp.int32)[None, :]).astype(jnp.int32)
    csum = jnp.cumsum(onehot, axis=0)
    count = csum[-1]
    rank = jnp.sum((csum - 1) * onehot, axis=1)
    tiles_per = (count + tm - 1) // tm
    tile_end = jnp.cumsum(tiles_per)
    row_start = (tile_end - tiles_per) * tm
    dest = (row_start[e_flat] + rank).astype(jnp.int32)
    tok_of_row = jnp.zeros((rows,), jnp.int32).at[dest].set(tok)
    n_tiles = tile_end[-1].astype(jnp.int32)
    tile_id = jnp.minimum(jnp.arange(n_tiles_max, dtype=jnp.int32), n_tiles - 1)
    tile_expert = jnp.sum((tile_id[:, None] >= tile_end[None, :]).astype(jnp.int32), axis=1)
    used = jnp.cumsum((tiles_per > 0).astype(jnp.int32))
    tile_ord = (used[tile_expert] - 1).astype(jnp.int32)
    ordinals = jnp.arange(N_EXPERTS, dtype=jnp.int32)
    ord_expert = jnp.minimum(jnp.sum((used[None, :] <= ordinals[:, None]).astype(jnp.int32), axis=1),
                             N_EXPERTS - 1).astype(jnp.int32)
    counts = jnp.stack([n_tiles, used[-1]]).astype(jnp.int32)
    return tok_of_row, tile_ord, ord_expert, counts, dest[:n], dest[n:]


def _rope_tables(positions):
    half = ROT_DIM // 2
    inv = ROPE_THETA ** (-jnp.arange(half, dtype=F32) * (2.0 / ROT_DIM))
    ang = positions.astype(F32).reshape(-1, 1) * inv[None, :]
    cos, sin = jnp.cos(ang), jnp.sin(ang)
    n = ang.shape[0]
    c = jnp.concatenate([cos, cos, jnp.ones((n, LANES - ROT_DIM), F32)], axis=1)
    s1 = jnp.concatenate([-sin, jnp.zeros((n, LANES - half), F32)], axis=1)
    s2 = jnp.concatenate([jnp.zeros((n, half), F32), sin, jnp.zeros((n, LANES - ROT_DIM), F32)], axis=1)
    return c, s1, s2


def _layer(x, mem, positions, g_mix, w_in, cmp_pe_k, cmp_w1_k, cmp_w2_k, cmp_pe_v, cmp_w1_v, cmp_w2_v,
           w_br_nsa, w_br_sb, w_o, g_cross, g_mem, w_cq, w_ck, w_cv, w_co,
           g_moe, w_rg, b_rg, w_re, b_re, w_eg, w_eu, w_ed):
    b, t, d = x.shape
    n = b * t
    xf = x.reshape(n, d)

    g0 = NSA_W
    w_t = w_in.T
    w_nsa = _cast_rows(w_t, 0, g0)
    w_rest = _cast_rows(w_t, g0 + NSA_GATES, w_t.shape[0] - g0 - NSA_GATES)
    w_g = w_t[g0:g0 + NSA_GATES].reshape(NSA_KV_HEADS, 3 * NSA_GROUP, d)
    w_g = jnp.pad(w_g, ((0, 0), (0, LANES - 3 * NSA_GROUP), (0, 0))).reshape(NSA_KV_HEADS * LANES, d).astype(BF16)
    h = _rmsnorm(xf, g_mix, BF16)
    c, s1, s2 = _rope_tables(positions)
    row_spec = pl.BlockSpec((1024, LANES), lambda i, j: (i, 0))
    qkv_nsa = _mm(functools.partial(_mm_nsa_kernel, tn=256), h, w_nsa, NSA_W, 0, 1024, 256, BF16, w_t=True,
                  extra=(c, s1, s2), extra_specs=(row_spec, row_spec, row_spec), name="in_proj_nsa")
    qkv_sb = _mm(functools.partial(_mm_sb_kernel, q_tiles=SB_W // 512), h, w_rest, 3 * SB_W, 0,
                 1024, 512, BF16, w_t=True, name="in_proj_sb")
    gates = _mm(_mm_sigmoid_kernel, h, w_rest, 2 * d, 3 * SB_W // 512, 1024, 512, BF16, w_t=True,
                name="in_proj_gates")
    nsa_gates = _mm(_mm_sigmoid_kernel, h, w_g, NSA_KV_HEADS * LANES, 0, 1024, NSA_KV_HEADS * LANES, F32,
                    w_t=True, name="in_proj_nsa_gates")

    def chunked(col0):
        tok = qkv_nsa[:, col0:col0 + NSA_KV_W].reshape(b, t, NSA_KV_HEADS, HEAD_DIM).transpose(0, 2, 1, 3)
        return tok.reshape(b, NSA_KV_HEADS, t // CMP_STRIDE, CMP_STRIDE * HEAD_DIM)

    kc = _compress(chunked(NSA_Q_W), cmp_pe_k, cmp_w1_k, cmp_w2_k)
    vc = _compress(chunked(NSA_Q_W + NSA_KV_W), cmp_pe_v, cmp_w1_v, cmp_w2_v)
    o_c, sel = _nsa_cmp(qkv_nsa, kc, vc, b, t)
    o_nsa = _nsa_sel_win(qkv_nsa, sel, o_c, nsa_gates, b, t)
    o_sb = _stickbreak(qkv_sb, b, t)
    merged = _merge(o_nsa, o_sb, w_br_nsa.astype(BF16), w_br_sb.astype(BF16), gates)
    x1 = _mm(_mm_resid_kernel, merged, w_o.astype(BF16), d, 0, 1024, 512, F32,
             extra=(xf,), extra_specs=(pl.BlockSpec((1024, 512), lambda i, j: (i, j)),), name="out_proj")

    w_kv = jnp.concatenate([w_ck, w_cv], axis=1).astype(BF16)
    kv = _norm_mm(mem.reshape(-1, d), g_mem, w_kv, BF16)
    x2 = _cross(x1, g_cross, w_cq.astype(BF16), kv, w_co.astype(BF16), b, t)

    w_r = jnp.pad(jnp.concatenate([w_rg, w_re], axis=1), ((0, 0), (0, LANES - N_GROUPS - N_EXPERTS)))
    b_r = jnp.pad(jnp.concatenate([b_rg, b_re]), (0, LANES - N_GROUPS - N_EXPERTS)).reshape(1, LANES)
    w_r_hi = w_r.astype(BF16)
    w_r2 = jnp.concatenate([w_r_hi, (w_r - w_r_hi.astype(F32)).astype(BF16)], axis=1)
    hm, route = _router(x2, g_moe, w_r2, b_r)
    e1 = route[:, 0].astype(jnp.int32)
    e2 = route[:, 1].astype(jnp.int32)
    tok_of_row, tile_ord, ord_expert, counts, pos0, pos1 = _dispatch_plan(e1, e2)
    f = w_eg.shape[-1]
    ys = _experts(hm, tok_of_row, w_eg.reshape(N_EXPERTS, d, f), w_eu.reshape(N_EXPERTS, d, f),
                  w_ed.reshape(N_EXPERTS, f, d), tile_ord, ord_expert, counts)
    return ys, pos0, pos1, x2, route


def kernel(x, mem, positions, g_mix, w_in, cmp_pe_k, cmp_w1_k, cmp_w2_k, cmp_pe_v, cmp_w1_v, cmp_w2_v, w_br_nsa, w_br_sb, w_o, g_cross, g_mem, w_cq, w_ck, w_cv, w_co, g_moe, w_rg, b_rg, w_re, b_re, w_eg, w_eu, w_ed, g_final):
    b, t, d = x.shape
    assert w_in.shape[0] == 1, "one layer"
    ys, pos0, pos1, x2, route = _layer(
        x, mem, positions, g_mix[0], w_in[0], cmp_pe_k[0], cmp_w1_k[0], cmp_w2_k[0], cmp_pe_v[0], cmp_w1_v[0],
        cmp_w2_v[0], w_br_nsa[0], w_br_sb[0], w_o[0], g_cross[0], g_mem[0], w_cq[0], w_ck[0], w_cv[0], w_co[0],
        g_moe[0], w_rg[0], b_rg[0], w_re[0], b_re[0], w_eg[0], w_eu[0], w_ed[0])
    return _combine(ys, pos0, pos1, x2, route, g_final).reshape(b, t, d)
```

```python
import functools

import jax
import jax.numpy as jnp
from jax import lax
from jax.experimental import pallas as pl
from jax.experimental.pallas import tpu as pltpu

F32 = jnp.float32
BF16 = jnp.bfloat16

HEAD_DIM = 128
NSA_HEADS = 8
NSA_KV_HEADS = 2
NSA_GROUP = NSA_HEADS // NSA_KV_HEADS
SB_HEADS = 8
ROT_DIM = HEAD_DIM // 4
ROPE_THETA = 500000.0
CMP_LEN = 32
CMP_STRIDE = 16
SEL_LEN = 64
SEL_SHIFT = 6
SEL_TOPN = 16
WINDOW = 512
FORCE_BONUS = 1000.0
X_HEADS = 4
N_GROUPS = 4
EXPERTS_PER_GROUP = 8
N_EXPERTS = N_GROUPS * EXPERTS_PER_GROUP
EPS = 1e-6
SCALE = HEAD_DIM ** -0.5
NEG = -1e30
F32_EXP_UNDERFLOW = -104.0

NSA_Q_W = NSA_HEADS * HEAD_DIM
NSA_KV_W = NSA_KV_HEADS * HEAD_DIM
NSA_W = NSA_Q_W + 6 * NSA_KV_W
NSA_GATES = 3 * NSA_HEADS
SB_W = SB_HEADS * HEAD_DIM
LANES = 128
VMEM_LIMIT = 56 * 1024 * 1024

EXPERT_TILE = 256


def _cp(*sem, vmem=VMEM_LIMIT):
    return pltpu.CompilerParams(dimension_semantics=sem, vmem_limit_bytes=vmem)


def _nt_dot(a, b):
    return lax.dot_general(a, b, (((1,), (1,)), ((), ())), preferred_element_type=F32)


def _dot(a, b):
    return jnp.dot(a, b, preferred_element_type=F32)


def _split3(x):
    hi = x.astype(BF16)
    r1 = x - hi.astype(F32)
    mid = r1.astype(BF16)
    lo = (r1 - mid.astype(F32)).astype(BF16)
    return hi, mid, lo


def _rms(x, g):
    return x * lax.rsqrt(jnp.mean(x * x, axis=-1, keepdims=True) + EPS) * g


def _rmsnorm_kernel(x_ref, g_ref, o_ref):
    o_ref[...] = _rms(x_ref[...].astype(F32), g_ref[...]).astype(o_ref.dtype)


def _rmsnorm(x, g, out_dtype, tm=512):
    n, d = x.shape
    return pl.pallas_call(
        _rmsnorm_kernel,
        grid=(n // tm,),
        in_specs=[pl.BlockSpec((tm, d), lambda i: (i, 0)), pl.BlockSpec((1, d), lambda i: (0, 0))],
        out_specs=pl.BlockSpec((tm, d), lambda i: (i, 0)),
        out_shape=jax.ShapeDtypeStruct((n, d), out_dtype),
        compiler_params=_cp("parallel"),
        name="rmsnorm",
    )(x, g.reshape(1, d))


def _cast_rows_kernel(w_ref, o_ref):
    o_ref[...] = w_ref[...].astype(o_ref.dtype)


def _cast_rows(w_t, row0, n_rows, tn=512):
    k = w_t.shape[1]
    assert n_rows % tn == 0 and row0 % 8 == 0
    return pl.pallas_call(
        _cast_rows_kernel,
        grid=(n_rows // tn,),
        in_specs=[pl.BlockSpec((pl.Element(tn), pl.Element(k)), lambda j: ((row0 // 8 + j * (tn // 8)) * 8, 0))],
        out_specs=pl.BlockSpec((tn, k), lambda j: (j, 0)),
        out_shape=jax.ShapeDtypeStruct((n_rows, k), BF16),
        compiler_params=_cp("parallel"),
        name="cast_rows",
    )(w_t)


def _rope_half(x, c, s1, s2):
    return x * c + pltpu.roll(x, LANES - ROT_DIM // 2, 1) * s1 + pltpu.roll(x, ROT_DIM // 2, 1) * s2


def _mm_nsa_kernel(a_ref, w_ref, c_ref, s1_ref, s2_ref, o_ref, *, tn):
    j = pl.program_id(1)
    acc = _nt_dot(a_ref[...], w_ref[...])
    is_rope = (j < 5) | (j == 6) | (j == 8)

    @pl.when(is_rope)
    def _():
        qs = jnp.where(j < NSA_Q_W // tn, SCALE, 1.0)
        c, s1, s2 = c_ref[...] * qs, s1_ref[...] * qs, s2_ref[...] * qs
        for hh in range(tn // LANES):
            sl = slice(hh * LANES, (hh + 1) * LANES)
            o_ref[:, sl] = _rope_half(acc[:, sl], c, s1, s2).astype(o_ref.dtype)

    @pl.when(jnp.logical_not(is_rope))
    def _():
        o_ref[...] = acc.astype(o_ref.dtype)


def _mm_sb_kernel(a_ref, w_ref, o_ref, *, q_tiles):
    acc = _nt_dot(a_ref[...], w_ref[...])
    is_q = pl.program_id(1) < q_tiles
    o_ref[...] = (acc * jnp.where(is_q, SCALE, 1.0)).astype(o_ref.dtype)


def _mm_sigmoid_kernel(a_ref, w_ref, o_ref):
    o_ref[...] = jax.nn.sigmoid(_nt_dot(a_ref[...], w_ref[...])).astype(o_ref.dtype)


def _mm_resid_kernel(a_ref, w_ref, x_ref, o_ref):
    o_ref[...] = x_ref[...] + _dot(a_ref[...], w_ref[...])


def _mm(kern, a, w, n_cols, col_block0, tm, tn, out_dtype, extra=(), extra_specs=(), name="mm", w_t=False):
    m, k = a.shape
    w_spec = (pl.BlockSpec((tn, k), lambda i, j: (col_block0 + j, 0)) if w_t
              else pl.BlockSpec((k, tn), lambda i, j: (0, col_block0 + j)))
    return pl.pallas_call(
        kern,
        grid=(m // tm, n_cols // tn),
        in_specs=[pl.BlockSpec((tm, k), lambda i, j: (i, 0)), w_spec, *extra_specs],
        out_specs=pl.BlockSpec((tm, tn), lambda i, j: (i, j)),
        out_shape=jax.ShapeDtypeStruct((m, n_cols), out_dtype),
        compiler_params=_cp("parallel", "arbitrary"),
        name=name,
    )(a, w, *extra)


def _merge_kernel(a1_ref, a2_ref, w1_ref, w2_ref, ga_ref, gb_ref, o_ref):
    ya = _dot(a1_ref[...], w1_ref[...])
    yb = _dot(a2_ref[...], w2_ref[...])
    o_ref[...] = (ga_ref[...].astype(F32) * ya + gb_ref[...].astype(F32) * yb).astype(o_ref.dtype)


def _merge(o_nsa, o_sb, w1, w2, gates, tm=1024, tn=512):
    n, k = o_nsa.shape
    d = w1.shape[1]
    nb = d // tn
    return pl.pallas_call(
        _merge_kernel,
        grid=(n // tm, nb),
        in_specs=[pl.BlockSpec((tm, k), lambda i, j: (i, 0)), pl.BlockSpec((tm, k), lambda i, j: (i, 0)),
                  pl.BlockSpec((k, tn), lambda i, j: (0, j)), pl.BlockSpec((k, tn), lambda i, j: (0, j)),
                  pl.BlockSpec((tm, tn), lambda i, j: (i, j)), pl.BlockSpec((tm, tn), lambda i, j: (i, nb + j))],
        out_specs=pl.BlockSpec((tm, tn), lambda i, j: (i, j)),
        out_shape=jax.ShapeDtypeStruct((n, d), BF16),
        compiler_params=_cp("parallel", "arbitrary"),
        name="merge",
    )(o_nsa, o_sb, w1, w2, gates, gates)


def _norm_mm_kernel(x_ref, g_ref, w_ref, o_ref):
    h = _rms(x_ref[...].astype(F32), g_ref[...]).astype(BF16)
    o_ref[...] = _dot(h, w_ref[...]).astype(o_ref.dtype)


def _norm_mm(x, g, w, out_dtype, tm=256):
    n, d = x.shape
    nc = w.shape[1]
    return pl.pallas_call(
        _norm_mm_kernel,
        grid=(n // tm,),
        in_specs=[pl.BlockSpec((tm, d), lambda i: (i, 0)), pl.BlockSpec((1, d), lambda i: (0, 0)),
                  pl.BlockSpec((d, nc), lambda i: (0, 0))],
        out_specs=pl.BlockSpec((tm, nc), lambda i: (i, 0)),
        out_shape=jax.ShapeDtypeStruct((n, nc), out_dtype),
        compiler_params=_cp("parallel"),
        name="norm_mm",
    )(x, g.reshape(1, d), w)


def _compress_kernel(x_ref, pe_ref, w1_ref, w2_ref, o_ref):
    x = x_ref[...].astype(F32)
    half = x.shape[1]
    xa = (x + pe_ref[0:1, :]).astype(BF16)
    xb = (x + pe_ref[1:2, :]).astype(BF16)
    y1 = _dot(xa, w1_ref[0:half, :].astype(BF16))
    y2 = _dot(xb, w1_ref[half:2 * half, :].astype(BF16))
    pre = y1 + pltpu.roll(y2, x.shape[0] - 1, 0)
    act = pre * (1.0 / (1.0 + jnp.exp(-pre)))
    o_ref[...] = _dot(act.astype(BF16), w2_ref[...].astype(BF16)).astype(o_ref.dtype)


def _compress(tok, pe, w1, w2):
    b, hkv, nchunk, width = tok.shape
    return pl.pallas_call(
        _compress_kernel,
        grid=(b, hkv),
        in_specs=[pl.BlockSpec((None, None, nchunk, width), lambda i, j: (i, j, 0, 0)),
                  pl.BlockSpec((2, width), lambda i, j: (0, 0)),
                  pl.BlockSpec((2 * width, HEAD_DIM), lambda i, j: (0, 0)),
                  pl.BlockSpec((HEAD_DIM, HEAD_DIM), lambda i, j: (0, 0))],
        out_specs=pl.BlockSpec((None, None, nchunk, HEAD_DIM), lambda i, j: (i, j, 0, 0)),
        out_shape=jax.ShapeDtypeStruct((b, hkv, nchunk, HEAD_DIM), BF16),
        compiler_params=_cp("parallel", "parallel"),
        name="nsa_compress",
    )(tok, pe.reshape(2, width), w1, w2)


def _nsa_cmp_kernel(q_ref, kc_ref, vc_ref, oc_ref, sel_ref, *, tq, nc, ns):
    q0 = pl.program_id(2) * tq
    kc = kc_ref[...]
    vc = vc_ref[...]
    t_row = q0 + lax.broadcasted_iota(jnp.int32, (tq, nc), 0)
    c_idx = lax.broadcasted_iota(jnp.int32, (tq, nc), 1)
    vis = (c_idx * CMP_STRIDE + (CMP_LEN - 1) <= t_row) & (c_idx < nc - 1)
    heads = [slice(h * HEAD_DIM, (h + 1) * HEAD_DIM) for h in range(NSA_GROUP)]
    scores = [_nt_dot(q_ref[:, sl], kc) for sl in heads]
    probs = []
    for s in scores:
        s = jnp.where(vis, s, NEG)
        e = jnp.where(vis, jnp.exp(s - jnp.max(s, axis=-1, keepdims=True)), 0.0)
        den = jnp.sum(e, axis=-1, keepdims=True)
        probs.append(e / jnp.where(den > 0, den, 1.0))
    for sl, p in zip(heads, probs):
        oc_ref[:, sl] = _dot(p.astype(BF16), vc).astype(oc_ref.dtype)
    imp = sum(probs[1:], probs[0])
    jj = lax.broadcasted_iota(jnp.int32, (ns, nc), 0)
    cc = lax.broadcasted_iota(jnp.int32, (ns, nc), 1) * CMP_STRIDE
    ov = jnp.where((cc < (jj + 1) * SEL_LEN) & (cc + CMP_LEN > jj * SEL_LEN), 1.0, 0.0).astype(BF16)
    imp_t = sum(_nt_dot(ov, part) for part in _split3(imp))
    jb = lax.broadcasted_iota(jnp.int32, (ns, tq), 0)
    tt = q0 + lax.broadcasted_iota(jnp.int32, (ns, tq), 1)
    cur = lax.shift_right_logical(tt, jnp.full_like(tt, SEL_SHIFT))
    forced = (jb == 0) | (jb == cur) | (jb == cur - 1)
    valid = jb * SEL_LEN <= tt
    score = jnp.where(valid, imp_t + jnp.where(forced, FORCE_BONUS, 0.0), NEG)
    rank = jnp.zeros((ns, tq), F32)
    for i in range(ns):
        row = score[i:i + 1, :]
        rank = rank + jnp.where(row > score, 1.0, jnp.where((row == score) & (jb > i), 1.0, 0.0))
    sel_t = jnp.where(valid & (rank < float(min(SEL_TOPN, ns))), 1.0, 0.0).astype(BF16)
    eye = jnp.where(lax.broadcasted_iota(jnp.int32, (tq, tq), 0) == lax.broadcasted_iota(jnp.int32, (tq, tq), 1),
                    1.0, 0.0).astype(BF16)
    sel_ref[...] = _nt_dot(eye, sel_t)


def _nsa_cmp(qkv, kc, vc, b, t, tq=512):
    nc = kc.shape[2]
    ns = t // SEL_LEN
    nq = t // tq
    gw = NSA_GROUP * HEAD_DIM
    return pl.pallas_call(
        functools.partial(_nsa_cmp_kernel, tq=tq, nc=nc, ns=ns),
        grid=(b, NSA_KV_HEADS, nq),
        in_specs=[pl.BlockSpec((tq, gw), lambda i, k, j: (i * nq + j, k)),
                  pl.BlockSpec((None, None, nc, HEAD_DIM), lambda i, k, j: (i, k, 0, 0)),
                  pl.BlockSpec((None, None, nc, HEAD_DIM), lambda i, k, j: (i, k, 0, 0))],
        out_specs=[pl.BlockSpec((tq, gw), lambda i, k, j: (i * nq + j, k)),
                   pl.BlockSpec((None, None, tq, ns), lambda i, k, j: (i, k, j, 0))],
        out_shape=[jax.ShapeDtypeStruct((b * t, NSA_Q_W), BF16),
                   jax.ShapeDtypeStruct((b, NSA_KV_HEADS, t, ns), F32)],
        compiler_params=_cp("parallel", "parallel", "parallel"),
        name="nsa_cmp_select",
    )(qkv, kc, vc)


def _flash_update(s, v, bias, state, g, tq):
    m, l, acc = state
    tk = v.shape[0]
    if bias is not None:
        s = (s.reshape(g, tq, tk) + bias[None]).reshape(g * tq, tk)
    m_new = jnp.maximum(m, jnp.max(s, axis=-1, keepdims=True))
    alpha = jnp.exp(m - m_new)
    p = jnp.exp((s - m_new).astype(BF16))
    pv = _dot(p, jnp.concatenate([v, jnp.ones_like(v)], axis=1))
    return m_new, alpha * l + pv[:, HEAD_DIM:], alpha * acc + pv[:, :HEAD_DIM]


def _nsa_sel_win_kernel(q_ref, ks_ref, vs_ref, kw_ref, vw_ref, sel_ref, oc_ref, g_ref, o_ref, *, tq, tk, stack, ns):
    q0 = pl.program_id(1) * tq
    g = NSA_GROUP
    nkv = NSA_KV_HEADS
    per_tile = tk // SEL_LEN
    units = [(kv, h0) for kv in range(nkv) for h0 in range(0, g, stack)]
    kv_lanes = [slice(kv * HEAD_DIM, (kv + 1) * HEAD_DIM) for kv in range(nkv)]

    def q_lanes(kv, h):
        return slice((kv * g + h) * HEAD_DIM, (kv * g + h + 1) * HEAD_DIM)

    qs = [jnp.concatenate([q_ref[:, q_lanes(kv, h0 + j)] for j in range(stack)], axis=0) for kv, h0 in units]
    sel_bias = [((sel_ref[kv] - 1.0) * (-NEG)).astype(BF16) for kv in range(nkv)]
    ej = lax.broadcasted_iota(jnp.int32, (ns, tk), 0)
    eb = lax.broadcasted_iota(jnp.int32, (ns, tk), 1)
    eb = lax.shift_right_logical(eb, jnp.full_like(eb, SEL_SHIFT))

    def init():
        return tuple((jnp.full((stack * tq, 1), NEG, F32), jnp.zeros((stack * tq, HEAD_DIM), F32),
                      jnp.zeros((stack * tq, HEAD_DIM), F32)) for _ in units)

    def flash_all(k_ref, v_ref, k0, width, biases, states):
        scores = [_nt_dot(qs[u], k_ref[pl.ds(k0, width), kv_lanes[kv]]) for u, (kv, _) in enumerate(units)]
        return tuple(_flash_update(scores[u], v_ref[pl.ds(k0, width), kv_lanes[kv]], biases[kv], states[u], stack, tq)
                     for u, (kv, _) in enumerate(units))

    def sel_step(kt, states, causal):
        k0 = pl.multiple_of(kt * tk, tk)
        expand = jnp.where(ej == eb + kt * per_tile, 1.0, 0.0).astype(BF16)
        biases = [_dot(sel_bias[kv], expand) for kv in range(nkv)]
        if causal is not None:
            biases = [bias + causal for bias in biases]
        return flash_all(ks_ref, vs_ref, k0, tk, biases, states)

    kt_d = q0 // tk
    t_row = q0 + lax.broadcasted_iota(jnp.int32, (tq, tk), 0)
    col = kt_d * tk + lax.broadcasted_iota(jnp.int32, (tq, tk), 1)
    states = sel_step(kt_d, init(), jnp.where(col <= t_row, 0.0, NEG))
    states = lax.fori_loop(0, kt_d, lambda kt, st: sel_step(kt, st, None), states)
    o_s = [acc * (1.0 / l) for (_, l, acc) in states]

    def win_step(k0, states, bias):
        return flash_all(kw_ref, vw_ref, k0, tq, [bias] * nkv, states)

    ri = lax.broadcasted_iota(jnp.int32, (tq, tq), 0)
    ci = lax.broadcasted_iota(jnp.int32, (tq, tq), 1)
    states = win_step(pl.multiple_of(q0, tq), init(), jnp.where(ci <= ri, 0.0, NEG))
    n_full = jnp.minimum(WINDOW // tq - 1, q0 // tq)
    states = lax.fori_loop(0, n_full, lambda i, st: win_step(pl.multiple_of(q0 - (i + 1) * tq, tq), st, None),
                           states)
    edge_bias = jnp.where(ci > ri, 0.0, NEG) + jnp.where(q0 >= WINDOW, 0.0, NEG)
    states = win_step(pl.multiple_of(jnp.maximum(q0 - WINDOW, 0), tq), states, edge_bias)
    o_w = [acc * (1.0 / l) for (_, l, acc) in states]

    for u, (kv, h0) in enumerate(units):
        gates = g_ref[:, kv * LANES:(kv + 1) * LANES]
        for j in range(stack):
            h = h0 + j
            sl = q_lanes(kv, h)
            rows = slice(j * tq, (j + 1) * tq)
            out = (gates[:, 3 * h:3 * h + 1] * oc_ref[:, sl].astype(F32)
                   + gates[:, 3 * h + 1:3 * h + 2] * o_s[u][rows] + gates[:, 3 * h + 2:3 * h + 3] * o_w[u][rows])
            o_ref[:, sl] = out.astype(o_ref.dtype)


def _nsa_sel_win(qkv, sel, oc, gates, b, t, tq=512, tk=512, stack=4):
    nq = t // tq
    ns = t // SEL_LEN
    kv_block = NSA_Q_W // NSA_KV_W

    def kv_spec(which):
        return pl.BlockSpec((t, NSA_KV_W), lambda i, j: (i, kv_block + which))

    return pl.pallas_call(
        functools.partial(_nsa_sel_win_kernel, tq=tq, tk=tk, stack=stack, ns=ns),
        grid=(b, nq),
        in_specs=[pl.BlockSpec((tq, NSA_Q_W), lambda i, j: (i * nq + j, 0)),
                  kv_spec(2), kv_spec(3), kv_spec(4), kv_spec(5),
                  pl.BlockSpec((None, NSA_KV_HEADS, tq, ns), lambda i, j: (i, 0, j, 0)),
                  pl.BlockSpec((tq, NSA_Q_W), lambda i, j: (i * nq + j, 0)),
                  pl.BlockSpec((tq, NSA_KV_HEADS * LANES), lambda i, j: (i * nq + j, 0))],
        out_specs=pl.BlockSpec((tq, NSA_Q_W), lambda i, j: (i * nq + j, 0)),
        out_shape=jax.ShapeDtypeStruct((b * t, NSA_Q_W), BF16),
        compiler_params=_cp("parallel", "parallel"),
        name="nsa_select_window",
    )(qkv, qkv, qkv, qkv, qkv, sel, oc, gates)


def _stickbreak_kernel(q_ref, k_ref, v_ref, o_ref, *, tile, nh):
    q0 = pl.program_id(2) * tile
    ri = lax.broadcasted_iota(jnp.int32, (tile, tile), 0)
    ci = lax.broadcasted_iota(jnp.int32, (tile, tile), 1)
    strict = ci < ri
    tri = jnp.where(ri > ci, 1.0, 0.0).astype(BF16)
    lanes = [slice(h * HEAD_DIM, (h + 1) * HEAD_DIM) for h in range(nh)]

    def step(k0, diag, carry):
        zs = [_nt_dot(q_ref[:, lanes[h]], k_ref[pl.ds(k0, tile), lanes[h]]) for h in range(nh)]
        log_bs, sufs, sums = [], [], []
        for z in zs:
            log_b = jnp.minimum(z, 0.0) - jnp.log(1.0 + jnp.exp(-jnp.abs(z)))
            log_1m = log_b - z
            if diag:
                log_1m = jnp.where(strict, log_1m, 0.0)
            hi = log_1m.astype(BF16)
            lo = (log_1m - hi.astype(F32)).astype(BF16)
            log_bs.append(log_b)
            sufs.append(_dot(hi, tri) + _dot(lo, tri))
            sums.append(jnp.sum(log_1m, axis=-1, keepdims=True))
        out = []
        for h in range(nh):
            later, acc = carry[2 * h], carry[2 * h + 1]
            a = jnp.exp(log_bs[h] + (sufs[h] + later))
            if diag:
                a = jnp.where(strict, a, 0.0)
            out.extend((later + sums[h], acc + _dot(a.astype(BF16), v_ref[pl.ds(k0, tile), lanes[h]])))
        return tuple(out)

    zero = (jnp.zeros((tile, 1), F32), jnp.zeros((tile, HEAD_DIM), F32)) * nh
    carry = step(pl.multiple_of(q0, tile), True, zero)

    def live(state):
        i, c = state
        top = c[0]
        for h in range(1, nh):
            top = jnp.maximum(top, c[2 * h])
        return (i < q0 // tile) & (jnp.max(top) > F32_EXP_UNDERFLOW)

    def advance(state):
        i, c = state
        return i + 1, step(pl.multiple_of(q0 - (i + 1) * tile, tile), False, c)

    _, carry = lax.while_loop(live, advance, (jnp.int32(0), carry))
    for h in range(nh):
        o_ref[:, lanes[h]] = carry[2 * h + 1].astype(o_ref.dtype)


def _stickbreak(qkv, b, t, tile=256, nh=4):
    nq = t // tile
    w = nh * HEAD_DIM
    nhb = SB_HEADS // nh
    return pl.pallas_call(
        functools.partial(_stickbreak_kernel, tile=tile, nh=nh),
        grid=(b, nhb, nq),
        in_specs=[pl.BlockSpec((tile, w), lambda i, h, j: (i * nq + j, h)),
                  pl.BlockSpec((t, w), lambda i, h, j: (i, nhb + h)),
                  pl.BlockSpec((t, w), lambda i, h, j: (i, 2 * nhb + h))],
        out_specs=pl.BlockSpec((tile, w), lambda i, h, j: (i * nq + j, h)),
        out_shape=jax.ShapeDtypeStruct((b * t, SB_W), BF16),
        compiler_params=_cp("parallel", "parallel", "parallel"),
        name="stickbreak",
    )(qkv, qkv, qkv)


def _cross_kernel(x_ref, g_ref, wq_ref, kv_ref, wo_ref, o_ref):
    x = x_ref[...]
    h = _rms(x, g_ref[...]).astype(BF16)
    q = _dot(h, wq_ref[...]).astype(BF16)
    xw = X_HEADS * HEAD_DIM
    outs = []
    for hd in range(X_HEADS):
        sl = slice(hd * HEAD_DIM, (hd + 1) * HEAD_DIM)
        s = _nt_dot(q[:, sl], kv_ref[:, sl]) * SCALE
        e = jnp.exp(s - jnp.max(s, axis=-1, keepdims=True))
        p = e / jnp.sum(e, axis=-1, keepdims=True)
        outs.append(_dot(p.astype(BF16), kv_ref[:, xw + hd * HEAD_DIM:xw + (hd + 1) * HEAD_DIM]).astype(BF16))
    o_ref[...] = x + _dot(jnp.concatenate(outs, axis=1), wo_ref[...])


def _cross(x, g, wq, kv, wo, b, t, tq=512):
    n, d = x.shape
    nq = t // tq
    mem_len = kv.shape[0] // b
    return pl.pallas_call(
        _cross_kernel,
        grid=(b, nq),
        in_specs=[pl.BlockSpec((tq, d), lambda i, j: (i * nq + j, 0)),
                  pl.BlockSpec((1, d), lambda i, j: (0, 0)),
                  pl.BlockSpec(wq.shape, lambda i, j: (0, 0)),
                  pl.BlockSpec((mem_len, kv.shape[1]), lambda i, j: (i, 0)),
                  pl.BlockSpec(wo.shape, lambda i, j: (0, 0))],
        out_specs=pl.BlockSpec((tq, d), lambda i, j: (i * nq + j, 0)),
        out_shape=jax.ShapeDtypeStruct((n, d), F32),
        compiler_params=_cp("parallel", "parallel"),
        name="cross_attention",
    )(x, g.reshape(1, d), wq, kv, wo)


def _to_row_tiles(ref, val):
    rows, width = val.shape
    c_per = width // LANES
    for c in range(c_per):
        ref[pl.ds(c, rows, stride=c_per), :] = val[:, c * LANES:(c + 1) * LANES]


def _from_row_tiles(ref, c_per):
    rows = ref.shape[0] // c_per
    return jnp.concatenate([ref[pl.ds(c, rows, stride=c_per), :] for c in range(c_per)], axis=1)


def _router_kernel(x_ref, g_ref, w_ref, b_ref, h_ref, r_ref):
    h = _rms(x_ref[...], g_ref[...])
    _to_row_tiles(h_ref, h)
    h_hi = h.astype(BF16)
    h_mid = (h - h_hi.astype(F32)).astype(BF16)
    two = _dot(h_hi, w_ref[...])
    logits = two[:, :LANES] + two[:, LANES:] + _dot(h_mid, w_ref[:, :LANES]) + b_ref[...]
    lane = lax.broadcasted_iota(jnp.int32, logits.shape, 1).astype(F32)
    big = float(LANES)
    is_g = lane < N_GROUPS
    lg = jnp.where(is_g, logits, NEG)
    mg = jnp.max(lg, axis=-1, keepdims=True)
    p_gtop = 1.0 / jnp.sum(jnp.where(is_g, jnp.exp(lg - mg), 0.0), axis=-1, keepdims=True)
    g_top = jnp.min(jnp.where(lg == mg, lane, big), axis=-1, keepdims=True)
    lo = N_GROUPS + EXPERTS_PER_GROUP * g_top
    in_grp = (lane >= lo) & (lane < lo + EXPERTS_PER_GROUP)
    le = jnp.where(in_grp, logits, NEG)
    ee = jnp.where(in_grp, jnp.exp(le - jnp.max(le, axis=-1, keepdims=True)), 0.0)
    p_in = ee / jnp.sum(ee, axis=-1, keepdims=True)
    p1 = jnp.where(in_grp, p_in, -1.0)
    m1 = jnp.max(p1, axis=-1, keepdims=True)
    i1 = jnp.min(jnp.where(p1 == m1, lane, big), axis=-1, keepdims=True)
    p2 = jnp.where(lane == i1, -1.0, p1)
    m2 = jnp.max(p2, axis=-1, keepdims=True)
    i2 = jnp.min(jnp.where(p2 == m2, lane, big), axis=-1, keepdims=True)
    tot = m1 + m2
    r_ref[...] = jnp.where(lane == 0, i1 - N_GROUPS,
                           jnp.where(lane == 1, i2 - N_GROUPS,
                                     jnp.where(lane == 2, m1 / tot * p_gtop,
                                               jnp.where(lane == 3, m2 / tot * p_gtop, 0.0))))


def _router(x, g, w_r, b_r, tm=512):
    n, d = x.shape
    return pl.pallas_call(
        _router_kernel,
        grid=(n // tm,),
        in_specs=[pl.BlockSpec((tm, d), lambda i: (i, 0)), pl.BlockSpec((1, d), lambda i: (0, 0)),
                  pl.BlockSpec((d, 2 * LANES), lambda i: (0, 0)), pl.BlockSpec((1, LANES), lambda i: (0, 0))],
        out_specs=[pl.BlockSpec((tm * (d // LANES), LANES), lambda i: (i, 0)),
                   pl.BlockSpec((tm, LANES), lambda i: (i, 0))],
        out_shape=[jax.ShapeDtypeStruct((n * (d // LANES), LANES), F32), jax.ShapeDtypeStruct((n, LANES), F32)],
        compiler_params=_cp("parallel"),
        name="moe_router",
    )(x, g.reshape(1, d), w_r, b_r)


ROW_CHUNKS = 16
WEIGHT_DMA_PRIORITY = 0
WEIGHT_SLOTS = 3


def _wait_rows(src_hbm, dst, sem):
    pltpu.make_async_copy(src_hbm.at[pl.ds(0, dst.shape[0]), :], dst, sem).wait()


def _row_copy(src_hbm, dst, sem, src_row, dst_row):
    c = ROW_CHUNKS
    return pltpu.make_async_copy(src_hbm.at[pl.ds(pl.multiple_of(src_row * c, c), c), :],
                                 dst.at[pl.ds(pl.multiple_of(dst_row * c, c), c), :], sem)


def _expert_kernel(idx_ref, ord_ref, oe_ref, cnt_ref, src_hbm, wg_hbm, wu_hbm, wd_hbm, o_ref,
                   xbuf, sems, wg_f, wu_f, wd_f, wsems, wg_s, wu_s, wd_s, *, tm):
    i = pl.program_id(0)
    nt = cnt_ref[0]
    n_ord = cnt_ref[1]
    live = i < nt
    slot = lax.rem(i, 2)

    def gather(tile, s):
        def issue(r2, c):
            for queue in range(2):
                r = 2 * r2 + queue
                _row_copy(src_hbm, xbuf.at[s], sems.at[s], idx_ref[tile * tm + r], r).start(priority=queue)
            return c
        lax.fori_loop(0, tm // 2, issue, 0, unroll=4)

    def weight_copies(k, s):
        e = oe_ref[k]
        return (pltpu.make_async_copy(wg_hbm.at[e], wg_f.at[s], wsems.at[s, 0]),
                pltpu.make_async_copy(wu_hbm.at[e], wu_f.at[s], wsems.at[s, 1]),
                pltpu.make_async_copy(wd_hbm.at[e], wd_f.at[s], wsems.at[s, 2]))

    ahead = WEIGHT_SLOTS - 1

    @pl.when(i == 0)
    def _():
        gather(0, 0)
        for j in range(ahead):
            @pl.when(j < n_ord)
            def _():
                for cp in weight_copies(j, j):
                    cp.start(priority=WEIGHT_DMA_PRIORITY)

    @pl.when(i + 1 < nt)
    def _():
        gather(i + 1, 1 - slot)

    k = ord_ref[i]
    fresh = live & ((i == 0) | (k != ord_ref[jnp.maximum(i - 1, 0)]))
    ws = lax.rem(k, WEIGHT_SLOTS)

    @pl.when(fresh)
    def _():
        for cp in weight_copies(k, ws):
            cp.wait()

        @pl.when(k + ahead < n_ord)
        def _():
            for cp in weight_copies(k + ahead, lax.rem(k + ahead, WEIGHT_SLOTS)):
                cp.start(priority=WEIGHT_DMA_PRIORITY)

        wg_s[...] = wg_f[ws].astype(BF16)
        wu_s[...] = wu_f[ws].astype(BF16)
        wd_s[...] = wd_f[ws].astype(BF16)

    @pl.when(live)
    def _():
        _wait_rows(src_hbm, xbuf.at[slot], sems.at[slot])
        x = _from_row_tiles(xbuf.at[slot], ROW_CHUNKS).astype(BF16)
        a = _dot(x, wg_s[...])
        u = _dot(x, wu_s[...])
        act = a * (1.0 / (1.0 + jnp.exp(-a))) * u
        _to_row_tiles(o_ref, _dot(act.astype(BF16), wd_s[...]))

    @pl.when(jnp.logical_not(live))
    def _():
        o_ref[...] = jnp.zeros_like(o_ref)


def _experts(src, tok_of_row, w_g, w_u, w_d, tile_ord, ord_expert, counts, tm=EXPERT_TILE):
    c = ROW_CHUNKS
    n_tiles_max = tok_of_row.shape[0] // tm
    d = c * LANES
    f = w_g.shape[2]
    any_spec = pl.BlockSpec(memory_space=pl.ANY)
    return pl.pallas_call(
        functools.partial(_expert_kernel, tm=tm),
        grid_spec=pltpu.PrefetchScalarGridSpec(
            num_scalar_prefetch=4,
            grid=(n_tiles_max,),
            in_specs=[any_spec, any_spec, any_spec, any_spec],
            out_specs=pl.BlockSpec((tm * c, LANES), lambda i, *_: (i, 0)),
            scratch_shapes=[pltpu.VMEM((2, tm * c, LANES), F32), pltpu.SemaphoreType.DMA((2,)),
                            pltpu.VMEM((WEIGHT_SLOTS, d, f), F32), pltpu.VMEM((WEIGHT_SLOTS, d, f), F32),
                            pltpu.VMEM((WEIGHT_SLOTS, f, d), F32), pltpu.SemaphoreType.DMA((WEIGHT_SLOTS, 3)),
                            pltpu.VMEM((d, f), BF16), pltpu.VMEM((d, f), BF16), pltpu.VMEM((f, d), BF16)]),
        out_shape=jax.ShapeDtypeStruct((n_tiles_max * tm * c, LANES), F32),
        compiler_params=_cp("arbitrary"),
        name="moe_experts",
    )(tok_of_row, tile_ord, ord_expert, counts, src, w_g, w_u, w_d)


def _combine_kernel(p0_ref, p1_ref, ys_hbm, x_ref, r_ref, g_ref, o_ref, buf, sems, *, tm):
    i = pl.program_id(0)
    slot = lax.rem(i, 2)

    def fetch(tile, s):
        def issue(r, c):
            _row_copy(ys_hbm, buf.at[s, 0], sems.at[s], p0_ref[tile * tm + r], r).start()
            _row_copy(ys_hbm, buf.at[s, 1], sems.at[s], p1_ref[tile * tm + r], r).start(priority=1)
            return c
        lax.fori_loop(0, tm, issue, 0, unroll=8)

    @pl.when(i == 0)
    def _():
        fetch(0, 0)

    @pl.when(i + 1 < pl.num_programs(0))
    def _():
        fetch(i + 1, 1 - slot)

    _wait_rows(ys_hbm, buf.at[slot, 0], sems.at[slot])
    _wait_rows(ys_hbm, buf.at[slot, 1], sems.at[slot])
    route = r_ref[...]
    y = (route[:, 2:3] * _from_row_tiles(buf.at[slot, 0], ROW_CHUNKS)
         + route[:, 3:4] * _from_row_tiles(buf.at[slot, 1], ROW_CHUNKS))
    o_ref[...] = _rms(x_ref[...] + y, g_ref[...])


def _combine(ys, pos0, pos1, x, route, g, tm=256):
    n, d = x.shape
    return pl.pallas_call(
        functools.partial(_combine_kernel, tm=tm),
        grid_spec=pltpu.PrefetchScalarGridSpec(
            num_scalar_prefetch=2,
            grid=(n // tm,),
            in_specs=[pl.BlockSpec(memory_space=pl.ANY),
                      pl.BlockSpec((tm, d), lambda i, p0, p1: (i, 0)),
                      pl.BlockSpec((tm, LANES), lambda i, p0, p1: (i, 0)),
                      pl.BlockSpec((1, d), lambda i, p0, p1: (0, 0))],
            out_specs=pl.BlockSpec((tm, d), lambda i, p0, p1: (i, 0)),
            scratch_shapes=[pltpu.VMEM((2, 2, tm * ROW_CHUNKS, LANES), F32), pltpu.SemaphoreType.DMA((2,))]),
        out_shape=jax.ShapeDtypeStruct((n, d), F32),
        compiler_params=_cp("arbitrary"),
        name="moe_combine_norm",
    )(pos0, pos1, ys, x, route, g.reshape(1, d))


def _dispatch_plan(e1, e2, tm=EXPERT_TILE):
    n = e1.shape[0]
    n_tiles_max = (2 * n) // tm + N_EXPERTS
    rows = n_tiles_max * tm
    e_flat = jnp.concatenate([e1, e2])
    tok = jnp.concatenate([jnp.arange(n, dtype=jnp.int32)] * 2)
    onehot = (e_flat[:, None] == jnp.arange(N_EXPERTS, dtype=jnp.int32)[None, :]).astype(jnp.int32)
    csum = jnp.cumsum(onehot, axis=0)
    count = csum[-1]
    rank = jnp.sum((csum - 1) * onehot, axis=1)
    tiles_per = (count + tm - 1) // tm
    tile_end = jnp.cumsum(tiles_per)
    row_start = (tile_end - tiles_per) * tm
    dest = (row_start[e_flat] + rank).astype(jnp.int32)
    tok_of_row = jnp.zeros((rows,), jnp.int32).at[dest].set(tok)
    n_tiles = tile_end[-1].astype(jnp.int32)
    tile_id = jnp.minimum(jnp.arange(n_tiles_max, dtype=jnp.int32), n_tiles - 1)
    tile_expert = jnp.sum((tile_id[:, None] >= tile_end[None, :]).astype(jnp.int32), axis=1)
    used = jnp.cumsum((tiles_per > 0).astype(jnp.int32))
    tile_ord = (used[tile_expert] - 1).astype(jnp.int32)
    ordinals = jnp.arange(N_EXPERTS, dtype=jnp.int32)
    ord_expert = jnp.minimum(jnp.sum((used[None, :] <= ordinals[:, None]).astype(jnp.int32), axis=1),
                             N_EXPERTS - 1).astype(jnp.int32)
    counts = jnp.stack([n_tiles, used[-1]]).astype(jnp.int32)
    return tok_of_row, tile_ord, ord_expert, counts, dest[:n], dest[n:]


def _rope_tables(positions):
    half = ROT_DIM // 2
    inv = ROPE_THETA ** (-jnp.arange(half, dtype=F32) * (2.0 / ROT_DIM))
    ang = positions.astype(F32).reshape(-1, 1) * inv[None, :]
    cos, sin = jnp.cos(ang), jnp.sin(ang)
    n = ang.shape[0]
    c = jnp.concatenate([cos, cos, jnp.ones((n, LANES - ROT_DIM), F32)], axis=1)
    s1 = jnp.concatenate([-sin, jnp.zeros((n, LANES - half), F32)], axis=1)
    s2 = jnp.concatenate([jnp.zeros((n, half), F32), sin, jnp.zeros((n, LANES - ROT_DIM), F32)], axis=1)
    return c, s1, s2


def _layer(x, mem, positions, g_mix, w_in, cmp_pe_k, cmp_w1_k, cmp_w2_k, cmp_pe_v, cmp_w1_v, cmp_w2_v,
           w_br_nsa, w_br_sb, w_o, g_cross, g_mem, w_cq, w_ck, w_cv, w_co,
           g_moe, w_rg, b_rg, w_re, b_re, w_eg, w_eu, w_ed):
    b, t, d = x.shape
    n = b * t
    xf = x.reshape(n, d)

    g0 = NSA_W
    w_t = w_in.T
    w_nsa = _cast_rows(w_t, 0, g0)
    w_rest = _cast_rows(w_t, g0 + NSA_GATES, w_t.shape[0] - g0 - NSA_GATES)
    w_g = w_t[g0:g0 + NSA_GATES].reshape(NSA_KV_HEADS, 3 * NSA_GROUP, d)
    w_g = jnp.pad(w_g, ((0, 0), (0, LANES - 3 * NSA_GROUP), (0, 0))).reshape(NSA_KV_HEADS * LANES, d).astype(BF16)
    h = _rmsnorm(xf, g_mix, BF16)
    c, s1, s2 = _rope_tables(positions)
    row_spec = pl.BlockSpec((1024, LANES), lambda i, j: (i, 0))
    qkv_nsa = _mm(functools.partial(_mm_nsa_kernel, tn=256), h, w_nsa, NSA_W, 0, 1024, 256, BF16, w_t=True,
                  extra=(c, s1, s2), extra_specs=(row_spec, row_spec, row_spec), name="in_proj_nsa")
    qkv_sb = _mm(functools.partial(_mm_sb_kernel, q_tiles=SB_W // 512), h, w_rest, 3 * SB_W, 0,
                 1024, 512, BF16, w_t=True, name="in_proj_sb")
    gates = _mm(_mm_sigmoid_kernel, h, w_rest, 2 * d, 3 * SB_W // 512, 1024, 512, BF16, w_t=True,
                name="in_proj_gates")
    nsa_gates = _mm(_mm_sigmoid_kernel, h, w_g, NSA_KV_HEADS * LANES, 0, 1024, NSA_KV_HEADS * LANES, F32,
                    w_t=True, name="in_proj_nsa_gates")

    def chunked(col0):
        tok = qkv_nsa[:, col0:col0 + NSA_KV_W].reshape(b, t, NSA_KV_HEADS, HEAD_DIM).transpose(0, 2, 1, 3)
        return tok.reshape(b, NSA_KV_HEADS, t // CMP_STRIDE, CMP_STRIDE * HEAD_DIM)

    kc = _compress(chunked(NSA_Q_W), cmp_pe_k, cmp_w1_k, cmp_w2_k)
    vc = _compress(chunked(NSA_Q_W + NSA_KV_W), cmp_pe_v, cmp_w1_v, cmp_w2_v)
    o_c, sel = _nsa_cmp(qkv_nsa, kc, vc, b, t)
    o_nsa = _nsa_sel_win(qkv_nsa, sel, o_c, nsa_gates, b, t)
    o_sb = _stickbreak(qkv_sb, b, t)
    merged = _merge(o_nsa, o_sb, w_br_nsa.astype(BF16), w_br_sb.astype(BF16), gates)
    x1 = _mm(_mm_resid_kernel, merged, w_o.astype(BF16), d, 0, 1024, 512, F32,
             extra=(xf,), extra_specs=(pl.BlockSpec((1024, 512), lambda i, j: (i, j)),), name="out_proj")

    w_kv = jnp.concatenate([w_ck, w_cv], axis=1).astype(BF16)
    kv = _norm_mm(mem.reshape(-1, d), g_mem, w_kv, BF16)
    x2 = _cross(x1, g_cross, w_cq.astype(BF16), kv, w_co.astype(BF16), b, t)

    w_r = jnp.pad(jnp.concatenate([w_rg, w_re], axis=1), ((0, 0), (0, LANES - N_GROUPS - N_EXPERTS)))
    b_r = jnp.pad(jnp.concatenate([b_rg, b_re]), (0, LANES - N_GROUPS - N_EXPERTS)).reshape(1, LANES)
    w_r_hi = w_r.astype(BF16)
    w_r2 = jnp.concatenate([w_r_hi, (w_r - w_r_hi.astype(F32)).astype(BF16)], axis=1)
    hm, route = _router(x2, g_moe, w_r2, b_r)
    e1 = route[:, 0].astype(jnp.int32)
    e2 = route[:, 1].astype(jnp.int32)
    tok_of_row, tile_ord, ord_expert, counts, pos0, pos1 = _dispatch_plan(e1, e2)
    f = w_eg.shape[-1]
    ys = _experts(hm, tok_of_row, w_eg.reshape(N_EXPERTS, d, f), w_eu.reshape(N_EXPERTS, d, f),
                  w_ed.reshape(N_EXPERTS, f, d), tile_ord, ord_expert, counts)
    return ys, pos0, pos1, x2, route


def kernel(x, mem, positions, g_mix, w_in, cmp_pe_k, cmp_w1_k, cmp_w2_k, cmp_pe_v, cmp_w1_v, cmp_w2_v, w_br_nsa, w_br_sb, w_o, g_cross, g_mem, w_cq, w_ck, w_cv, w_co, g_moe, w_rg, b_rg, w_re, b_re, w_eg, w_eu, w_ed, g_final):
    b, t, d = x.shape
    assert w_in.shape[0] == 1, "one layer"
    ys, pos0, pos1, x2, route = _layer(
        x, mem, positions, g_mix[0], w_in[0], cmp_pe_k[0], cmp_w1_k[0], cmp_w2_k[0], cmp_pe_v[0], cmp_w1_v[0],
        cmp_w2_v[0], w_br_nsa[0], w_br_sb[0], w_o[0], g_cross[0], g_mem[0], w_cq[0], w_ck[0], w_cv[0], w_co[0],
        g_moe[0], w_rg[0], b_rg[0], w_re[0], b_re[0], w_eg[0], w_eu[0], w_ed[0])
    return _combine(ys, pos0, pos1, x2, route, g_final).reshape(b, t, d)
```

```python
import functools

import jax
import jax.numpy as jnp
from jax import lax
from jax.experimental import pallas as pl
from jax.experimental.pallas import tpu as pltpu

F32 = jnp.float32
BF16 = jnp.bfloat16

HEAD_DIM = 128
NSA_HEADS = 8
NSA_KV_HEADS = 2
NSA_GROUP = NSA_HEADS // NSA_KV_HEADS
SB_HEADS = 8
ROT_DIM = HEAD_DIM // 4
ROPE_THETA = 500000.0
CMP_LEN = 32
CMP_STRIDE = 16
SEL_LEN = 64
SEL_SHIFT = 6
SEL_TOPN = 16
WINDOW = 512
FORCE_BONUS = 1000.0
X_HEADS = 4
N_GROUPS = 4
EXPERTS_PER_GROUP = 8
N_EXPERTS = N_GROUPS * EXPERTS_PER_GROUP
EPS = 1e-6
SCALE = HEAD_DIM ** -0.5
NEG = -1e30
F32_EXP_UNDERFLOW = -104.0

NSA_Q_W = NSA_HEADS * HEAD_DIM
NSA_KV_W = NSA_KV_HEADS * HEAD_DIM
NSA_W = NSA_Q_W + 6 * NSA_KV_W
NSA_GATES = 3 * NSA_HEADS
SB_W = SB_HEADS * HEAD_DIM
LANES = 128
VMEM_LIMIT = 56 * 1024 * 1024

EXPERT_TILE = 256


def _cp(*sem, vmem=VMEM_LIMIT):
    return pltpu.CompilerParams(dimension_semantics=sem, vmem_limit_bytes=vmem)


def _nt_dot(a, b):
    return lax.dot_general(a, b, (((1,), (1,)), ((), ())), preferred_element_type=F32)


def _dot(a, b):
    return jnp.dot(a, b, preferred_element_type=F32)


def _split3(x):
    hi = x.astype(BF16)
    r1 = x - hi.astype(F32)
    mid = r1.astype(BF16)
    lo = (r1 - mid.astype(F32)).astype(BF16)
    return hi, mid, lo


def _rms(x, g):
    return x * lax.rsqrt(jnp.mean(x * x, axis=-1, keepdims=True) + EPS) * g


def _rmsnorm_kernel(x_ref, g_ref, o_ref):
    o_ref[...] = _rms(x_ref[...].astype(F32), g_ref[...]).astype(o_ref.dtype)


def _rmsnorm(x, g, out_dtype, tm=512):
    n, d = x.shape
    return pl.pallas_call(
        _rmsnorm_kernel,
        grid=(n // tm,),
        in_specs=[pl.BlockSpec((tm, d), lambda i: (i, 0)), pl.BlockSpec((1, d), lambda i: (0, 0))],
        out_specs=pl.BlockSpec((tm, d), lambda i: (i, 0)),
        out_shape=jax.ShapeDtypeStruct((n, d), out_dtype),
        compiler_params=_cp("parallel"),
        name="rmsnorm",
    )(x, g.reshape(1, d))


def _cast_rows_kernel(w_ref, o_ref):
    o_ref[...] = w_ref[...].astype(o_ref.dtype)


def _cast_rows(w_t, row0, n_rows, tn=512):
    k = w_t.shape[1]
    assert n_rows % tn == 0 and row0 % 8 == 0
    return pl.pallas_call(
        _cast_rows_kernel,
        grid=(n_rows // tn,),
        in_specs=[pl.BlockSpec((pl.Element(tn), pl.Element(k)), lambda j: ((row0 // 8 + j * (tn // 8)) * 8, 0))],
        out_specs=pl.BlockSpec((tn, k), lambda j: (j, 0)),
        out_shape=jax.ShapeDtypeStruct((n_rows, k), BF16),
        compiler_params=_cp("parallel"),
        name="cast_rows",
    )(w_t)


def _rope_half(x, c, s1, s2):
    return x * c + pltpu.roll(x, LANES - ROT_DIM // 2, 1) * s1 + pltpu.roll(x, ROT_DIM // 2, 1) * s2


def _mm_nsa_kernel(a_ref, w_ref, c_ref, s1_ref, s2_ref, o_ref, *, tn):
    j = pl.program_id(1)
    acc = _nt_dot(a_ref[...], w_ref[...])
    is_rope = (j < 5) | (j == 6) | (j == 8)

    @pl.when(is_rope)
    def _():
        qs = jnp.where(j < NSA_Q_W // tn, SCALE, 1.0)
        c, s1, s2 = c_ref[...] * qs, s1_ref[...] * qs, s2_ref[...] * qs
        for hh in range(tn // LANES):
            sl = slice(hh * LANES, (hh + 1) * LANES)
            o_ref[:, sl] = _rope_half(acc[:, sl], c, s1, s2).astype(o_ref.dtype)

    @pl.when(jnp.logical_not(is_rope))
    def _():
        o_ref[...] = acc.astype(o_ref.dtype)


def _mm_sb_kernel(a_ref, w_ref, o_ref, *, q_tiles):
    acc = _nt_dot(a_ref[...], w_ref[...])
    is_q = pl.program_id(1) < q_tiles
    o_ref[...] = (acc * jnp.where(is_q, SCALE, 1.0)).astype(o_ref.dtype)


def _mm_sigmoid_kernel(a_ref, w_ref, o_ref):
    o_ref[...] = jax.nn.sigmoid(_nt_dot(a_ref[...], w_ref[...])).astype(o_ref.dtype)


def _mm_resid_kernel(a_ref, w_ref, x_ref, o_ref):
    o_ref[...] = x_ref[...] + _dot(a_ref[...], w_ref[...])


def _mm(kern, a, w, n_cols, col_block0, tm, tn, out_dtype, extra=(), extra_specs=(), name="mm", w_t=False):
    m, k = a.shape
    w_spec = (pl.BlockSpec((tn, k), lambda i, j: (col_block0 + j, 0)) if w_t
              else pl.BlockSpec((k, tn), lambda i, j: (0, col_block0 + j)))
    return pl.pallas_call(
        kern,
        grid=(m // tm, n_cols // tn),
        in_specs=[pl.BlockSpec((tm, k), lambda i, j: (i, 0)), w_spec, *extra_specs],
        out_specs=pl.BlockSpec((tm, tn), lambda i, j: (i, j)),
        out_shape=jax.ShapeDtypeStruct((m, n_cols), out_dtype),
        compiler_params=_cp("parallel", "arbitrary"),
        name=name,
    )(a, w, *extra)


def _merge_kernel(a1_ref, a2_ref, w1_ref, w2_ref, ga_ref, gb_ref, o_ref):
    ya = _dot(a1_ref[...], w1_ref[...])
    yb = _dot(a2_ref[...], w2_ref[...])
    o_ref[...] = (ga_ref[...].astype(F32) * ya + gb_ref[...].astype(F32) * yb).astype(o_ref.dtype)


def _merge(o_nsa, o_sb, w1, w2, gates, tm=1024, tn=512):
    n, k = o_nsa.shape
    d = w1.shape[1]
    nb = d // tn
    return pl.pallas_call(
        _merge_kernel,
        grid=(n // tm, nb),
        in_specs=[pl.BlockSpec((tm, k), lambda i, j: (i, 0)), pl.BlockSpec((tm, k), lambda i, j: (i, 0)),
                  pl.BlockSpec((k, tn), lambda i, j: (0, j)), pl.BlockSpec((k, tn), lambda i, j: (0, j)),
                  pl.BlockSpec((tm, tn), lambda i, j: (i, j)), pl.BlockSpec((tm, tn), lambda i, j: (i, nb + j))],
        out_specs=pl.BlockSpec((tm, tn), lambda i, j: (i, j)),
        out_shape=jax.ShapeDtypeStruct((n, d), BF16),
        compiler_params=_cp("parallel", "arbitrary"),
        name="merge",
    )(o_nsa, o_sb, w1, w2, gates, gates)


def _norm_mm_kernel(x_ref, g_ref, w_ref, o_ref):
    h = _rms(x_ref[...].astype(F32), g_ref[...]).astype(BF16)
    o_ref[...] = _dot(h, w_ref[...]).astype(o_ref.dtype)


def _norm_mm(x, g, w, out_dtype, tm=256):
    n, d = x.shape
    nc = w.shape[1]
    return pl.pallas_call(
        _norm_mm_kernel,
        grid=(n // tm,),
        in_specs=[pl.BlockSpec((tm, d), lambda i: (i, 0)), pl.BlockSpec((1, d), lambda i: (0, 0)),
                  pl.BlockSpec((d, nc), lambda i: (0, 0))],
        out_specs=pl.BlockSpec((tm, nc), lambda i: (i, 0)),
        out_shape=jax.ShapeDtypeStruct((n, nc), out_dtype),
        compiler_params=_cp("parallel"),
        name="norm_mm",
    )(x, g.reshape(1, d), w)


def _compress_kernel(x_ref, pe_ref, w1_ref, w2_ref, o_ref):
    x = x_ref[...].astype(F32)
    half = x.shape[1]
    xa = (x + pe_ref[0:1, :]).astype(BF16)
    xb = (x + pe_ref[1:2, :]).astype(BF16)
    y1 = _dot(xa, w1_ref[0:half, :].astype(BF16))
    y2 = _dot(xb, w1_ref[half:2 * half, :].astype(BF16))
    pre = y1 + pltpu.roll(y2, x.shape[0] - 1, 0)
    act = pre * (1.0 / (1.0 + jnp.exp(-pre)))
    o_ref[...] = _dot(act.astype(BF16), w2_ref[...].astype(BF16)).astype(o_ref.dtype)


def _compress(tok, pe, w1, w2):
    b, hkv, nchunk, width = tok.shape
    return pl.pallas_call(
        _compress_kernel,
        grid=(b, hkv),
        in_specs=[pl.BlockSpec((None, None, nchunk, width), lambda i, j: (i, j, 0, 0)),
                  pl.BlockSpec((2, width), lambda i, j: (0, 0)),
                  pl.BlockSpec((2 * width, HEAD_DIM), lambda i, j: (0, 0)),
                  pl.BlockSpec((HEAD_DIM, HEAD_DIM), lambda i, j: (0, 0))],
        out_specs=pl.BlockSpec((None, None, nchunk, HEAD_DIM), lambda i, j: (i, j, 0, 0)),
        out_shape=jax.ShapeDtypeStruct((b, hkv, nchunk, HEAD_DIM), BF16),
        compiler_params=_cp("parallel", "parallel"),
        name="nsa_compress",
    )(tok, pe.reshape(2, width), w1, w2)


def _nsa_cmp_kernel(q_ref, kc_ref, vc_ref, oc_ref, sel_ref, *, tq, nc, ns):
    q0 = pl.program_id(2) * tq
    kc = kc_ref[...]
    vc = vc_ref[...]
    t_row = q0 + lax.broadcasted_iota(jnp.int32, (tq, nc), 0)
    c_idx = lax.broadcasted_iota(jnp.int32, (tq, nc), 1)
    vis = (c_idx * CMP_STRIDE + (CMP_LEN - 1) <= t_row) & (c_idx < nc - 1)
    heads = [slice(h * HEAD_DIM, (h + 1) * HEAD_DIM) for h in range(NSA_GROUP)]
    scores = [_nt_dot(q_ref[:, sl], kc) for sl in heads]
    probs = []
    for s in scores:
        s = jnp.where(vis, s, NEG)
        e = jnp.where(vis, jnp.exp(s - jnp.max(s, axis=-1, keepdims=True)), 0.0)
        den = jnp.sum(e, axis=-1, keepdims=True)
        probs.append(e / jnp.where(den > 0, den, 1.0))
    for sl, p in zip(heads, probs):
        oc_ref[:, sl] = _dot(p.astype(BF16), vc).astype(oc_ref.dtype)
    imp = sum(probs[1:], probs[0])
    jj = lax.broadcasted_iota(jnp.int32, (ns, nc), 0)
    cc = lax.broadcasted_iota(jnp.int32, (ns, nc), 1) * CMP_STRIDE
    ov = jnp.where((cc < (jj + 1) * SEL_LEN) & (cc + CMP_LEN > jj * SEL_LEN), 1.0, 0.0).astype(BF16)
    imp_t = sum(_nt_dot(ov, part) for part in _split3(imp))
    jb = lax.broadcasted_iota(jnp.int32, (ns, tq), 0)
    tt = q0 + lax.broadcasted_iota(jnp.int32, (ns, tq), 1)
    cur = lax.shift_right_logical(tt, jnp.full_like(tt, SEL_SHIFT))
    forced = (jb == 0) | (jb == cur) | (jb == cur - 1)
    valid = jb * SEL_LEN <= tt
    score = jnp.where(valid, imp_t + jnp.where(forced, FORCE_BONUS, 0.0), NEG)
    rank = jnp.zeros((ns, tq), F32)
    for i in range(ns):
        row = score[i:i + 1, :]
        rank = rank + jnp.where(row > score, 1.0, jnp.where((row == score) & (jb > i), 1.0, 0.0))
    sel_t = jnp.where(valid & (rank < float(min(SEL_TOPN, ns))), 1.0, 0.0).astype(BF16)
    eye = jnp.where(lax.broadcasted_iota(jnp.int32, (tq, tq), 0) == lax.broadcasted_iota(jnp.int32, (tq, tq), 1),
                    1.0, 0.0).astype(BF16)
    sel_ref[...] = _nt_dot(eye, sel_t)


def _nsa_cmp(qkv, kc, vc, b, t, tq=512):
    nc = kc.shape[2]
    ns = t // SEL_LEN
    nq = t // tq
    gw = NSA_GROUP * HEAD_DIM
    return pl.pallas_call(
        functools.partial(_nsa_cmp_kernel, tq=tq, nc=nc, ns=ns),
        grid=(b, NSA_KV_HEADS, nq),
        in_specs=[pl.BlockSpec((tq, gw), lambda i, k, j: (i * nq + j, k)),
                  pl.BlockSpec((None, None, nc, HEAD_DIM), lambda i, k, j: (i, k, 0, 0)),
                  pl.BlockSpec((None, None, nc, HEAD_DIM), lambda i, k, j: (i, k, 0, 0))],
        out_specs=[pl.BlockSpec((tq, gw), lambda i, k, j: (i * nq + j, k)),
                   pl.BlockSpec((None, None, tq, ns), lambda i, k, j: (i, k, j, 0))],
        out_shape=[jax.ShapeDtypeStruct((b * t, NSA_Q_W), BF16),
                   jax.ShapeDtypeStruct((b, NSA_KV_HEADS, t, ns), F32)],
        compiler_params=_cp("parallel", "parallel", "parallel"),
        name="nsa_cmp_select",
    )(qkv, kc, vc)


def _flash_update(s, v, bias, state, g, tq):
    m, l, acc = state
    tk = v.shape[0]
    if bias is not None:
        s = (s.reshape(g, tq, tk) + bias[None]).reshape(g * tq, tk)
    m_new = jnp.maximum(m, jnp.max(s, axis=-1, keepdims=True))
    alpha = jnp.exp(m - m_new)
    p = jnp.exp((s - m_new).astype(BF16))
    pv = _dot(p, jnp.concatenate([v, jnp.ones_like(v)], axis=1))
    return m_new, alpha * l + pv[:, HEAD_DIM:], alpha * acc + pv[:, :HEAD_DIM]


def _nsa_sel_win_kernel(q_ref, ks_ref, vs_ref, kw_ref, vw_ref, sel_ref, oc_ref, g_ref, o_ref, *, tq, tk, stack, ns):
    q0 = pl.program_id(1) * tq
    g = NSA_GROUP
    nkv = NSA_KV_HEADS
    per_tile = tk // SEL_LEN
    units = [(kv, h0) for kv in range(nkv) for h0 in range(0, g, stack)]
    kv_lanes = [slice(kv * HEAD_DIM, (kv + 1) * HEAD_DIM) for kv in range(nkv)]

    def q_lanes(kv, h):
        return slice((kv * g + h) * HEAD_DIM, (kv * g + h + 1) * HEAD_DIM)

    qs = [jnp.concatenate([q_ref[:, q_lanes(kv, h0 + j)] for j in range(stack)], axis=0) for kv, h0 in units]
    sel_bias = [((sel_ref[kv] - 1.0) * (-NEG)).astype(BF16) for kv in range(nkv)]
    ej = lax.broadcasted_iota(jnp.int32, (ns, tk), 0)
    eb = lax.broadcasted_iota(jnp.int32, (ns, tk), 1)
    eb = lax.shift_right_logical(eb, jnp.full_like(eb, SEL_SHIFT))

    def init():
        return tuple((jnp.full((stack * tq, 1), NEG, F32), jnp.zeros((stack * tq, HEAD_DIM), F32),
                      jnp.zeros((stack * tq, HEAD_DIM), F32)) for _ in units)

    def flash_all(k_ref, v_ref, k0, width, biases, states):
        scores = [_nt_dot(qs[u], k_ref[pl.ds(k0, width), kv_lanes[kv]]) for u, (kv, _) in enumerate(units)]
        return tuple(_flash_update(scores[u], v_ref[pl.ds(k0, width), kv_lanes[kv]], biases[kv], states[u], stack, tq)
                     for u, (kv, _) in enumerate(units))

    def sel_step(kt, states, causal):
        k0 = pl.multiple_of(kt * tk, tk)
        expand = jnp.where(ej == eb + kt * per_tile, 1.0, 0.0).astype(BF16)
        biases = [_dot(sel_bias[kv], expand) for kv in range(nkv)]
        if causal is not None:
            biases = [bias + causal for bias in biases]
        return flash_all(ks_ref, vs_ref, k0, tk, biases, states)

    kt_d = q0 // tk
    t_row = q0 + lax.broadcasted_iota(jnp.int32, (tq, tk), 0)
    col = kt_d * tk + lax.broadcasted_iota(jnp.int32, (tq, tk), 1)
    states = sel_step(kt_d, init(), jnp.where(col <= t_row, 0.0, NEG))
    states = lax.fori_loop(0, kt_d, lambda kt, st: sel_step(kt, st, None), states)
    o_s = [acc * (1.0 / l) for (_, l, acc) in states]

    def win_step(k0, states, bias):
        return flash_all(kw_ref, vw_ref, k0, tq, [bias] * nkv, states)

    ri = lax.broadcasted_iota(jnp.int32, (tq, tq), 0)
    ci = lax.broadcasted_iota(jnp.int32, (tq, tq), 1)
    states = win_step(pl.multiple_of(q0, tq), init(), jnp.where(ci <= ri, 0.0, NEG))
    n_full = jnp.minimum(WINDOW // tq - 1, q0 // tq)
    states = lax.fori_loop(0, n_full, lambda i, st: win_step(pl.multiple_of(q0 - (i + 1) * tq, tq), st, None),
                           states)
    edge_bias = jnp.where(ci > ri, 0.0, NEG) + jnp.where(q0 >= WINDOW, 0.0, NEG)
    states = win_step(pl.multiple_of(jnp.maximum(q0 - WINDOW, 0), tq), states, edge_bias)
    o_w = [acc * (1.0 / l) for (_, l, acc) in states]

    for u, (kv, h0) in enumerate(units):
        gates = g_ref[:, kv * LANES:(kv + 1) * LANES]
        for j in range(stack):
            h = h0 + j
            sl = q_lanes(kv, h)
            rows = slice(j * tq, (j + 1) * tq)
            out = (gates[:, 3 * h:3 * h + 1] * oc_ref[:, sl].astype(F32)
                   + gates[:, 3 * h + 1:3 * h + 2] * o_s[u][rows] + gates[:, 3 * h + 2:3 * h + 3] * o_w[u][rows])
            o_ref[:, sl] = out.astype(o_ref.dtype)


def _nsa_sel_win(qkv, sel, oc, gates, b, t, tq=512, tk=512, stack=4):
    nq = t // tq
    ns = t // SEL_LEN
    kv_block = NSA_Q_W // NSA_KV_W

    def kv_spec(which):
        return pl.BlockSpec((t, NSA_KV_W), lambda i, j: (i, kv_block + which))

    return pl.pallas_call(
        functools.partial(_nsa_sel_win_kernel, tq=tq, tk=tk, stack=stack, ns=ns),
        grid=(b, nq),
        in_specs=[pl.BlockSpec((tq, NSA_Q_W), lambda i, j: (i * nq + j, 0)),
                  kv_spec(2), kv_spec(3), kv_spec(4), kv_spec(5),
                  pl.BlockSpec((None, NSA_KV_HEADS, tq, ns), lambda i, j: (i, 0, j, 0)),
                  pl.BlockSpec((tq, NSA_Q_W), lambda i, j: (i * nq + j, 0)),
                  pl.BlockSpec((tq, NSA_KV_HEADS * LANES), lambda i, j: (i * nq + j, 0))],
        out_specs=pl.BlockSpec((tq, NSA_Q_W), lambda i, j: (i * nq + j, 0)),
        out_shape=jax.ShapeDtypeStruct((b * t, NSA_Q_W), BF16),
        compiler_params=_cp("parallel", "parallel"),
        name="nsa_select_window",
    )(qkv, qkv, qkv, qkv, qkv, sel, oc, gates)


def _stickbreak_kernel(q_ref, k_ref, v_ref, o_ref, *, tile, nh):
    q0 = pl.program_id(2) * tile
    ri = lax.broadcasted_iota(jnp.int32, (tile, tile), 0)
    ci = lax.broadcasted_iota(jnp.int32, (tile, tile), 1)
    strict = ci < ri
    tri = jnp.where(ri > ci, 1.0, 0.0).astype(BF16)
    lanes = [slice(h * HEAD_DIM, (h + 1) * HEAD_DIM) for h in range(nh)]

    def step(k0, diag, carry):
        zs = [_nt_dot(q_ref[:, lanes[h]], k_ref[pl.ds(k0, tile), lanes[h]]) for h in range(nh)]
        log_bs, sufs, sums = [], [], []
        for z in zs:
            log_b = jnp.minimum(z, 0.0) - jnp.log(1.0 + jnp.exp(-jnp.abs(z)))
            log_1m = log_b - z
            if diag:
                log_1m = jnp.where(strict, log_1m, 0.0)
            hi = log_1m.astype(BF16)
            lo = (log_1m - hi.astype(F32)).astype(BF16)
            log_bs.append(log_b)
            sufs.append(_dot(hi, tri) + _dot(lo, tri))
            sums.append(jnp.sum(log_1m, axis=-1, keepdims=True))
        out = []
        for h in range(nh):
            later, acc = carry[2 * h], carry[2 * h + 1]
            a = jnp.exp(log_bs[h] + (sufs[h] + later))
            if diag:
                a = jnp.where(strict, a, 0.0)
            out.extend((later + sums[h], acc + _dot(a.astype(BF16), v_ref[pl.ds(k0, tile), lanes[h]])))
        return tuple(out)

    zero = (jnp.zeros((tile, 1), F32), jnp.zeros((tile, HEAD_DIM), F32)) * nh
    carry = step(pl.multiple_of(q0, tile), True, zero)

    def live(state):
        i, c = state
        top = c[0]
        for h in range(1, nh):
            top = jnp.maximum(top, c[2 * h])
        return (i < q0 // tile) & (jnp.max(top) > F32_EXP_UNDERFLOW)

    def advance(state):
        i, c = state
        return i + 1, step(pl.multiple_of(q0 - (i + 1) * tile, tile), False, c)

    _, carry = lax.while_loop(live, advance, (jnp.int32(0), carry))
    for h in range(nh):
        o_ref[:, lanes[h]] = carry[2 * h + 1].astype(o_ref.dtype)


def _stickbreak(qkv, b, t, tile=256, nh=4):
    nq = t // tile
    w = nh * HEAD_DIM
    nhb = SB_HEADS // nh
    return pl.pallas_call(
        functools.partial(_stickbreak_kernel, tile=tile, nh=nh),
        grid=(b, nhb, nq),
        in_specs=[pl.BlockSpec((tile, w), lambda i, h, j: (i * nq + j, h)),
                  pl.BlockSpec((t, w), lambda i, h, j: (i, nhb + h)),
                  pl.BlockSpec((t, w), lambda i, h, j: (i, 2 * nhb + h))],
        out_specs=pl.BlockSpec((tile, w), lambda i, h, j: (i * nq + j, h)),
        out_shape=jax.ShapeDtypeStruct((b * t, SB_W), BF16),
        compiler_params=_cp("parallel", "parallel", "parallel"),
        name="stickbreak",
    )(qkv, qkv, qkv)


def _cross_kernel(x_ref, g_ref, wq_ref, kv_ref, wo_ref, o_ref):
    x = x_ref[...]
    h = _rms(x, g_ref[...]).astype(BF16)
    q = _dot(h, wq_ref[...]).astype(BF16)
    xw = X_HEADS * HEAD_DIM
    outs = []
    for hd in range(X_HEADS):
        sl = slice(hd * HEAD_DIM, (hd + 1) * HEAD_DIM)
        s = _nt_dot(q[:, sl], kv_ref[:, sl]) * SCALE
        e = jnp.exp(s - jnp.max(s, axis=-1, keepdims=True))
        p = e / jnp.sum(e, axis=-1, keepdims=True)
        outs.append(_dot(p.astype(BF16), kv_ref[:, xw + hd * HEAD_DIM:xw + (hd + 1) * HEAD_DIM]).astype(BF16))
    o_ref[...] = x + _dot(jnp.concatenate(outs, axis=1), wo_ref[...])


def _cross(x, g, wq, kv, wo, b, t, tq=512):
    n, d = x.shape
    nq = t // tq
    mem_len = kv.shape[0] // b
    return pl.pallas_call(
        _cross_kernel,
        grid=(b, nq),
        in_specs=[pl.BlockSpec((tq, d), lambda i, j: (i * nq + j, 0)),
                  pl.BlockSpec((1, d), lambda i, j: (0, 0)),
                  pl.BlockSpec(wq.shape, lambda i, j: (0, 0)),
                  pl.BlockSpec((mem_len, kv.shape[1]), lambda i, j: (i, 0)),
                  pl.BlockSpec(wo.shape, lambda i, j: (0, 0))],
        out_specs=pl.BlockSpec((tq, d), lambda i, j: (i * nq + j, 0)),
        out_shape=jax.ShapeDtypeStruct((n, d), F32),
        compiler_params=_cp("parallel", "parallel"),
        name="cross_attention",
    )(x, g.reshape(1, d), wq, kv, wo)


def _to_row_tiles(ref, val):
    rows, width = val.shape
    c_per = width // LANES
    for c in range(c_per):
        ref[pl.ds(c, rows, stride=c_per), :] = val[:, c * LANES:(c + 1) * LANES]


def _from_row_tiles(ref, c_per):
    rows = ref.shape[0] // c_per
    return jnp.concatenate([ref[pl.ds(c, rows, stride=c_per), :] for c in range(c_per)], axis=1)


def _router_kernel(x_ref, g_ref, w_ref, b_ref, h_ref, r_ref):
    h = _rms(x_ref[...], g_ref[...])
    _to_row_tiles(h_ref, h)
    h_hi = h.astype(BF16)
    h_mid = (h - h_hi.astype(F32)).astype(BF16)
    two = _dot(h_hi, w_ref[...])
    logits = two[:, :LANES] + two[:, LANES:] + _dot(h_mid, w_ref[:, :LANES]) + b_ref[...]
    lane = lax.broadcasted_iota(jnp.int32, logits.shape, 1).astype(F32)
    big = float(LANES)
    is_g = lane < N_GROUPS
    lg = jnp.where(is_g, logits, NEG)
    mg = jnp.max(lg, axis=-1, keepdims=True)
    p_gtop = 1.0 / jnp.sum(jnp.where(is_g, jnp.exp(lg - mg), 0.0), axis=-1, keepdims=True)
    g_top = jnp.min(jnp.where(lg == mg, lane, big), axis=-1, keepdims=True)
    lo = N_GROUPS + EXPERTS_PER_GROUP * g_top
    in_grp = (lane >= lo) & (lane < lo + EXPERTS_PER_GROUP)
    le = jnp.where(in_grp, logits, NEG)
    ee = jnp.where(in_grp, jnp.exp(le - jnp.max(le, axis=-1, keepdims=True)), 0.0)
    p_in = ee / jnp.sum(ee, axis=-1, keepdims=True)
    p1 = jnp.where(in_grp, p_in, -1.0)
    m1 = jnp.max(p1, axis=-1, keepdims=True)
    i1 = jnp.min(jnp.where(p1 == m1, lane, big), axis=-1, keepdims=True)
    p2 = jnp.where(lane == i1, -1.0, p1)
    m2 = jnp.max(p2, axis=-1, keepdims=True)
    i2 = jnp.min(jnp.where(p2 == m2, lane, big), axis=-1, keepdims=True)
    tot = m1 + m2
    r_ref[...] = jnp.where(lane == 0, i1 - N_GROUPS,
                           jnp.where(lane == 1, i2 - N_GROUPS,
                                     jnp.where(lane == 2, m1 / tot * p_gtop,
                                               jnp.where(lane == 3, m2 / tot * p_gtop, 0.0))))


def _router(x, g, w_r, b_r, tm=512):
    n, d = x.shape
    return pl.pallas_call(
        _router_kernel,
        grid=(n // tm,),
        in_specs=[pl.BlockSpec((tm, d), lambda i: (i, 0)), pl.BlockSpec((1, d), lambda i: (0, 0)),
                  pl.BlockSpec((d, 2 * LANES), lambda i: (0, 0)), pl.BlockSpec((1, LANES), lambda i: (0, 0))],
        out_specs=[pl.BlockSpec((tm * (d // LANES), LANES), lambda i: (i, 0)),
                   pl.BlockSpec((tm, LANES), lambda i: (i, 0))],
        out_shape=[jax.ShapeDtypeStruct((n * (d // LANES), LANES), F32), jax.ShapeDtypeStruct((n, LANES), F32)],
        compiler_params=_cp("parallel"),
        name="moe_router",
    )(x, g.reshape(1, d), w_r, b_r)


ROW_CHUNKS = 16
WEIGHT_DMA_PRIORITY = 0
WEIGHT_SLOTS = 2
GATHER_SLOTS = 3


def _wait_rows(src_hbm, dst, sem):
    pltpu.make_async_copy(src_hbm.at[pl.ds(0, dst.shape[0]), :], dst, sem).wait()


def _row_copy(src_hbm, dst, sem, src_row, dst_row):
    c = ROW_CHUNKS
    return pltpu.make_async_copy(src_hbm.at[pl.ds(pl.multiple_of(src_row * c, c), c), :],
                                 dst.at[pl.ds(pl.multiple_of(dst_row * c, c), c), :], sem)


def _expert_kernel(idx_ref, ord_ref, oe_ref, cnt_ref, src_hbm, wg_hbm, wu_hbm, wd_hbm, o_ref,
                   xbuf, sems, wg_f, wu_f, wd_f, wsems, wg_s, wu_s, wd_s, *, tm):
    i = pl.program_id(0)
    nt = cnt_ref[0]
    n_ord = cnt_ref[1]
    live = i < nt
    slot = lax.rem(i, GATHER_SLOTS)
    lead = GATHER_SLOTS - 1

    def gather(tile, s):
        def issue(r2, c):
            for queue in range(2):
                r = 2 * r2 + queue
                _row_copy(src_hbm, xbuf.at[s], sems.at[s], idx_ref[tile * tm + r], r).start(priority=queue)
            return c
        lax.fori_loop(0, tm // 2, issue, 0, unroll=4)

    def weight_copies(k, s):
        e = oe_ref[k]
        return (pltpu.make_async_copy(wg_hbm.at[e], wg_f.at[s], wsems.at[s, 0]),
                pltpu.make_async_copy(wu_hbm.at[e], wu_f.at[s], wsems.at[s, 1]),
                pltpu.make_async_copy(wd_hbm.at[e], wd_f.at[s], wsems.at[s, 2]))

    ahead = WEIGHT_SLOTS - 1

    @pl.when(i == 0)
    def _():
        for j in range(lead):
            @pl.when(j < nt)
            def _():
                gather(j, j)
        for j in range(ahead):
            @pl.when(j < n_ord)
            def _():
                for cp in weight_copies(j, j):
                    cp.start(priority=WEIGHT_DMA_PRIORITY)

    @pl.when(i + lead < nt)
    def _():
        gather(i + lead, lax.rem(i + lead, GATHER_SLOTS))

    k = ord_ref[i]
    fresh = live & ((i == 0) | (k != ord_ref[jnp.maximum(i - 1, 0)]))
    ws = lax.rem(k, WEIGHT_SLOTS)

    @pl.when(fresh)
    def _():
        for cp in weight_copies(k, ws):
            cp.wait()

        @pl.when(k + ahead < n_ord)
        def _():
            for cp in weight_copies(k + ahead, lax.rem(k + ahead, WEIGHT_SLOTS)):
                cp.start(priority=WEIGHT_DMA_PRIORITY)

        wg_s[...] = wg_f[ws].astype(BF16)
        wu_s[...] = wu_f[ws].astype(BF16)
        wd_s[...] = wd_f[ws].astype(BF16)

    @pl.when(live)
    def _():
        _wait_rows(src_hbm, xbuf.at[slot], sems.at[slot])
        x = _from_row_tiles(xbuf.at[slot], ROW_CHUNKS).astype(BF16)
        a = _dot(x, wg_s[...])
        u = _dot(x, wu_s[...])
        act = a * (1.0 / (1.0 + jnp.exp(-a))) * u
        _to_row_tiles(o_ref, _dot(act.astype(BF16), wd_s[...]))

    @pl.when(jnp.logical_not(live))
    def _():
        o_ref[...] = jnp.zeros_like(o_ref)


def _experts(src, tok_of_row, w_g, w_u, w_d, tile_ord, ord_expert, counts, tm=EXPERT_TILE):
    c = ROW_CHUNKS
    n_tiles_max = tok_of_row.shape[0] // tm
    d = c * LANES
    f = w_g.shape[2]
    any_spec = pl.BlockSpec(memory_space=pl.ANY)
    return pl.pallas_call(
        functools.partial(_expert_kernel, tm=tm),
        grid_spec=pltpu.PrefetchScalarGridSpec(
            num_scalar_prefetch=4,
            grid=(n_tiles_max,),
            in_specs=[any_spec, any_spec, any_spec, any_spec],
            out_specs=pl.BlockSpec((tm * c, LANES), lambda i, *_: (i, 0)),
            scratch_shapes=[pltpu.VMEM((GATHER_SLOTS, tm * c, LANES), F32), pltpu.SemaphoreType.DMA((GATHER_SLOTS,)),
                            pltpu.VMEM((WEIGHT_SLOTS, d, f), F32), pltpu.VMEM((WEIGHT_SLOTS, d, f), F32),
                            pltpu.VMEM((WEIGHT_SLOTS, f, d), F32), pltpu.SemaphoreType.DMA((WEIGHT_SLOTS, 3)),
                            pltpu.VMEM((d, f), BF16), pltpu.VMEM((d, f), BF16), pltpu.VMEM((f, d), BF16)]),
        out_shape=jax.ShapeDtypeStruct((n_tiles_max * tm * c, LANES), F32),
        compiler_params=_cp("arbitrary"),
        name="moe_experts",
    )(tok_of_row, tile_ord, ord_expert, counts, src, w_g, w_u, w_d)


def _combine_kernel(p0_ref, p1_ref, ys_hbm, x_ref, r_ref, g_ref, o_ref, buf, sems, *, tm):
    i = pl.program_id(0)
    slot = lax.rem(i, 2)

    def fetch(tile, s):
        def issue(r, c):
            _row_copy(ys_hbm, buf.at[s, 0], sems.at[s], p0_ref[tile * tm + r], r).start()
            _row_copy(ys_hbm, buf.at[s, 1], sems.at[s], p1_ref[tile * tm + r], r).start(priority=1)
            return c
        lax.fori_loop(0, tm, issue, 0, unroll=8)

    @pl.when(i == 0)
    def _():
        fetch(0, 0)

    @pl.when(i + 1 < pl.num_programs(0))
    def _():
        fetch(i + 1, 1 - slot)

    _wait_rows(ys_hbm, buf.at[slot, 0], sems.at[slot])
    _wait_rows(ys_hbm, buf.at[slot, 1], sems.at[slot])
    route = r_ref[...]
    y = (route[:, 2:3] * _from_row_tiles(buf.at[slot, 0], ROW_CHUNKS)
         + route[:, 3:4] * _from_row_tiles(buf.at[slot, 1], ROW_CHUNKS))
    o_ref[...] = _rms(x_ref[...] + y, g_ref[...])


def _combine(ys, pos0, pos1, x, route, g, tm=256):
    n, d = x.shape
    return pl.pallas_call(
        functools.partial(_combine_kernel, tm=tm),
        grid_spec=pltpu.PrefetchScalarGridSpec(
            num_scalar_prefetch=2,
            grid=(n // tm,),
            in_specs=[pl.BlockSpec(memory_space=pl.ANY),
                      pl.BlockSpec((tm, d), lambda i, p0, p1: (i, 0)),
                      pl.BlockSpec((tm, LANES), lambda i, p0, p1: (i, 0)),
                      pl.BlockSpec((1, d), lambda i, p0, p1: (0, 0))],
            out_specs=pl.BlockSpec((tm, d), lambda i, p0, p1: (i, 0)),
            scratch_shapes=[pltpu.VMEM((2, 2, tm * ROW_CHUNKS, LANES), F32), pltpu.SemaphoreType.DMA((2,))]),
        out_shape=jax.ShapeDtypeStruct((n, d), F32),
        compiler_params=_cp("arbitrary"),
        name="moe_combine_norm",
    )(pos0, pos1, ys, x, route, g.reshape(1, d))


def _dispatch_plan(e1, e2, tm=EXPERT_TILE):
    n = e1.shape[0]
    n_tiles_max = (2 * n) // tm + N_EXPERTS
    rows = n_tiles_max * tm
    e_flat = jnp.concatenate([e1, e2])
    tok = jnp.concatenate([jnp.arange(n, dtype=jnp.int32)] * 2)
    onehot = (e_flat[:, None] == jnp.arange(N_EXPERTS, dtype=jnp.int32)[None, :]).astype(jnp.int32)
    csum = jnp.cumsum(onehot, axis=0)
    count = csum[-1]
    rank = jnp.sum((csum - 1) * onehot, axis=1)
    tiles_per = (count + tm - 1) // tm
    tile_end = jnp.cumsum(tiles_per)
    row_start = (tile_end - tiles_per) * tm
    dest = (row_start[e_flat] + rank).astype(jnp.int32)
    tok_of_row = jnp.zeros((rows,), jnp.int32).at[dest].set(tok)
    n_tiles = tile_end[-1].astype(jnp.int32)
    tile_id = jnp.minimum(jnp.arange(n_tiles_max, dtype=jnp.int32), n_tiles - 1)
    tile_expert = jnp.sum((tile_id[:, None] >= tile_end[None, :]).astype(jnp.int32), axis=1)
    used = jnp.cumsum((tiles_per > 0).astype(jnp.int32))
    tile_ord = (used[tile_expert] - 1).astype(jnp.int32)
    ordinals = jnp.arange(N_EXPERTS, dtype=jnp.int32)
    ord_expert = jnp.minimum(jnp.sum((used[None, :] <= ordinals[:, None]).astype(jnp.int32), axis=1),
                             N_EXPERTS - 1).astype(jnp.int32)
    counts = jnp.stack([n_tiles, used[-1]]).astype(jnp.int32)
    return tok_of_row, tile_ord, ord_expert, counts, dest[:n], dest[n:]


def _rope_tables(positions):
    half = ROT_DIM // 2
    inv = ROPE_THETA ** (-jnp.arange(half, dtype=F32) * (2.0 / ROT_DIM))
    ang = positions.astype(F32).reshape(-1, 1) * inv[None, :]
    cos, sin = jnp.cos(ang), jnp.sin(ang)
    n = ang.shape[0]
    c = jnp.concatenate([cos, cos, jnp.ones((n, LANES - ROT_DIM), F32)], axis=1)
    s1 = jnp.concatenate([-sin, jnp.zeros((n, LANES - half), F32)], axis=1)
    s2 = jnp.concatenate([jnp.zeros((n, half), F32), sin, jnp.zeros((n, LANES - ROT_DIM), F32)], axis=1)
    return c, s1, s2


def _layer(x, mem, positions, g_mix, w_in, cmp_pe_k, cmp_w1_k, cmp_w2_k, cmp_pe_v, cmp_w1_v, cmp_w2_v,
           w_br_nsa, w_br_sb, w_o, g_cross, g_mem, w_cq, w_ck, w_cv, w_co,
           g_moe, w_rg, b_rg, w_re, b_re, w_eg, w_eu, w_ed):
    b, t, d = x.shape
    n = b * t
    xf = x.reshape(n, d)

    g0 = NSA_W
    w_t = w_in.T
    w_nsa = _cast_rows(w_t, 0, g0)
    w_rest = _cast_rows(w_t, g0 + NSA_GATES, w_t.shape[0] - g0 - NSA_GATES)
    w_g = w_t[g0:g0 + NSA_GATES].reshape(NSA_KV_HEADS, 3 * NSA_GROUP, d)
    w_g = jnp.pad(w_g, ((0, 0), (0, LANES - 3 * NSA_GROUP), (0, 0))).reshape(NSA_KV_HEADS * LANES, d).astype(BF16)
    h = _rmsnorm(xf, g_mix, BF16)
    c, s1, s2 = _rope_tables(positions)
    row_spec = pl.BlockSpec((1024, LANES), lambda i, j: (i, 0))
    qkv_nsa = _mm(functools.partial(_mm_nsa_kernel, tn=256), h, w_nsa, NSA_W, 0, 1024, 256, BF16, w_t=True,
                  extra=(c, s1, s2), extra_specs=(row_spec, row_spec, row_spec), name="in_proj_nsa")
    qkv_sb = _mm(functools.partial(_mm_sb_kernel, q_tiles=SB_W // 512), h, w_rest, 3 * SB_W, 0,
                 1024, 512, BF16, w_t=True, name="in_proj_sb")
    gates = _mm(_mm_sigmoid_kernel, h, w_rest, 2 * d, 3 * SB_W // 512, 1024, 512, BF16, w_t=True,
                name="in_proj_gates")
    nsa_gates = _mm(_mm_sigmoid_kernel, h, w_g, NSA_KV_HEADS * LANES, 0, 1024, NSA_KV_HEADS * LANES, F32,
                    w_t=True, name="in_proj_nsa_gates")

    def chunked(col0):
        tok = qkv_nsa[:, col0:col0 + NSA_KV_W].reshape(b, t, NSA_KV_HEADS, HEAD_DIM).transpose(0, 2, 1, 3)
        return tok.reshape(b, NSA_KV_HEADS, t // CMP_STRIDE, CMP_STRIDE * HEAD_DIM)

    kc = _compress(chunked(NSA_Q_W), cmp_pe_k, cmp_w1_k, cmp_w2_k)
    vc = _compress(chunked(NSA_Q_W + NSA_KV_W), cmp_pe_v, cmp_w1_v, cmp_w2_v)
    o_c, sel = _nsa_cmp(qkv_nsa, kc, vc, b, t)
    o_nsa = _nsa_sel_win(qkv_nsa, sel, o_c, nsa_gates, b, t)
    o_sb = _stickbreak(qkv_sb, b, t)
    merged = _merge(o_nsa, o_sb, w_br_nsa.astype(BF16), w_br_sb.astype(BF16), gates)
    x1 = _mm(_mm_resid_kernel, merged, w_o.astype(BF16), d, 0, 1024, 512, F32,
             extra=(xf,), extra_specs=(pl.BlockSpec((1024, 512), lambda i, j: (i, j)),), name="out_proj")

    w_kv = jnp.concatenate([w_ck, w_cv], axis=1).astype(BF16)
    kv = _norm_mm(mem.reshape(-1, d), g_mem, w_kv, BF16)
    x2 = _cross(x1, g_cross, w_cq.astype(BF16), kv, w_co.astype(BF16), b, t)

    w_r = jnp.pad(jnp.concatenate([w_rg, w_re], axis=1), ((0, 0), (0, LANES - N_GROUPS - N_EXPERTS)))
    b_r = jnp.pad(jnp.concatenate([b_rg, b_re]), (0, LANES - N_GROUPS - N_EXPERTS)).reshape(1, LANES)
    w_r_hi = w_r.astype(BF16)
    w_r2 = jnp.concatenate([w_r_hi, (w_r - w_r_hi.astype(F32)).astype(BF16)], axis=1)
    hm, route = _router(x2, g_moe, w_r2, b_r)
    e1 = route[:, 0].astype(jnp.int32)
    e2 = route[:, 1].astype(jnp.int32)
    tok_of_row, tile_ord, ord_expert, counts, pos0, pos1 = _dispatch_plan(e1, e2)
    f = w_eg.shape[-1]
    ys = _experts(hm, tok_of_row, w_eg.reshape(N_EXPERTS, d, f), w_eu.reshape(N_EXPERTS, d, f),
                  w_ed.reshape(N_EXPERTS, f, d), tile_ord, ord_expert, counts)
    return ys, pos0, pos1, x2, route


def kernel(x, mem, positions, g_mix, w_in, cmp_pe_k, cmp_w1_k, cmp_w2_k, cmp_pe_v, cmp_w1_v, cmp_w2_v, w_br_nsa, w_br_sb, w_o, g_cross, g_mem, w_cq, w_ck, w_cv, w_co, g_moe, w_rg, b_rg, w_re, b_re, w_eg, w_eu, w_ed, g_final):
    b, t, d = x.shape
    assert w_in.shape[0] == 1, "one layer"
    ys, pos0, pos1, x2, route = _layer(
        x, mem, positions, g_mix[0], w_in[0], cmp_pe_k[0], cmp_w1_k[0], cmp_w2_k[0], cmp_pe_v[0], cmp_w1_v[0],
        cmp_w2_v[0], w_br_nsa[0], w_br_sb[0], w_o[0], g_cross[0], g_mem[0], w_cq[0], w_ck[0], w_cv[0], w_co[0],
        g_moe[0], w_rg[0], b_rg[0], w_re[0], b_re[0], w_eg[0], w_eu[0], w_ed[0])
    return _combine(ys, pos0, pos1, x2, route, g_final).reshape(b, t, d)
```

```python
import functools

import jax
import jax.numpy as jnp
from jax import lax
from jax.experimental import pallas as pl
from jax.experimental.pallas import tpu as pltpu

F32 = jnp.float32
BF16 = jnp.bfloat16

HEAD_DIM = 128
NSA_HEADS = 8
NSA_KV_HEADS = 2
NSA_GROUP = NSA_HEADS // NSA_KV_HEADS
SB_HEADS = 8
ROT_DIM = HEAD_DIM // 4
ROPE_THETA = 500000.0
CMP_LEN = 32
CMP_STRIDE = 16
SEL_LEN = 64
SEL_SHIFT = 6
SEL_TOPN = 16
WINDOW = 512
FORCE_BONUS = 1000.0
X_HEADS = 4
N_GROUPS = 4
EXPERTS_PER_GROUP = 8
N_EXPERTS = N_GROUPS * EXPERTS_PER_GROUP
EPS = 1e-6
SCALE = HEAD_DIM ** -0.5
NEG = -1e30
F32_EXP_UNDERFLOW = -104.0

NSA_Q_W = NSA_HEADS * HEAD_DIM
NSA_KV_W = NSA_KV_HEADS * HEAD_DIM
NSA_W = NSA_Q_W + 6 * NSA_KV_W
NSA_GATES = 3 * NSA_HEADS
SB_W = SB_HEADS * HEAD_DIM
LANES = 128
VMEM_LIMIT = 56 * 1024 * 1024

EXPERT_TILE = 256
WIDE_TN = 1024


def _cp(*sem, vmem=VMEM_LIMIT):
    return pltpu.CompilerParams(dimension_semantics=sem, vmem_limit_bytes=vmem)


def _nt_dot(a, b):
    return lax.dot_general(a, b, (((1,), (1,)), ((), ())), preferred_element_type=F32)


def _dot(a, b):
    return jnp.dot(a, b, preferred_element_type=F32)


def _split3(x):
    hi = x.astype(BF16)
    r1 = x - hi.astype(F32)
    mid = r1.astype(BF16)
    lo = (r1 - mid.astype(F32)).astype(BF16)
    return hi, mid, lo


def _rms(x, g):
    return x * lax.rsqrt(jnp.mean(x * x, axis=-1, keepdims=True) + EPS) * g


def _rmsnorm_kernel(x_ref, g_ref, o_ref):
    o_ref[...] = _rms(x_ref[...].astype(F32), g_ref[...]).astype(o_ref.dtype)


def _rmsnorm(x, g, out_dtype, tm=512):
    n, d = x.shape
    return pl.pallas_call(
        _rmsnorm_kernel,
        grid=(n // tm,),
        in_specs=[pl.BlockSpec((tm, d), lambda i: (i, 0)), pl.BlockSpec((1, d), lambda i: (0, 0))],
        out_specs=pl.BlockSpec((tm, d), lambda i: (i, 0)),
        out_shape=jax.ShapeDtypeStruct((n, d), out_dtype),
        compiler_params=_cp("parallel"),
        name="rmsnorm",
    )(x, g.reshape(1, d))


def _cast_rows_kernel(w_ref, o_ref):
    o_ref[...] = w_ref[...].astype(o_ref.dtype)


def _cast_rows(w_t, row0, n_rows, tn=512):
    k = w_t.shape[1]
    assert n_rows % tn == 0 and row0 % 8 == 0
    return pl.pallas_call(
        _cast_rows_kernel,
        grid=(n_rows // tn,),
        in_specs=[pl.BlockSpec((pl.Element(tn), pl.Element(k)), lambda j: ((row0 // 8 + j * (tn // 8)) * 8, 0))],
        out_specs=pl.BlockSpec((tn, k), lambda j: (j, 0)),
        out_shape=jax.ShapeDtypeStruct((n_rows, k), BF16),
        compiler_params=_cp("parallel"),
        name="cast_rows",
    )(w_t)


def _rope_half(x, c, s1, s2):
    return x * c + pltpu.roll(x, LANES - ROT_DIM // 2, 1) * s1 + pltpu.roll(x, ROT_DIM // 2, 1) * s2


def _mm_nsa_kernel(a_ref, w_ref, c_ref, s1_ref, s2_ref, o_ref, *, tn):
    j = pl.program_id(1)
    acc = _nt_dot(a_ref[...], w_ref[...])
    is_rope = (j < 5) | (j == 6) | (j == 8)

    @pl.when(is_rope)
    def _():
        qs = jnp.where(j < NSA_Q_W // tn, SCALE, 1.0)
        c, s1, s2 = c_ref[...] * qs, s1_ref[...] * qs, s2_ref[...] * qs
        for hh in range(tn // LANES):
            sl = slice(hh * LANES, (hh + 1) * LANES)
            o_ref[:, sl] = _rope_half(acc[:, sl], c, s1, s2).astype(o_ref.dtype)

    @pl.when(jnp.logical_not(is_rope))
    def _():
        o_ref[...] = acc.astype(o_ref.dtype)


def _mm_sb_kernel(a_ref, w_ref, o_ref, *, q_tiles):
    acc = _nt_dot(a_ref[...], w_ref[...])
    is_q = pl.program_id(1) < q_tiles
    o_ref[...] = (acc * jnp.where(is_q, SCALE, 1.0)).astype(o_ref.dtype)


def _mm_sigmoid_kernel(a_ref, w_ref, o_ref):
    o_ref[...] = jax.nn.sigmoid(_nt_dot(a_ref[...], w_ref[...])).astype(o_ref.dtype)


def _mm_resid_kernel(a_ref, w_ref, x_ref, o_ref):
    o_ref[...] = x_ref[...] + _dot(a_ref[...], w_ref[...])


def _mm(kern, a, w, n_cols, col_block0, tm, tn, out_dtype, extra=(), extra_specs=(), name="mm", w_t=False):
    m, k = a.shape
    w_spec = (pl.BlockSpec((tn, k), lambda i, j: (col_block0 + j, 0)) if w_t
              else pl.BlockSpec((k, tn), lambda i, j: (0, col_block0 + j)))
    return pl.pallas_call(
        kern,
        grid=(m // tm, n_cols // tn),
        in_specs=[pl.BlockSpec((tm, k), lambda i, j: (i, 0)), w_spec, *extra_specs],
        out_specs=pl.BlockSpec((tm, tn), lambda i, j: (i, j)),
        out_shape=jax.ShapeDtypeStruct((m, n_cols), out_dtype),
        compiler_params=_cp("parallel", "arbitrary"),
        name=name,
    )(a, w, *extra)


def _merge_kernel(a1_ref, a2_ref, w1_ref, w2_ref, ga_ref, gb_ref, o_ref):
    ya = _dot(a1_ref[...], w1_ref[...])
    yb = _dot(a2_ref[...], w2_ref[...])
    o_ref[...] = (ga_ref[...].astype(F32) * ya + gb_ref[...].astype(F32) * yb).astype(o_ref.dtype)


def _merge(o_nsa, o_sb, w1, w2, gates, tm=1024, tn=WIDE_TN):
    n, k = o_nsa.shape
    d = w1.shape[1]
    nb = d // tn
    return pl.pallas_call(
        _merge_kernel,
        grid=(n // tm, nb),
        in_specs=[pl.BlockSpec((tm, k), lambda i, j: (i, 0)), pl.BlockSpec((tm, k), lambda i, j: (i, 0)),
                  pl.BlockSpec((k, tn), lambda i, j: (0, j)), pl.BlockSpec((k, tn), lambda i, j: (0, j)),
                  pl.BlockSpec((tm, tn), lambda i, j: (i, j)), pl.BlockSpec((tm, tn), lambda i, j: (i, nb + j))],
        out_specs=pl.BlockSpec((tm, tn), lambda i, j: (i, j)),
        out_shape=jax.ShapeDtypeStruct((n, d), BF16),
        compiler_params=_cp("parallel", "arbitrary"),
        name="merge",
    )(o_nsa, o_sb, w1, w2, gates, gates)


def _norm_mm_kernel(x_ref, g_ref, w_ref, o_ref):
    h = _rms(x_ref[...].astype(F32), g_ref[...]).astype(BF16)
    o_ref[...] = _dot(h, w_ref[...]).astype(o_ref.dtype)


def _norm_mm(x, g, w, out_dtype, tm=256):
    n, d = x.shape
    nc = w.shape[1]
    return pl.pallas_call(
        _norm_mm_kernel,
        grid=(n // tm,),
        in_specs=[pl.BlockSpec((tm, d), lambda i: (i, 0)), pl.BlockSpec((1, d), lambda i: (0, 0)),
                  pl.BlockSpec((d, nc), lambda i: (0, 0))],
        out_specs=pl.BlockSpec((tm, nc), lambda i: (i, 0)),
        out_shape=jax.ShapeDtypeStruct((n, nc), out_dtype),
        compiler_params=_cp("parallel"),
        name="norm_mm",
    )(x, g.reshape(1, d), w)


def _compress_kernel(x_ref, pe_ref, w1_ref, w2_ref, o_ref):
    x = x_ref[...].astype(F32)
    half = x.shape[1]
    xa = (x + pe_ref[0:1, :]).astype(BF16)
    xb = (x + pe_ref[1:2, :]).astype(BF16)
    y1 = _dot(xa, w1_ref[0:half, :].astype(BF16))
    y2 = _dot(xb, w1_ref[half:2 * half, :].astype(BF16))
    pre = y1 + pltpu.roll(y2, x.shape[0] - 1, 0)
    act = pre * (1.0 / (1.0 + jnp.exp(-pre)))
    o_ref[...] = _dot(act.astype(BF16), w2_ref[...].astype(BF16)).astype(o_ref.dtype)


def _compress(tok, pe, w1, w2):
    b, hkv, nchunk, width = tok.shape
    return pl.pallas_call(
        _compress_kernel,
        grid=(b, hkv),
        in_specs=[pl.BlockSpec((None, None, nchunk, width), lambda i, j: (i, j, 0, 0)),
                  pl.BlockSpec((2, width), lambda i, j: (0, 0)),
                  pl.BlockSpec((2 * width, HEAD_DIM), lambda i, j: (0, 0)),
                  pl.BlockSpec((HEAD_DIM, HEAD_DIM), lambda i, j: (0, 0))],
        out_specs=pl.BlockSpec((None, None, nchunk, HEAD_DIM), lambda i, j: (i, j, 0, 0)),
        out_shape=jax.ShapeDtypeStruct((b, hkv, nchunk, HEAD_DIM), BF16),
        compiler_params=_cp("parallel", "parallel"),
        name="nsa_compress",
    )(tok, pe.reshape(2, width), w1, w2)


def _nsa_cmp_kernel(q_ref, kc_ref, vc_ref, oc_ref, sel_ref, *, tq, nc, ns):
    q0 = pl.program_id(2) * tq
    kc = kc_ref[...]
    vc = vc_ref[...]
    t_row = q0 + lax.broadcasted_iota(jnp.int32, (tq, nc), 0)
    c_idx = lax.broadcasted_iota(jnp.int32, (tq, nc), 1)
    vis = (c_idx * CMP_STRIDE + (CMP_LEN - 1) <= t_row) & (c_idx < nc - 1)
    heads = [slice(h * HEAD_DIM, (h + 1) * HEAD_DIM) for h in range(NSA_GROUP)]
    scores = [_nt_dot(q_ref[:, sl], kc) for sl in heads]
    probs = []
    for s in scores:
        s = jnp.where(vis, s, NEG)
        e = jnp.where(vis, jnp.exp(s - jnp.max(s, axis=-1, keepdims=True)), 0.0)
        den = jnp.sum(e, axis=-1, keepdims=True)
        probs.append(e / jnp.where(den > 0, den, 1.0))
    for sl, p in zip(heads, probs):
        oc_ref[:, sl] = _dot(p.astype(BF16), vc).astype(oc_ref.dtype)
    imp = sum(probs[1:], probs[0])
    jj = lax.broadcasted_iota(jnp.int32, (ns, nc), 0)
    cc = lax.broadcasted_iota(jnp.int32, (ns, nc), 1) * CMP_STRIDE
    ov = jnp.where((cc < (jj + 1) * SEL_LEN) & (cc + CMP_LEN > jj * SEL_LEN), 1.0, 0.0).astype(BF16)
    imp_t = sum(_nt_dot(ov, part) for part in _split3(imp))
    jb = lax.broadcasted_iota(jnp.int32, (ns, tq), 0)
    tt = q0 + lax.broadcasted_iota(jnp.int32, (ns, tq), 1)
    cur = lax.shift_right_logical(tt, jnp.full_like(tt, SEL_SHIFT))
    forced = (jb == 0) | (jb == cur) | (jb == cur - 1)
    valid = jb * SEL_LEN <= tt
    score = jnp.where(valid, imp_t + jnp.where(forced, FORCE_BONUS, 0.0), NEG)
    rank = jnp.zeros((ns, tq), F32)
    for i in range(ns):
        row = score[i:i + 1, :]
        rank = rank + jnp.where(row > score, 1.0, jnp.where((row == score) & (jb > i), 1.0, 0.0))
    sel_t = jnp.where(valid & (rank < float(min(SEL_TOPN, ns))), 1.0, 0.0).astype(BF16)
    eye = jnp.where(lax.broadcasted_iota(jnp.int32, (tq, tq), 0) == lax.broadcasted_iota(jnp.int32, (tq, tq), 1),
                    1.0, 0.0).astype(BF16)
    sel_ref[...] = _nt_dot(eye, sel_t)


def _nsa_cmp(qkv, kc, vc, b, t, tq=512):
    nc = kc.shape[2]
    ns = t // SEL_LEN
    nq = t // tq
    gw = NSA_GROUP * HEAD_DIM
    return pl.pallas_call(
        functools.partial(_nsa_cmp_kernel, tq=tq, nc=nc, ns=ns),
        grid=(b, NSA_KV_HEADS, nq),
        in_specs=[pl.BlockSpec((tq, gw), lambda i, k, j: (i * nq + j, k)),
                  pl.BlockSpec((None, None, nc, HEAD_DIM), lambda i, k, j: (i, k, 0, 0)),
                  pl.BlockSpec((None, None, nc, HEAD_DIM), lambda i, k, j: (i, k, 0, 0))],
        out_specs=[pl.BlockSpec((tq, gw), lambda i, k, j: (i * nq + j, k)),
                   pl.BlockSpec((None, None, tq, ns), lambda i, k, j: (i, k, j, 0))],
        out_shape=[jax.ShapeDtypeStruct((b * t, NSA_Q_W), BF16),
                   jax.ShapeDtypeStruct((b, NSA_KV_HEADS, t, ns), F32)],
        compiler_params=_cp("parallel", "parallel", "parallel"),
        name="nsa_cmp_select",
    )(qkv, kc, vc)


def _flash_update(s, v, bias, state, g, tq):
    m, l, acc = state
    tk = v.shape[0]
    if bias is not None:
        s = (s.reshape(g, tq, tk) + bias[None]).reshape(g * tq, tk)
    m_new = jnp.maximum(m, jnp.max(s, axis=-1, keepdims=True))
    alpha = jnp.exp(m - m_new)
    p = jnp.exp((s - m_new).astype(BF16))
    pv = _dot(p, jnp.concatenate([v, jnp.ones_like(v)], axis=1))
    return m_new, alpha * l + pv[:, HEAD_DIM:], alpha * acc + pv[:, :HEAD_DIM]


def _nsa_sel_win_kernel(q_ref, ks_ref, vs_ref, kw_ref, vw_ref, sel_ref, oc_ref, g_ref, o_ref, *, tq, tk, stack, ns):
    q0 = pl.program_id(1) * tq
    g = NSA_GROUP
    nkv = NSA_KV_HEADS
    per_tile = tk // SEL_LEN
    units = [(kv, h0) for kv in range(nkv) for h0 in range(0, g, stack)]
    kv_lanes = [slice(kv * HEAD_DIM, (kv + 1) * HEAD_DIM) for kv in range(nkv)]

    def q_lanes(kv, h):
        return slice((kv * g + h) * HEAD_DIM, (kv * g + h + 1) * HEAD_DIM)

    qs = [jnp.concatenate([q_ref[:, q_lanes(kv, h0 + j)] for j in range(stack)], axis=0) for kv, h0 in units]
    sel_bias = [((sel_ref[kv] - 1.0) * (-NEG)).astype(BF16) for kv in range(nkv)]
    ej = lax.broadcasted_iota(jnp.int32, (ns, tk), 0)
    eb = lax.broadcasted_iota(jnp.int32, (ns, tk), 1)
    eb = lax.shift_right_logical(eb, jnp.full_like(eb, SEL_SHIFT))

    def init():
        return tuple((jnp.full((stack * tq, 1), NEG, F32), jnp.zeros((stack * tq, HEAD_DIM), F32),
                      jnp.zeros((stack * tq, HEAD_DIM), F32)) for _ in units)

    def flash_all(k_ref, v_ref, k0, width, biases, states):
        scores = [_nt_dot(qs[u], k_ref[pl.ds(k0, width), kv_lanes[kv]]) for u, (kv, _) in enumerate(units)]
        return tuple(_flash_update(scores[u], v_ref[pl.ds(k0, width), kv_lanes[kv]], biases[kv], states[u], stack, tq)
                     for u, (kv, _) in enumerate(units))

    def sel_step(kt, states, causal):
        k0 = pl.multiple_of(kt * tk, tk)
        expand = jnp.where(ej == eb + kt * per_tile, 1.0, 0.0).astype(BF16)
        biases = [_dot(sel_bias[kv], expand) for kv in range(nkv)]
        if causal is not None:
            biases = [bias + causal for bias in biases]
        return flash_all(ks_ref, vs_ref, k0, tk, biases, states)

    kt_d = q0 // tk
    t_row = q0 + lax.broadcasted_iota(jnp.int32, (tq, tk), 0)
    col = kt_d * tk + lax.broadcasted_iota(jnp.int32, (tq, tk), 1)
    states = sel_step(kt_d, init(), jnp.where(col <= t_row, 0.0, NEG))
    states = lax.fori_loop(0, kt_d, lambda kt, st: sel_step(kt, st, None), states)
    o_s = [acc * (1.0 / l) for (_, l, acc) in states]

    def win_step(k0, states, bias):
        return flash_all(kw_ref, vw_ref, k0, tq, [bias] * nkv, states)

    ri = lax.broadcasted_iota(jnp.int32, (tq, tq), 0)
    ci = lax.broadcasted_iota(jnp.int32, (tq, tq), 1)
    states = win_step(pl.multiple_of(q0, tq), init(), jnp.where(ci <= ri, 0.0, NEG))
    n_full = jnp.minimum(WINDOW // tq - 1, q0 // tq)
    states = lax.fori_loop(0, n_full, lambda i, st: win_step(pl.multiple_of(q0 - (i + 1) * tq, tq), st, None),
                           states)
    edge_bias = jnp.where(ci > ri, 0.0, NEG) + jnp.where(q0 >= WINDOW, 0.0, NEG)
    states = win_step(pl.multiple_of(jnp.maximum(q0 - WINDOW, 0), tq), states, edge_bias)
    o_w = [acc * (1.0 / l) for (_, l, acc) in states]

    for u, (kv, h0) in enumerate(units):
        gates = g_ref[:, kv * LANES:(kv + 1) * LANES]
        for j in range(stack):
            h = h0 + j
            sl = q_lanes(kv, h)
            rows = slice(j * tq, (j + 1) * tq)
            out = (gates[:, 3 * h:3 * h + 1] * oc_ref[:, sl].astype(F32)
                   + gates[:, 3 * h + 1:3 * h + 2] * o_s[u][rows] + gates[:, 3 * h + 2:3 * h + 3] * o_w[u][rows])
            o_ref[:, sl] = out.astype(o_ref.dtype)


def _nsa_sel_win(qkv, sel, oc, gates, b, t, tq=512, tk=512, stack=4):
    nq = t // tq
    ns = t // SEL_LEN
    kv_block = NSA_Q_W // NSA_KV_W

    def kv_spec(which):
        return pl.BlockSpec((t, NSA_KV_W), lambda i, j: (i, kv_block + which))

    return pl.pallas_call(
        functools.partial(_nsa_sel_win_kernel, tq=tq, tk=tk, stack=stack, ns=ns),
        grid=(b, nq),
        in_specs=[pl.BlockSpec((tq, NSA_Q_W), lambda i, j: (i * nq + j, 0)),
                  kv_spec(2), kv_spec(3), kv_spec(4), kv_spec(5),
                  pl.BlockSpec((None, NSA_KV_HEADS, tq, ns), lambda i, j: (i, 0, j, 0)),
                  pl.BlockSpec((tq, NSA_Q_W), lambda i, j: (i * nq + j, 0)),
                  pl.BlockSpec((tq, NSA_KV_HEADS * LANES), lambda i, j: (i * nq + j, 0))],
        out_specs=pl.BlockSpec((tq, NSA_Q_W), lambda i, j: (i * nq + j, 0)),
        out_shape=jax.ShapeDtypeStruct((b * t, NSA_Q_W), BF16),
        compiler_params=_cp("parallel", "parallel"),
        name="nsa_select_window",
    )(qkv, qkv, qkv, qkv, qkv, sel, oc, gates)


def _stickbreak_kernel(q_ref, k_ref, v_ref, o_ref, *, tile, nh):
    q0 = pl.program_id(2) * tile
    ri = lax.broadcasted_iota(jnp.int32, (tile, tile), 0)
    ci = lax.broadcasted_iota(jnp.int32, (tile, tile), 1)
    strict = ci < ri
    tri = jnp.where(ri > ci, 1.0, 0.0).astype(BF16)
    lanes = [slice(h * HEAD_DIM, (h + 1) * HEAD_DIM) for h in range(nh)]

    def step(k0, diag, carry):
        zs = [_nt_dot(q_ref[:, lanes[h]], k_ref[pl.ds(k0, tile), lanes[h]]) for h in range(nh)]
        log_bs, sufs, sums = [], [], []
        for z in zs:
            log_b = jnp.minimum(z, 0.0) - jnp.log(1.0 + jnp.exp(-jnp.abs(z)))
            log_1m = log_b - z
            if diag:
                log_1m = jnp.where(strict, log_1m, 0.0)
            hi = log_1m.astype(BF16)
            lo = (log_1m - hi.astype(F32)).astype(BF16)
            log_bs.append(log_b)
            sufs.append(_dot(hi, tri) + _dot(lo, tri))
            sums.append(jnp.sum(log_1m, axis=-1, keepdims=True))
        out = []
        for h in range(nh):
            later, acc = carry[2 * h], carry[2 * h + 1]
            a = jnp.exp(log_bs[h] + (sufs[h] + later))
            if diag:
                a = jnp.where(strict, a, 0.0)
            out.extend((later + sums[h], acc + _dot(a.astype(BF16), v_ref[pl.ds(k0, tile), lanes[h]])))
        return tuple(out)

    zero = (jnp.zeros((tile, 1), F32), jnp.zeros((tile, HEAD_DIM), F32)) * nh
    carry = step(pl.multiple_of(q0, tile), True, zero)

    def live(state):
        i, c = state
        top = c[0]
        for h in range(1, nh):
            top = jnp.maximum(top, c[2 * h])
        return (i < q0 // tile) & (jnp.max(top) > F32_EXP_UNDERFLOW)

    def advance(state):
        i, c = state
        return i + 1, step(pl.multiple_of(q0 - (i + 1) * tile, tile), False, c)

    _, carry = lax.while_loop(live, advance, (jnp.int32(0), carry))
    for h in range(nh):
        o_ref[:, lanes[h]] = carry[2 * h + 1].astype(o_ref.dtype)


def _stickbreak(qkv, b, t, tile=256, nh=4):
    nq = t // tile
    w = nh * HEAD_DIM
    nhb = SB_HEADS // nh
    return pl.pallas_call(
        functools.partial(_stickbreak_kernel, tile=tile, nh=nh),
        grid=(b, nhb, nq),
        in_specs=[pl.BlockSpec((tile, w), lambda i, h, j: (i * nq + j, h)),
                  pl.BlockSpec((t, w), lambda i, h, j: (i, nhb + h)),
                  pl.BlockSpec((t, w), lambda i, h, j: (i, 2 * nhb + h))],
        out_specs=pl.BlockSpec((tile, w), lambda i, h, j: (i * nq + j, h)),
        out_shape=jax.ShapeDtypeStruct((b * t, SB_W), BF16),
        compiler_params=_cp("parallel", "parallel", "parallel"),
        name="stickbreak",
    )(qkv, qkv, qkv)


def _cross_kernel(x_ref, g_ref, wq_ref, kv_ref, wo_ref, o_ref):
    x = x_ref[...]
    h = _rms(x, g_ref[...]).astype(BF16)
    q = _dot(h, wq_ref[...]).astype(BF16)
    xw = X_HEADS * HEAD_DIM
    outs = []
    for hd in range(X_HEADS):
        sl = slice(hd * HEAD_DIM, (hd + 1) * HEAD_DIM)
        s = _nt_dot(q[:, sl], kv_ref[:, sl]) * SCALE
        e = jnp.exp(s - jnp.max(s, axis=-1, keepdims=True))
        p = e / jnp.sum(e, axis=-1, keepdims=True)
        outs.append(_dot(p.astype(BF16), kv_ref[:, xw + hd * HEAD_DIM:xw + (hd + 1) * HEAD_DIM]).astype(BF16))
    o_ref[...] = x + _dot(jnp.concatenate(outs, axis=1), wo_ref[...])


def _cross(x, g, wq, kv, wo, b, t, tq=512):
    n, d = x.shape
    nq = t // tq
    mem_len = kv.shape[0] // b
    return pl.pallas_call(
        _cross_kernel,
        grid=(b, nq),
        in_specs=[pl.BlockSpec((tq, d), lambda i, j: (i * nq + j, 0)),
                  pl.BlockSpec((1, d), lambda i, j: (0, 0)),
                  pl.BlockSpec(wq.shape, lambda i, j: (0, 0)),
                  pl.BlockSpec((mem_len, kv.shape[1]), lambda i, j: (i, 0)),
                  pl.BlockSpec(wo.shape, lambda i, j: (0, 0))],
        out_specs=pl.BlockSpec((tq, d), lambda i, j: (i * nq + j, 0)),
        out_shape=jax.ShapeDtypeStruct((n, d), F32),
        compiler_params=_cp("parallel", "parallel"),
        name="cross_attention",
    )(x, g.reshape(1, d), wq, kv, wo)


def _to_row_tiles(ref, val):
    rows, width = val.shape
    c_per = width // LANES
    for c in range(c_per):
        ref[pl.ds(c, rows, stride=c_per), :] = val[:, c * LANES:(c + 1) * LANES]


def _from_row_tiles(ref, c_per):
    rows = ref.shape[0] // c_per
    return jnp.concatenate([ref[pl.ds(c, rows, stride=c_per), :] for c in range(c_per)], axis=1)


def _router_kernel(x_ref, g_ref, w_ref, b_ref, h_ref, r_ref):
    h = _rms(x_ref[...], g_ref[...])
    _to_row_tiles(h_ref, h)
    h_hi = h.astype(BF16)
    h_mid = (h - h_hi.astype(F32)).astype(BF16)
    two = _dot(h_hi, w_ref[...])
    logits = two[:, :LANES] + two[:, LANES:] + _dot(h_mid, w_ref[:, :LANES]) + b_ref[...]
    lane = lax.broadcasted_iota(jnp.int32, logits.shape, 1).astype(F32)
    big = float(LANES)
    is_g = lane < N_GROUPS
    lg = jnp.where(is_g, logits, NEG)
    mg = jnp.max(lg, axis=-1, keepdims=True)
    p_gtop = 1.0 / jnp.sum(jnp.where(is_g, jnp.exp(lg - mg), 0.0), axis=-1, keepdims=True)
    g_top = jnp.min(jnp.where(lg == mg, lane, big), axis=-1, keepdims=True)
    lo = N_GROUPS + EXPERTS_PER_GROUP * g_top
    in_grp = (lane >= lo) & (lane < lo + EXPERTS_PER_GROUP)
    le = jnp.where(in_grp, logits, NEG)
    ee = jnp.where(in_grp, jnp.exp(le - jnp.max(le, axis=-1, keepdims=True)), 0.0)
    p_in = ee / jnp.sum(ee, axis=-1, keepdims=True)
    p1 = jnp.where(in_grp, p_in, -1.0)
    m1 = jnp.max(p1, axis=-1, keepdims=True)
    i1 = jnp.min(jnp.where(p1 == m1, lane, big), axis=-1, keepdims=True)
    p2 = jnp.where(lane == i1, -1.0, p1)
    m2 = jnp.max(p2, axis=-1, keepdims=True)
    i2 = jnp.min(jnp.where(p2 == m2, lane, big), axis=-1, keepdims=True)
    tot = m1 + m2
    r_ref[...] = jnp.where(lane == 0, i1 - N_GROUPS,
                           jnp.where(lane == 1, i2 - N_GROUPS,
                                     jnp.where(lane == 2, m1 / tot * p_gtop,
                                               jnp.where(lane == 3, m2 / tot * p_gtop, 0.0))))


def _router(x, g, w_r, b_r, tm=512):
    n, d = x.shape
    return pl.pallas_call(
        _router_kernel,
        grid=(n // tm,),
        in_specs=[pl.BlockSpec((tm, d), lambda i: (i, 0)), pl.BlockSpec((1, d), lambda i: (0, 0)),
                  pl.BlockSpec((d, 2 * LANES), lambda i: (0, 0)), pl.BlockSpec((1, LANES), lambda i: (0, 0))],
        out_specs=[pl.BlockSpec((tm * (d // LANES), LANES), lambda i: (i, 0)),
                   pl.BlockSpec((tm, LANES), lambda i: (i, 0))],
        out_shape=[jax.ShapeDtypeStruct((n * (d // LANES), LANES), F32), jax.ShapeDtypeStruct((n, LANES), F32)],
        compiler_params=_cp("parallel"),
        name="moe_router",
    )(x, g.reshape(1, d), w_r, b_r)


ROW_CHUNKS = 16
WEIGHT_DMA_PRIORITY = 0
WEIGHT_SLOTS = 2


def _wait_rows(src_hbm, dst, sem):
    pltpu.make_async_copy(src_hbm.at[pl.ds(0, dst.shape[0]), :], dst, sem).wait()


def _row_copy(src_hbm, dst, sem, src_row, dst_row):
    c = ROW_CHUNKS
    return pltpu.make_async_copy(src_hbm.at[pl.ds(pl.multiple_of(src_row * c, c), c), :],
                                 dst.at[pl.ds(pl.multiple_of(dst_row * c, c), c), :], sem)


def _expert_kernel(idx_ref, ord_ref, oe_ref, cnt_ref, src_hbm, wg_hbm, wu_hbm, wd_hbm, o_ref,
                   xbuf, sems, wg_f, wu_f, wd_f, wsems, wg_s, wu_s, wd_s, *, tm):
    i = pl.program_id(0)
    nt = cnt_ref[0]
    n_ord = cnt_ref[1]
    live = i < nt
    slot = lax.rem(i, 2)

    def gather(tile, s):
        def issue(r2, c):
            for queue in range(2):
                r = 2 * r2 + queue
                _row_copy(src_hbm, xbuf.at[s], sems.at[s], idx_ref[tile * tm + r], r).start(priority=queue)
            return c
        lax.fori_loop(0, tm // 2, issue, 0, unroll=4)

    def weight_copies(k, s):
        e = oe_ref[k]
        return (pltpu.make_async_copy(wg_hbm.at[e], wg_f.at[s], wsems.at[s, 0]),
                pltpu.make_async_copy(wu_hbm.at[e], wu_f.at[s], wsems.at[s, 1]),
                pltpu.make_async_copy(wd_hbm.at[e], wd_f.at[s], wsems.at[s, 2]))

    ahead = WEIGHT_SLOTS - 1

    @pl.when(i == 0)
    def _():
        gather(0, 0)
        for j in range(ahead):
            @pl.when(j < n_ord)
            def _():
                for cp in weight_copies(j, j):
                    cp.start(priority=WEIGHT_DMA_PRIORITY)

    @pl.when(i + 1 < nt)
    def _():
        gather(i + 1, 1 - slot)

    k = ord_ref[i]
    fresh = live & ((i == 0) | (k != ord_ref[jnp.maximum(i - 1, 0)]))
    ws = lax.rem(k, WEIGHT_SLOTS)

    @pl.when(fresh)
    def _():
        for cp in weight_copies(k, ws):
            cp.wait()

        @pl.when(k + ahead < n_ord)
        def _():
            for cp in weight_copies(k + ahead, lax.rem(k + ahead, WEIGHT_SLOTS)):
                cp.start(priority=WEIGHT_DMA_PRIORITY)

        wg_s[...] = wg_f[ws].astype(BF16)
        wu_s[...] = wu_f[ws].astype(BF16)
        wd_s[...] = wd_f[ws].astype(BF16)

    @pl.when(live)
    def _():
        _wait_rows(src_hbm, xbuf.at[slot], sems.at[slot])
        x = _from_row_tiles(xbuf.at[slot], ROW_CHUNKS).astype(BF16)
        a = _dot(x, wg_s[...])
        u = _dot(x, wu_s[...])
        act = a * (1.0 / (1.0 + jnp.exp(-a))) * u
        _to_row_tiles(o_ref, _dot(act.astype(BF16), wd_s[...]))

    @pl.when(jnp.logical_not(live))
    def _():
        o_ref[...] = jnp.zeros_like(o_ref)


def _experts(src, tok_of_row, w_g, w_u, w_d, tile_ord, ord_expert, counts, tm=EXPERT_TILE):
    c = ROW_CHUNKS
    n_tiles_max = tok_of_row.shape[0] // tm
    d = c * LANES
    f = w_g.shape[2]
    any_spec = pl.BlockSpec(memory_space=pl.ANY)
    return pl.pallas_call(
        functools.partial(_expert_kernel, tm=tm),
        grid_spec=pltpu.PrefetchScalarGridSpec(
            num_scalar_prefetch=4,
            grid=(n_tiles_max,),
            in_specs=[any_spec, any_spec, any_spec, any_spec],
            out_specs=pl.BlockSpec((tm * c, LANES), lambda i, *_: (i, 0)),
            scratch_shapes=[pltpu.VMEM((2, tm * c, LANES), F32), pltpu.SemaphoreType.DMA((2,)),
                            pltpu.VMEM((WEIGHT_SLOTS, d, f), F32), pltpu.VMEM((WEIGHT_SLOTS, d, f), F32),
                            pltpu.VMEM((WEIGHT_SLOTS, f, d), F32), pltpu.SemaphoreType.DMA((WEIGHT_SLOTS, 3)),
                            pltpu.VMEM((d, f), BF16), pltpu.VMEM((d, f), BF16), pltpu.VMEM((f, d), BF16)]),
        out_shape=jax.ShapeDtypeStruct((n_tiles_max * tm * c, LANES), F32),
        compiler_params=_cp("arbitrary"),
        name="moe_experts",
    )(tok_of_row, tile_ord, ord_expert, counts, src, w_g, w_u, w_d)


def _combine_kernel(p0_ref, p1_ref, ys_hbm, x_ref, r_ref, g_ref, o_ref, buf, sems, *, tm):
    i = pl.program_id(0)
    slot = lax.rem(i, 2)

    def fetch(tile, s):
        def issue(r, c):
            _row_copy(ys_hbm, buf.at[s, 0], sems.at[s], p0_ref[tile * tm + r], r).start()
            _row_copy(ys_hbm, buf.at[s, 1], sems.at[s], p1_ref[tile * tm + r], r).start(priority=1)
            return c
        lax.fori_loop(0, tm, issue, 0, unroll=8)

    @pl.when(i == 0)
    def _():
        fetch(0, 0)

    @pl.when(i + 1 < pl.num_programs(0))
    def _():
        fetch(i + 1, 1 - slot)

    _wait_rows(ys_hbm, buf.at[slot, 0], sems.at[slot])
    _wait_rows(ys_hbm, buf.at[slot, 1], sems.at[slot])
    route = r_ref[...]
    y = (route[:, 2:3] * _from_row_tiles(buf.at[slot, 0], ROW_CHUNKS)
         + route[:, 3:4] * _from_row_tiles(buf.at[slot, 1], ROW_CHUNKS))
    o_ref[...] = _rms(x_ref[...] + y, g_ref[...])


def _combine(ys, pos0, pos1, x, route, g, tm=256):
    n, d = x.shape
    return pl.pallas_call(
        functools.partial(_combine_kernel, tm=tm),
        grid_spec=pltpu.PrefetchScalarGridSpec(
            num_scalar_prefetch=2,
            grid=(n // tm,),
            in_specs=[pl.BlockSpec(memory_space=pl.ANY),
                      pl.BlockSpec((tm, d), lambda i, p0, p1: (i, 0)),
                      pl.BlockSpec((tm, LANES), lambda i, p0, p1: (i, 0)),
                      pl.BlockSpec((1, d), lambda i, p0, p1: (0, 0))],
            out_specs=pl.BlockSpec((tm, d), lambda i, p0, p1: (i, 0)),
            scratch_shapes=[pltpu.VMEM((2, 2, tm * ROW_CHUNKS, LANES), F32), pltpu.SemaphoreType.DMA((2,))]),
        out_shape=jax.ShapeDtypeStruct((n, d), F32),
        compiler_params=_cp("arbitrary"),
        name="moe_combine_norm",
    )(pos0, pos1, ys, x, route, g.reshape(1, d))


def _dispatch_plan(e1, e2, tm=EXPERT_TILE):
    n = e1.shape[0]
    n_tiles_max = (2 * n) // tm + N_EXPERTS
    rows = n_tiles_max * tm
    e_flat = jnp.concatenate([e1, e2])
    tok = jnp.concatenate([jnp.arange(n, dtype=jnp.int32)] * 2)
    onehot = (e_flat[:, None] == jnp.arange(N_EXPERTS, dtype=jnp.int32)[None, :]).astype(jnp.int32)
    csum = jnp.cumsum(onehot, axis=0)
    count = csum[-1]
    rank = jnp.sum((csum - 1) * onehot, axis=1)
    tiles_per = (count + tm - 1) // tm
    tile_end = jnp.cumsum(tiles_per)
    row_start = (tile_end - tiles_per) * tm
    dest = (row_start[e_flat] + rank).astype(jnp.int32)
    tok_of_row = jnp.zeros((rows,), jnp.int32).at[dest].set(tok)
    n_tiles = tile_end[-1].astype(jnp.int32)
    tile_id = jnp.minimum(jnp.arange(n_tiles_max, dtype=jnp.int32), n_tiles - 1)
    tile_expert = jnp.sum((tile_id[:, None] >= tile_end[None, :]).astype(jnp.int32), axis=1)
    used = jnp.cumsum((tiles_per > 0).astype(jnp.int32))
    tile_ord = (used[tile_expert] - 1).astype(jnp.int32)
    ordinals = jnp.arange(N_EXPERTS, dtype=jnp.int32)
    ord_expert = jnp.minimum(jnp.sum((used[None, :] <= ordinals[:, None]).astype(jnp.int32), axis=1),
                             N_EXPERTS - 1).astype(jnp.int32)
    counts = jnp.stack([n_tiles, used[-1]]).astype(jnp.int32)
    return tok_of_row, tile_ord, ord_expert, counts, dest[:n], dest[n:]


def _rope_tables(positions):
    half = ROT_DIM // 2
    inv = ROPE_THETA ** (-jnp.arange(half, dtype=F32) * (2.0 / ROT_DIM))
    ang = positions.astype(F32).reshape(-1, 1) * inv[None, :]
    cos, sin = jnp.cos(ang), jnp.sin(ang)
    n = ang.shape[0]
    c = jnp.concatenate([cos, cos, jnp.ones((n, LANES - ROT_DIM), F32)], axis=1)
    s1 = jnp.concatenate([-sin, jnp.zeros((n, LANES - half), F32)], axis=1)
    s2 = jnp.concatenate([jnp.zeros((n, half), F32), sin, jnp.zeros((n, LANES - ROT_DIM), F32)], axis=1)
    return c, s1, s2


def _layer(x, mem, positions, g_mix, w_in, cmp_pe_k, cmp_w1_k, cmp_w2_k, cmp_pe_v, cmp_w1_v, cmp_w2_v,
           w_br_nsa, w_br_sb, w_o, g_cross, g_mem, w_cq, w_ck, w_cv, w_co,
           g_moe, w_rg, b_rg, w_re, b_re, w_eg, w_eu, w_ed):
    b, t, d = x.shape
    n = b * t
    xf = x.reshape(n, d)

    g0 = NSA_W
    w_t = w_in.T
    w_nsa = _cast_rows(w_t, 0, g0)
    w_rest = _cast_rows(w_t, g0 + NSA_GATES, w_t.shape[0] - g0 - NSA_GATES)
    w_g = w_t[g0:g0 + NSA_GATES].reshape(NSA_KV_HEADS, 3 * NSA_GROUP, d)
    w_g = jnp.pad(w_g, ((0, 0), (0, LANES - 3 * NSA_GROUP), (0, 0))).reshape(NSA_KV_HEADS * LANES, d).astype(BF16)
    h = _rmsnorm(xf, g_mix, BF16)
    c, s1, s2 = _rope_tables(positions)
    row_spec = pl.BlockSpec((1024, LANES), lambda i, j: (i, 0))
    qkv_nsa = _mm(functools.partial(_mm_nsa_kernel, tn=256), h, w_nsa, NSA_W, 0, 1024, 256, BF16, w_t=True,
                  extra=(c, s1, s2), extra_specs=(row_spec, row_spec, row_spec), name="in_proj_nsa")
    qkv_sb = _mm(functools.partial(_mm_sb_kernel, q_tiles=SB_W // WIDE_TN), h, w_rest, 3 * SB_W, 0,
                 1024, WIDE_TN, BF16, w_t=True, name="in_proj_sb")
    gates = _mm(_mm_sigmoid_kernel, h, w_rest, 2 * d, 3 * SB_W // WIDE_TN, 1024, WIDE_TN, BF16, w_t=True,
                name="in_proj_gates")
    nsa_gates = _mm(_mm_sigmoid_kernel, h, w_g, NSA_KV_HEADS * LANES, 0, 1024, NSA_KV_HEADS * LANES, F32,
                    w_t=True, name="in_proj_nsa_gates")

    def chunked(col0):
        tok = qkv_nsa[:, col0:col0 + NSA_KV_W].reshape(b, t, NSA_KV_HEADS, HEAD_DIM).transpose(0, 2, 1, 3)
        return tok.reshape(b, NSA_KV_HEADS, t // CMP_STRIDE, CMP_STRIDE * HEAD_DIM)

    kc = _compress(chunked(NSA_Q_W), cmp_pe_k, cmp_w1_k, cmp_w2_k)
    vc = _compress(chunked(NSA_Q_W + NSA_KV_W), cmp_pe_v, cmp_w1_v, cmp_w2_v)
    o_c, sel = _nsa_cmp(qkv_nsa, kc, vc, b, t)
    o_nsa = _nsa_sel_win(qkv_nsa, sel, o_c, nsa_gates, b, t)
    o_sb = _stickbreak(qkv_sb, b, t)
    merged = _merge(o_nsa, o_sb, w_br_nsa.astype(BF16), w_br_sb.astype(BF16), gates)
    x1 = _mm(_mm_resid_kernel, merged, w_o.astype(BF16), d, 0, 1024, WIDE_TN, F32,
             extra=(xf,), extra_specs=(pl.BlockSpec((1024, WIDE_TN), lambda i, j: (i, j)),), name="out_proj")

    w_kv = jnp.concatenate([w_ck, w_cv], axis=1).astype(BF16)
    kv = _norm_mm(mem.reshape(-1, d), g_mem, w_kv, BF16)
    x2 = _cross(x1, g_cross, w_cq.astype(BF16), kv, w_co.astype(BF16), b, t)

    w_r = jnp.pad(jnp.concatenate([w_rg, w_re], axis=1), ((0, 0), (0, LANES - N_GROUPS - N_EXPERTS)))
    b_r = jnp.pad(jnp.concatenate([b_rg, b_re]), (0, LANES - N_GROUPS - N_EXPERTS)).reshape(1, LANES)
    w_r_hi = w_r.astype(BF16)
    w_r2 = jnp.concatenate([w_r_hi, (w_r - w_r_hi.astype(F32)).astype(BF16)], axis=1)
    hm, route = _router(x2, g_moe, w_r2, b_r)
    e1 = route[:, 0].astype(jnp.int32)
    e2 = route[:, 1].astype(jnp.int32)
    tok_of_row, tile_ord, ord_expert, counts, pos0, pos1 = _dispatch_plan(e1, e2)
    f = w_eg.shape[-1]
    ys = _experts(hm, tok_of_row, w_eg.reshape(N_EXPERTS, d, f), w_eu.reshape(N_EXPERTS, d, f),
                  w_ed.reshape(N_EXPERTS, f, d), tile_ord, ord_expert, counts)
    return ys, pos0, pos1, x2, route


def kernel(x, mem, positions, g_mix, w_in, cmp_pe_k, cmp_w1_k, cmp_w2_k, cmp_pe_v, cmp_w1_v, cmp_w2_v, w_br_nsa, w_br_sb, w_o, g_cross, g_mem, w_cq, w_ck, w_cv, w_co, g_moe, w_rg, b_rg, w_re, b_re, w_eg, w_eu, w_ed, g_final):
    b, t, d = x.shape
    assert w_in.shape[0] == 1, "one layer"
    ys, pos0, pos1, x2, route = _layer(
        x, mem, positions, g_mix[0], w_in[0], cmp_pe_k[0], cmp_w1_k[0], cmp_w2_k[0], cmp_pe_v[0], cmp_w1_v[0],
        cmp_w2_v[0], w_br_nsa[0], w_br_sb[0], w_o[0], g_cross[0], g_mem[0], w_cq[0], w_ck[0], w_cv[0], w_co[0],
        g_moe[0], w_rg[0], b_rg[0], w_re[0], b_re[0], w_eg[0], w_eu[0], w_ed[0])
    return _combine(ys, pos0, pos1, x2, route, g_final).reshape(b, t, d)
```

```python
import functools

import jax
import jax.numpy as jnp
from jax import lax
from jax.experimental import pallas as pl
from jax.experimental.pallas import tpu as pltpu

F32 = jnp.float32
BF16 = jnp.bfloat16

HEAD_DIM = 128
NSA_HEADS = 8
NSA_KV_HEADS = 2
NSA_GROUP = NSA_HEADS // NSA_KV_HEADS
SB_HEADS = 8
ROT_DIM = HEAD_DIM // 4
ROPE_THETA = 500000.0
CMP_LEN = 32
CMP_STRIDE = 16
SEL_LEN = 64
SEL_SHIFT = 6
SEL_TOPN = 16
WINDOW = 512
FORCE_BONUS = 1000.0
X_HEADS = 4
N_GROUPS = 4
EXPERTS_PER_GROUP = 8
N_EXPERTS = N_GROUPS * EXPERTS_PER_GROUP
EPS = 1e-6
SCALE = HEAD_DIM ** -0.5
NEG = -1e30
F32_EXP_UNDERFLOW = -104.0

NSA_Q_W = NSA_HEADS * HEAD_DIM
NSA_KV_W = NSA_KV_HEADS * HEAD_DIM
NSA_W = NSA_Q_W + 6 * NSA_KV_W
NSA_GATES = 3 * NSA_HEADS
SB_W = SB_HEADS * HEAD_DIM
LANES = 128
VMEM_LIMIT = 56 * 1024 * 1024

EXPERT_TILE = 256
WIDE_TN = 1024


def _cp(*sem, vmem=VMEM_LIMIT):
    return pltpu.CompilerParams(dimension_semantics=sem, vmem_limit_bytes=vmem)


def _nt_dot(a, b):
    return lax.dot_general(a, b, (((1,), (1,)), ((), ())), preferred_element_type=F32)


def _dot(a, b):
    return jnp.dot(a, b, preferred_element_type=F32)


def _split3(x):
    hi = x.astype(BF16)
    r1 = x - hi.astype(F32)
    mid = r1.astype(BF16)
    lo = (r1 - mid.astype(F32)).astype(BF16)
    return hi, mid, lo


def _rms(x, g):
    return x * lax.rsqrt(jnp.mean(x * x, axis=-1, keepdims=True) + EPS) * g


def _rmsnorm_kernel(x_ref, g_ref, o_ref):
    o_ref[...] = _rms(x_ref[...].astype(F32), g_ref[...]).astype(o_ref.dtype)


def _rmsnorm(x, g, out_dtype, tm=512):
    n, d = x.shape
    return pl.pallas_call(
        _rmsnorm_kernel,
        grid=(n // tm,),
        in_specs=[pl.BlockSpec((tm, d), lambda i: (i, 0)), pl.BlockSpec((1, d), lambda i: (0, 0))],
        out_specs=pl.BlockSpec((tm, d), lambda i: (i, 0)),
        out_shape=jax.ShapeDtypeStruct((n, d), out_dtype),
        compiler_params=_cp("parallel"),
        name="rmsnorm",
    )(x, g.reshape(1, d))


def _cast_rows_kernel(w_ref, o_ref):
    o_ref[...] = w_ref[...].astype(o_ref.dtype)


def _cast_rows(w_t, row0, n_rows, tn=512):
    k = w_t.shape[1]
    assert n_rows % tn == 0 and row0 % 8 == 0
    return pl.pallas_call(
        _cast_rows_kernel,
        grid=(n_rows // tn,),
        in_specs=[pl.BlockSpec((pl.Element(tn), pl.Element(k)), lambda j: ((row0 // 8 + j * (tn // 8)) * 8, 0))],
        out_specs=pl.BlockSpec((tn, k), lambda j: (j, 0)),
        out_shape=jax.ShapeDtypeStruct((n_rows, k), BF16),
        compiler_params=_cp("parallel"),
        name="cast_rows",
    )(w_t)


def _rope_half(x, c, s1, s2):
    return x * c + pltpu.roll(x, LANES - ROT_DIM // 2, 1) * s1 + pltpu.roll(x, ROT_DIM // 2, 1) * s2


def _mm_nsa_kernel(a_ref, w_ref, c_ref, s1_ref, s2_ref, o_ref, *, tn):
    j = pl.program_id(1)
    acc = _nt_dot(a_ref[...], w_ref[...])
    is_rope = (j < 5) | (j == 6) | (j == 8)

    @pl.when(is_rope)
    def _():
        qs = jnp.where(j < NSA_Q_W // tn, SCALE, 1.0)
        c, s1, s2 = c_ref[...] * qs, s1_ref[...] * qs, s2_ref[...] * qs
        for hh in range(tn // LANES):
            sl = slice(hh * LANES, (hh + 1) * LANES)
            o_ref[:, sl] = _rope_half(acc[:, sl], c, s1, s2).astype(o_ref.dtype)

    @pl.when(jnp.logical_not(is_rope))
    def _():
        o_ref[...] = acc.astype(o_ref.dtype)


def _mm_sb_kernel(a_ref, w_ref, o_ref, *, q_tiles):
    acc = _nt_dot(a_ref[...], w_ref[...])
    is_q = pl.program_id(1) < q_tiles
    o_ref[...] = (acc * jnp.where(is_q, SCALE, 1.0)).astype(o_ref.dtype)


def _mm_sigmoid_kernel(a_ref, w_ref, o_ref):
    o_ref[...] = jax.nn.sigmoid(_nt_dot(a_ref[...], w_ref[...])).astype(o_ref.dtype)


def _mm_resid_kernel(a_ref, w_ref, x_ref, o_ref):
    o_ref[...] = x_ref[...] + _dot(a_ref[...], w_ref[...])


def _mm(kern, a, w, n_cols, col_block0, tm, tn, out_dtype, extra=(), extra_specs=(), name="mm", w_t=False):
    m, k = a.shape
    w_spec = (pl.BlockSpec((tn, k), lambda i, j: (col_block0 + j, 0)) if w_t
              else pl.BlockSpec((k, tn), lambda i, j: (0, col_block0 + j)))
    return pl.pallas_call(
        kern,
        grid=(m // tm, n_cols // tn),
        in_specs=[pl.BlockSpec((tm, k), lambda i, j: (i, 0)), w_spec, *extra_specs],
        out_specs=pl.BlockSpec((tm, tn), lambda i, j: (i, j)),
        out_shape=jax.ShapeDtypeStruct((m, n_cols), out_dtype),
        compiler_params=_cp("parallel", "arbitrary"),
        name=name,
    )(a, w, *extra)


def _merge_kernel(a1_ref, a2_ref, w1_ref, w2_ref, ga_ref, gb_ref, o_ref):
    ya = _dot(a1_ref[...], w1_ref[...])
    yb = _dot(a2_ref[...], w2_ref[...])
    o_ref[...] = (ga_ref[...].astype(F32) * ya + gb_ref[...].astype(F32) * yb).astype(o_ref.dtype)


def _merge(o_nsa, o_sb, w1, w2, gates, tm=1024, tn=WIDE_TN):
    n, k = o_nsa.shape
    d = w1.shape[1]
    nb = d // tn
    return pl.pallas_call(
        _merge_kernel,
        grid=(n // tm, nb),
        in_specs=[pl.BlockSpec((tm, k), lambda i, j: (i, 0)), pl.BlockSpec((tm, k), lambda i, j: (i, 0)),
                  pl.BlockSpec((k, tn), lambda i, j: (0, j)), pl.BlockSpec((k, tn), lambda i, j: (0, j)),
                  pl.BlockSpec((tm, tn), lambda i, j: (i, j)), pl.BlockSpec((tm, tn), lambda i, j: (i, nb + j))],
        out_specs=pl.BlockSpec((tm, tn), lambda i, j: (i, j)),
        out_shape=jax.ShapeDtypeStruct((n, d), BF16),
        compiler_params=_cp("parallel", "arbitrary"),
        name="merge",
    )(o_nsa, o_sb, w1, w2, gates, gates)


def _norm_mm_kernel(x_ref, g_ref, w_ref, o_ref):
    h = _rms(x_ref[...].astype(F32), g_ref[...]).astype(BF16)
    o_ref[...] = _dot(h, w_ref[...]).astype(o_ref.dtype)


def _norm_mm(x, g, w, out_dtype, tm=256):
    n, d = x.shape
    nc = w.shape[1]
    return pl.pallas_call(
        _norm_mm_kernel,
        grid=(n // tm,),
        in_specs=[pl.BlockSpec((tm, d), lambda i: (i, 0)), pl.BlockSpec((1, d), lambda i: (0, 0)),
                  pl.BlockSpec((d, nc), lambda i: (0, 0))],
        out_specs=pl.BlockSpec((tm, nc), lambda i: (i, 0)),
        out_shape=jax.ShapeDtypeStruct((n, nc), out_dtype),
        compiler_params=_cp("parallel"),
        name="norm_mm",
    )(x, g.reshape(1, d), w)


def _compress_kernel(x_ref, pe_ref, w1_ref, w2_ref, o_ref):
    x = x_ref[...].astype(F32)
    half = x.shape[1]
    xa = (x + pe_ref[0:1, :]).astype(BF16)
    xb = (x + pe_ref[1:2, :]).astype(BF16)
    y1 = _dot(xa, w1_ref[0:half, :].astype(BF16))
    y2 = _dot(xb, w1_ref[half:2 * half, :].astype(BF16))
    pre = y1 + pltpu.roll(y2, x.shape[0] - 1, 0)
    act = pre * (1.0 / (1.0 + jnp.exp(-pre)))
    o_ref[...] = _dot(act.astype(BF16), w2_ref[...].astype(BF16)).astype(o_ref.dtype)


def _compress(tok, pe, w1, w2):
    b, hkv, nchunk, width = tok.shape
    return pl.pallas_call(
        _compress_kernel,
        grid=(b, hkv),
        in_specs=[pl.BlockSpec((None, None, nchunk, width), lambda i, j: (i, j, 0, 0)),
                  pl.BlockSpec((2, width), lambda i, j: (0, 0)),
                  pl.BlockSpec((2 * width, HEAD_DIM), lambda i, j: (0, 0)),
                  pl.BlockSpec((HEAD_DIM, HEAD_DIM), lambda i, j: (0, 0))],
        out_specs=pl.BlockSpec((None, None, nchunk, HEAD_DIM), lambda i, j: (i, j, 0, 0)),
        out_shape=jax.ShapeDtypeStruct((b, hkv, nchunk, HEAD_DIM), BF16),
        compiler_params=_cp("parallel", "parallel"),
        name="nsa_compress",
    )(tok, pe.reshape(2, width), w1, w2)


def _nsa_cmp_kernel(q_ref, kc_ref, vc_ref, oc_ref, sel_ref, *, tq, nc, ns):
    q0 = pl.program_id(2) * tq
    kc = kc_ref[...]
    vc = vc_ref[...]
    t_row = q0 + lax.broadcasted_iota(jnp.int32, (tq, nc), 0)
    c_idx = lax.broadcasted_iota(jnp.int32, (tq, nc), 1)
    vis = (c_idx * CMP_STRIDE + (CMP_LEN - 1) <= t_row) & (c_idx < nc - 1)
    heads = [slice(h * HEAD_DIM, (h + 1) * HEAD_DIM) for h in range(NSA_GROUP)]
    scores = [_nt_dot(q_ref[:, sl], kc) for sl in heads]
    probs = []
    for s in scores:
        s = jnp.where(vis, s, NEG)
        e = jnp.where(vis, jnp.exp(s - jnp.max(s, axis=-1, keepdims=True)), 0.0)
        den = jnp.sum(e, axis=-1, keepdims=True)
        probs.append(e / jnp.where(den > 0, den, 1.0))
    for sl, p in zip(heads, probs):
        oc_ref[:, sl] = _dot(p.astype(BF16), vc).astype(oc_ref.dtype)
    imp = sum(probs[1:], probs[0])
    jj = lax.broadcasted_iota(jnp.int32, (ns, nc), 0)
    cc = lax.broadcasted_iota(jnp.int32, (ns, nc), 1) * CMP_STRIDE
    ov = jnp.where((cc < (jj + 1) * SEL_LEN) & (cc + CMP_LEN > jj * SEL_LEN), 1.0, 0.0).astype(BF16)
    imp_t = sum(_nt_dot(ov, part) for part in _split3(imp))
    jb = lax.broadcasted_iota(jnp.int32, (ns, tq), 0)
    tt = q0 + lax.broadcasted_iota(jnp.int32, (ns, tq), 1)
    cur = lax.shift_right_logical(tt, jnp.full_like(tt, SEL_SHIFT))
    forced = (jb == 0) | (jb == cur) | (jb == cur - 1)
    valid = jb * SEL_LEN <= tt
    score = jnp.where(valid, imp_t + jnp.where(forced, FORCE_BONUS, 0.0), NEG)
    rank = jnp.zeros((ns, tq), F32)
    for i in range(ns):
        row = score[i:i + 1, :]
        rank = rank + jnp.where(row > score, 1.0, jnp.where((row == score) & (jb > i), 1.0, 0.0))
    sel_t = jnp.where(valid & (rank < float(min(SEL_TOPN, ns))), 1.0, 0.0).astype(BF16)
    eye = jnp.where(lax.broadcasted_iota(jnp.int32, (tq, tq), 0) == lax.broadcasted_iota(jnp.int32, (tq, tq), 1),
                    1.0, 0.0).astype(BF16)
    sel_ref[...] = _nt_dot(eye, sel_t)


def _nsa_cmp(qkv, kc, vc, b, t, tq=1024):
    nc = kc.shape[2]
    ns = t // SEL_LEN
    nq = t // tq
    gw = NSA_GROUP * HEAD_DIM
    return pl.pallas_call(
        functools.partial(_nsa_cmp_kernel, tq=tq, nc=nc, ns=ns),
        grid=(b, NSA_KV_HEADS, nq),
        in_specs=[pl.BlockSpec((tq, gw), lambda i, k, j: (i * nq + j, k)),
                  pl.BlockSpec((None, None, nc, HEAD_DIM), lambda i, k, j: (i, k, 0, 0)),
                  pl.BlockSpec((None, None, nc, HEAD_DIM), lambda i, k, j: (i, k, 0, 0))],
        out_specs=[pl.BlockSpec((tq, gw), lambda i, k, j: (i * nq + j, k)),
                   pl.BlockSpec((None, None, tq, ns), lambda i, k, j: (i, k, j, 0))],
        out_shape=[jax.ShapeDtypeStruct((b * t, NSA_Q_W), BF16),
                   jax.ShapeDtypeStruct((b, NSA_KV_HEADS, t, ns), F32)],
        compiler_params=_cp("parallel", "parallel", "parallel"),
        name="nsa_cmp_select",
    )(qkv, kc, vc)


def _flash_update(s, v, bias, state, g, tq):
    m, l, acc = state
    tk = v.shape[0]
    if bias is not None:
        s = (s.reshape(g, tq, tk) + bias[None]).reshape(g * tq, tk)
    m_new = jnp.maximum(m, jnp.max(s, axis=-1, keepdims=True))
    alpha = jnp.exp(m - m_new)
    p = jnp.exp((s - m_new).astype(BF16))
    pv = _dot(p, jnp.concatenate([v, jnp.ones_like(v)], axis=1))
    return m_new, alpha * l + pv[:, HEAD_DIM:], alpha * acc + pv[:, :HEAD_DIM]


def _nsa_sel_win_kernel(q_ref, ks_ref, vs_ref, kw_ref, vw_ref, sel_ref, oc_ref, g_ref, o_ref, *, tq, tk, stack, ns):
    q0 = pl.program_id(1) * tq
    g = NSA_GROUP
    nkv = NSA_KV_HEADS
    per_tile = tk // SEL_LEN
    units = [(kv, h0) for kv in range(nkv) for h0 in range(0, g, stack)]
    kv_lanes = [slice(kv * HEAD_DIM, (kv + 1) * HEAD_DIM) for kv in range(nkv)]

    def q_lanes(kv, h):
        return slice((kv * g + h) * HEAD_DIM, (kv * g + h + 1) * HEAD_DIM)

    qs = [jnp.concatenate([q_ref[:, q_lanes(kv, h0 + j)] for j in range(stack)], axis=0) for kv, h0 in units]
    sel_bias = [((sel_ref[kv] - 1.0) * (-NEG)).astype(BF16) for kv in range(nkv)]
    ej = lax.broadcasted_iota(jnp.int32, (ns, tk), 0)
    eb = lax.broadcasted_iota(jnp.int32, (ns, tk), 1)
    eb = lax.shift_right_logical(eb, jnp.full_like(eb, SEL_SHIFT))

    def init():
        return tuple((jnp.full((stack * tq, 1), NEG, F32), jnp.zeros((stack * tq, HEAD_DIM), F32),
                      jnp.zeros((stack * tq, HEAD_DIM), F32)) for _ in units)

    def flash_all(k_ref, v_ref, k0, width, biases, states):
        scores = [_nt_dot(qs[u], k_ref[pl.ds(k0, width), kv_lanes[kv]]) for u, (kv, _) in enumerate(units)]
        return tuple(_flash_update(scores[u], v_ref[pl.ds(k0, width), kv_lanes[kv]], biases[kv], states[u], stack, tq)
                     for u, (kv, _) in enumerate(units))

    def sel_step(kt, states, causal):
        k0 = pl.multiple_of(kt * tk, tk)
        expand = jnp.where(ej == eb + kt * per_tile, 1.0, 0.0).astype(BF16)
        biases = [_dot(sel_bias[kv], expand) for kv in range(nkv)]
        if causal is not None:
            biases = [bias + causal for bias in biases]
        return flash_all(ks_ref, vs_ref, k0, tk, biases, states)

    kt_d = q0 // tk
    t_row = q0 + lax.broadcasted_iota(jnp.int32, (tq, tk), 0)
    col = kt_d * tk + lax.broadcasted_iota(jnp.int32, (tq, tk), 1)
    states = sel_step(kt_d, init(), jnp.where(col <= t_row, 0.0, NEG))
    states = lax.fori_loop(0, kt_d, lambda kt, st: sel_step(kt, st, None), states)
    o_s = [acc * (1.0 / l) for (_, l, acc) in states]

    def win_step(k0, states, bias):
        return flash_all(kw_ref, vw_ref, k0, tq, [bias] * nkv, states)

    ri = lax.broadcasted_iota(jnp.int32, (tq, tq), 0)
    ci = lax.broadcasted_iota(jnp.int32, (tq, tq), 1)
    states = win_step(pl.multiple_of(q0, tq), init(), jnp.where(ci <= ri, 0.0, NEG))
    n_full = jnp.minimum(WINDOW // tq - 1, q0 // tq)
    states = lax.fori_loop(0, n_full, lambda i, st: win_step(pl.multiple_of(q0 - (i + 1) * tq, tq), st, None),
                           states)
    edge_bias = jnp.where(ci > ri, 0.0, NEG) + jnp.where(q0 >= WINDOW, 0.0, NEG)
    states = win_step(pl.multiple_of(jnp.maximum(q0 - WINDOW, 0), tq), states, edge_bias)
    o_w = [acc * (1.0 / l) for (_, l, acc) in states]

    for u, (kv, h0) in enumerate(units):
        gates = g_ref[:, kv * LANES:(kv + 1) * LANES]
        for j in range(stack):
            h = h0 + j
            sl = q_lanes(kv, h)
            rows = slice(j * tq, (j + 1) * tq)
            out = (gates[:, 3 * h:3 * h + 1] * oc_ref[:, sl].astype(F32)
                   + gates[:, 3 * h + 1:3 * h + 2] * o_s[u][rows] + gates[:, 3 * h + 2:3 * h + 3] * o_w[u][rows])
            o_ref[:, sl] = out.astype(o_ref.dtype)


def _nsa_sel_win(qkv, sel, oc, gates, b, t, tq=512, tk=512, stack=4):
    nq = t // tq
    ns = t // SEL_LEN
    kv_block = NSA_Q_W // NSA_KV_W

    def kv_spec(which):
        return pl.BlockSpec((t, NSA_KV_W), lambda i, j: (i, kv_block + which))

    return pl.pallas_call(
        functools.partial(_nsa_sel_win_kernel, tq=tq, tk=tk, stack=stack, ns=ns),
        grid=(b, nq),
        in_specs=[pl.BlockSpec((tq, NSA_Q_W), lambda i, j: (i * nq + j, 0)),
                  kv_spec(2), kv_spec(3), kv_spec(4), kv_spec(5),
                  pl.BlockSpec((None, NSA_KV_HEADS, tq, ns), lambda i, j: (i, 0, j, 0)),
                  pl.BlockSpec((tq, NSA_Q_W), lambda i, j: (i * nq + j, 0)),
                  pl.BlockSpec((tq, NSA_KV_HEADS * LANES), lambda i, j: (i * nq + j, 0))],
        out_specs=pl.BlockSpec((tq, NSA_Q_W), lambda i, j: (i * nq + j, 0)),
        out_shape=jax.ShapeDtypeStruct((b * t, NSA_Q_W), BF16),
        compiler_params=_cp("parallel", "parallel"),
        name="nsa_select_window",
    )(qkv, qkv, qkv, qkv, qkv, sel, oc, gates)


def _stickbreak_kernel(q_ref, k_ref, v_ref, o_ref, *, tile, nh):
    q0 = pl.program_id(2) * tile
    ri = lax.broadcasted_iota(jnp.int32, (tile, tile), 0)
    ci = lax.broadcasted_iota(jnp.int32, (tile, tile), 1)
    strict = ci < ri
    tri = jnp.where(ri > ci, 1.0, 0.0).astype(BF16)
    lanes = [slice(h * HEAD_DIM, (h + 1) * HEAD_DIM) for h in range(nh)]

    def step(k0, diag, carry):
        zs = [_nt_dot(q_ref[:, lanes[h]], k_ref[pl.ds(k0, tile), lanes[h]]) for h in range(nh)]
        log_bs, sufs, sums = [], [], []
        for z in zs:
            log_b = jnp.minimum(z, 0.0) - jnp.log(1.0 + jnp.exp(-jnp.abs(z)))
            log_1m = log_b - z
            if diag:
                log_1m = jnp.where(strict, log_1m, 0.0)
            hi = log_1m.astype(BF16)
            lo = (log_1m - hi.astype(F32)).astype(BF16)
            log_bs.append(log_b)
            sufs.append(_dot(hi, tri) + _dot(lo, tri))
            sums.append(jnp.sum(log_1m, axis=-1, keepdims=True))
        out = []
        for h in range(nh):
            later, acc = carry[2 * h], carry[2 * h + 1]
            a = jnp.exp(log_bs[h] + (sufs[h] + later))
            if diag:
                a = jnp.where(strict, a, 0.0)
            out.extend((later + sums[h], acc + _dot(a.astype(BF16), v_ref[pl.ds(k0, tile), lanes[h]])))
        return tuple(out)

    zero = (jnp.zeros((tile, 1), F32), jnp.zeros((tile, HEAD_DIM), F32)) * nh
    carry = step(pl.multiple_of(q0, tile), True, zero)

    def live(state):
        i, c = state
        top = c[0]
        for h in range(1, nh):
            top = jnp.maximum(top, c[2 * h])
        return (i < q0 // tile) & (jnp.max(top) > F32_EXP_UNDERFLOW)

    def advance(state):
        i, c = state
        return i + 1, step(pl.multiple_of(q0 - (i + 1) * tile, tile), False, c)

    _, carry = lax.while_loop(live, advance, (jnp.int32(0), carry))
    for h in range(nh):
        o_ref[:, lanes[h]] = carry[2 * h + 1].astype(o_ref.dtype)


def _stickbreak(qkv, b, t, tile=256, nh=4):
    nq = t // tile
    w = nh * HEAD_DIM
    nhb = SB_HEADS // nh
    return pl.pallas_call(
        functools.partial(_stickbreak_kernel, tile=tile, nh=nh),
        grid=(b, nhb, nq),
        in_specs=[pl.BlockSpec((tile, w), lambda i, h, j: (i * nq + j, h)),
                  pl.BlockSpec((t, w), lambda i, h, j: (i, nhb + h)),
                  pl.BlockSpec((t, w), lambda i, h, j: (i, 2 * nhb + h))],
        out_specs=pl.BlockSpec((tile, w), lambda i, h, j: (i * nq + j, h)),
        out_shape=jax.ShapeDtypeStruct((b * t, SB_W), BF16),
        compiler_params=_cp("parallel", "parallel", "parallel"),
        name="stickbreak",
    )(qkv, qkv, qkv)


def _cross_kernel(x_ref, g_ref, wq_ref, kv_ref, wo_ref, o_ref):
    x = x_ref[...]
    h = _rms(x, g_ref[...]).astype(BF16)
    q = _dot(h, wq_ref[...]).astype(BF16)
    xw = X_HEADS * HEAD_DIM
    outs = []
    for hd in range(X_HEADS):
        sl = slice(hd * HEAD_DIM, (hd + 1) * HEAD_DIM)
        s = _nt_dot(q[:, sl], kv_ref[:, sl]) * SCALE
        e = jnp.exp(s - jnp.max(s, axis=-1, keepdims=True))
        p = e / jnp.sum(e, axis=-1, keepdims=True)
        outs.append(_dot(p.astype(BF16), kv_ref[:, xw + hd * HEAD_DIM:xw + (hd + 1) * HEAD_DIM]).astype(BF16))
    o_ref[...] = x + _dot(jnp.concatenate(outs, axis=1), wo_ref[...])


def _cross(x, g, wq, kv, wo, b, t, tq=1024):
    n, d = x.shape
    nq = t // tq
    mem_len = kv.shape[0] // b
    return pl.pallas_call(
        _cross_kernel,
        grid=(b, nq),
        in_specs=[pl.BlockSpec((tq, d), lambda i, j: (i * nq + j, 0)),
                  pl.BlockSpec((1, d), lambda i, j: (0, 0)),
                  pl.BlockSpec(wq.shape, lambda i, j: (0, 0)),
                  pl.BlockSpec((mem_len, kv.shape[1]), lambda i, j: (i, 0)),
                  pl.BlockSpec(wo.shape, lambda i, j: (0, 0))],
        out_specs=pl.BlockSpec((tq, d), lambda i, j: (i * nq + j, 0)),
        out_shape=jax.ShapeDtypeStruct((n, d), F32),
        compiler_params=_cp("parallel", "parallel"),
        name="cross_attention",
    )(x, g.reshape(1, d), wq, kv, wo)


def _to_row_tiles(ref, val):
    rows, width = val.shape
    c_per = width // LANES
    for c in range(c_per):
        ref[pl.ds(c, rows, stride=c_per), :] = val[:, c * LANES:(c + 1) * LANES]


def _from_row_tiles(ref, c_per):
    rows = ref.shape[0] // c_per
    return jnp.concatenate([ref[pl.ds(c, rows, stride=c_per), :] for c in range(c_per)], axis=1)


def _router_kernel(x_ref, g_ref, w_ref, b_ref, h_ref, r_ref):
    h = _rms(x_ref[...], g_ref[...])
    _to_row_tiles(h_ref, h)
    h_hi = h.astype(BF16)
    h_mid = (h - h_hi.astype(F32)).astype(BF16)
    two = _dot(h_hi, w_ref[...])
    logits = two[:, :LANES] + two[:, LANES:] + _dot(h_mid, w_ref[:, :LANES]) + b_ref[...]
    lane = lax.broadcasted_iota(jnp.int32, logits.shape, 1).astype(F32)
    big = float(LANES)
    is_g = lane < N_GROUPS
    lg = jnp.where(is_g, logits, NEG)
    mg = jnp.max(lg, axis=-1, keepdims=True)
    p_gtop = 1.0 / jnp.sum(jnp.where(is_g, jnp.exp(lg - mg), 0.0), axis=-1, keepdims=True)
    g_top = jnp.min(jnp.where(lg == mg, lane, big), axis=-1, keepdims=True)
    lo = N_GROUPS + EXPERTS_PER_GROUP * g_top
    in_grp = (lane >= lo) & (lane < lo + EXPERTS_PER_GROUP)
    le = jnp.where(in_grp, logits, NEG)
    ee = jnp.where(in_grp, jnp.exp(le - jnp.max(le, axis=-1, keepdims=True)), 0.0)
    p_in = ee / jnp.sum(ee, axis=-1, keepdims=True)
    p1 = jnp.where(in_grp, p_in, -1.0)
    m1 = jnp.max(p1, axis=-1, keepdims=True)
    i1 = jnp.min(jnp.where(p1 == m1, lane, big), axis=-1, keepdims=True)
    p2 = jnp.where(lane == i1, -1.0, p1)
    m2 = jnp.max(p2, axis=-1, keepdims=True)
    i2 = jnp.min(jnp.where(p2 == m2, lane, big), axis=-1, keepdims=True)
    tot = m1 + m2
    r_ref[...] = jnp.where(lane == 0, i1 - N_GROUPS,
                           jnp.where(lane == 1, i2 - N_GROUPS,
                                     jnp.where(lane == 2, m1 / tot * p_gtop,
                                               jnp.where(lane == 3, m2 / tot * p_gtop, 0.0))))


def _router(x, g, w_r, b_r, tm=512):
    n, d = x.shape
    return pl.pallas_call(
        _router_kernel,
        grid=(n // tm,),
        in_specs=[pl.BlockSpec((tm, d), lambda i: (i, 0)), pl.BlockSpec((1, d), lambda i: (0, 0)),
                  pl.BlockSpec((d, 2 * LANES), lambda i: (0, 0)), pl.BlockSpec((1, LANES), lambda i: (0, 0))],
        out_specs=[pl.BlockSpec((tm * (d // LANES), LANES), lambda i: (i, 0)),
                   pl.BlockSpec((tm, LANES), lambda i: (i, 0))],
        out_shape=[jax.ShapeDtypeStruct((n * (d // LANES), LANES), F32), jax.ShapeDtypeStruct((n, LANES), F32)],
        compiler_params=_cp("parallel"),
        name="moe_router",
    )(x, g.reshape(1, d), w_r, b_r)


ROW_CHUNKS = 16
WEIGHT_DMA_PRIORITY = 0
WEIGHT_SLOTS = 2


def _wait_rows(src_hbm, dst, sem):
    pltpu.make_async_copy(src_hbm.at[pl.ds(0, dst.shape[0]), :], dst, sem).wait()


def _row_copy(src_hbm, dst, sem, src_row, dst_row):
    c = ROW_CHUNKS
    return pltpu.make_async_copy(src_hbm.at[pl.ds(pl.multiple_of(src_row * c, c), c), :],
                                 dst.at[pl.ds(pl.multiple_of(dst_row * c, c), c), :], sem)


def _expert_kernel(idx_ref, ord_ref, oe_ref, cnt_ref, src_hbm, wg_hbm, wu_hbm, wd_hbm, o_ref,
                   xbuf, sems, wg_f, wu_f, wd_f, wsems, wg_s, wu_s, wd_s, *, tm):
    i = pl.program_id(0)
    nt = cnt_ref[0]
    n_ord = cnt_ref[1]
    live = i < nt
    slot = lax.rem(i, 2)

    def gather(tile, s):
        def issue(r2, c):
            for queue in range(2):
                r = 2 * r2 + queue
                _row_copy(src_hbm, xbuf.at[s], sems.at[s], idx_ref[tile * tm + r], r).start(priority=queue)
            return c
        lax.fori_loop(0, tm // 2, issue, 0, unroll=4)

    def weight_copies(k, s):
        e = oe_ref[k]
        return (pltpu.make_async_copy(wg_hbm.at[e], wg_f.at[s], wsems.at[s, 0]),
                pltpu.make_async_copy(wu_hbm.at[e], wu_f.at[s], wsems.at[s, 1]),
                pltpu.make_async_copy(wd_hbm.at[e], wd_f.at[s], wsems.at[s, 2]))

    ahead = WEIGHT_SLOTS - 1

    @pl.when(i == 0)
    def _():
        gather(0, 0)
        for j in range(ahead):
            @pl.when(j < n_ord)
            def _():
                for cp in weight_copies(j, j):
                    cp.start(priority=WEIGHT_DMA_PRIORITY)

    @pl.when(i + 1 < nt)
    def _():
        gather(i + 1, 1 - slot)

    k = ord_ref[i]
    fresh = live & ((i == 0) | (k != ord_ref[jnp.maximum(i - 1, 0)]))
    ws = lax.rem(k, WEIGHT_SLOTS)

    @pl.when(fresh)
    def _():
        for cp in weight_copies(k, ws):
            cp.wait()

        @pl.when(k + ahead < n_ord)
        def _():
            for cp in weight_copies(k + ahead, lax.rem(k + ahead, WEIGHT_SLOTS)):
                cp.start(priority=WEIGHT_DMA_PRIORITY)

        wg_s[...] = wg_f[ws].astype(BF16)
        wu_s[...] = wu_f[ws].astype(BF16)
        wd_s[...] = wd_f[ws].astype(BF16)

    @pl.when(live)
    def _():
        _wait_rows(src_hbm, xbuf.at[slot], sems.at[slot])
        x = _from_row_tiles(xbuf.at[slot], ROW_CHUNKS).astype(BF16)
        a = _dot(x, wg_s[...])
        u = _dot(x, wu_s[...])
        act = a * (1.0 / (1.0 + jnp.exp(-a))) * u
        _to_row_tiles(o_ref, _dot(act.astype(BF16), wd_s[...]))

    @pl.when(jnp.logical_not(live))
    def _():
        o_ref[...] = jnp.zeros_like(o_ref)


def _experts(src, tok_of_row, w_g, w_u, w_d, tile_ord, ord_expert, counts, tm=EXPERT_TILE):
    c = ROW_CHUNKS
    n_tiles_max = tok_of_row.shape[0] // tm
    d = c * LANES
    f = w_g.shape[2]
    any_spec = pl.BlockSpec(memory_space=pl.ANY)
    return pl.pallas_call(
        functools.partial(_expert_kernel, tm=tm),
        grid_spec=pltpu.PrefetchScalarGridSpec(
            num_scalar_prefetch=4,
            grid=(n_tiles_max,),
            in_specs=[any_spec, any_spec, any_spec, any_spec],
            out_specs=pl.BlockSpec((tm * c, LANES), lambda i, *_: (i, 0)),
            scratch_shapes=[pltpu.VMEM((2, tm * c, LANES), F32), pltpu.SemaphoreType.DMA((2,)),
                            pltpu.VMEM((WEIGHT_SLOTS, d, f), F32), pltpu.VMEM((WEIGHT_SLOTS, d, f), F32),
                            pltpu.VMEM((WEIGHT_SLOTS, f, d), F32), pltpu.SemaphoreType.DMA((WEIGHT_SLOTS, 3)),
                            pltpu.VMEM((d, f), BF16), pltpu.VMEM((d, f), BF16), pltpu.VMEM((f, d), BF16)]),
        out_shape=jax.ShapeDtypeStruct((n_tiles_max * tm * c, LANES), F32),
        compiler_params=_cp("arbitrary"),
        name="moe_experts",
    )(tok_of_row, tile_ord, ord_expert, counts, src, w_g, w_u, w_d)


def _combine_kernel(p0_ref, p1_ref, ys_hbm, x_ref, r_ref, g_ref, o_ref, buf, sems, *, tm):
    i = pl.program_id(0)
    slot = lax.rem(i, 2)

    def fetch(tile, s):
        def issue(r, c):
            _row_copy(ys_hbm, buf.at[s, 0], sems.at[s], p0_ref[tile * tm + r], r).start()
            _row_copy(ys_hbm, buf.at[s, 1], sems.at[s], p1_ref[tile * tm + r], r).start(priority=1)
            return c
        lax.fori_loop(0, tm, issue, 0, unroll=8)

    @pl.when(i == 0)
    def _():
        fetch(0, 0)

    @pl.when(i + 1 < pl.num_programs(0))
    def _():
        fetch(i + 1, 1 - slot)

    _wait_rows(ys_hbm, buf.at[slot, 0], sems.at[slot])
    _wait_rows(ys_hbm, buf.at[slot, 1], sems.at[slot])
    route = r_ref[...]
    y = (route[:, 2:3] * _from_row_tiles(buf.at[slot, 0], ROW_CHUNKS)
         + route[:, 3:4] * _from_row_tiles(buf.at[slot, 1], ROW_CHUNKS))
    o_ref[...] = _rms(x_ref[...] + y, g_ref[...])


def _combine(ys, pos0, pos1, x, route, g, tm=256):
    n, d = x.shape
    return pl.pallas_call(
        functools.partial(_combine_kernel, tm=tm),
        grid_spec=pltpu.PrefetchScalarGridSpec(
            num_scalar_prefetch=2,
            grid=(n // tm,),
            in_specs=[pl.BlockSpec(memory_space=pl.ANY),
                      pl.BlockSpec((tm, d), lambda i, p0, p1: (i, 0)),
                      pl.BlockSpec((tm, LANES), lambda i, p0, p1: (i, 0)),
                      pl.BlockSpec((1, d), lambda i, p0, p1: (0, 0))],
            out_specs=pl.BlockSpec((tm, d), lambda i, p0, p1: (i, 0)),
            scratch_shapes=[pltpu.VMEM((2, 2, tm * ROW_CHUNKS, LANES), F32), pltpu.SemaphoreType.DMA((2,))]),
        out_shape=jax.ShapeDtypeStruct((n, d), F32),
        compiler_params=_cp("arbitrary"),
        name="moe_combine_norm",
    )(pos0, pos1, ys, x, route, g.reshape(1, d))


def _dispatch_plan(e1, e2, tm=EXPERT_TILE):
    n = e1.shape[0]
    n_tiles_max = (2 * n) // tm + N_EXPERTS
    rows = n_tiles_max * tm
    e_flat = jnp.concatenate([e1, e2])
    tok = jnp.concatenate([jnp.arange(n, dtype=jnp.int32)] * 2)
    onehot = (e_flat[:, None] == jnp.arange(N_EXPERTS, dtype=jnp.int32)[None, :]).astype(jnp.int32)
    csum = jnp.cumsum(onehot, axis=0)
    count = csum[-1]
    rank = jnp.sum((csum - 1) * onehot, axis=1)
    tiles_per = (count + tm - 1) // tm
    tile_end = jnp.cumsum(tiles_per)
    row_start = (tile_end - tiles_per) * tm
    dest = (row_start[e_flat] + rank).astype(jnp.int32)
    tok_of_row = jnp.zeros((rows,), jnp.int32).at[dest].set(tok)
    n_tiles = tile_end[-1].astype(jnp.int32)
    tile_id = jnp.minimum(jnp.arange(n_tiles_max, dtype=jnp.int32), n_tiles - 1)
    tile_expert = jnp.sum((tile_id[:, None] >= tile_end[None, :]).astype(jnp.int32), axis=1)
    used = jnp.cumsum((tiles_per > 0).astype(jnp.int32))
    tile_ord = (used[tile_expert] - 1).astype(jnp.int32)
    ordinals = jnp.arange(N_EXPERTS, dtype=jnp.int32)
    ord_expert = jnp.minimum(jnp.sum((used[None, :] <= ordinals[:, None]).astype(jnp.int32), axis=1),
                             N_EXPERTS - 1).astype(jnp.int32)
    counts = jnp.stack([n_tiles, used[-1]]).astype(jnp.int32)
    return tok_of_row, tile_ord, ord_expert, counts, dest[:n], dest[n:]


def _rope_tables(positions):
    half = ROT_DIM // 2
    inv = ROPE_THETA ** (-jnp.arange(half, dtype=F32) * (2.0 / ROT_DIM))
    ang = positions.astype(F32).reshape(-1, 1) * inv[None, :]
    cos, sin = jnp.cos(ang), jnp.sin(ang)
    n = ang.shape[0]
    c = jnp.concatenate([cos, cos, jnp.ones((n, LANES - ROT_DIM), F32)], axis=1)
    s1 = jnp.concatenate([-sin, jnp.zeros((n, LANES - half), F32)], axis=1)
    s2 = jnp.concatenate([jnp.zeros((n, half), F32), sin, jnp.zeros((n, LANES - ROT_DIM), F32)], axis=1)
    return c, s1, s2


def _layer(x, mem, positions, g_mix, w_in, cmp_pe_k, cmp_w1_k, cmp_w2_k, cmp_pe_v, cmp_w1_v, cmp_w2_v,
           w_br_nsa, w_br_sb, w_o, g_cross, g_mem, w_cq, w_ck, w_cv, w_co,
           g_moe, w_rg, b_rg, w_re, b_re, w_eg, w_eu, w_ed):
    b, t, d = x.shape
    n = b * t
    xf = x.reshape(n, d)

    g0 = NSA_W
    w_t = w_in.T
    w_nsa = _cast_rows(w_t, 0, g0)
    w_rest = _cast_rows(w_t, g0 + NSA_GATES, w_t.shape[0] - g0 - NSA_GATES)
    w_g = w_t[g0:g0 + NSA_GATES].reshape(NSA_KV_HEADS, 3 * NSA_GROUP, d)
    w_g = jnp.pad(w_g, ((0, 0), (0, LANES - 3 * NSA_GROUP), (0, 0))).reshape(NSA_KV_HEADS * LANES, d).astype(BF16)
    h = _rmsnorm(xf, g_mix, BF16)
    c, s1, s2 = _rope_tables(positions)
    row_spec = pl.BlockSpec((2048, LANES), lambda i, j: (i, 0))
    qkv_nsa = _mm(functools.partial(_mm_nsa_kernel, tn=256), h, w_nsa, NSA_W, 0, 2048, 256, BF16, w_t=True,
                  extra=(c, s1, s2), extra_specs=(row_spec, row_spec, row_spec), name="in_proj_nsa")
    qkv_sb = _mm(functools.partial(_mm_sb_kernel, q_tiles=SB_W // WIDE_TN), h, w_rest, 3 * SB_W, 0,
                 1024, WIDE_TN, BF16, w_t=True, name="in_proj_sb")
    gates = _mm(_mm_sigmoid_kernel, h, w_rest, 2 * d, 3 * SB_W // WIDE_TN, 1024, WIDE_TN, BF16, w_t=True,
                name="in_proj_gates")
    nsa_gates = _mm(_mm_sigmoid_kernel, h, w_g, NSA_KV_HEADS * LANES, 0, 1024, NSA_KV_HEADS * LANES, F32,
                    w_t=True, name="in_proj_nsa_gates")

    def chunked(col0):
        tok = qkv_nsa[:, col0:col0 + NSA_KV_W].reshape(b, t, NSA_KV_HEADS, HEAD_DIM).transpose(0, 2, 1, 3)
        return tok.reshape(b, NSA_KV_HEADS, t // CMP_STRIDE, CMP_STRIDE * HEAD_DIM)

    kc = _compress(chunked(NSA_Q_W), cmp_pe_k, cmp_w1_k, cmp_w2_k)
    vc = _compress(chunked(NSA_Q_W + NSA_KV_W), cmp_pe_v, cmp_w1_v, cmp_w2_v)
    o_c, sel = _nsa_cmp(qkv_nsa, kc, vc, b, t)
    o_nsa = _nsa_sel_win(qkv_nsa, sel, o_c, nsa_gates, b, t)
    o_sb = _stickbreak(qkv_sb, b, t)
    merged = _merge(o_nsa, o_sb, w_br_nsa.astype(BF16), w_br_sb.astype(BF16), gates)
    x1 = _mm(_mm_resid_kernel, merged, w_o.astype(BF16), d, 0, 1024, WIDE_TN, F32,
             extra=(xf,), extra_specs=(pl.BlockSpec((1024, WIDE_TN), lambda i, j: (i, j)),), name="out_proj")

    w_kv = jnp.concatenate([w_ck, w_cv], axis=1).astype(BF16)
    kv = _norm_mm(mem.reshape(-1, d), g_mem, w_kv, BF16)
    x2 = _cross(x1, g_cross, w_cq.astype(BF16), kv, w_co.astype(BF16), b, t)

    w_r = jnp.pad(jnp.concatenate([w_rg, w_re], axis=1), ((0, 0), (0, LANES - N_GROUPS - N_EXPERTS)))
    b_r = jnp.pad(jnp.concatenate([b_rg, b_re]), (0, LANES - N_GROUPS - N_EXPERTS)).reshape(1, LANES)
    w_r_hi = w_r.astype(BF16)
    w_r2 = jnp.concatenate([w_r_hi, (w_r - w_r_hi.astype(F32)).astype(BF16)], axis=1)
    hm, route = _router(x2, g_moe, w_r2, b_r)
    e1 = route[:, 0].astype(jnp.int32)
    e2 = route[:, 1].astype(jnp.int32)
    tok_of_row, tile_ord, ord_expert, counts, pos0, pos1 = _dispatch_plan(e1, e2)
    f = w_eg.shape[-1]
    ys = _experts(hm, tok_of_row, w_eg.reshape(N_EXPERTS, d, f), w_eu.reshape(N_EXPERTS, d, f),
                  w_ed.reshape(N_EXPERTS, f, d), tile_ord, ord_expert, counts)
    return ys, pos0, pos1, x2, route


def kernel(x, mem, positions, g_mix, w_in, cmp_pe_k, cmp_w1_k, cmp_w2_k, cmp_pe_v, cmp_w1_v, cmp_w2_v, w_br_nsa, w_br_sb, w_o, g_cross, g_mem, w_cq, w_ck, w_cv, w_co, g_moe, w_rg, b_rg, w_re, b_re, w_eg, w_eu, w_ed, g_final):
    b, t, d = x.shape
    assert w_in.shape[0] == 1, "one layer"
    ys, pos0, pos1, x2, route = _layer(
        x, mem, positions, g_mix[0], w_in[0], cmp_pe_k[0], cmp_w1_k[0], cmp_w2_k[0], cmp_pe_v[0], cmp_w1_v[0],
        cmp_w2_v[0], w_br_nsa[0], w_br_sb[0], w_o[0], g_cross[0], g_mem[0], w_cq[0], w_ck[0], w_cv[0], w_co[0],
        g_moe[0], w_rg[0], b_rg[0], w_re[0], b_re[0], w_eg[0], w_eu[0], w_ed[0])
    return _combine(ys, pos0, pos1, x2, route, g_final).reshape(b, t, d)
```

```python
import functools

import jax
import jax.numpy as jnp
from jax import lax
from jax.experimental import pallas as pl
from jax.experimental.pallas import tpu as pltpu

F32 = jnp.float32
BF16 = jnp.bfloat16

HEAD_DIM = 128
NSA_HEADS = 8
NSA_KV_HEADS = 2
NSA_GROUP = NSA_HEADS // NSA_KV_HEADS
SB_HEADS = 8
ROT_DIM = HEAD_DIM // 4
ROPE_THETA = 500000.0
CMP_LEN = 32
CMP_STRIDE = 16
SEL_LEN = 64
SEL_SHIFT = 6
SEL_TOPN = 16
WINDOW = 512
FORCE_BONUS = 1000.0
X_HEADS = 4
N_GROUPS = 4
EXPERTS_PER_GROUP = 8
N_EXPERTS = N_GROUPS * EXPERTS_PER_GROUP
EPS = 1e-6
SCALE = HEAD_DIM ** -0.5
NEG = -1e30
F32_EXP_UNDERFLOW = -104.0

NSA_Q_W = NSA_HEADS * HEAD_DIM
NSA_KV_W = NSA_KV_HEADS * HEAD_DIM
NSA_W = NSA_Q_W + 6 * NSA_KV_W
NSA_GATES = 3 * NSA_HEADS
SB_W = SB_HEADS * HEAD_DIM
LANES = 128
VMEM_LIMIT = 56 * 1024 * 1024

EXPERT_TILE = 256
WIDE_TN = 1024


def _cp(*sem, vmem=VMEM_LIMIT):
    return pltpu.CompilerParams(dimension_semantics=sem, vmem_limit_bytes=vmem)


def _nt_dot(a, b):
    return lax.dot_general(a, b, (((1,), (1,)), ((), ())), preferred_element_type=F32)


def _dot(a, b):
    return jnp.dot(a, b, preferred_element_type=F32)


def _split3(x):
    hi = x.astype(BF16)
    r1 = x - hi.astype(F32)
    mid = r1.astype(BF16)
    lo = (r1 - mid.astype(F32)).astype(BF16)
    return hi, mid, lo


def _rms(x, g):
    return x * lax.rsqrt(jnp.mean(x * x, axis=-1, keepdims=True) + EPS) * g


def _rmsnorm_kernel(x_ref, g_ref, o_ref):
    o_ref[...] = _rms(x_ref[...].astype(F32), g_ref[...]).astype(o_ref.dtype)


def _rmsnorm(x, g, out_dtype, tm=512):
    n, d = x.shape
    return pl.pallas_call(
        _rmsnorm_kernel,
        grid=(n // tm,),
        in_specs=[pl.BlockSpec((tm, d), lambda i: (i, 0)), pl.BlockSpec((1, d), lambda i: (0, 0))],
        out_specs=pl.BlockSpec((tm, d), lambda i: (i, 0)),
        out_shape=jax.ShapeDtypeStruct((n, d), out_dtype),
        compiler_params=_cp("parallel"),
        name="rmsnorm",
    )(x, g.reshape(1, d))


def _cast_rows_kernel(w_ref, o_ref):
    o_ref[...] = w_ref[...].astype(o_ref.dtype)


def _cast_rows(w_t, row0, n_rows, tn=512):
    k = w_t.shape[1]
    assert n_rows % tn == 0 and row0 % 8 == 0
    return pl.pallas_call(
        _cast_rows_kernel,
        grid=(n_rows // tn,),
        in_specs=[pl.BlockSpec((pl.Element(tn), pl.Element(k)), lambda j: ((row0 // 8 + j * (tn // 8)) * 8, 0))],
        out_specs=pl.BlockSpec((tn, k), lambda j: (j, 0)),
        out_shape=jax.ShapeDtypeStruct((n_rows, k), BF16),
        compiler_params=_cp("parallel"),
        name="cast_rows",
    )(w_t)


def _rope_half(x, c, s1, s2):
    return x * c + pltpu.roll(x, LANES - ROT_DIM // 2, 1) * s1 + pltpu.roll(x, ROT_DIM // 2, 1) * s2


def _mm_nsa_kernel(a_ref, w_ref, c_ref, s1_ref, s2_ref, o_ref, *, tn):
    j = pl.program_id(1)
    acc = _nt_dot(a_ref[...], w_ref[...])
    is_rope = (j < 5) | (j == 6) | (j == 8)

    @pl.when(is_rope)
    def _():
        qs = jnp.where(j < NSA_Q_W // tn, SCALE, 1.0)
        c, s1, s2 = c_ref[...] * qs, s1_ref[...] * qs, s2_ref[...] * qs
        for hh in range(tn // LANES):
            sl = slice(hh * LANES, (hh + 1) * LANES)
            o_ref[:, sl] = _rope_half(acc[:, sl], c, s1, s2).astype(o_ref.dtype)

    @pl.when(jnp.logical_not(is_rope))
    def _():
        o_ref[...] = acc.astype(o_ref.dtype)


def _mm_sb_kernel(a_ref, w_ref, o_ref, *, q_tiles):
    acc = _nt_dot(a_ref[...], w_ref[...])
    is_q = pl.program_id(1) < q_tiles
    o_ref[...] = (acc * jnp.where(is_q, SCALE, 1.0)).astype(o_ref.dtype)


def _mm_sigmoid_kernel(a_ref, w_ref, o_ref):
    o_ref[...] = jax.nn.sigmoid(_nt_dot(a_ref[...], w_ref[...])).astype(o_ref.dtype)


def _mm_resid_kernel(a_ref, w_ref, x_ref, o_ref):
    o_ref[...] = x_ref[...] + _dot(a_ref[...], w_ref[...])


def _mm(kern, a, w, n_cols, col_block0, tm, tn, out_dtype, extra=(), extra_specs=(), name="mm", w_t=False):
    m, k = a.shape
    w_spec = (pl.BlockSpec((tn, k), lambda i, j: (col_block0 + j, 0)) if w_t
              else pl.BlockSpec((k, tn), lambda i, j: (0, col_block0 + j)))
    return pl.pallas_call(
        kern,
        grid=(m // tm, n_cols // tn),
        in_specs=[pl.BlockSpec((tm, k), lambda i, j: (i, 0)), w_spec, *extra_specs],
        out_specs=pl.BlockSpec((tm, tn), lambda i, j: (i, j)),
        out_shape=jax.ShapeDtypeStruct((m, n_cols), out_dtype),
        compiler_params=_cp("parallel", "arbitrary"),
        name=name,
    )(a, w, *extra)


def _merge_kernel(a1_ref, a2_ref, w1_ref, w2_ref, ga_ref, gb_ref, o_ref):
    ya = _dot(a1_ref[...], w1_ref[...])
    yb = _dot(a2_ref[...], w2_ref[...])
    o_ref[...] = (ga_ref[...].astype(F32) * ya + gb_ref[...].astype(F32) * yb).astype(o_ref.dtype)


def _merge(o_nsa, o_sb, w1, w2, gates, tm=1024, tn=WIDE_TN):
    n, k = o_nsa.shape
    d = w1.shape[1]
    nb = d // tn
    return pl.pallas_call(
        _merge_kernel,
        grid=(n // tm, nb),
        in_specs=[pl.BlockSpec((tm, k), lambda i, j: (i, 0)), pl.BlockSpec((tm, k), lambda i, j: (i, 0)),
                  pl.BlockSpec((k, tn), lambda i, j: (0, j)), pl.BlockSpec((k, tn), lambda i, j: (0, j)),
                  pl.BlockSpec((tm, tn), lambda i, j: (i, j)), pl.BlockSpec((tm, tn), lambda i, j: (i, nb + j))],
        out_specs=pl.BlockSpec((tm, tn), lambda i, j: (i, j)),
        out_shape=jax.ShapeDtypeStruct((n, d), BF16),
        compiler_params=_cp("parallel", "arbitrary"),
        name="merge",
    )(o_nsa, o_sb, w1, w2, gates, gates)


def _norm_mm_kernel(x_ref, g_ref, w_ref, o_ref):
    h = _rms(x_ref[...].astype(F32), g_ref[...]).astype(BF16)
    o_ref[...] = _dot(h, w_ref[...]).astype(o_ref.dtype)


def _norm_mm(x, g, w, out_dtype, tm=256):
    n, d = x.shape
    nc = w.shape[1]
    return pl.pallas_call(
        _norm_mm_kernel,
        grid=(n // tm,),
        in_specs=[pl.BlockSpec((tm, d), lambda i: (i, 0)), pl.BlockSpec((1, d), lambda i: (0, 0)),
                  pl.BlockSpec((d, nc), lambda i: (0, 0))],
        out_specs=pl.BlockSpec((tm, nc), lambda i: (i, 0)),
        out_shape=jax.ShapeDtypeStruct((n, nc), out_dtype),
        compiler_params=_cp("parallel"),
        name="norm_mm",
    )(x, g.reshape(1, d), w)


def _compress_kernel(x_ref, pe_ref, w1_ref, w2_ref, o_ref):
    x = x_ref[...].astype(F32)
    half = x.shape[1]
    xa = (x + pe_ref[0:1, :]).astype(BF16)
    xb = (x + pe_ref[1:2, :]).astype(BF16)
    y1 = _dot(xa, w1_ref[0:half, :].astype(BF16))
    y2 = _dot(xb, w1_ref[half:2 * half, :].astype(BF16))
    pre = y1 + pltpu.roll(y2, x.shape[0] - 1, 0)
    act = pre * (1.0 / (1.0 + jnp.exp(-pre)))
    o_ref[...] = _dot(act.astype(BF16), w2_ref[...].astype(BF16)).astype(o_ref.dtype)


def _compress(tok, pe, w1, w2):
    b, hkv, nchunk, width = tok.shape
    return pl.pallas_call(
        _compress_kernel,
        grid=(b, hkv),
        in_specs=[pl.BlockSpec((None, None, nchunk, width), lambda i, j: (i, j, 0, 0)),
                  pl.BlockSpec((2, width), lambda i, j: (0, 0)),
                  pl.BlockSpec((2 * width, HEAD_DIM), lambda i, j: (0, 0)),
                  pl.BlockSpec((HEAD_DIM, HEAD_DIM), lambda i, j: (0, 0))],
        out_specs=pl.BlockSpec((None, None, nchunk, HEAD_DIM), lambda i, j: (i, j, 0, 0)),
        out_shape=jax.ShapeDtypeStruct((b, hkv, nchunk, HEAD_DIM), BF16),
        compiler_params=_cp("parallel", "parallel"),
        name="nsa_compress",
    )(tok, pe.reshape(2, width), w1, w2)


def _nsa_cmp_kernel(q_ref, kc_ref, vc_ref, oc_ref, sel_ref, *, tq, nc, ns):
    q0 = pl.program_id(2) * tq
    kc = kc_ref[...]
    vc = vc_ref[...]
    t_row = q0 + lax.broadcasted_iota(jnp.int32, (tq, nc), 0)
    c_idx = lax.broadcasted_iota(jnp.int32, (tq, nc), 1)
    vis = (c_idx * CMP_STRIDE + (CMP_LEN - 1) <= t_row) & (c_idx < nc - 1)
    heads = [slice(h * HEAD_DIM, (h + 1) * HEAD_DIM) for h in range(NSA_GROUP)]
    scores = [_nt_dot(q_ref[:, sl], kc) for sl in heads]
    probs = []
    for s in scores:
        s = jnp.where(vis, s, NEG)
        e = jnp.where(vis, jnp.exp(s - jnp.max(s, axis=-1, keepdims=True)), 0.0)
        den = jnp.sum(e, axis=-1, keepdims=True)
        probs.append(e / jnp.where(den > 0, den, 1.0))
    for sl, p in zip(heads, probs):
        oc_ref[:, sl] = _dot(p.astype(BF16), vc).astype(oc_ref.dtype)
    imp = sum(probs[1:], probs[0])
    jj = lax.broadcasted_iota(jnp.int32, (ns, nc), 0)
    cc = lax.broadcasted_iota(jnp.int32, (ns, nc), 1) * CMP_STRIDE
    ov = jnp.where((cc < (jj + 1) * SEL_LEN) & (cc + CMP_LEN > jj * SEL_LEN), 1.0, 0.0).astype(BF16)
    imp_t = sum(_nt_dot(ov, part) for part in _split3(imp))
    jb = lax.broadcasted_iota(jnp.int32, (ns, tq), 0)
    tt = q0 + lax.broadcasted_iota(jnp.int32, (ns, tq), 1)
    cur = lax.shift_right_logical(tt, jnp.full_like(tt, SEL_SHIFT))
    forced = (jb == 0) | (jb == cur) | (jb == cur - 1)
    valid = jb * SEL_LEN <= tt
    score = jnp.where(valid, imp_t + jnp.where(forced, FORCE_BONUS, 0.0), NEG)
    rank = jnp.zeros((ns, tq), F32)
    for i in range(ns):
        row = score[i:i + 1, :]
        rank = rank + jnp.where(row > score, 1.0, jnp.where((row == score) & (jb > i), 1.0, 0.0))
    sel_t = jnp.where(valid & (rank < float(min(SEL_TOPN, ns))), 1.0, 0.0).astype(BF16)
    eye = jnp.where(lax.broadcasted_iota(jnp.int32, (tq, tq), 0) == lax.broadcasted_iota(jnp.int32, (tq, tq), 1),
                    1.0, 0.0).astype(BF16)
    sel_ref[...] = _nt_dot(eye, sel_t)


def _nsa_cmp(qkv, kc, vc, b, t, tq=1024):
    nc = kc.shape[2]
    ns = t // SEL_LEN
    nq = t // tq
    gw = NSA_GROUP * HEAD_DIM
    return pl.pallas_call(
        functools.partial(_nsa_cmp_kernel, tq=tq, nc=nc, ns=ns),
        grid=(b, NSA_KV_HEADS, nq),
        in_specs=[pl.BlockSpec((tq, gw), lambda i, k, j: (i * nq + j, k)),
                  pl.BlockSpec((None, None, nc, HEAD_DIM), lambda i, k, j: (i, k, 0, 0)),
                  pl.BlockSpec((None, None, nc, HEAD_DIM), lambda i, k, j: (i, k, 0, 0))],
        out_specs=[pl.BlockSpec((tq, gw), lambda i, k, j: (i * nq + j, k)),
                   pl.BlockSpec((None, None, tq, ns), lambda i, k, j: (i, k, j, 0))],
        out_shape=[jax.ShapeDtypeStruct((b * t, NSA_Q_W), BF16),
                   jax.ShapeDtypeStruct((b, NSA_KV_HEADS, t, ns), F32)],
        compiler_params=_cp("parallel", "parallel", "parallel"),
        name="nsa_cmp_select",
    )(qkv, kc, vc)


def _flash_update(s, v, bias, state, g, tq):
    m, l, acc = state
    tk = v.shape[0]
    if bias is not None:
        s = (s.reshape(g, tq, tk) + bias[None]).reshape(g * tq, tk)
    m_new = jnp.maximum(m, jnp.max(s, axis=-1, keepdims=True))
    alpha = jnp.exp(m - m_new)
    p = jnp.exp((s - m_new).astype(BF16))
    pv = _dot(p, jnp.concatenate([v, jnp.ones_like(v)], axis=1))
    return m_new, alpha * l + pv[:, HEAD_DIM:], alpha * acc + pv[:, :HEAD_DIM]


def _nsa_sel_win_kernel(q_ref, ks_ref, vs_ref, kw_ref, vw_ref, sel_ref, oc_ref, g_ref, o_ref, *, tq, tk, stack, ns):
    q0 = pl.program_id(1) * tq
    g = NSA_GROUP
    nkv = NSA_KV_HEADS
    per_tile = tk // SEL_LEN
    units = [(kv, h0) for kv in range(nkv) for h0 in range(0, g, stack)]
    kv_lanes = [slice(kv * HEAD_DIM, (kv + 1) * HEAD_DIM) for kv in range(nkv)]

    def q_lanes(kv, h):
        return slice((kv * g + h) * HEAD_DIM, (kv * g + h + 1) * HEAD_DIM)

    qs = [jnp.concatenate([q_ref[:, q_lanes(kv, h0 + j)] for j in range(stack)], axis=0) for kv, h0 in units]
    sel_bias = [((sel_ref[kv] - 1.0) * (-NEG)).astype(BF16) for kv in range(nkv)]
    ej = lax.broadcasted_iota(jnp.int32, (ns, tk), 0)
    eb = lax.broadcasted_iota(jnp.int32, (ns, tk), 1)
    eb = lax.shift_right_logical(eb, jnp.full_like(eb, SEL_SHIFT))

    def init():
        return tuple((jnp.full((stack * tq, 1), NEG, F32), jnp.zeros((stack * tq, HEAD_DIM), F32),
                      jnp.zeros((stack * tq, HEAD_DIM), F32)) for _ in units)

    def flash_all(k_ref, v_ref, k0, width, biases, states):
        scores = [_nt_dot(qs[u], k_ref[pl.ds(k0, width), kv_lanes[kv]]) for u, (kv, _) in enumerate(units)]
        return tuple(_flash_update(scores[u], v_ref[pl.ds(k0, width), kv_lanes[kv]], biases[kv], states[u], stack, tq)
                     for u, (kv, _) in enumerate(units))

    def sel_step(kt, states, causal):
        k0 = pl.multiple_of(kt * tk, tk)
        expand = jnp.where(ej == eb + kt * per_tile, 1.0, 0.0).astype(BF16)
        biases = [_dot(sel_bias[kv], expand) for kv in range(nkv)]
        if causal is not None:
            biases = [bias + causal for bias in biases]
        return flash_all(ks_ref, vs_ref, k0, tk, biases, states)

    kt_d = q0 // tk
    t_row = q0 + lax.broadcasted_iota(jnp.int32, (tq, tk), 0)
    col = kt_d * tk + lax.broadcasted_iota(jnp.int32, (tq, tk), 1)
    states = sel_step(kt_d, init(), jnp.where(col <= t_row, 0.0, NEG))
    states = lax.fori_loop(0, kt_d, lambda kt, st: sel_step(kt, st, None), states)
    o_s = [acc * (1.0 / l) for (_, l, acc) in states]

    def win_step(k0, states, bias):
        return flash_all(kw_ref, vw_ref, k0, tq, [bias] * nkv, states)

    ri = lax.broadcasted_iota(jnp.int32, (tq, tq), 0)
    ci = lax.broadcasted_iota(jnp.int32, (tq, tq), 1)
    states = win_step(pl.multiple_of(q0, tq), init(), jnp.where(ci <= ri, 0.0, NEG))
    n_full = jnp.minimum(WINDOW // tq - 1, q0 // tq)
    states = lax.fori_loop(0, n_full, lambda i, st: win_step(pl.multiple_of(q0 - (i + 1) * tq, tq), st, None),
                           states)
    edge_bias = jnp.where(ci > ri, 0.0, NEG) + jnp.where(q0 >= WINDOW, 0.0, NEG)
    states = win_step(pl.multiple_of(jnp.maximum(q0 - WINDOW, 0), tq), states, edge_bias)
    o_w = [acc * (1.0 / l) for (_, l, acc) in states]

    for u, (kv, h0) in enumerate(units):
        gates = g_ref[:, kv * LANES:(kv + 1) * LANES]
        for j in range(stack):
            h = h0 + j
            sl = q_lanes(kv, h)
            rows = slice(j * tq, (j + 1) * tq)
            out = (gates[:, 3 * h:3 * h + 1] * oc_ref[:, sl].astype(F32)
                   + gates[:, 3 * h + 1:3 * h + 2] * o_s[u][rows] + gates[:, 3 * h + 2:3 * h + 3] * o_w[u][rows])
            o_ref[:, sl] = out.astype(o_ref.dtype)


def _nsa_sel_win(qkv, sel, oc, gates, b, t, tq=512, tk=512, stack=4):
    nq = t // tq
    ns = t // SEL_LEN
    kv_block = NSA_Q_W // NSA_KV_W

    def kv_spec(which):
        return pl.BlockSpec((t, NSA_KV_W), lambda i, j: (i, kv_block + which))

    return pl.pallas_call(
        functools.partial(_nsa_sel_win_kernel, tq=tq, tk=tk, stack=stack, ns=ns),
        grid=(b, nq),
        in_specs=[pl.BlockSpec((tq, NSA_Q_W), lambda i, j: (i * nq + j, 0)),
                  kv_spec(2), kv_spec(3), kv_spec(4), kv_spec(5),
                  pl.BlockSpec((None, NSA_KV_HEADS, tq, ns), lambda i, j: (i, 0, j, 0)),
                  pl.BlockSpec((tq, NSA_Q_W), lambda i, j: (i * nq + j, 0)),
                  pl.BlockSpec((tq, NSA_KV_HEADS * LANES), lambda i, j: (i * nq + j, 0))],
        out_specs=pl.BlockSpec((tq, NSA_Q_W), lambda i, j: (i * nq + j, 0)),
        out_shape=jax.ShapeDtypeStruct((b * t, NSA_Q_W), BF16),
        compiler_params=_cp("parallel", "parallel"),
        name="nsa_select_window",
    )(qkv, qkv, qkv, qkv, qkv, sel, oc, gates)


def _stickbreak_kernel(q_ref, k_ref, v_ref, o_ref, *, tile, nh):
    q0 = pl.program_id(2) * tile
    ri = lax.broadcasted_iota(jnp.int32, (tile, tile), 0)
    ci = lax.broadcasted_iota(jnp.int32, (tile, tile), 1)
    strict = ci < ri
    tri = jnp.where(ri > ci, 1.0, 0.0).astype(BF16)
    lanes = [slice(h * HEAD_DIM, (h + 1) * HEAD_DIM) for h in range(nh)]

    def step(k0, diag, carry):
        zs = [_nt_dot(q_ref[:, lanes[h]], k_ref[pl.ds(k0, tile), lanes[h]]) for h in range(nh)]
        log_bs, sufs, sums = [], [], []
        for z in zs:
            log_b = jnp.minimum(z, 0.0) - jnp.log(1.0 + jnp.exp(-jnp.abs(z)))
            log_1m = log_b - z
            if diag:
                log_1m = jnp.where(strict, log_1m, 0.0)
            hi = log_1m.astype(BF16)
            lo = (log_1m - hi.astype(F32)).astype(BF16)
            log_bs.append(log_b)
            sufs.append(_dot(hi, tri) + _dot(lo, tri))
            sums.append(jnp.sum(log_1m, axis=-1, keepdims=True))
        out = []
        for h in range(nh):
            later, acc = carry[2 * h], carry[2 * h + 1]
            a = jnp.exp(log_bs[h] + (sufs[h] + later))
            if diag:
                a = jnp.where(strict, a, 0.0)
            out.extend((later + sums[h], acc + _dot(a.astype(BF16), v_ref[pl.ds(k0, tile), lanes[h]])))
        return tuple(out)

    zero = (jnp.zeros((tile, 1), F32), jnp.zeros((tile, HEAD_DIM), F32)) * nh
    carry = step(pl.multiple_of(q0, tile), True, zero)

    def live(state):
        i, c = state
        top = c[0]
        for h in range(1, nh):
            top = jnp.maximum(top, c[2 * h])
        return (i < q0 // tile) & (jnp.max(top) > F32_EXP_UNDERFLOW)

    def advance(state):
        i, c = state
        return i + 1, step(pl.multiple_of(q0 - (i + 1) * tile, tile), False, c)

    _, carry = lax.while_loop(live, advance, (jnp.int32(0), carry))
    for h in range(nh):
        o_ref[:, lanes[h]] = carry[2 * h + 1].astype(o_ref.dtype)


def _stickbreak(qkv, b, t, tile=256, nh=4):
    nq = t // tile
    w = nh * HEAD_DIM
    nhb = SB_HEADS // nh
    return pl.pallas_call(
        functools.partial(_stickbreak_kernel, tile=tile, nh=nh),
        grid=(b, nhb, nq),
        in_specs=[pl.BlockSpec((tile, w), lambda i, h, j: (i * nq + j, h)),
                  pl.BlockSpec((t, w), lambda i, h, j: (i, nhb + h)),
                  pl.BlockSpec((t, w), lambda i, h, j: (i, 2 * nhb + h))],
        out_specs=pl.BlockSpec((tile, w), lambda i, h, j: (i * nq + j, h)),
        out_shape=jax.ShapeDtypeStruct((b * t, SB_W), BF16),
        compiler_params=_cp("parallel", "parallel", "parallel"),
        name="stickbreak",
    )(qkv, qkv, qkv)


def _cross_kernel(x_ref, g_ref, wq_ref, kv_ref, wo_ref, o_ref):
    x = x_ref[...]
    h = _rms(x, g_ref[...]).astype(BF16)
    q = _dot(h, wq_ref[...]).astype(BF16)
    xw = X_HEADS * HEAD_DIM
    outs = []
    for hd in range(X_HEADS):
        sl = slice(hd * HEAD_DIM, (hd + 1) * HEAD_DIM)
        s = _nt_dot(q[:, sl], kv_ref[:, sl]) * SCALE
        e = jnp.exp(s - jnp.max(s, axis=-1, keepdims=True))
        p = e / jnp.sum(e, axis=-1, keepdims=True)
        outs.append(_dot(p.astype(BF16), kv_ref[:, xw + hd * HEAD_DIM:xw + (hd + 1) * HEAD_DIM]).astype(BF16))
    o_ref[...] = x + _dot(jnp.concatenate(outs, axis=1), wo_ref[...])


def _cross(x, g, wq, kv, wo, b, t, tq=1024):
    n, d = x.shape
    nq = t // tq
    mem_len = kv.shape[0] // b
    return pl.pallas_call(
        _cross_kernel,
        grid=(b, nq),
        in_specs=[pl.BlockSpec((tq, d), lambda i, j: (i * nq + j, 0)),
                  pl.BlockSpec((1, d), lambda i, j: (0, 0)),
                  pl.BlockSpec(wq.shape, lambda i, j: (0, 0)),
                  pl.BlockSpec((mem_len, kv.shape[1]), lambda i, j: (i, 0)),
                  pl.BlockSpec(wo.shape, lambda i, j: (0, 0))],
        out_specs=pl.BlockSpec((tq, d), lambda i, j: (i * nq + j, 0)),
        out_shape=jax.ShapeDtypeStruct((n, d), F32),
        compiler_params=_cp("parallel", "parallel"),
        name="cross_attention",
    )(x, g.reshape(1, d), wq, kv, wo)


def _to_row_tiles(ref, val):
    rows, width = val.shape
    c_per = width // LANES
    for c in range(c_per):
        ref[pl.ds(c, rows, stride=c_per), :] = val[:, c * LANES:(c + 1) * LANES]


def _from_row_tiles(ref, c_per):
    rows = ref.shape[0] // c_per
    return jnp.concatenate([ref[pl.ds(c, rows, stride=c_per), :] for c in range(c_per)], axis=1)


def _router_kernel(x_ref, g_ref, w_ref, b_ref, h_ref, r_ref):
    h = _rms(x_ref[...], g_ref[...])
    _to_row_tiles(h_ref, h)
    h_hi = h.astype(BF16)
    h_mid = (h - h_hi.astype(F32)).astype(BF16)
    two = _dot(h_hi, w_ref[...])
    logits = two[:, :LANES] + two[:, LANES:] + _dot(h_mid, w_ref[:, :LANES]) + b_ref[...]
    lane = lax.broadcasted_iota(jnp.int32, logits.shape, 1).astype(F32)
    big = float(LANES)
    is_g = lane < N_GROUPS
    lg = jnp.where(is_g, logits, NEG)
    mg = jnp.max(lg, axis=-1, keepdims=True)
    p_gtop = 1.0 / jnp.sum(jnp.where(is_g, jnp.exp(lg - mg), 0.0), axis=-1, keepdims=True)
    g_top = jnp.min(jnp.where(lg == mg, lane, big), axis=-1, keepdims=True)
    lo = N_GROUPS + EXPERTS_PER_GROUP * g_top
    in_grp = (lane >= lo) & (lane < lo + EXPERTS_PER_GROUP)
    le = jnp.where(in_grp, logits, NEG)
    ee = jnp.where(in_grp, jnp.exp(le - jnp.max(le, axis=-1, keepdims=True)), 0.0)
    p_in = ee / jnp.sum(ee, axis=-1, keepdims=True)
    p1 = jnp.where(in_grp, p_in, -1.0)
    m1 = jnp.max(p1, axis=-1, keepdims=True)
    i1 = jnp.min(jnp.where(p1 == m1, lane, big), axis=-1, keepdims=True)
    p2 = jnp.where(lane == i1, -1.0, p1)
    m2 = jnp.max(p2, axis=-1, keepdims=True)
    i2 = jnp.min(jnp.where(p2 == m2, lane, big), axis=-1, keepdims=True)
    tot = m1 + m2
    r_ref[...] = jnp.where(lane == 0, i1 - N_GROUPS,
                           jnp.where(lane == 1, i2 - N_GROUPS,
                                     jnp.where(lane == 2, m1 / tot * p_gtop,
                                               jnp.where(lane == 3, m2 / tot * p_gtop, 0.0))))


def _router(x, g, w_r, b_r, tm=1024):
    n, d = x.shape
    return pl.pallas_call(
        _router_kernel,
        grid=(n // tm,),
        in_specs=[pl.BlockSpec((tm, d), lambda i: (i, 0)), pl.BlockSpec((1, d), lambda i: (0, 0)),
                  pl.BlockSpec((d, 2 * LANES), lambda i: (0, 0)), pl.BlockSpec((1, LANES), lambda i: (0, 0))],
        out_specs=[pl.BlockSpec((tm * (d // LANES), LANES), lambda i: (i, 0)),
                   pl.BlockSpec((tm, LANES), lambda i: (i, 0))],
        out_shape=[jax.ShapeDtypeStruct((n * (d // LANES), LANES), F32), jax.ShapeDtypeStruct((n, LANES), F32)],
        compiler_params=_cp("parallel"),
        name="moe_router",
    )(x, g.reshape(1, d), w_r, b_r)


ROW_CHUNKS = 16
WEIGHT_DMA_PRIORITY = 0
WEIGHT_SLOTS = 2


def _wait_rows(src_hbm, dst, sem):
    pltpu.make_async_copy(src_hbm.at[pl.ds(0, dst.shape[0]), :], dst, sem).wait()


def _row_copy(src_hbm, dst, sem, src_row, dst_row):
    c = ROW_CHUNKS
    return pltpu.make_async_copy(src_hbm.at[pl.ds(pl.multiple_of(src_row * c, c), c), :],
                                 dst.at[pl.ds(pl.multiple_of(dst_row * c, c), c), :], sem)


def _expert_kernel(idx_ref, ord_ref, oe_ref, cnt_ref, src_hbm, wg_hbm, wu_hbm, wd_hbm, o_ref,
                   xbuf, sems, wg_f, wu_f, wd_f, wsems, wg_s, wu_s, wd_s, *, tm):
    i = pl.program_id(0)
    nt = cnt_ref[0]
    n_ord = cnt_ref[1]
    live = i < nt
    slot = lax.rem(i, 2)

    def gather(tile, s):
        def issue(r2, c):
            for queue in range(2):
                r = 2 * r2 + queue
                _row_copy(src_hbm, xbuf.at[s], sems.at[s], idx_ref[tile * tm + r], r).start(priority=queue)
            return c
        lax.fori_loop(0, tm // 2, issue, 0, unroll=4)

    def weight_copies(k, s):
        e = oe_ref[k]
        return (pltpu.make_async_copy(wg_hbm.at[e], wg_f.at[s], wsems.at[s, 0]),
                pltpu.make_async_copy(wu_hbm.at[e], wu_f.at[s], wsems.at[s, 1]),
                pltpu.make_async_copy(wd_hbm.at[e], wd_f.at[s], wsems.at[s, 2]))

    ahead = WEIGHT_SLOTS - 1

    @pl.when(i == 0)
    def _():
        gather(0, 0)
        for j in range(ahead):
            @pl.when(j < n_ord)
            def _():
                for cp in weight_copies(j, j):
                    cp.start(priority=WEIGHT_DMA_PRIORITY)

    @pl.when(i + 1 < nt)
    def _():
        gather(i + 1, 1 - slot)

    k = ord_ref[i]
    fresh = live & ((i == 0) | (k != ord_ref[jnp.maximum(i - 1, 0)]))
    ws = lax.rem(k, WEIGHT_SLOTS)

    @pl.when(fresh)
    def _():
        for cp in weight_copies(k, ws):
            cp.wait()

        @pl.when(k + ahead < n_ord)
        def _():
            for cp in weight_copies(k + ahead, lax.rem(k + ahead, WEIGHT_SLOTS)):
                cp.start(priority=WEIGHT_DMA_PRIORITY)

        wg_s[...] = wg_f[ws].astype(BF16)
        wu_s[...] = wu_f[ws].astype(BF16)
        wd_s[...] = wd_f[ws].astype(BF16)

    @pl.when(live)
    def _():
        _wait_rows(src_hbm, xbuf.at[slot], sems.at[slot])
        x = _from_row_tiles(xbuf.at[slot], ROW_CHUNKS).astype(BF16)
        a = _dot(x, wg_s[...])
        u = _dot(x, wu_s[...])
        act = a * (1.0 / (1.0 + jnp.exp(-a))) * u
        _to_row_tiles(o_ref, _dot(act.astype(BF16), wd_s[...]))

    @pl.when(jnp.logical_not(live))
    def _():
        o_ref[...] = jnp.zeros_like(o_ref)


def _experts(src, tok_of_row, w_g, w_u, w_d, tile_ord, ord_expert, counts, tm=EXPERT_TILE):
    c = ROW_CHUNKS
    n_tiles_max = tok_of_row.shape[0] // tm
    d = c * LANES
    f = w_g.shape[2]
    any_spec = pl.BlockSpec(memory_space=pl.ANY)
    return pl.pallas_call(
        functools.partial(_expert_kernel, tm=tm),
        grid_spec=pltpu.PrefetchScalarGridSpec(
            num_scalar_prefetch=4,
            grid=(n_tiles_max,),
            in_specs=[any_spec, any_spec, any_spec, any_spec],
            out_specs=pl.BlockSpec((tm * c, LANES), lambda i, *_: (i, 0)),
            scratch_shapes=[pltpu.VMEM((2, tm * c, LANES), F32), pltpu.SemaphoreType.DMA((2,)),
                            pltpu.VMEM((WEIGHT_SLOTS, d, f), F32), pltpu.VMEM((WEIGHT_SLOTS, d, f), F32),
                            pltpu.VMEM((WEIGHT_SLOTS, f, d), F32), pltpu.SemaphoreType.DMA((WEIGHT_SLOTS, 3)),
                            pltpu.VMEM((d, f), BF16), pltpu.VMEM((d, f), BF16), pltpu.VMEM((f, d), BF16)]),
        out_shape=jax.ShapeDtypeStruct((n_tiles_max * tm * c, LANES), F32),
        compiler_params=_cp("arbitrary"),
        name="moe_experts",
    )(tok_of_row, tile_ord, ord_expert, counts, src, w_g, w_u, w_d)


def _combine_kernel(p0_ref, p1_ref, ys_hbm, x_ref, r_ref, g_ref, o_ref, buf, sems, *, tm):
    i = pl.program_id(0)
    slot = lax.rem(i, 2)

    def fetch(tile, s):
        def issue(r, c):
            _row_copy(ys_hbm, buf.at[s, 0], sems.at[s], p0_ref[tile * tm + r], r).start()
            _row_copy(ys_hbm, buf.at[s, 1], sems.at[s], p1_ref[tile * tm + r], r).start(priority=1)
            return c
        lax.fori_loop(0, tm, issue, 0, unroll=8)

    @pl.when(i == 0)
    def _():
        fetch(0, 0)

    @pl.when(i + 1 < pl.num_programs(0))
    def _():
        fetch(i + 1, 1 - slot)

    _wait_rows(ys_hbm, buf.at[slot, 0], sems.at[slot])
    _wait_rows(ys_hbm, buf.at[slot, 1], sems.at[slot])
    route = r_ref[...]
    y = (route[:, 2:3] * _from_row_tiles(buf.at[slot, 0], ROW_CHUNKS)
         + route[:, 3:4] * _from_row_tiles(buf.at[slot, 1], ROW_CHUNKS))
    o_ref[...] = _rms(x_ref[...] + y, g_ref[...])


def _combine(ys, pos0, pos1, x, route, g, tm=256):
    n, d = x.shape
    return pl.pallas_call(
        functools.partial(_combine_kernel, tm=tm),
        grid_spec=pltpu.PrefetchScalarGridSpec(
            num_scalar_prefetch=2,
            grid=(n // tm,),
            in_specs=[pl.BlockSpec(memory_space=pl.ANY),
                      pl.BlockSpec((tm, d), lambda i, p0, p1: (i, 0)),
                      pl.BlockSpec((tm, LANES), lambda i, p0, p1: (i, 0)),
                      pl.BlockSpec((1, d), lambda i, p0, p1: (0, 0))],
            out_specs=pl.BlockSpec((tm, d), lambda i, p0, p1: (i, 0)),
            scratch_shapes=[pltpu.VMEM((2, 2, tm * ROW_CHUNKS, LANES), F32), pltpu.SemaphoreType.DMA((2,))]),
        out_shape=jax.ShapeDtypeStruct((n, d), F32),
        compiler_params=_cp("arbitrary"),
        name="moe_combine_norm",
    )(pos0, pos1, ys, x, route, g.reshape(1, d))


def _dispatch_plan(e1, e2, tm=EXPERT_TILE):
    n = e1.shape[0]
    n_tiles_max = (2 * n) // tm + N_EXPERTS
    rows = n_tiles_max * tm
    e_flat = jnp.concatenate([e1, e2])
    tok = jnp.concatenate([jnp.arange(n, dtype=jnp.int32)] * 2)
    onehot = (e_flat[:, None] == jnp.arange(N_EXPERTS, dtype=jnp.int32)[None, :]).astype(jnp.int32)
    csum = jnp.cumsum(onehot, axis=0)
    count = csum[-1]
    rank = jnp.sum((csum - 1) * onehot, axis=1)
    tiles_per = (count + tm - 1) // tm
    tile_end = jnp.cumsum(tiles_per)
    row_start = (tile_end - tiles_per) * tm
    dest = (row_start[e_flat] + rank).astype(jnp.int32)
    tok_of_row = jnp.zeros((rows,), jnp.int32).at[dest].set(tok)
    n_tiles = tile_end[-1].astype(jnp.int32)
    tile_id = jnp.minimum(jnp.arange(n_tiles_max, dtype=jnp.int32), n_tiles - 1)
    tile_expert = jnp.sum((tile_id[:, None] >= tile_end[None, :]).astype(jnp.int32), axis=1)
    used = jnp.cumsum((tiles_per > 0).astype(jnp.int32))
    tile_ord = (used[tile_expert] - 1).astype(jnp.int32)
    ordinals = jnp.arange(N_EXPERTS, dtype=jnp.int32)
    ord_expert = jnp.minimum(jnp.sum((used[None, :] <= ordinals[:, None]).astype(jnp.int32), axis=1),
                             N_EXPERTS - 1).astype(jnp.int32)
    counts = jnp.stack([n_tiles, used[-1]]).astype(jnp.int32)
    return tok_of_row, tile_ord, ord_expert, counts, dest[:n], dest[n:]


def _rope_tables(positions):
    half = ROT_DIM // 2
    inv = ROPE_THETA ** (-jnp.arange(half, dtype=F32) * (2.0 / ROT_DIM))
    ang = positions.astype(F32).reshape(-1, 1) * inv[None, :]
    cos, sin = jnp.cos(ang), jnp.sin(ang)
    n = ang.shape[0]
    c = jnp.concatenate([cos, cos, jnp.ones((n, LANES - ROT_DIM), F32)], axis=1)
    s1 = jnp.concatenate([-sin, jnp.zeros((n, LANES - half), F32)], axis=1)
    s2 = jnp.concatenate([jnp.zeros((n, half), F32), sin, jnp.zeros((n, LANES - ROT_DIM), F32)], axis=1)
    return c, s1, s2


def _layer(x, mem, positions, g_mix, w_in, cmp_pe_k, cmp_w1_k, cmp_w2_k, cmp_pe_v, cmp_w1_v, cmp_w2_v,
           w_br_nsa, w_br_sb, w_o, g_cross, g_mem, w_cq, w_ck, w_cv, w_co,
           g_moe, w_rg, b_rg, w_re, b_re, w_eg, w_eu, w_ed):
    b, t, d = x.shape
    n = b * t
    xf = x.reshape(n, d)

    g0 = NSA_W
    w_t = w_in.T
    w_nsa = _cast_rows(w_t, 0, g0)
    w_rest = _cast_rows(w_t, g0 + NSA_GATES, w_t.shape[0] - g0 - NSA_GATES)
    w_g = w_t[g0:g0 + NSA_GATES].reshape(NSA_KV_HEADS, 3 * NSA_GROUP, d)
    w_g = jnp.pad(w_g, ((0, 0), (0, LANES - 3 * NSA_GROUP), (0, 0))).reshape(NSA_KV_HEADS * LANES, d).astype(BF16)
    h = _rmsnorm(xf, g_mix, BF16)
    c, s1, s2 = _rope_tables(positions)
    row_spec = pl.BlockSpec((2048, LANES), lambda i, j: (i, 0))
    qkv_nsa = _mm(functools.partial(_mm_nsa_kernel, tn=256), h, w_nsa, NSA_W, 0, 2048, 256, BF16, w_t=True,
                  extra=(c, s1, s2), extra_specs=(row_spec, row_spec, row_spec), name="in_proj_nsa")
    qkv_sb = _mm(functools.partial(_mm_sb_kernel, q_tiles=SB_W // WIDE_TN), h, w_rest, 3 * SB_W, 0,
                 1024, WIDE_TN, BF16, w_t=True, name="in_proj_sb")
    gates = _mm(_mm_sigmoid_kernel, h, w_rest, 2 * d, 3 * SB_W // WIDE_TN, 1024, WIDE_TN, BF16, w_t=True,
                name="in_proj_gates")
    nsa_gates = _mm(_mm_sigmoid_kernel, h, w_g, NSA_KV_HEADS * LANES, 0, 1024, NSA_KV_HEADS * LANES, F32,
                    w_t=True, name="in_proj_nsa_gates")

    def chunked(col0):
        tok = qkv_nsa[:, col0:col0 + NSA_KV_W].reshape(b, t, NSA_KV_HEADS, HEAD_DIM).transpose(0, 2, 1, 3)
        return tok.reshape(b, NSA_KV_HEADS, t // CMP_STRIDE, CMP_STRIDE * HEAD_DIM)

    kc = _compress(chunked(NSA_Q_W), cmp_pe_k, cmp_w1_k, cmp_w2_k)
    vc = _compress(chunked(NSA_Q_W + NSA_KV_W), cmp_pe_v, cmp_w1_v, cmp_w2_v)
    o_c, sel = _nsa_cmp(qkv_nsa, kc, vc, b, t)
    o_nsa = _nsa_sel_win(qkv_nsa, sel, o_c, nsa_gates, b, t)
    o_sb = _stickbreak(qkv_sb, b, t)
    merged = _merge(o_nsa, o_sb, w_br_nsa.astype(BF16), w_br_sb.astype(BF16), gates)
    x1 = _mm(_mm_resid_kernel, merged, w_o.astype(BF16), d, 0, 1024, WIDE_TN, F32,
             extra=(xf,), extra_specs=(pl.BlockSpec((1024, WIDE_TN), lambda i, j: (i, j)),), name="out_proj")

    w_kv = jnp.concatenate([w_ck, w_cv], axis=1).astype(BF16)
    kv = _norm_mm(mem.reshape(-1, d), g_mem, w_kv, BF16)
    x2 = _cross(x1, g_cross, w_cq.astype(BF16), kv, w_co.astype(BF16), b, t)

    w_r = jnp.pad(jnp.concatenate([w_rg, w_re], axis=1), ((0, 0), (0, LANES - N_GROUPS - N_EXPERTS)))
    b_r = jnp.pad(jnp.concatenate([b_rg, b_re]), (0, LANES - N_GROUPS - N_EXPERTS)).reshape(1, LANES)
    w_r_hi = w_r.astype(BF16)
    w_r2 = jnp.concatenate([w_r_hi, (w_r - w_r_hi.astype(F32)).astype(BF16)], axis=1)
    hm, route = _router(x2, g_moe, w_r2, b_r)
    e1 = route[:, 0].astype(jnp.int32)
    e2 = route[:, 1].astype(jnp.int32)
    tok_of_row, tile_ord, ord_expert, counts, pos0, pos1 = _dispatch_plan(e1, e2)
    f = w_eg.shape[-1]
    ys = _experts(hm, tok_of_row, w_eg.reshape(N_EXPERTS, d, f), w_eu.reshape(N_EXPERTS, d, f),
                  w_ed.reshape(N_EXPERTS, f, d), tile_ord, ord_expert, counts)
    return ys, pos0, pos1, x2, route


def kernel(x, mem, positions, g_mix, w_in, cmp_pe_k, cmp_w1_k, cmp_w2_k, cmp_pe_v, cmp_w1_v, cmp_w2_v, w_br_nsa, w_br_sb, w_o, g_cross, g_mem, w_cq, w_ck, w_cv, w_co, g_moe, w_rg, b_rg, w_re, b_re, w_eg, w_eu, w_ed, g_final):
    b, t, d = x.shape
    assert w_in.shape[0] == 1, "one layer"
    ys, pos0, pos1, x2, route = _layer(
        x, mem, positions, g_mix[0], w_in[0], cmp_pe_k[0], cmp_w1_k[0], cmp_w2_k[0], cmp_pe_v[0], cmp_w1_v[0],
        cmp_w2_v[0], w_br_nsa[0], w_br_sb[0], w_o[0], g_cross[0], g_mem[0], w_cq[0], w_ck[0], w_cv[0], w_co[0],
        g_moe[0], w_rg[0], b_rg[0], w_re[0], b_re[0], w_eg[0], w_eu[0], w_ed[0])
    return _combine(ys, pos0, pos1, x2, route, g_final).reshape(b, t, d)
```

```python
import functools

import jax
import jax.numpy as jnp
from jax import lax
from jax.experimental import pallas as pl
from jax.experimental.pallas import tpu as pltpu

F32 = jnp.float32
BF16 = jnp.bfloat16

HEAD_DIM = 128
NSA_HEADS = 8
NSA_KV_HEADS = 2
NSA_GROUP = NSA_HEADS // NSA_KV_HEADS
SB_HEADS = 8
ROT_DIM = HEAD_DIM // 4
ROPE_THETA = 500000.0
CMP_LEN = 32
CMP_STRIDE = 16
SEL_LEN = 64
SEL_SHIFT = 6
SEL_TOPN = 16
WINDOW = 512
FORCE_BONUS = 1000.0
X_HEADS = 4
N_GROUPS = 4
EXPERTS_PER_GROUP = 8
N_EXPERTS = N_GROUPS * EXPERTS_PER_GROUP
EPS = 1e-6
SCALE = HEAD_DIM ** -0.5
NEG = -1e30
F32_EXP_UNDERFLOW = -104.0

NSA_Q_W = NSA_HEADS * HEAD_DIM
NSA_KV_W = NSA_KV_HEADS * HEAD_DIM
NSA_W = NSA_Q_W + 6 * NSA_KV_W
NSA_GATES = 3 * NSA_HEADS
SB_W = SB_HEADS * HEAD_DIM
LANES = 128
VMEM_LIMIT = 56 * 1024 * 1024

EXPERT_TILE = 256
WIDE_TN = 1024


def _cp(*sem, vmem=VMEM_LIMIT):
    return pltpu.CompilerParams(dimension_semantics=sem, vmem_limit_bytes=vmem)


def _nt_dot(a, b):
    return lax.dot_general(a, b, (((1,), (1,)), ((), ())), preferred_element_type=F32)


def _dot(a, b):
    return jnp.dot(a, b, preferred_element_type=F32)


def _split3(x):
    hi = x.astype(BF16)
    r1 = x - hi.astype(F32)
    mid = r1.astype(BF16)
    lo = (r1 - mid.astype(F32)).astype(BF16)
    return hi, mid, lo


def _rms(x, g):
    return x * lax.rsqrt(jnp.mean(x * x, axis=-1, keepdims=True) + EPS) * g


def _rmsnorm_kernel(x_ref, g_ref, o_ref):
    o_ref[...] = _rms(x_ref[...].astype(F32), g_ref[...]).astype(o_ref.dtype)


def _rmsnorm(x, g, out_dtype, tm=512):
    n, d = x.shape
    return pl.pallas_call(
        _rmsnorm_kernel,
        grid=(n // tm,),
        in_specs=[pl.BlockSpec((tm, d), lambda i: (i, 0)), pl.BlockSpec((1, d), lambda i: (0, 0))],
        out_specs=pl.BlockSpec((tm, d), lambda i: (i, 0)),
        out_shape=jax.ShapeDtypeStruct((n, d), out_dtype),
        compiler_params=_cp("parallel"),
        name="rmsnorm",
    )(x, g.reshape(1, d))


def _cast_rows_kernel(w_ref, o_ref):
    o_ref[...] = w_ref[...].astype(o_ref.dtype)


def _cast_rows(w_t, row0, n_rows, tn=512):
    k = w_t.shape[1]
    assert n_rows % tn == 0 and row0 % 8 == 0
    return pl.pallas_call(
        _cast_rows_kernel,
        grid=(n_rows // tn,),
        in_specs=[pl.BlockSpec((pl.Element(tn), pl.Element(k)), lambda j: ((row0 // 8 + j * (tn // 8)) * 8, 0))],
        out_specs=pl.BlockSpec((tn, k), lambda j: (j, 0)),
        out_shape=jax.ShapeDtypeStruct((n_rows, k), BF16),
        compiler_params=_cp("parallel"),
        name="cast_rows",
    )(w_t)


def _rope_half(x, c, s1, s2):
    return x * c + pltpu.roll(x, LANES - ROT_DIM // 2, 1) * s1 + pltpu.roll(x, ROT_DIM // 2, 1) * s2


def _mm_nsa_kernel(a_ref, w_ref, c_ref, s1_ref, s2_ref, o_ref, *, tn):
    j = pl.program_id(1)
    acc = _nt_dot(a_ref[...], w_ref[...])
    is_rope = (j < 5) | (j == 6) | (j == 8)

    @pl.when(is_rope)
    def _():
        qs = jnp.where(j < NSA_Q_W // tn, SCALE, 1.0)
        c, s1, s2 = c_ref[...] * qs, s1_ref[...] * qs, s2_ref[...] * qs
        for hh in range(tn // LANES):
            sl = slice(hh * LANES, (hh + 1) * LANES)
            o_ref[:, sl] = _rope_half(acc[:, sl], c, s1, s2).astype(o_ref.dtype)

    @pl.when(jnp.logical_not(is_rope))
    def _():
        o_ref[...] = acc.astype(o_ref.dtype)


def _mm_sb_kernel(a_ref, w_ref, o_ref, *, q_tiles):
    acc = _nt_dot(a_ref[...], w_ref[...])
    is_q = pl.program_id(1) < q_tiles
    o_ref[...] = (acc * jnp.where(is_q, SCALE, 1.0)).astype(o_ref.dtype)


def _mm_sigmoid_kernel(a_ref, w_ref, o_ref):
    o_ref[...] = jax.nn.sigmoid(_nt_dot(a_ref[...], w_ref[...])).astype(o_ref.dtype)


def _mm_resid_kernel(a_ref, w_ref, x_ref, o_ref):
    o_ref[...] = x_ref[...] + _dot(a_ref[...], w_ref[...])


def _mm(kern, a, w, n_cols, col_block0, tm, tn, out_dtype, extra=(), extra_specs=(), name="mm", w_t=False):
    m, k = a.shape
    w_spec = (pl.BlockSpec((tn, k), lambda i, j: (col_block0 + j, 0)) if w_t
              else pl.BlockSpec((k, tn), lambda i, j: (0, col_block0 + j)))
    return pl.pallas_call(
        kern,
        grid=(m // tm, n_cols // tn),
        in_specs=[pl.BlockSpec((tm, k), lambda i, j: (i, 0)), w_spec, *extra_specs],
        out_specs=pl.BlockSpec((tm, tn), lambda i, j: (i, j)),
        out_shape=jax.ShapeDtypeStruct((m, n_cols), out_dtype),
        compiler_params=_cp("parallel", "arbitrary"),
        name=name,
    )(a, w, *extra)


def _merge_kernel(a1_ref, a2_ref, w1_ref, w2_ref, ga_ref, gb_ref, o_ref):
    ya = _dot(a1_ref[...], w1_ref[...])
    yb = _dot(a2_ref[...], w2_ref[...])
    o_ref[...] = (ga_ref[...].astype(F32) * ya + gb_ref[...].astype(F32) * yb).astype(o_ref.dtype)


def _merge(o_nsa, o_sb, w1, w2, gates, tm=1024, tn=WIDE_TN):
    n, k = o_nsa.shape
    d = w1.shape[1]
    nb = d // tn
    return pl.pallas_call(
        _merge_kernel,
        grid=(n // tm, nb),
        in_specs=[pl.BlockSpec((tm, k), lambda i, j: (i, 0)), pl.BlockSpec((tm, k), lambda i, j: (i, 0)),
                  pl.BlockSpec((k, tn), lambda i, j: (0, j)), pl.BlockSpec((k, tn), lambda i, j: (0, j)),
                  pl.BlockSpec((tm, tn), lambda i, j: (i, j)), pl.BlockSpec((tm, tn), lambda i, j: (i, nb + j))],
        out_specs=pl.BlockSpec((tm, tn), lambda i, j: (i, j)),
        out_shape=jax.ShapeDtypeStruct((n, d), BF16),
        compiler_params=_cp("parallel", "arbitrary"),
        name="merge",
    )(o_nsa, o_sb, w1, w2, gates, gates)


def _norm_mm_kernel(x_ref, g_ref, w_ref, o_ref):
    h = _rms(x_ref[...].astype(F32), g_ref[...]).astype(BF16)
    o_ref[...] = _dot(h, w_ref[...]).astype(o_ref.dtype)


def _norm_mm(x, g, w, out_dtype, tm=256):
    n, d = x.shape
    nc = w.shape[1]
    return pl.pallas_call(
        _norm_mm_kernel,
        grid=(n // tm,),
        in_specs=[pl.BlockSpec((tm, d), lambda i: (i, 0)), pl.BlockSpec((1, d), lambda i: (0, 0)),
                  pl.BlockSpec((d, nc), lambda i: (0, 0))],
        out_specs=pl.BlockSpec((tm, nc), lambda i: (i, 0)),
        out_shape=jax.ShapeDtypeStruct((n, nc), out_dtype),
        compiler_params=_cp("parallel"),
        name="norm_mm",
    )(x, g.reshape(1, d), w)


def _compress_kernel(x_ref, pe_ref, w1_ref, w2_ref, o_ref):
    x = x_ref[...].astype(F32)
    half = x.shape[1]
    xa = (x + pe_ref[0:1, :]).astype(BF16)
    xb = (x + pe_ref[1:2, :]).astype(BF16)
    y1 = _dot(xa, w1_ref[0:half, :].astype(BF16))
    y2 = _dot(xb, w1_ref[half:2 * half, :].astype(BF16))
    pre = y1 + pltpu.roll(y2, x.shape[0] - 1, 0)
    act = pre * (1.0 / (1.0 + jnp.exp(-pre)))
    o_ref[...] = _dot(act.astype(BF16), w2_ref[...].astype(BF16)).astype(o_ref.dtype)


def _compress(tok, pe, w1, w2):
    b, hkv, nchunk, width = tok.shape
    return pl.pallas_call(
        _compress_kernel,
        grid=(b, hkv),
        in_specs=[pl.BlockSpec((None, None, nchunk, width), lambda i, j: (i, j, 0, 0)),
                  pl.BlockSpec((2, width), lambda i, j: (0, 0)),
                  pl.BlockSpec((2 * width, HEAD_DIM), lambda i, j: (0, 0)),
                  pl.BlockSpec((HEAD_DIM, HEAD_DIM), lambda i, j: (0, 0))],
        out_specs=pl.BlockSpec((None, None, nchunk, HEAD_DIM), lambda i, j: (i, j, 0, 0)),
        out_shape=jax.ShapeDtypeStruct((b, hkv, nchunk, HEAD_DIM), BF16),
        compiler_params=_cp("parallel", "parallel"),
        name="nsa_compress",
    )(tok, pe.reshape(2, width), w1, w2)


def _nsa_cmp_kernel(q_ref, kc_ref, vc_ref, oc_ref, sel_ref, *, tq, nc, ns):
    q0 = pl.program_id(2) * tq
    kc = kc_ref[...]
    vc = vc_ref[...]
    t_row = q0 + lax.broadcasted_iota(jnp.int32, (tq, nc), 0)
    c_idx = lax.broadcasted_iota(jnp.int32, (tq, nc), 1)
    vis = (c_idx * CMP_STRIDE + (CMP_LEN - 1) <= t_row) & (c_idx < nc - 1)
    heads = [slice(h * HEAD_DIM, (h + 1) * HEAD_DIM) for h in range(NSA_GROUP)]
    scores = [_nt_dot(q_ref[:, sl], kc) for sl in heads]
    probs = []
    for s in scores:
        s = jnp.where(vis, s, NEG)
        e = jnp.where(vis, jnp.exp(s - jnp.max(s, axis=-1, keepdims=True)), 0.0)
        den = jnp.sum(e, axis=-1, keepdims=True)
        probs.append(e / jnp.where(den > 0, den, 1.0))
    for sl, p in zip(heads, probs):
        oc_ref[:, sl] = _dot(p.astype(BF16), vc).astype(oc_ref.dtype)
    imp = sum(probs[1:], probs[0])
    jj = lax.broadcasted_iota(jnp.int32, (ns, nc), 0)
    cc = lax.broadcasted_iota(jnp.int32, (ns, nc), 1) * CMP_STRIDE
    ov = jnp.where((cc < (jj + 1) * SEL_LEN) & (cc + CMP_LEN > jj * SEL_LEN), 1.0, 0.0).astype(BF16)
    imp_t = sum(_nt_dot(ov, part) for part in _split3(imp))
    jb = lax.broadcasted_iota(jnp.int32, (ns, tq), 0)
    tt = q0 + lax.broadcasted_iota(jnp.int32, (ns, tq), 1)
    cur = lax.shift_right_logical(tt, jnp.full_like(tt, SEL_SHIFT))
    forced = (jb == 0) | (jb == cur) | (jb == cur - 1)
    valid = jb * SEL_LEN <= tt
    score = jnp.where(valid, imp_t + jnp.where(forced, FORCE_BONUS, 0.0), NEG)
    rank = jnp.zeros((ns, tq), F32)
    for i in range(ns):
        row = score[i:i + 1, :]
        rank = rank + jnp.where(row > score, 1.0, jnp.where((row == score) & (jb > i), 1.0, 0.0))
    sel_t = jnp.where(valid & (rank < float(min(SEL_TOPN, ns))), 1.0, 0.0).astype(BF16)
    eye = jnp.where(lax.broadcasted_iota(jnp.int32, (tq, tq), 0) == lax.broadcasted_iota(jnp.int32, (tq, tq), 1),
                    1.0, 0.0).astype(BF16)
    sel_ref[...] = _nt_dot(eye, sel_t)


def _nsa_cmp(qkv, kc, vc, b, t, tq=1024):
    nc = kc.shape[2]
    ns = t // SEL_LEN
    nq = t // tq
    gw = NSA_GROUP * HEAD_DIM
    return pl.pallas_call(
        functools.partial(_nsa_cmp_kernel, tq=tq, nc=nc, ns=ns),
        grid=(b, NSA_KV_HEADS, nq),
        in_specs=[pl.BlockSpec((tq, gw), lambda i, k, j: (i * nq + j, k)),
                  pl.BlockSpec((None, None, nc, HEAD_DIM), lambda i, k, j: (i, k, 0, 0)),
                  pl.BlockSpec((None, None, nc, HEAD_DIM), lambda i, k, j: (i, k, 0, 0))],
        out_specs=[pl.BlockSpec((tq, gw), lambda i, k, j: (i * nq + j, k)),
                   pl.BlockSpec((None, None, tq, ns), lambda i, k, j: (i, k, j, 0))],
        out_shape=[jax.ShapeDtypeStruct((b * t, NSA_Q_W), BF16),
                   jax.ShapeDtypeStruct((b, NSA_KV_HEADS, t, ns), F32)],
        compiler_params=_cp("parallel", "parallel", "parallel"),
        name="nsa_cmp_select",
    )(qkv, kc, vc)


def _flash_update(s, v, bias, state, g, tq):
    m, l, acc = state
    tk = v.shape[0]
    if bias is not None:
        s = (s.reshape(g, tq, tk) + bias[None]).reshape(g * tq, tk)
    m_new = jnp.maximum(m, jnp.max(s, axis=-1, keepdims=True))
    alpha = jnp.exp(m - m_new)
    v_ext = jnp.concatenate([v, jnp.ones_like(v)], axis=1)
    half = tk // 2
    pv = None
    for c0 in (0, half):
        p = jnp.exp((s[:, c0:c0 + half] - m_new).astype(BF16))
        part = _dot(p, v_ext[c0:c0 + half])
        pv = part if pv is None else pv + part
    return m_new, alpha * l + pv[:, HEAD_DIM:], alpha * acc + pv[:, :HEAD_DIM]


def _nsa_sel_win_kernel(q_ref, ks_ref, vs_ref, kw_ref, vw_ref, sel_ref, oc_ref, g_ref, o_ref, *, tq, tk, stack, ns):
    q0 = pl.program_id(1) * tq
    g = NSA_GROUP
    nkv = NSA_KV_HEADS
    per_tile = tk // SEL_LEN
    units = [(kv, h0) for kv in range(nkv) for h0 in range(0, g, stack)]
    kv_lanes = [slice(kv * HEAD_DIM, (kv + 1) * HEAD_DIM) for kv in range(nkv)]

    def q_lanes(kv, h):
        return slice((kv * g + h) * HEAD_DIM, (kv * g + h + 1) * HEAD_DIM)

    qs = [jnp.concatenate([q_ref[:, q_lanes(kv, h0 + j)] for j in range(stack)], axis=0) for kv, h0 in units]
    sel_bias = [((sel_ref[kv] - 1.0) * (-NEG)).astype(BF16) for kv in range(nkv)]
    ej = lax.broadcasted_iota(jnp.int32, (ns, tk), 0)
    eb = lax.broadcasted_iota(jnp.int32, (ns, tk), 1)
    eb = lax.shift_right_logical(eb, jnp.full_like(eb, SEL_SHIFT))

    def init():
        return tuple((jnp.full((stack * tq, 1), NEG, F32), jnp.zeros((stack * tq, HEAD_DIM), F32),
                      jnp.zeros((stack * tq, HEAD_DIM), F32)) for _ in units)

    def flash_all(k_ref, v_ref, k0, width, biases, states):
        scores = [_nt_dot(qs[u], k_ref[pl.ds(k0, width), kv_lanes[kv]]) for u, (kv, _) in enumerate(units)]
        return tuple(_flash_update(scores[u], v_ref[pl.ds(k0, width), kv_lanes[kv]], biases[kv], states[u], stack, tq)
                     for u, (kv, _) in enumerate(units))

    def sel_step(kt, states, causal):
        k0 = pl.multiple_of(kt * tk, tk)
        expand = jnp.where(ej == eb + kt * per_tile, 1.0, 0.0).astype(BF16)
        biases = [_dot(sel_bias[kv], expand) for kv in range(nkv)]
        if causal is not None:
            biases = [bias + causal for bias in biases]
        return flash_all(ks_ref, vs_ref, k0, tk, biases, states)

    kt_d = q0 // tk
    t_row = q0 + lax.broadcasted_iota(jnp.int32, (tq, tk), 0)
    col = kt_d * tk + lax.broadcasted_iota(jnp.int32, (tq, tk), 1)
    states = sel_step(kt_d, init(), jnp.where(col <= t_row, 0.0, NEG))
    states = lax.fori_loop(0, kt_d, lambda kt, st: sel_step(kt, st, None), states)
    o_s = [acc * (1.0 / l) for (_, l, acc) in states]

    def win_step(k0, states, bias):
        return flash_all(kw_ref, vw_ref, k0, tq, [bias] * nkv, states)

    ri = lax.broadcasted_iota(jnp.int32, (tq, tq), 0)
    ci = lax.broadcasted_iota(jnp.int32, (tq, tq), 1)
    states = win_step(pl.multiple_of(q0, tq), init(), jnp.where(ci <= ri, 0.0, NEG))
    n_full = jnp.minimum(WINDOW // tq - 1, q0 // tq)
    states = lax.fori_loop(0, n_full, lambda i, st: win_step(pl.multiple_of(q0 - (i + 1) * tq, tq), st, None),
                           states)
    edge_bias = jnp.where(ci > ri, 0.0, NEG) + jnp.where(q0 >= WINDOW, 0.0, NEG)
    states = win_step(pl.multiple_of(jnp.maximum(q0 - WINDOW, 0), tq), states, edge_bias)
    o_w = [acc * (1.0 / l) for (_, l, acc) in states]

    for u, (kv, h0) in enumerate(units):
        gates = g_ref[:, kv * LANES:(kv + 1) * LANES]
        for j in range(stack):
            h = h0 + j
            sl = q_lanes(kv, h)
            rows = slice(j * tq, (j + 1) * tq)
            out = (gates[:, 3 * h:3 * h + 1] * oc_ref[:, sl].astype(F32)
                   + gates[:, 3 * h + 1:3 * h + 2] * o_s[u][rows] + gates[:, 3 * h + 2:3 * h + 3] * o_w[u][rows])
            o_ref[:, sl] = out.astype(o_ref.dtype)


def _nsa_sel_win(qkv, sel, oc, gates, b, t, tq=512, tk=512, stack=4):
    nq = t // tq
    ns = t // SEL_LEN
    kv_block = NSA_Q_W // NSA_KV_W

    def kv_spec(which):
        return pl.BlockSpec((t, NSA_KV_W), lambda i, j: (i, kv_block + which))

    return pl.pallas_call(
        functools.partial(_nsa_sel_win_kernel, tq=tq, tk=tk, stack=stack, ns=ns),
        grid=(b, nq),
        in_specs=[pl.BlockSpec((tq, NSA_Q_W), lambda i, j: (i * nq + j, 0)),
                  kv_spec(2), kv_spec(3), kv_spec(4), kv_spec(5),
                  pl.BlockSpec((None, NSA_KV_HEADS, tq, ns), lambda i, j: (i, 0, j, 0)),
                  pl.BlockSpec((tq, NSA_Q_W), lambda i, j: (i * nq + j, 0)),
                  pl.BlockSpec((tq, NSA_KV_HEADS * LANES), lambda i, j: (i * nq + j, 0))],
        out_specs=pl.BlockSpec((tq, NSA_Q_W), lambda i, j: (i * nq + j, 0)),
        out_shape=jax.ShapeDtypeStruct((b * t, NSA_Q_W), BF16),
        compiler_params=_cp("parallel", "parallel"),
        name="nsa_select_window",
    )(qkv, qkv, qkv, qkv, qkv, sel, oc, gates)


def _stickbreak_kernel(q_ref, k_ref, v_ref, o_ref, *, tile, nh):
    q0 = pl.program_id(2) * tile
    ri = lax.broadcasted_iota(jnp.int32, (tile, tile), 0)
    ci = lax.broadcasted_iota(jnp.int32, (tile, tile), 1)
    strict = ci < ri
    tri = jnp.where(ri > ci, 1.0, 0.0).astype(BF16)
    lanes = [slice(h * HEAD_DIM, (h + 1) * HEAD_DIM) for h in range(nh)]

    def step(k0, diag, carry):
        zs = [_nt_dot(q_ref[:, lanes[h]], k_ref[pl.ds(k0, tile), lanes[h]]) for h in range(nh)]
        log_bs, sufs, sums = [], [], []
        for z in zs:
            log_b = jnp.minimum(z, 0.0) - jnp.log(1.0 + jnp.exp(-jnp.abs(z)))
            log_1m = log_b - z
            if diag:
                log_1m = jnp.where(strict, log_1m, 0.0)
            hi = log_1m.astype(BF16)
            lo = (log_1m - hi.astype(F32)).astype(BF16)
            log_bs.append(log_b)
            sufs.append(_dot(hi, tri) + _dot(lo, tri))
            sums.append(jnp.sum(log_1m, axis=-1, keepdims=True))
        out = []
        for h in range(nh):
            later, acc = carry[2 * h], carry[2 * h + 1]
            a = jnp.exp(log_bs[h] + (sufs[h] + later))
            if diag:
                a = jnp.where(strict, a, 0.0)
            out.extend((later + sums[h], acc + _dot(a.astype(BF16), v_ref[pl.ds(k0, tile), lanes[h]])))
        return tuple(out)

    zero = (jnp.zeros((tile, 1), F32), jnp.zeros((tile, HEAD_DIM), F32)) * nh
    carry = step(pl.multiple_of(q0, tile), True, zero)

    def live(state):
        i, c = state
        top = c[0]
        for h in range(1, nh):
            top = jnp.maximum(top, c[2 * h])
        return (i < q0 // tile) & (jnp.max(top) > F32_EXP_UNDERFLOW)

    def advance(state):
        i, c = state
        return i + 1, step(pl.multiple_of(q0 - (i + 1) * tile, tile), False, c)

    _, carry = lax.while_loop(live, advance, (jnp.int32(0), carry))
    for h in range(nh):
        o_ref[:, lanes[h]] = carry[2 * h + 1].astype(o_ref.dtype)


def _stickbreak(qkv, b, t, tile=256, nh=4):
    nq = t // tile
    w = nh * HEAD_DIM
    nhb = SB_HEADS // nh
    return pl.pallas_call(
        functools.partial(_stickbreak_kernel, tile=tile, nh=nh),
        grid=(b, nhb, nq),
        in_specs=[pl.BlockSpec((tile, w), lambda i, h, j: (i * nq + j, h)),
                  pl.BlockSpec((t, w), lambda i, h, j: (i, nhb + h)),
                  pl.BlockSpec((t, w), lambda i, h, j: (i, 2 * nhb + h))],
        out_specs=pl.BlockSpec((tile, w), lambda i, h, j: (i * nq + j, h)),
        out_shape=jax.ShapeDtypeStruct((b * t, SB_W), BF16),
        compiler_params=_cp("parallel", "parallel", "parallel"),
        name="stickbreak",
    )(qkv, qkv, qkv)


def _cross_kernel(x_ref, g_ref, wq_ref, kv_ref, wo_ref, o_ref):
    x = x_ref[...]
    h = _rms(x, g_ref[...]).astype(BF16)
    q = _dot(h, wq_ref[...]).astype(BF16)
    xw = X_HEADS * HEAD_DIM
    outs = []
    for hd in range(X_HEADS):
        sl = slice(hd * HEAD_DIM, (hd + 1) * HEAD_DIM)
        s = _nt_dot(q[:, sl], kv_ref[:, sl]) * SCALE
        e = jnp.exp(s - jnp.max(s, axis=-1, keepdims=True))
        p = e / jnp.sum(e, axis=-1, keepdims=True)
        outs.append(_dot(p.astype(BF16), kv_ref[:, xw + hd * HEAD_DIM:xw + (hd + 1) * HEAD_DIM]).astype(BF16))
    o_ref[...] = x + _dot(jnp.concatenate(outs, axis=1), wo_ref[...])


def _cross(x, g, wq, kv, wo, b, t, tq=1024):
    n, d = x.shape
    nq = t // tq
    mem_len = kv.shape[0] // b
    return pl.pallas_call(
        _cross_kernel,
        grid=(b, nq),
        in_specs=[pl.BlockSpec((tq, d), lambda i, j: (i * nq + j, 0)),
                  pl.BlockSpec((1, d), lambda i, j: (0, 0)),
                  pl.BlockSpec(wq.shape, lambda i, j: (0, 0)),
                  pl.BlockSpec((mem_len, kv.shape[1]), lambda i, j: (i, 0)),
                  pl.BlockSpec(wo.shape, lambda i, j: (0, 0))],
        out_specs=pl.BlockSpec((tq, d), lambda i, j: (i * nq + j, 0)),
        out_shape=jax.ShapeDtypeStruct((n, d), F32),
        compiler_params=_cp("parallel", "parallel"),
        name="cross_attention",
    )(x, g.reshape(1, d), wq, kv, wo)


def _to_row_tiles(ref, val):
    rows, width = val.shape
    c_per = width // LANES
    for c in range(c_per):
        ref[pl.ds(c, rows, stride=c_per), :] = val[:, c * LANES:(c + 1) * LANES]


def _from_row_tiles(ref, c_per):
    rows = ref.shape[0] // c_per
    return jnp.concatenate([ref[pl.ds(c, rows, stride=c_per), :] for c in range(c_per)], axis=1)


def _router_kernel(x_ref, g_ref, w_ref, b_ref, h_ref, r_ref):
    h = _rms(x_ref[...], g_ref[...])
    _to_row_tiles(h_ref, h)
    h_hi = h.astype(BF16)
    h_mid = (h - h_hi.astype(F32)).astype(BF16)
    two = _dot(h_hi, w_ref[...])
    logits = two[:, :LANES] + two[:, LANES:] + _dot(h_mid, w_ref[:, :LANES]) + b_ref[...]
    lane = lax.broadcasted_iota(jnp.int32, logits.shape, 1).astype(F32)
    big = float(LANES)
    is_g = lane < N_GROUPS
    lg = jnp.where(is_g, logits, NEG)
    mg = jnp.max(lg, axis=-1, keepdims=True)
    p_gtop = 1.0 / jnp.sum(jnp.where(is_g, jnp.exp(lg - mg), 0.0), axis=-1, keepdims=True)
    g_top = jnp.min(jnp.where(lg == mg, lane, big), axis=-1, keepdims=True)
    lo = N_GROUPS + EXPERTS_PER_GROUP * g_top
    in_grp = (lane >= lo) & (lane < lo + EXPERTS_PER_GROUP)
    le = jnp.where(in_grp, logits, NEG)
    ee = jnp.where(in_grp, jnp.exp(le - jnp.max(le, axis=-1, keepdims=True)), 0.0)
    p_in = ee / jnp.sum(ee, axis=-1, keepdims=True)
    p1 = jnp.where(in_grp, p_in, -1.0)
    m1 = jnp.max(p1, axis=-1, keepdims=True)
    i1 = jnp.min(jnp.where(p1 == m1, lane, big), axis=-1, keepdims=True)
    p2 = jnp.where(lane == i1, -1.0, p1)
    m2 = jnp.max(p2, axis=-1, keepdims=True)
    i2 = jnp.min(jnp.where(p2 == m2, lane, big), axis=-1, keepdims=True)
    tot = m1 + m2
    r_ref[...] = jnp.where(lane == 0, i1 - N_GROUPS,
                           jnp.where(lane == 1, i2 - N_GROUPS,
                                     jnp.where(lane == 2, m1 / tot * p_gtop,
                                               jnp.where(lane == 3, m2 / tot * p_gtop, 0.0))))


def _router(x, g, w_r, b_r, tm=1024):
    n, d = x.shape
    return pl.pallas_call(
        _router_kernel,
        grid=(n // tm,),
        in_specs=[pl.BlockSpec((tm, d), lambda i: (i, 0)), pl.BlockSpec((1, d), lambda i: (0, 0)),
                  pl.BlockSpec((d, 2 * LANES), lambda i: (0, 0)), pl.BlockSpec((1, LANES), lambda i: (0, 0))],
        out_specs=[pl.BlockSpec((tm * (d // LANES), LANES), lambda i: (i, 0)),
                   pl.BlockSpec((tm, LANES), lambda i: (i, 0))],
        out_shape=[jax.ShapeDtypeStruct((n * (d // LANES), LANES), F32), jax.ShapeDtypeStruct((n, LANES), F32)],
        compiler_params=_cp("parallel"),
        name="moe_router",
    )(x, g.reshape(1, d), w_r, b_r)


ROW_CHUNKS = 16
WEIGHT_DMA_PRIORITY = 0
WEIGHT_SLOTS = 2


def _wait_rows(src_hbm, dst, sem):
    pltpu.make_async_copy(src_hbm.at[pl.ds(0, dst.shape[0]), :], dst, sem).wait()


def _row_copy(src_hbm, dst, sem, src_row, dst_row):
    c = ROW_CHUNKS
    return pltpu.make_async_copy(src_hbm.at[pl.ds(pl.multiple_of(src_row * c, c), c), :],
                                 dst.at[pl.ds(pl.multiple_of(dst_row * c, c), c), :], sem)


def _expert_kernel(idx_ref, ord_ref, oe_ref, cnt_ref, src_hbm, wg_hbm, wu_hbm, wd_hbm, o_ref,
                   xbuf, sems, wg_f, wu_f, wd_f, wsems, wg_s, wu_s, wd_s, *, tm):
    i = pl.program_id(0)
    nt = cnt_ref[0]
    n_ord = cnt_ref[1]
    live = i < nt
    slot = lax.rem(i, 2)

    def gather(tile, s):
        def issue(r2, c):
            for queue in range(2):
                r = 2 * r2 + queue
                _row_copy(src_hbm, xbuf.at[s], sems.at[s], idx_ref[tile * tm + r], r).start(priority=queue)
            return c
        lax.fori_loop(0, tm // 2, issue, 0, unroll=4)

    def weight_copies(k, s):
        e = oe_ref[k]
        return (pltpu.make_async_copy(wg_hbm.at[e], wg_f.at[s], wsems.at[s, 0]),
                pltpu.make_async_copy(wu_hbm.at[e], wu_f.at[s], wsems.at[s, 1]),
                pltpu.make_async_copy(wd_hbm.at[e], wd_f.at[s], wsems.at[s, 2]))

    ahead = WEIGHT_SLOTS - 1

    @pl.when(i == 0)
    def _():
        gather(0, 0)
        for j in range(ahead):
            @pl.when(j < n_ord)
            def _():
                for cp in weight_copies(j, j):
                    cp.start(priority=WEIGHT_DMA_PRIORITY)

    @pl.when(i + 1 < nt)
    def _():
        gather(i + 1, 1 - slot)

    k = ord_ref[i]
    fresh = live & ((i == 0) | (k != ord_ref[jnp.maximum(i - 1, 0)]))
    ws = lax.rem(k, WEIGHT_SLOTS)

    @pl.when(fresh)
    def _():
        for cp in weight_copies(k, ws):
            cp.wait()

        @pl.when(k + ahead < n_ord)
        def _():
            for cp in weight_copies(k + ahead, lax.rem(k + ahead, WEIGHT_SLOTS)):
                cp.start(priority=WEIGHT_DMA_PRIORITY)

        wg_s[...] = wg_f[ws].astype(BF16)
        wu_s[...] = wu_f[ws].astype(BF16)
        wd_s[...] = wd_f[ws].astype(BF16)

    @pl.when(live)
    def _():
        _wait_rows(src_hbm, xbuf.at[slot], sems.at[slot])
        x = _from_row_tiles(xbuf.at[slot], ROW_CHUNKS).astype(BF16)
        a = _dot(x, wg_s[...])
        u = _dot(x, wu_s[...])
        act = a * (1.0 / (1.0 + jnp.exp(-a))) * u
        _to_row_tiles(o_ref, _dot(act.astype(BF16), wd_s[...]))

    @pl.when(jnp.logical_not(live))
    def _():
        o_ref[...] = jnp.zeros_like(o_ref)


def _experts(src, tok_of_row, w_g, w_u, w_d, tile_ord, ord_expert, counts, tm=EXPERT_TILE):
    c = ROW_CHUNKS
    n_tiles_max = tok_of_row.shape[0] // tm
    d = c * LANES
    f = w_g.shape[2]
    any_spec = pl.BlockSpec(memory_space=pl.ANY)
    return pl.pallas_call(
        functools.partial(_expert_kernel, tm=tm),
        grid_spec=pltpu.PrefetchScalarGridSpec(
            num_scalar_prefetch=4,
            grid=(n_tiles_max,),
            in_specs=[any_spec, any_spec, any_spec, any_spec],
            out_specs=pl.BlockSpec((tm * c, LANES), lambda i, *_: (i, 0)),
            scratch_shapes=[pltpu.VMEM((2, tm * c, LANES), F32), pltpu.SemaphoreType.DMA((2,)),
                            pltpu.VMEM((WEIGHT_SLOTS, d, f), F32), pltpu.VMEM((WEIGHT_SLOTS, d, f), F32),
                            pltpu.VMEM((WEIGHT_SLOTS, f, d), F32), pltpu.SemaphoreType.DMA((WEIGHT_SLOTS, 3)),
                            pltpu.VMEM((d, f), BF16), pltpu.VMEM((d, f), BF16), pltpu.VMEM((f, d), BF16)]),
        out_shape=jax.ShapeDtypeStruct((n_tiles_max * tm * c, LANES), F32),
        compiler_params=_cp("arbitrary"),
        name="moe_experts",
    )(tok_of_row, tile_ord, ord_expert, counts, src, w_g, w_u, w_d)


def _combine_kernel(p0_ref, p1_ref, ys_hbm, x_ref, r_ref, g_ref, o_ref, buf, sems, *, tm):
    i = pl.program_id(0)
    slot = lax.rem(i, 2)

    def fetch(tile, s):
        def issue(r, c):
            _row_copy(ys_hbm, buf.at[s, 0], sems.at[s], p0_ref[tile * tm + r], r).start()
            _row_copy(ys_hbm, buf.at[s, 1], sems.at[s], p1_ref[tile * tm + r], r).start(priority=1)
            return c
        lax.fori_loop(0, tm, issue, 0, unroll=8)

    @pl.when(i == 0)
    def _():
        fetch(0, 0)

    @pl.when(i + 1 < pl.num_programs(0))
    def _():
        fetch(i + 1, 1 - slot)

    _wait_rows(ys_hbm, buf.at[slot, 0], sems.at[slot])
    _wait_rows(ys_hbm, buf.at[slot, 1], sems.at[slot])
    route = r_ref[...]
    y = (route[:, 2:3] * _from_row_tiles(buf.at[slot, 0], ROW_CHUNKS)
         + route[:, 3:4] * _from_row_tiles(buf.at[slot, 1], ROW_CHUNKS))
    o_ref[...] = _rms(x_ref[...] + y, g_ref[...])


def _combine(ys, pos0, pos1, x, route, g, tm=256):
    n, d = x.shape
    return pl.pallas_call(
        functools.partial(_combine_kernel, tm=tm),
        grid_spec=pltpu.PrefetchScalarGridSpec(
            num_scalar_prefetch=2,
            grid=(n // tm,),
            in_specs=[pl.BlockSpec(memory_space=pl.ANY),
                      pl.BlockSpec((tm, d), lambda i, p0, p1: (i, 0)),
                      pl.BlockSpec((tm, LANES), lambda i, p0, p1: (i, 0)),
                      pl.BlockSpec((1, d), lambda i, p0, p1: (0, 0))],
            out_specs=pl.BlockSpec((tm, d), lambda i, p0, p1: (i, 0)),
            scratch_shapes=[pltpu.VMEM((2, 2, tm * ROW_CHUNKS, LANES), F32), pltpu.SemaphoreType.DMA((2,))]),
        out_shape=jax.ShapeDtypeStruct((n, d), F32),
        compiler_params=_cp("arbitrary"),
        name="moe_combine_norm",
    )(pos0, pos1, ys, x, route, g.reshape(1, d))


def _dispatch_plan(e1, e2, tm=EXPERT_TILE):
    n = e1.shape[0]
    n_tiles_max = (2 * n) // tm + N_EXPERTS
    rows = n_tiles_max * tm
    e_flat = jnp.concatenate([e1, e2])
    tok = jnp.concatenate([jnp.arange(n, dtype=jnp.int32)] * 2)
    onehot = (e_flat[:, None] == jnp.arange(N_EXPERTS, dtype=jnp.int32)[None, :]).astype(jnp.int32)
    csum = jnp.cumsum(onehot, axis=0)
    count = csum[-1]
    rank = jnp.sum((csum - 1) * onehot, axis=1)
    tiles_per = (count + tm - 1) // tm
    tile_end = jnp.cumsum(tiles_per)
    row_start = (tile_end - tiles_per) * tm
    dest = (row_start[e_flat] + rank).astype(jnp.int32)
    tok_of_row = jnp.zeros((rows,), jnp.int32).at[dest].set(tok)
    n_tiles = tile_end[-1].astype(jnp.int32)
    tile_id = jnp.minimum(jnp.arange(n_tiles_max, dtype=jnp.int32), n_tiles - 1)
    tile_expert = jnp.sum((tile_id[:, None] >= tile_end[None, :]).astype(jnp.int32), axis=1)
    used = jnp.cumsum((tiles_per > 0).astype(jnp.int32))
    tile_ord = (used[tile_expert] - 1).astype(jnp.int32)
    ordinals = jnp.arange(N_EXPERTS, dtype=jnp.int32)
    ord_expert = jnp.minimum(jnp.sum((used[None, :] <= ordinals[:, None]).astype(jnp.int32), axis=1),
                             N_EXPERTS - 1).astype(jnp.int32)
    counts = jnp.stack([n_tiles, used[-1]]).astype(jnp.int32)
    return tok_of_row, tile_ord, ord_expert, counts, dest[:n], dest[n:]


def _rope_tables(positions):
    half = ROT_DIM // 2
    inv = ROPE_THETA ** (-jnp.arange(half, dtype=F32) * (2.0 / ROT_DIM))
    ang = positions.astype(F32).reshape(-1, 1) * inv[None, :]
    cos, sin = jnp.cos(ang), jnp.sin(ang)
    n = ang.shape[0]
    c = jnp.concatenate([cos, cos, jnp.ones((n, LANES - ROT_DIM), F32)], axis=1)
    s1 = jnp.concatenate([-sin, jnp.zeros((n, LANES - half), F32)], axis=1)
    s2 = jnp.concatenate([jnp.zeros((n, half), F32), sin, jnp.zeros((n, LANES - ROT_DIM), F32)], axis=1)
    return c, s1, s2


def _layer(x, mem, positions, g_mix, w_in, cmp_pe_k, cmp_w1_k, cmp_w2_k, cmp_pe_v, cmp_w1_v, cmp_w2_v,
           w_br_nsa, w_br_sb, w_o, g_cross, g_mem, w_cq, w_ck, w_cv, w_co,
           g_moe, w_rg, b_rg, w_re, b_re, w_eg, w_eu, w_ed):
    b, t, d = x.shape
    n = b * t
    xf = x.reshape(n, d)

    g0 = NSA_W
    w_t = w_in.T
    w_nsa = _cast_rows(w_t, 0, g0)
    w_rest = _cast_rows(w_t, g0 + NSA_GATES, w_t.shape[0] - g0 - NSA_GATES)
    w_g = w_t[g0:g0 + NSA_GATES].reshape(NSA_KV_HEADS, 3 * NSA_GROUP, d)
    w_g = jnp.pad(w_g, ((0, 0), (0, LANES - 3 * NSA_GROUP), (0, 0))).reshape(NSA_KV_HEADS * LANES, d).astype(BF16)
    h = _rmsnorm(xf, g_mix, BF16)
    c, s1, s2 = _rope_tables(positions)
    row_spec = pl.BlockSpec((2048, LANES), lambda i, j: (i, 0))
    qkv_nsa = _mm(functools.partial(_mm_nsa_kernel, tn=256), h, w_nsa, NSA_W, 0, 2048, 256, BF16, w_t=True,
                  extra=(c, s1, s2), extra_specs=(row_spec, row_spec, row_spec), name="in_proj_nsa")
    qkv_sb = _mm(functools.partial(_mm_sb_kernel, q_tiles=SB_W // WIDE_TN), h, w_rest, 3 * SB_W, 0,
                 1024, WIDE_TN, BF16, w_t=True, name="in_proj_sb")
    gates = _mm(_mm_sigmoid_kernel, h, w_rest, 2 * d, 3 * SB_W // WIDE_TN, 1024, WIDE_TN, BF16, w_t=True,
                name="in_proj_gates")
    nsa_gates = _mm(_mm_sigmoid_kernel, h, w_g, NSA_KV_HEADS * LANES, 0, 1024, NSA_KV_HEADS * LANES, F32,
                    w_t=True, name="in_proj_nsa_gates")

    def chunked(col0):
        tok = qkv_nsa[:, col0:col0 + NSA_KV_W].reshape(b, t, NSA_KV_HEADS, HEAD_DIM).transpose(0, 2, 1, 3)
        return tok.reshape(b, NSA_KV_HEADS, t // CMP_STRIDE, CMP_STRIDE * HEAD_DIM)

    kc = _compress(chunked(NSA_Q_W), cmp_pe_k, cmp_w1_k, cmp_w2_k)
    vc = _compress(chunked(NSA_Q_W + NSA_KV_W), cmp_pe_v, cmp_w1_v, cmp_w2_v)
    o_c, sel = _nsa_cmp(qkv_nsa, kc, vc, b, t)
    o_nsa = _nsa_sel_win(qkv_nsa, sel, o_c, nsa_gates, b, t)
    o_sb = _stickbreak(qkv_sb, b, t)
    merged = _merge(o_nsa, o_sb, w_br_nsa.astype(BF16), w_br_sb.astype(BF16), gates)
    x1 = _mm(_mm_resid_kernel, merged, w_o.astype(BF16), d, 0, 1024, WIDE_TN, F32,
             extra=(xf,), extra_specs=(pl.BlockSpec((1024, WIDE_TN), lambda i, j: (i, j)),), name="out_proj")

    w_kv = jnp.concatenate([w_ck, w_cv], axis=1).astype(BF16)
    kv = _norm_mm(mem.reshape(-1, d), g_mem, w_kv, BF16)
    x2 = _cross(x1, g_cross, w_cq.astype(BF16), kv, w_co.astype(BF16), b, t)

    w_r = jnp.pad(jnp.concatenate([w_rg, w_re], axis=1), ((0, 0), (0, LANES - N_GROUPS - N_EXPERTS)))
    b_r = jnp.pad(jnp.concatenate([b_rg, b_re]), (0, LANES - N_GROUPS - N_EXPERTS)).reshape(1, LANES)
    w_r_hi = w_r.astype(BF16)
    w_r2 = jnp.concatenate([w_r_hi, (w_r - w_r_hi.astype(F32)).astype(BF16)], axis=1)
    hm, route = _router(x2, g_moe, w_r2, b_r)
    e1 = route[:, 0].astype(jnp.int32)
    e2 = route[:, 1].astype(jnp.int32)
    tok_of_row, tile_ord, ord_expert, counts, pos0, pos1 = _dispatch_plan(e1, e2)
    f = w_eg.shape[-1]
    ys = _experts(hm, tok_of_row, w_eg.reshape(N_EXPERTS, d, f), w_eu.reshape(N_EXPERTS, d, f),
                  w_ed.reshape(N_EXPERTS, f, d), tile_ord, ord_expert, counts)
    return ys, pos0, pos1, x2, route


def kernel(x, mem, positions, g_mix, w_in, cmp_pe_k, cmp_w1_k, cmp_w2_k, cmp_pe_v, cmp_w1_v, cmp_w2_v, w_br_nsa, w_br_sb, w_o, g_cross, g_mem, w_cq, w_ck, w_cv, w_co, g_moe, w_rg, b_rg, w_re, b_re, w_eg, w_eu, w_ed, g_final):
    b, t, d = x.shape
    assert w_in.shape[0] == 1, "one layer"
    ys, pos0, pos1, x2, route = _layer(
        x, mem, positions, g_mix[0], w_in[0], cmp_pe_k[0], cmp_w1_k[0], cmp_w2_k[0], cmp_pe_v[0], cmp_w1_v[0],
        cmp_w2_v[0], w_br_nsa[0], w_br_sb[0], w_o[0], g_cross[0], g_mem[0], w_cq[0], w_ck[0], w_cv[0], w_co[0],
        g_moe[0], w_rg[0], b_rg[0], w_re[0], b_re[0], w_eg[0], w_eu[0], w_ed[0])
    return _combine(ys, pos0, pos1, x2, route, g_final).reshape(b, t, d)
```
